```python
import numpy as np
import jax
import jax.numpy as jnp
from jax import lax

D_MODEL = 1024
BATCH = 1
SEQ = 16384
DEPTH = 4
DEC_BATCH = 8
DEC_SEQ = 2048
PAST_LEN = 128

N_MIXERS = 4
HEAD_DIM = 64
D_FF = 2816
NORM_EPS = 1e-6
NEG_INF = -1e30
Q_BLOCK = 128
GRID_W = 64
ROPE_THETA = 10000.0
RESIDUAL_HALF = 0.5

A_HEADS = 16
A_KV_HEADS = 4
A_HALF_WINDOW = 128
B_HEADS = 16
B_Q_LORA = 512
B_KV_LORA = 256
B_NOPE = 64
B_ROPE = 32
B_V = 64
C_HEADS = 8
C_PATTERNS = ((128, 1), (512, 4), (2048, 16))
D_HEADS = 16
D_KV_HEADS = 4

kernel_name = "hybrid_bidir_encoder_interleaved"


def n_layers_of(m):
    return len(range(m, DEPTH, N_MIXERS))


def rms_norm(x, g):
    xf = x.astype(jnp.float32)
    y = xf * lax.rsqrt(jnp.mean(xf * xf, axis=-1, keepdims=True) + NORM_EPS)
    return (y * g.astype(jnp.float32)).astype(x.dtype)


def swiglu(x, wg, wu, wd):
    return (jax.nn.silu(x @ wg) * (x @ wu)) @ wd


def alibi_slopes(n):
    return jnp.asarray(2.0 ** (-8.0 * np.arange(1, n + 1) / n), dtype=jnp.float32)


def rope(x, pos):
    dr = x.shape[-1]
    half = dr // 2
    inv = ROPE_THETA ** (-jnp.arange(half, dtype=jnp.float32) / half)
    ang = pos.astype(jnp.float32)[:, None] * inv[None, :]
    cos = jnp.cos(ang)[:, None, :]
    sin = jnp.sin(ang)[:, None, :]
    xf = x.astype(jnp.float32)
    x1, x2 = xf[..., :half], xf[..., half:]
    return jnp.concatenate([x1 * cos - x2 * sin, x1 * sin + x2 * cos], axis=-1).astype(x.dtype)


def banded_attention(q, k, v, w, slopes, dist_scale, n_valid, sink=None):
    N, L, Hk, G, hd = q.shape
    nb = L // w
    qb = q.reshape(N, nb, w, Hk, G, hd)
    pad = ((0, 0), (w, w), (0, 0), (0, 0))
    kp = jnp.pad(k, pad)
    vp = jnp.pad(v, pad)

    def neigh(t):
        return jnp.concatenate([t[:, o * w:o * w + L].reshape(N, nb, w, Hk, hd) for o in range(3)], axis=2)

    kb, vb = neigh(kp), neigh(vp)
    rel = jnp.arange(3 * w)[None, :] - w - jnp.arange(w)[:, None]
    key_pos = jnp.arange(nb)[:, None, None] * w + jnp.arange(3 * w)[None, None, :] - w
    valid = (jnp.abs(rel) <= w)[None] & (key_pos >= 0) & (key_pos < n_valid)
    dist = (dist_scale * jnp.abs(rel)).astype(jnp.float32)
    bias = -slopes.reshape(Hk, G)[:, :, None, None] * dist
    s = jnp.einsum('nbqhgd,nbkhd->nbhgqk', qb, kb, preferred_element_type=jnp.float32) * (hd ** -0.5)
    s = jnp.where(valid[None, :, None, None], s + bias, NEG_INF)
    m = jnp.max(s, axis=-1, keepdims=True)
    if sink is not None:
        sk = sink.reshape(Hk, G)[:, :, None, None].astype(jnp.float32)
        m = jnp.maximum(m, sk)
    p = jnp.exp(s - m)
    denom = jnp.sum(p, axis=-1, keepdims=True)
    if sink is not None:
        denom = denom + jnp.exp(sk - m)
    p = p / denom
    o = jnp.einsum('nbhgqk,nbkhd->nbqhgd', p.astype(vb.dtype), vb, preferred_element_type=jnp.float32)
    lse = (m + jnp.log(denom))[..., 0].transpose(0, 1, 4, 2, 3)
    return o.reshape(N, L, Hk, G, hd), lse.reshape(N, L, Hk, G)


def dense_blocked_attention(q, k, v):
    N, L, Hk, G, dq = q.shape
    dv = v.shape[-1]
    nb = L // Q_BLOCK
    qb = jnp.moveaxis(q.reshape(N, nb, Q_BLOCK, Hk, G, dq), 1, 0)
    scale = dq ** -0.5

    def one_block(qblk):
        s = jnp.einsum('nqhgd,nkhd->nhgqk', qblk, k, preferred_element_type=jnp.float32) * scale
        p = jax.nn.softmax(s, axis=-1)
        return jnp.einsum('nhgqk,nkhd->nqhgd', p.astype(v.dtype), v, preferred_element_type=jnp.float32)

    ob = lax.map(one_block, qb)
    return jnp.moveaxis(ob, 0, 1).reshape(N, L, Hk, G, dv)


def mixer_a(h, wqkv, gq, gk, sink, wo):
    B, L, _ = h.shape
    G = A_HEADS // A_KV_HEADS
    nq, nk = A_HEADS * HEAD_DIM, A_KV_HEADS * HEAD_DIM
    qkv = h @ wqkv
    q = rms_norm(qkv[..., :nq].reshape(B, L, A_KV_HEADS, G, HEAD_DIM), gq)
    k = rms_norm(qkv[..., nq:nq + nk].reshape(B, L, A_KV_HEADS, HEAD_DIM), gk)
    v = qkv[..., nq + nk:].reshape(B, L, A_KV_HEADS, HEAD_DIM)
    o, _ = banded_attention(q, k, v, A_HALF_WINDOW, alibi_slopes(A_HEADS), 1, L, sink)
    return o.reshape(B, L, nq).astype(h.dtype) @ wo


def mixer_b(h, wdq, gcq, wuq, wdkv, gckv, wukv, gq, gk, wo):
    B, L, _ = h.shape
    pos = jnp.arange(L)
    c_q = rms_norm(h @ wdq, gcq)
    q = (c_q @ wuq).reshape(B, L, B_HEADS, B_NOPE + B_ROPE)
    kv_in = h @ wdkv
    c_kv = rms_norm(kv_in[..., :B_KV_LORA], gckv)
    k_rope = jnp.broadcast_to(kv_in[..., None, B_KV_LORA:], (B, L, B_HEADS, B_ROPE))
    kv = (c_kv @ wukv).reshape(B, L, B_HEADS, B_NOPE + B_V)
    k = jnp.concatenate([kv[..., :B_NOPE], k_rope], axis=-1)
    v = kv[..., B_NOPE:]
    q = rms_norm(q, gq)
    k = rms_norm(k, gk)
    q = jnp.concatenate([q[..., :B_NOPE], rope(q[..., B_NOPE:], pos)], axis=-1)
    k = jnp.concatenate([k[..., :B_NOPE], rope(k[..., B_NOPE:], pos)], axis=-1)
    o = dense_blocked_attention(q[:, :, :, None], k, v)
    return o.reshape(B, L, B_HEADS * B_V).astype(h.dtype) @ wo


def mixer_c(h, wqkv, gq, gk, wo):
    B, L, _ = h.shape
    n_groups = len(C_PATTERNS)
    qkv = (h @ wqkv).reshape(B, L, n_groups, 3, C_HEADS, HEAD_DIM)
    slopes = alibi_slopes(C_HEADS)
    outs, lses = [], []
    for g, (window, dil) in enumerate(C_PATTERNS):
        steps = window // (2 * dil)
        sub = L // dil
        sub_p = ((sub + steps - 1) // steps) * steps

        def to_sub(t):
            t = t.reshape(B, sub, dil, C_HEADS, HEAD_DIM).transpose(0, 2, 1, 3, 4)
            t = t.reshape(B * dil, sub, C_HEADS, HEAD_DIM)
            return jnp.pad(t, ((0, 0), (0, sub_p - sub), (0, 0), (0, 0)))

        q = to_sub(rms_norm(qkv[:, :, g, 0], gq))
        k = to_sub(rms_norm(qkv[:, :, g, 1], gk))
        v = to_sub(qkv[:, :, g, 2])
        o, lse = banded_attention(q[:, :, :, None], k, v, steps, slopes, dil, sub)
        o = o[:, :sub, :, 0].reshape(B, dil, sub, C_HEADS, HEAD_DIM).transpose(0, 2, 1, 3, 4)
        lse = lse[:, :sub, :, 0].reshape(B, dil, sub, C_HEADS).transpose(0, 2, 1, 3)
        outs.append(o.reshape(B, L, C_HEADS, HEAD_DIM))
        lses.append(lse.reshape(B, L, C_HEADS))
    wts = jax.nn.softmax(jnp.stack(lses, axis=0), axis=0)
    o = jnp.sum(wts[..., None] * jnp.stack(outs, axis=0), axis=0)
    return o.reshape(B, L, C_HEADS * HEAD_DIM).astype(h.dtype) @ wo


def mixer_d(h, wqkv, gq, gk, wo):
    B, L, _ = h.shape
    G = D_HEADS // D_KV_HEADS
    n_rows = L // GRID_W
    rows = jnp.repeat(jnp.arange(n_rows), GRID_W)
    cols = jnp.tile(jnp.arange(GRID_W), n_rows)
    nq, nk = D_HEADS * HEAD_DIM, D_KV_HEADS * HEAD_DIM
    qkv = h @ wqkv
    q = rms_norm(qkv[..., :nq].reshape(B, L, D_HEADS, HEAD_DIM), gq)
    k = rms_norm(qkv[..., nq:nq + nk].reshape(B, L, D_KV_HEADS, HEAD_DIM), gk)
    v = qkv[..., nq + nk:].reshape(B, L, D_KV_HEADS, HEAD_DIM)
    half = HEAD_DIM // 2

    def axial(t):
        return jnp.concatenate([rope(t[..., :half], rows), rope(t[..., half:], cols)], axis=-1)

    q = axial(q).reshape(B, L, D_KV_HEADS, G, HEAD_DIM)
    k = axial(k)
    o = dense_blocked_attention(q, k, v)
    return o.reshape(B, L, nq).astype(h.dtype) @ wo


def run_trunk(x, p):
    for i in range(DEPTH):
        m, j = i % N_MIXERS, i // N_MIXERS
        x = x + RESIDUAL_HALF * swiglu(rms_norm(x, p['norm_ffn1'][i]), p['ffn1_wg'][i], p['ffn1_wu'][i], p['ffn1_wd'][i])
        h = rms_norm(x, p['norm_mix'][i])
        if m == 0:
            y = mixer_a(h, p['a_wqkv'][j], p['a_gq'][j], p['a_gk'][j], p['a_sink'][j], p['a_wo'][j])
        elif m == 1:
            y = mixer_b(h, p['b_wdq'][j], p['b_gcq'][j], p['b_wuq'][j], p['b_wdkv'][j], p['b_gckv'][j],
                        p['b_wukv'][j], p['b_gq'][j], p['b_gk'][j], p['b_wo'][j])
        elif m == 2:
            y = mixer_c(h, p['c_wqkv'][j], p['c_gq'][j], p['c_gk'][j], p['c_wo'][j])
        else:
            y = mixer_d(h, p['d_wqkv'][j], p['d_gq'][j], p['d_gk'][j], p['d_wo'][j])
        x = x + y
        x = x + RESIDUAL_HALF * swiglu(rms_norm(x, p['norm_ffn2'][i]), p['ffn2_wg'][i], p['ffn2_wu'][i], p['ffn2_wd'][i])
    return x


def setup_inputs(seed: int = 0) -> dict:
    key = jax.random.key(seed)
    counter = [0]

    def nxt():
        counter[0] += 1
        return jax.random.fold_in(key, counter[0])

    def dense(shape):
        return jax.random.normal(nxt(), shape, jnp.float32) * (shape[-2] ** -0.5)

    def gain(shape):
        return 1.0 + 0.02 * jax.random.normal(nxt(), shape, jnp.float32)

    nA, nB, nC, nD = [n_layers_of(m) for m in range(N_MIXERS)]
    hd = HEAD_DIM
    return {
        "x_prompt": jax.random.normal(nxt(), (BATCH, SEQ, D_MODEL), jnp.float32),
        "x_sample": jax.random.normal(nxt(), (DEC_BATCH, DEC_SEQ, D_MODEL), jnp.float32),
        "norm_ffn1": gain((DEPTH, D_MODEL)),
        "ffn1_wg": dense((DEPTH, D_MODEL, D_FF)),
        "ffn1_wu": dense((DEPTH, D_MODEL, D_FF)),
        "ffn1_wd": dense((DEPTH, D_FF, D_MODEL)),
        "norm_mix": gain((DEPTH, D_MODEL)),
        "norm_ffn2": gain((DEPTH, D_MODEL)),
        "ffn2_wg": dense((DEPTH, D_MODEL, D_FF)),
        "ffn2_wu": dense((DEPTH, D_MODEL, D_FF)),
        "ffn2_wd": dense((DEPTH, D_FF, D_MODEL)),
        "a_wqkv": dense((nA, D_MODEL, (A_HEADS + 2 * A_KV_HEADS) * hd)),
        "a_gq": gain((nA, hd)),
        "a_gk": gain((nA, hd)),
        "a_sink": 0.5 * jax.random.normal(nxt(), (nA, A_HEADS), jnp.float32),
        "a_wo": dense((nA, A_HEADS * hd, D_MODEL)),
        "b_wdq": dense((nB, D_MODEL, B_Q_LORA)),
        "b_gcq": gain((nB, B_Q_LORA)),
        "b_wuq": dense((nB, B_Q_LORA, B_HEADS * (B_NOPE + B_ROPE))),
        "b_wdkv": dense((nB, D_MODEL, B_KV_LORA + B_ROPE)),
        "b_gckv": gain((nB, B_KV_LORA)),
        "b_wukv": dense((nB, B_KV_LORA, B_HEADS * (B_NOPE + B_V))),
        "b_gq": gain((nB, B_NOPE + B_ROPE)),
        "b_gk": gain((nB, B_NOPE + B_ROPE)),
        "b_wo": dense((nB, B_HEADS * B_V, D_MODEL)),
        "c_wqkv": dense((nC, D_MODEL, len(C_PATTERNS) * 3 * C_HEADS * hd)),
        "c_gq": gain((nC, hd)),
        "c_gk": gain((nC, hd)),
        "c_wo": dense((nC, C_HEADS * hd, D_MODEL)),
        "d_wqkv": dense((nD, D_MODEL, (D_HEADS + 2 * D_KV_HEADS) * hd)),
        "d_gq": gain((nD, hd)),
        "d_gk": gain((nD, hd)),
        "d_wo": dense((nD, D_HEADS * hd, D_MODEL)),
    }


def reference(x_prompt, x_sample, norm_ffn1, ffn1_wg, ffn1_wu, ffn1_wd, norm_mix, norm_ffn2,
              ffn2_wg, ffn2_wu, ffn2_wd, a_wqkv, a_gq, a_gk, a_sink, a_wo, b_wdq, b_gcq, b_wuq,
              b_wdkv, b_gckv, b_wukv, b_gq, b_gk, b_wo, c_wqkv, c_gq, c_gk, c_wo, d_wqkv, d_gq,
              d_gk, d_wo):
    p = dict(norm_ffn1=norm_ffn1, ffn1_wg=ffn1_wg, ffn1_wu=ffn1_wu, ffn1_wd=ffn1_wd,
             norm_mix=norm_mix, norm_ffn2=norm_ffn2, ffn2_wg=ffn2_wg, ffn2_wu=ffn2_wu,
             ffn2_wd=ffn2_wd, a_wqkv=a_wqkv, a_gq=a_gq, a_gk=a_gk, a_sink=a_sink, a_wo=a_wo,
             b_wdq=b_wdq, b_gcq=b_gcq, b_wuq=b_wuq, b_wdkv=b_wdkv, b_gckv=b_gckv,
             b_wukv=b_wukv, b_gq=b_gq, b_gk=b_gk, b_wo=b_wo, c_wqkv=c_wqkv, c_gq=c_gq,
             c_gk=c_gk, c_wo=c_wo, d_wqkv=d_wqkv, d_gq=d_gq, d_gk=d_gk, d_wo=d_wo)
    y_prompt = run_trunk(x_prompt, p)
    y_sample = run_trunk(x_sample, p)
    return (y_prompt, y_sample)
```

```python
import functools
import math

import numpy as np
import jax
import jax.numpy as jnp
from jax import lax
from jax.experimental import pallas as pl
from jax.experimental.pallas import tpu as pltpu

F32 = jnp.float32
BF16 = jnp.bfloat16

D_MODEL = 1024
DEPTH = 4
HEAD_DIM = 64
D_FF = 2816
NORM_EPS = 1e-6
NEG_INF = -1e30
GRID_W = 64
ROPE_THETA = 10000.0

A_HEADS, A_KV_HEADS, A_HALF_WINDOW = 16, 4, 128
B_HEADS, B_Q_LORA, B_KV_LORA, B_NOPE, B_ROPE, B_V = 16, 512, 256, 64, 32, 64
B_QK = B_NOPE + B_ROPE
B_PAD = 128
C_HEADS = 8
C_PATTERNS = ((128, 1), (512, 4), (2048, 16))
C_STEPS = 64
D_HEADS, D_KV_HEADS = 16, 4

FF_CHUNK = 256
N_FF_CHUNKS = D_FF // FF_CHUNK
TOKEN_TILE = 512
VMEM_LIMIT = 56 * 1024 * 1024


def _alibi_slopes(n):
    return 2.0 ** (-8.0 * np.arange(1, n + 1) / n)


def _params(sem, vmem=VMEM_LIMIT):
    return pltpu.CompilerParams(dimension_semantics=sem, vmem_limit_bytes=vmem)


def _const_spec(shape):
    nd = len(shape)
    return pl.BlockSpec(tuple(shape), lambda *_: (0,) * nd, pipeline_mode=pl.Buffered(1))


def _seq_bounds(row0, p_len, s_len):
    in_prompt = row0 < p_len
    b = lax.shift_right_logical(jnp.maximum(row0 - p_len, 0), int(math.log2(s_len)))
    lo = jnp.where(in_prompt, 0, p_len + b * s_len)
    hi = jnp.where(in_prompt, p_len, lo + s_len)
    return lo, hi


def _rms(x, g):
    return x * lax.rsqrt(jnp.mean(x * x, axis=-1, keepdims=True) + NORM_EPS) * g


def _dot(a, b):
    return jnp.dot(a, b, preferred_element_type=F32)


def _dot_nt(a, b):
    return lax.dot_general(a, b, (((1,), (1,)), ((), ())), preferred_element_type=F32)


def _ffn_kernel(*refs, mode):
    if mode == "plain":
        x_ref, gn_ref, wg_ref, wu_ref, wd_ref, out_ref, acc_ref = refs
        x = x_ref[...]
    elif mode == "proj":
        x_ref, o_ref, wo_ref, gn_ref, wg_ref, wu_ref, wd_ref, out_ref, acc_ref = refs
        x = x_ref[...] + _dot(o_ref[...], wo_ref[...])
    else:
        (x_ref, o0, o1, o2, l0, l1, l2, wo_ref, gn_ref, wg_ref, wu_ref, wd_ref,
         out_ref, acc_ref) = refs
        la, lb, lc = l0[...], l1[...], l2[...]
        m = jnp.maximum(jnp.maximum(la, lb), lc)
        ea, eb, ec = jnp.exp(la - m), jnp.exp(lb - m), jnp.exp(lc - m)
        o = (ea * o0[...] + eb * o1[...] + ec * o2[...]) / (ea + eb + ec)
        x = x_ref[...] + _dot(o.astype(BF16), wo_ref[...])

    h = _rms(x, gn_ref[...]).astype(BF16)
    acc_ref[...] = jnp.zeros_like(acc_ref)

    def chunk(c, carry):
        g = _dot(h, wg_ref[c])
        u = _dot(h, wu_ref[c])
        a = (g * jax.nn.sigmoid(g) * u).astype(BF16)
        acc_ref[...] += _dot(a, wd_ref[c])
        return carry

    lax.fori_loop(0, N_FF_CHUNKS, chunk, 0)
    out_ref[...] = x + 0.5 * acc_ref[...]


def _ffn_weights(wg, wu, wd):
    wg3 = wg.astype(BF16).reshape(D_MODEL, N_FF_CHUNKS, FF_CHUNK).transpose(1, 0, 2)
    wu3 = wu.astype(BF16).reshape(D_MODEL, N_FF_CHUNKS, FF_CHUNK).transpose(1, 0, 2)
    wd3 = wd.astype(BF16).reshape(N_FF_CHUNKS, FF_CHUNK, D_MODEL)
    return wg3, wu3, wd3


def _ffn_call(x, gn, wg, wu, wd, pre=None):
    t_rows = x.shape[0]
    tm = TOKEN_TILE
    wg3, wu3, wd3 = _ffn_weights(wg, wu, wd)
    row = lambda w: pl.BlockSpec((tm, w), lambda i: (i, 0))
    ffn_args = [gn.reshape(1, D_MODEL), wg3, wu3, wd3]
    ffn_specs = [_const_spec((1, D_MODEL)), _const_spec(wg3.shape), _const_spec(wu3.shape),
                 _const_spec(wd3.shape)]
    if pre is None:
        mode, args, specs = "plain", [x], [row(D_MODEL)]
    elif len(pre) == 2:
        o, wo = pre
        wo = wo.astype(BF16)
        mode = "proj"
        args = [x, o, wo]
        specs = [row(D_MODEL), row(o.shape[1]), _const_spec(wo.shape)]
    else:
        outs, lses, wo = pre
        wo = wo.astype(BF16)
        mode = "merge"
        args = [x, *outs, *lses, wo]
        specs = [row(D_MODEL)] + [row(outs[0].shape[1])] * 6 + [_const_spec(wo.shape)]
    return pl.pallas_call(
        functools.partial(_ffn_kernel, mode=mode),
        grid=(t_rows // tm,),
        in_specs=specs + ffn_specs,
        out_specs=row(D_MODEL),
        out_shape=jax.ShapeDtypeStruct((t_rows, D_MODEL), F32),
        scratch_shapes=[pltpu.VMEM((tm, D_MODEL), F32)],
        compiler_params=_params(("parallel",)),
        name="ffn_" + mode,
    )(*args, *ffn_args)


def _head_norm(t, width):
    return lax.rsqrt(jnp.sum(t * t, axis=-1, keepdims=True) * (1.0 / width) + NORM_EPS)


def _proj_a_kernel(x_ref, gn_ref, w_ref, gq_ref, gk_ref, q_ref, k_ref, v_ref):
    hd = HEAD_DIM
    h = _rms(x_ref[...], gn_ref[...]).astype(BF16)
    y = _dot(h, w_ref[...])
    gq, gk = gq_ref[...], gk_ref[...]
    for n in range(A_HEADS):
        t = y[:, n * hd:(n + 1) * hd]
        q_ref[n] = (t * _head_norm(t, hd) * gq).astype(BF16)
    base = A_HEADS * hd
    for n in range(A_KV_HEADS):
        t = y[:, base + n * hd: base + (n + 1) * hd]
        k_ref[n] = (t * _head_norm(t, hd) * gk).astype(BF16)
    base += A_KV_HEADS * hd
    for n in range(A_KV_HEADS):
        v_ref[n] = y[:, base + n * hd: base + (n + 1) * hd].astype(BF16)


def _proj_a_call(x, gn, wqkv, gq, gk):
    t_rows = x.shape[0]
    tm, hd = TOKEN_TILE, HEAD_DIM
    w = wqkv.astype(BF16)
    head_major = lambda n: pl.BlockSpec((n, tm, hd), lambda i: (0, i, 0))
    return pl.pallas_call(
        _proj_a_kernel,
        grid=(t_rows // tm,),
        in_specs=[pl.BlockSpec((tm, D_MODEL), lambda i: (i, 0)), _const_spec((1, D_MODEL)),
                  _const_spec(w.shape), _const_spec((1, hd)), _const_spec((1, hd))],
        out_specs=[head_major(A_HEADS), head_major(A_KV_HEADS), head_major(A_KV_HEADS)],
        out_shape=[jax.ShapeDtypeStruct((A_HEADS, t_rows, hd), BF16),
                   jax.ShapeDtypeStruct((A_KV_HEADS, t_rows, hd), BF16),
                   jax.ShapeDtypeStruct((A_KV_HEADS, t_rows, hd), BF16)],
        compiler_params=_params(("parallel",)),
        name="proj_a",
    )(x, gn.reshape(1, D_MODEL), w, (gq * hd ** -0.5).reshape(1, hd), gk.reshape(1, hd))


def _proj_d_kernel(x_ref, gn_ref, w_ref, cq_ref, sq_ref, ck_ref, sk_ref, q_ref, k_ref, v_ref):
    hd = HEAD_DIM
    nq, nk = D_HEADS * hd, D_KV_HEADS * hd
    h = _rms(x_ref[...], gn_ref[...]).astype(BF16)
    y = _dot(h, w_ref[...])
    cq, sq, ck, sk = cq_ref[...], sq_ref[...], ck_ref[...], sk_ref[...]
    swapped = nq + 2 * nk
    for n in range(D_HEADS):
        t = y[:, n * hd:(n + 1) * hd]
        ts = y[:, swapped + n * hd: swapped + (n + 1) * hd]
        q_ref[n] = ((t * cq + ts * sq) * _head_norm(t, hd)).astype(BF16)
    for n in range(D_KV_HEADS):
        t = y[:, nq + n * hd: nq + (n + 1) * hd]
        ts = y[:, swapped + nq + n * hd: swapped + nq + (n + 1) * hd]
        k_ref[n] = ((t * ck + ts * sk) * _head_norm(t, hd)).astype(BF16)
    for n in range(D_KV_HEADS):
        v_ref[n] = y[:, nq + nk + n * hd: nq + nk + (n + 1) * hd].astype(BF16)


def _swap_halves_cols(w, group):
    k, n = w.shape
    return w.reshape(k, n // group, 2, group // 2)[:, :, ::-1, :].reshape(k, n)


def _rope_tables(pos, gain, scale):
    half = gain.shape[-1] // 2
    inv = ROPE_THETA ** (-jnp.arange(half, dtype=F32) / half)
    ang = pos.astype(F32)[:, None] * inv[None, :]
    cos, sin = jnp.cos(ang), jnp.sin(ang)
    g_sw = jnp.concatenate([gain[half:], gain[:half]])
    c = jnp.concatenate([cos, cos], axis=-1) * gain[None, :] * scale
    s = jnp.concatenate([-sin, sin], axis=-1) * g_sw[None, :] * scale
    return c, s


def _proj_d_call(x, gn, wqkv, gq, gk, pos):
    t_rows = x.shape[0]
    tm, hd = TOKEN_TILE, HEAD_DIM
    nq, nk = D_HEADS * hd, D_KV_HEADS * hd
    half = hd // 2
    w = jnp.concatenate([wqkv, _swap_halves_cols(wqkv[:, :nq + nk], half)], axis=1).astype(BF16)
    rows, cols = pos // GRID_W, pos % GRID_W

    def tables(g, scale):
        cr, sr = _rope_tables(rows, g[:half], scale)
        cc, sc = _rope_tables(cols, g[half:], scale)
        return jnp.concatenate([cr, cc], axis=-1), jnp.concatenate([sr, sc], axis=-1)

    cq, sq = tables(gq, hd ** -0.5)
    ck, sk = tables(gk, 1.0)
    head_major = lambda n: pl.BlockSpec((n, tm, hd), lambda i: (0, i, 0))
    tab = pl.BlockSpec((tm, hd), lambda i: (i, 0))
    return pl.pallas_call(
        _proj_d_kernel,
        grid=(t_rows // tm,),
        in_specs=[pl.BlockSpec((tm, D_MODEL), lambda i: (i, 0)), _const_spec((1, D_MODEL)),
                  _const_spec(w.shape), tab, tab, tab, tab],
        out_specs=[head_major(D_HEADS), head_major(D_KV_HEADS), head_major(D_KV_HEADS)],
        out_shape=[jax.ShapeDtypeStruct((D_HEADS, t_rows, hd), BF16),
                   jax.ShapeDtypeStruct((D_KV_HEADS, t_rows, hd), BF16),
                   jax.ShapeDtypeStruct((D_KV_HEADS, t_rows, hd), BF16)],
        compiler_params=_params(("parallel",)),
        name="proj_d",
    )(x, gn.reshape(1, D_MODEL), w, cq, sq, ck, sk)


def _proj_b_kernel(x_ref, gn_ref, w1_ref, gcq_ref, gckv_ref, wq_ref, wqs_ref, wk_ref, wv_ref,
                   cq_ref, sq_ref, ck_ref, sk_ref, q_ref, k_ref, v_ref):
    h = _rms(x_ref[...], gn_ref[...]).astype(BF16)
    y1 = _dot(h, w1_ref[...])
    c_q = _rms(y1[:, :B_Q_LORA], gcq_ref[...]).astype(BF16)
    c_kv = _rms(y1[:, B_Q_LORA:B_Q_LORA + B_KV_LORA], gckv_ref[...]).astype(BF16)
    off = B_Q_LORA + B_KV_LORA
    kr = y1[:, off:off + B_PAD]
    krs = y1[:, off + B_PAD:off + 2 * B_PAD]
    q = _dot(c_q, wq_ref[...])
    qs = _dot(c_q, wqs_ref[...])
    kn = _dot(c_kv, wk_ref[...])
    v = _dot(c_kv, wv_ref[...])
    cq, sq, ck, sk = cq_ref[...], sq_ref[...], ck_ref[...], sk_ref[...]
    krs_s = krs * sk
    for n in range(B_HEADS):
        t = q[:, n * B_PAD:(n + 1) * B_PAD]
        ts = qs[:, n * B_PAD:(n + 1) * B_PAD]
        q_ref[n] = ((t * cq + ts * sq) * _head_norm(t, B_QK)).astype(BF16)
        t = kn[:, n * B_PAD:(n + 1) * B_PAD] + kr
        k_ref[n] = ((t * ck + krs_s) * _head_norm(t, B_QK)).astype(BF16)
        v_ref[n] = v[:, n * B_V:(n + 1) * B_V].astype(BF16)


def _pad_heads(w, lo, width):
    k, nh, _ = w.shape
    out = jnp.zeros((k, nh, B_PAD), w.dtype).at[:, :, lo:lo + width].set(w)
    return out.reshape(k, nh * B_PAD)


def _proj_b_call(x, gn, wdq, gcq, wuq, wdkv, gckv, wukv, gq, gk, pos):
    t_rows = x.shape[0]
    tm = TOKEN_TILE
    wuq3 = wuq.reshape(B_Q_LORA, B_HEADS, B_QK)
    wq = _pad_heads(wuq3, 0, B_QK).astype(BF16)
    wuq_rope_sw = _swap_halves_cols(wuq3[:, :, B_NOPE:].reshape(B_Q_LORA, -1), B_ROPE)
    wqs = _pad_heads(wuq_rope_sw.reshape(B_Q_LORA, B_HEADS, B_ROPE), B_NOPE, B_ROPE).astype(BF16)
    wukv3 = wukv.reshape(B_KV_LORA, B_HEADS, B_NOPE + B_V)
    wk = _pad_heads(wukv3[:, :, :B_NOPE], 0, B_NOPE).astype(BF16)
    wv = wukv3[:, :, B_NOPE:].reshape(B_KV_LORA, B_HEADS * B_V).astype(BF16)
    w_kr = wdkv[:, B_KV_LORA:]
    kr_pad = _pad_heads(w_kr[:, None, :], B_NOPE, B_ROPE)
    krs_pad = _pad_heads(_swap_halves_cols(w_kr, B_ROPE)[:, None, :], B_NOPE, B_ROPE)
    w1 = jnp.concatenate([wdq, wdkv[:, :B_KV_LORA], kr_pad, krs_pad], axis=1).astype(BF16)

    def tables(g, scale):
        c, s = _rope_tables(pos, g[B_NOPE:], scale)
        zeros = jnp.zeros((t_rows, B_PAD - B_QK), F32)
        c_full = jnp.concatenate(
            [jnp.broadcast_to(g[None, :B_NOPE] * scale, (t_rows, B_NOPE)), c, zeros], axis=-1)
        s_full = jnp.concatenate([jnp.zeros((t_rows, B_NOPE), F32), s, zeros], axis=-1)
        return c_full, s_full

    cq, sq = tables(gq, B_QK ** -0.5)
    ck, sk = tables(gk, 1.0)
    tab = pl.BlockSpec((tm, B_PAD), lambda i: (i, 0))
    head_major = lambda w_: pl.BlockSpec((B_HEADS, tm, w_), lambda i: (0, i, 0))
    return pl.pallas_call(
        _proj_b_kernel,
        grid=(t_rows // tm,),
        in_specs=[pl.BlockSpec((tm, D_MODEL), lambda i: (i, 0)), _const_spec((1, D_MODEL)),
                  _const_spec(w1.shape), _const_spec((1, B_Q_LORA)), _const_spec((1, B_KV_LORA)),
                  _const_spec(wq.shape), _const_spec(wqs.shape), _const_spec(wk.shape),
                  _const_spec(wv.shape), tab, tab, tab, tab],
        out_specs=[head_major(B_PAD), head_major(B_PAD), head_major(B_V)],
        out_shape=[jax.ShapeDtypeStruct((B_HEADS, t_rows, B_PAD), BF16),
                   jax.ShapeDtypeStruct((B_HEADS, t_rows, B_PAD), BF16),
                   jax.ShapeDtypeStruct((B_HEADS, t_rows, B_V), BF16)],
        compiler_params=_params(("parallel",)),
        name="proj_b",
    )(x, gn.reshape(1, D_MODEL), w1, gcq.reshape(1, -1), gckv.reshape(1, -1), wq, wqs, wk, wv,
      cq, sq, ck, sk)


def _proj_c_kernel(x_ref, gn_ref, w_ref, gq_ref, gk_ref, q_ref, k_ref, v_ref):
    hd = HEAD_DIM
    width = C_HEADS * hd
    h = _rms(x_ref[...], gn_ref[...]).astype(BF16)
    y = _dot(h, w_ref[...])
    gains = (gq_ref[...], gk_ref[...])
    outs = (q_ref, k_ref)
    for g in range(len(C_PATTERNS)):
        for j in range(2):
            base = (g * 3 + j) * width
            heads = []
            for n in range(C_HEADS):
                t = y[:, base + n * hd: base + (n + 1) * hd]
                heads.append(t * _head_norm(t, hd) * gains[j])
            outs[j][g] = jnp.concatenate(heads, axis=-1).astype(BF16)
        base = (g * 3 + 2) * width
        v_ref[g] = y[:, base: base + width].astype(BF16)


def _proj_c_call(x, gn, wqkv, gq, gk):
    t_rows = x.shape[0]
    tm, hd = TOKEN_TILE // 2, HEAD_DIM
    ng, width = len(C_PATTERNS), C_HEADS * hd
    w = wqkv.astype(BF16)
    out = pl.BlockSpec((ng, tm, width), lambda i: (0, i, 0))
    shape = jax.ShapeDtypeStruct((ng, t_rows, width), BF16)
    return pl.pallas_call(
        _proj_c_kernel,
        grid=(t_rows // tm,),
        in_specs=[pl.BlockSpec((tm, D_MODEL), lambda i: (i, 0)), _const_spec((1, D_MODEL)),
                  _const_spec(w.shape), _const_spec((1, hd)), _const_spec((1, hd))],
        out_specs=[out, out, out],
        out_shape=[shape, shape, shape],
        compiler_params=_params(("parallel",)),
        name="proj_c",
    )(x, gn.reshape(1, D_MODEL), w, (gq * hd ** -0.5).reshape(1, hd), gk.reshape(1, hd))


def _band_geometry(row0, lo, hi, tq, w):
    nk = tq + 2 * w
    r = lax.broadcasted_iota(jnp.int32, (tq, nk), 0)
    c = lax.broadcasted_iota(jnp.int32, (tq, nk), 1)
    dist = jnp.abs(r + w - c)
    jpos = row0 - w + lax.broadcasted_iota(jnp.int32, (1, nk), 1)
    outside = jnp.where(jpos < lo, w + 1, 0) + jnp.where(jpos >= hi, w + 1, 0)
    valid = (dist + outside) <= w
    return dist.astype(F32), valid


def _band_head(q, kk, vv, bias, valid, sink):
    s = jnp.where(valid, _dot_nt(q, kk) + bias, NEG_INF)
    m = jnp.max(s, axis=-1, keepdims=True)
    if sink is not None:
        m = jnp.maximum(m, sink)
    p = jnp.exp(s - m)
    den = jnp.sum(p, axis=-1, keepdims=True)
    if sink is not None:
        den = den + jnp.exp(sink - m)
    o = _dot(p.astype(BF16), vv) / den
    return o, m + jnp.log(den)


def _attn_a_kernel(slope_ref, sink_ref, q_ref, kp_ref, kc_ref, kn_ref, vp_ref, vc_ref, vn_ref,
                   o_ref, *, tq, p_len, s_len):
    w = A_HALF_WINDOW
    group = A_HEADS // A_KV_HEADS
    kvh = pl.program_id(0)
    row0 = pl.program_id(1) * tq
    lo, hi = _seq_bounds(row0, p_len, s_len)
    dist, valid = _band_geometry(row0, lo, hi, tq, w)
    kk = jnp.concatenate([kp_ref[0], kc_ref[0], kn_ref[0]], axis=0)
    vv = jnp.concatenate([vp_ref[0], vc_ref[0], vn_ref[0]], axis=0)
    outs = []
    for g in range(group):
        n = kvh * group + g
        o, _ = _band_head(q_ref[g], kk, vv, dist * (-slope_ref[n]), valid, sink_ref[n])
        outs.append(o)
    o_ref[...] = jnp.concatenate(outs, axis=-1).astype(BF16)


def _attn_a_call(q, k, v, sink, geo):
    p_len, s_len, t_rows = geo
    tq, w, hd = 512, A_HALF_WINDOW, HEAD_DIM
    group = A_HEADS // A_KV_HEADS
    per = tq // w
    last = t_rows // w - 1
    prev = pl.BlockSpec((1, w, hd), lambda h, i: (h, jnp.maximum(i * per - 1, 0), 0))
    cur = pl.BlockSpec((1, tq, hd), lambda h, i: (h, i, 0))
    nxt = pl.BlockSpec((1, w, hd), lambda h, i: (h, jnp.minimum((i + 1) * per, last), 0))
    smem = pl.BlockSpec(memory_space=pltpu.SMEM)
    slopes = jnp.asarray(_alibi_slopes(A_HEADS), F32)
    return pl.pallas_call(
        functools.partial(_attn_a_kernel, tq=tq, p_len=p_len, s_len=s_len),
        grid=(A_KV_HEADS, t_rows // tq),
        in_specs=[smem, smem, pl.BlockSpec((group, tq, hd), lambda h, i: (h, i, 0)),
                  prev, cur, nxt, prev, cur, nxt],
        out_specs=pl.BlockSpec((tq, group * hd), lambda h, i: (i, h)),
        out_shape=jax.ShapeDtypeStruct((t_rows, A_HEADS * hd), BF16),
        compiler_params=_params(("parallel", "parallel")),
        name="attn_a",
    )(slopes, sink.astype(F32), q, k, k, k, v, v, v)


def _attn_c_kernel(q_ref, kp_ref, kc_ref, kn_ref, vp_ref, vc_ref, vn_ref, o_ref, lse_ref,
                   *, tq, dil, p_len, s_len):
    w, hd = C_STEPS, HEAD_DIM
    row0 = pl.program_id(0) * tq
    lo, hi = _seq_bounds(row0, p_len // dil, s_len // dil)
    dist, valid = _band_geometry(row0, lo, hi, tq, w)
    slopes = _alibi_slopes(C_HEADS)
    outs, lses = [], []
    for n in range(C_HEADS):
        hs = slice(n * hd, (n + 1) * hd)
        kk = jnp.concatenate([kp_ref[:, hs], kc_ref[:, hs], kn_ref[:, hs]], axis=0)
        vv = jnp.concatenate([vp_ref[:, hs], vc_ref[:, hs], vn_ref[:, hs]], axis=0)
        o, lse = _band_head(q_ref[:, hs], kk, vv, dist * float(-slopes[n] * dil), valid, None)
        outs.append(o)
        lses.append(jnp.broadcast_to(lse, (tq, hd)))
    o_ref[...] = jnp.concatenate(outs, axis=-1)
    lse_ref[...] = jnp.concatenate(lses, axis=-1)


def _attn_c_call(q, k, v, dil, geo):
    p_len, s_len, t_rows = geo
    w, width = C_STEPS, C_HEADS * HEAD_DIM
    rows = t_rows // dil
    tq = min(512, s_len // dil)
    per = tq // w
    last = rows // w - 1
    view = lambda a: a.reshape(rows, dil * width)
    prev = pl.BlockSpec((w, width), lambda i, r: (jnp.maximum(i * per - 1, 0), r))
    cur = pl.BlockSpec((tq, width), lambda i, r: (i, r))
    nxt = pl.BlockSpec((w, width), lambda i, r: (jnp.minimum((i + 1) * per, last), r))
    shape = jax.ShapeDtypeStruct((rows, dil * width), F32)
    qv, kv, vv = view(q), view(k), view(v)
    o, lse = pl.pallas_call(
        functools.partial(_attn_c_kernel, tq=tq, dil=dil, p_len=p_len, s_len=s_len),
        grid=(rows // tq, dil),
        in_specs=[cur, prev, cur, nxt, prev, cur, nxt],
        out_specs=[cur, cur],
        out_shape=[shape, shape],
        compiler_params=_params(("parallel", "parallel")),
        name="attn_c_d%d" % dil,
    )(qv, kv, kv, kv, vv, vv, vv)
    return o.reshape(t_rows, width), lse.reshape(t_rows, width)


def _attn_dense_kernel(q_ref, k_ref, v_ref, o_ref, *, nq, tq, tk, p_len, s_len):
    dq = q_ref.shape[-1]
    dv = v_ref.shape[-1]
    rows = nq * tq
    row0 = pl.program_id(1) * tq
    lo, hi = _seq_bounds(row0, p_len, s_len)
    n_chunks = lax.shift_right_logical(hi - lo, int(math.log2(tk)))
    q = q_ref[...].reshape(rows, dq)

    def body(j, carry):
        m, l, acc = carry
        start = pl.multiple_of(lo + j * tk, tk)
        kk = k_ref[0, pl.ds(start, tk), :]
        vv = v_ref[0, pl.ds(start, tk), :]
        s = _dot_nt(q, kk)
        m_new = jnp.maximum(m, jnp.max(s, axis=-1, keepdims=True))
        alpha = jnp.exp(m - m_new)
        p = jnp.exp(s - m_new)
        l = alpha * l + jnp.sum(p, axis=-1, keepdims=True)
        acc = alpha * acc + _dot(p.astype(BF16), vv)
        return m_new, l, acc

    init = (jnp.full((rows, 1), NEG_INF, F32), jnp.zeros((rows, 1), F32),
            jnp.zeros((rows, dv), F32))
    _, l, acc = lax.fori_loop(0, n_chunks, body, init)
    o_ref[...] = (acc / l).reshape(nq, tq, dv).astype(BF16)


def _attn_dense_call(q, k, v, geo, tq, tk=512):
    p_len, s_len, t_rows = geo
    hq, _, dq = q.shape
    hk, _, dv = v.shape
    nq = hq // hk
    o = pl.pallas_call(
        functools.partial(_attn_dense_kernel, nq=nq, tq=tq, tk=tk, p_len=p_len, s_len=s_len),
        grid=(hk, t_rows // tq),
        in_specs=[pl.BlockSpec((nq, tq, dq), lambda h, i: (h, i, 0)),
                  pl.BlockSpec((1, t_rows, dq), lambda h, i: (h, 0, 0), pipeline_mode=pl.Buffered(1)),
                  pl.BlockSpec((1, t_rows, dv), lambda h, i: (h, 0, 0), pipeline_mode=pl.Buffered(1))],
        out_specs=pl.BlockSpec((nq, tq, dv), lambda h, i: (h, i, 0)),
        out_shape=jax.ShapeDtypeStruct((hq, t_rows, dv), BF16),
        compiler_params=_params(("parallel", "parallel")),
        name="attn_dense_%d" % dq,
    )(q, k, v)
    return o.transpose(1, 0, 2).reshape(t_rows, hq * dv)


def _trunk(x, p, geo, pos):
    for i in range(DEPTH):
        mixer, j = i % 4, i // 4
        x = _ffn_call(x, p["norm_ffn1"][i], p["ffn1_wg"][i], p["ffn1_wu"][i], p["ffn1_wd"][i])
        gn = p["norm_mix"][i]
        if mixer == 0:
            q, k, v = _proj_a_call(x, gn, p["a_wqkv"][j], p["a_gq"][j], p["a_gk"][j])
            pre = (_attn_a_call(q, k, v, p["a_sink"][j], geo), p["a_wo"][j])
        elif mixer == 1:
            q, k, v = _proj_b_call(x, gn, p["b_wdq"][j], p["b_gcq"][j], p["b_wuq"][j], p["b_wdkv"][j],
                                   p["b_gckv"][j], p["b_wukv"][j], p["b_gq"][j], p["b_gk"][j], pos)
            pre = (_attn_dense_call(q, k, v, geo, tq=512), p["b_wo"][j])
        elif mixer == 2:
            q, k, v = _proj_c_call(x, gn, p["c_wqkv"][j], p["c_gq"][j], p["c_gk"][j])
            res = [_attn_c_call(q[g], k[g], v[g], dil, geo) for g, (_, dil) in enumerate(C_PATTERNS)]
            pre = ([r[0] for r in res], [r[1] for r in res], p["c_wo"][j])
        else:
            q, k, v = _proj_d_call(x, gn, p["d_wqkv"][j], p["d_gq"][j], p["d_gk"][j], pos)
            pre = (_attn_dense_call(q, k, v, geo, tq=256), p["d_wo"][j])
        x = _ffn_call(x, p["norm_ffn2"][i], p["ffn2_wg"][i], p["ffn2_wu"][i], p["ffn2_wd"][i], pre=pre)
    return x


def kernel(x_prompt, x_sample, norm_ffn1, ffn1_wg, ffn1_wu, ffn1_wd, norm_mix, norm_ffn2, ffn2_wg, ffn2_wu, ffn2_wd, a_wqkv, a_gq, a_gk, a_sink, a_wo, b_wdq, b_gcq, b_wuq, b_wdkv, b_gckv, b_wukv, b_gq, b_gk, b_wo, c_wqkv, c_gq, c_gk, c_wo, d_wqkv, d_gq, d_gk, d_wo):
    p = dict(norm_ffn1=norm_ffn1, ffn1_wg=ffn1_wg, ffn1_wu=ffn1_wu, ffn1_wd=ffn1_wd,
             norm_mix=norm_mix, norm_ffn2=norm_ffn2, ffn2_wg=ffn2_wg, ffn2_wu=ffn2_wu,
             ffn2_wd=ffn2_wd, a_wqkv=a_wqkv, a_gq=a_gq, a_gk=a_gk, a_sink=a_sink, a_wo=a_wo,
             b_wdq=b_wdq, b_gcq=b_gcq, b_wuq=b_wuq, b_wdkv=b_wdkv, b_gckv=b_gckv,
             b_wukv=b_wukv, b_gq=b_gq, b_gk=b_gk, b_wo=b_wo, c_wqkv=c_wqkv, c_gq=c_gq,
             c_gk=c_gk, c_wo=c_wo, d_wqkv=d_wqkv, d_gq=d_gq, d_gk=d_gk, d_wo=d_wo)
    n_p, p_seq, _ = x_prompt.shape
    n_s, s_len, _ = x_sample.shape
    assert n_p == 1 and s_len & (s_len - 1) == 0 and p_seq % s_len == 0
    p_len = n_p * p_seq
    t_rows = p_len + n_s * s_len
    geo = (p_len, s_len, t_rows)
    pos = jnp.concatenate([jnp.arange(p_len, dtype=jnp.int32),
                           jnp.tile(jnp.arange(s_len, dtype=jnp.int32), n_s)])
    x = jnp.concatenate([x_prompt.reshape(p_len, D_MODEL), x_sample.reshape(n_s * s_len, D_MODEL)])
    y = _trunk(x, p, geo, pos)
    return y[:p_len].reshape(x_prompt.shape), y[p_len:].reshape(x_sample.shape)
```

```python
import functools
import math

import numpy as np
import jax
import jax.numpy as jnp
from jax import lax
from jax.experimental import pallas as pl
from jax.experimental.pallas import tpu as pltpu

F32 = jnp.float32
BF16 = jnp.bfloat16

D_MODEL = 1024
DEPTH = 4
HEAD_DIM = 64
D_FF = 2816
NORM_EPS = 1e-6
NEG_INF = -1e30
GRID_W = 64
ROPE_THETA = 10000.0
LOG2E = math.log2(math.e)

A_HEADS, A_KV_HEADS, A_HALF_WINDOW = 16, 4, 128
B_HEADS, B_Q_LORA, B_KV_LORA, B_NOPE, B_ROPE, B_V = 16, 512, 256, 64, 32, 64
B_QK = B_NOPE + B_ROPE
B_PAD = 128
C_HEADS = 8
C_PATTERNS = ((128, 1), (512, 4), (2048, 16))
C_STEPS = 64
D_HEADS, D_KV_HEADS = 16, 4

FF_CHUNK = 256
N_FF_CHUNKS = D_FF // FF_CHUNK
TOKEN_TILE = 512
VMEM_LIMIT = 56 * 1024 * 1024


def _alibi_slopes(n):
    return 2.0 ** (-8.0 * np.arange(1, n + 1) / n)


def _params(sem, vmem=VMEM_LIMIT):
    return pltpu.CompilerParams(dimension_semantics=sem, vmem_limit_bytes=vmem)


def _const_spec(shape):
    nd = len(shape)
    return pl.BlockSpec(tuple(shape), lambda *_: (0,) * nd, pipeline_mode=pl.Buffered(1))


def _seq_bounds(row0, p_len, s_len):
    in_prompt = row0 < p_len
    b = lax.shift_right_logical(jnp.maximum(row0 - p_len, 0), int(math.log2(s_len)))
    lo = jnp.where(in_prompt, 0, p_len + b * s_len)
    hi = jnp.where(in_prompt, p_len, lo + s_len)
    return lo, hi


def _rms(x, g):
    return x * lax.rsqrt(jnp.mean(x * x, axis=-1, keepdims=True) + NORM_EPS) * g


def _dot(a, b):
    return jnp.dot(a, b, preferred_element_type=F32)


def _dot_nt(a, b):
    return lax.dot_general(a, b, (((1,), (1,)), ((), ())), preferred_element_type=F32)


def _ffn_kernel(*refs, mode):
    if mode == "plain":
        x_ref, gn_ref, wg_ref, wu_ref, wd_ref, out_ref, acc_ref = refs
        x = x_ref[...]
    elif mode == "proj":
        x_ref, o_ref, wo_ref, gn_ref, wg_ref, wu_ref, wd_ref, out_ref, acc_ref = refs
        x = x_ref[...] + _dot(o_ref[...], wo_ref[...])
    else:
        (x_ref, o0, o1, o2, l0, l1, l2, wo_ref, gn_ref, wg_ref, wu_ref, wd_ref,
         out_ref, acc_ref) = refs
        la, lb, lc = l0[...], l1[...], l2[...]
        m = jnp.maximum(jnp.maximum(la, lb), lc)
        ea, eb, ec = jnp.exp(la - m), jnp.exp(lb - m), jnp.exp(lc - m)
        o = (ea * o0[...] + eb * o1[...] + ec * o2[...]) / (ea + eb + ec)
        x = x_ref[...] + _dot(o.astype(BF16), wo_ref[...])

    h = _rms(x, gn_ref[...]).astype(BF16)
    acc_ref[...] = jnp.zeros_like(acc_ref)

    def chunk(c, carry):
        g = _dot(h, wg_ref[c])
        u = _dot(h, wu_ref[c])
        a = (g * jax.nn.sigmoid(g) * u).astype(BF16)
        acc_ref[...] += _dot(a, wd_ref[c])
        return carry

    lax.fori_loop(0, N_FF_CHUNKS, chunk, 0)
    out_ref[...] = x + 0.5 * acc_ref[...]


def _ffn_weights(wg, wu, wd):
    wg3 = wg.astype(BF16).reshape(D_MODEL, N_FF_CHUNKS, FF_CHUNK).transpose(1, 0, 2)
    wu3 = wu.astype(BF16).reshape(D_MODEL, N_FF_CHUNKS, FF_CHUNK).transpose(1, 0, 2)
    wd3 = wd.astype(BF16).reshape(N_FF_CHUNKS, FF_CHUNK, D_MODEL)
    return wg3, wu3, wd3


def _ffn_call(x, gn, wg, wu, wd, pre=None):
    t_rows = x.shape[0]
    tm = TOKEN_TILE
    wg3, wu3, wd3 = _ffn_weights(wg, wu, wd)
    row = lambda w: pl.BlockSpec((tm, w), lambda i: (i, 0))
    ffn_args = [gn.reshape(1, D_MODEL), wg3, wu3, wd3]
    ffn_specs = [_const_spec((1, D_MODEL)), _const_spec(wg3.shape), _const_spec(wu3.shape),
                 _const_spec(wd3.shape)]
    if pre is None:
        mode, args, specs = "plain", [x], [row(D_MODEL)]
    elif len(pre) == 2:
        o, wo = pre
        wo = wo.astype(BF16)
        mode = "proj"
        args = [x, o, wo]
        specs = [row(D_MODEL), row(o.shape[1]), _const_spec(wo.shape)]
    else:
        outs, lses, wo = pre
        wo = wo.astype(BF16)
        mode = "merge"
        args = [x, *outs, *lses, wo]
        specs = [row(D_MODEL)] + [row(outs[0].shape[1])] * 6 + [_const_spec(wo.shape)]
    return pl.pallas_call(
        functools.partial(_ffn_kernel, mode=mode),
        grid=(t_rows // tm,),
        in_specs=specs + ffn_specs,
        out_specs=row(D_MODEL),
        out_shape=jax.ShapeDtypeStruct((t_rows, D_MODEL), F32),
        scratch_shapes=[pltpu.VMEM((tm, D_MODEL), F32)],
        compiler_params=_params(("parallel",)),
        name="ffn_" + mode,
    )(*args, *ffn_args)


def _head_norm(t, width):
    return lax.rsqrt(jnp.sum(t * t, axis=-1, keepdims=True) * (1.0 / width) + NORM_EPS)


def _proj_a_kernel(x_ref, gn_ref, w_ref, gq_ref, gk_ref, q_ref, k_ref, v_ref):
    hd = HEAD_DIM
    h = _rms(x_ref[...], gn_ref[...]).astype(BF16)
    y = _dot(h, w_ref[...])
    gq, gk = gq_ref[...], gk_ref[...]
    for n in range(A_HEADS):
        t = y[:, n * hd:(n + 1) * hd]
        q_ref[n] = (t * _head_norm(t, hd) * gq).astype(BF16)
    base = A_HEADS * hd
    for n in range(A_KV_HEADS):
        t = y[:, base + n * hd: base + (n + 1) * hd]
        k_ref[n] = (t * _head_norm(t, hd) * gk).astype(BF16)
    base += A_KV_HEADS * hd
    for n in range(A_KV_HEADS):
        v_ref[n] = y[:, base + n * hd: base + (n + 1) * hd].astype(BF16)


def _proj_a_call(x, gn, wqkv, gq, gk):
    t_rows = x.shape[0]
    tm, hd = TOKEN_TILE, HEAD_DIM
    w = wqkv.astype(BF16)
    head_major = lambda n: pl.BlockSpec((n, tm, hd), lambda i: (0, i, 0))
    return pl.pallas_call(
        _proj_a_kernel,
        grid=(t_rows // tm,),
        in_specs=[pl.BlockSpec((tm, D_MODEL), lambda i: (i, 0)), _const_spec((1, D_MODEL)),
                  _const_spec(w.shape), _const_spec((1, hd)), _const_spec((1, hd))],
        out_specs=[head_major(A_HEADS), head_major(A_KV_HEADS), head_major(A_KV_HEADS)],
        out_shape=[jax.ShapeDtypeStruct((A_HEADS, t_rows, hd), BF16),
                   jax.ShapeDtypeStruct((A_KV_HEADS, t_rows, hd), BF16),
                   jax.ShapeDtypeStruct((A_KV_HEADS, t_rows, hd), BF16)],
        compiler_params=_params(("parallel",)),
        name="proj_a",
    )(x, gn.reshape(1, D_MODEL), w, (gq * hd ** -0.5).reshape(1, hd), gk.reshape(1, hd))


def _proj_d_kernel(x_ref, gn_ref, w_ref, cq_ref, sq_ref, ck_ref, sk_ref, q_ref, k_ref, v_ref):
    hd = HEAD_DIM
    nq, nk = D_HEADS * hd, D_KV_HEADS * hd
    h = _rms(x_ref[...], gn_ref[...]).astype(BF16)
    y = _dot(h, w_ref[...])
    cq, sq, ck, sk = cq_ref[...], sq_ref[...], ck_ref[...], sk_ref[...]
    swapped = nq + 2 * nk
    for n in range(D_HEADS):
        t = y[:, n * hd:(n + 1) * hd]
        ts = y[:, swapped + n * hd: swapped + (n + 1) * hd]
        q_ref[n] = ((t * cq + ts * sq) * _head_norm(t, hd)).astype(BF16)
    for n in range(D_KV_HEADS):
        t = y[:, nq + n * hd: nq + (n + 1) * hd]
        ts = y[:, swapped + nq + n * hd: swapped + nq + (n + 1) * hd]
        k_ref[n] = ((t * ck + ts * sk) * _head_norm(t, hd)).astype(BF16)
    for n in range(D_KV_HEADS):
        v_ref[n] = y[:, nq + nk + n * hd: nq + nk + (n + 1) * hd].astype(BF16)


def _swap_halves_cols(w, group):
    k, n = w.shape
    return w.reshape(k, n // group, 2, group // 2)[:, :, ::-1, :].reshape(k, n)


def _rope_tables(pos, gain, scale):
    half = gain.shape[-1] // 2
    inv = ROPE_THETA ** (-jnp.arange(half, dtype=F32) / half)
    ang = pos.astype(F32)[:, None] * inv[None, :]
    cos, sin = jnp.cos(ang), jnp.sin(ang)
    g_sw = jnp.concatenate([gain[half:], gain[:half]])
    c = jnp.concatenate([cos, cos], axis=-1) * gain[None, :] * scale
    s = jnp.concatenate([-sin, sin], axis=-1) * g_sw[None, :] * scale
    return c, s


def _proj_d_call(x, gn, wqkv, gq, gk, pos):
    t_rows = x.shape[0]
    tm, hd = TOKEN_TILE, HEAD_DIM
    nq, nk = D_HEADS * hd, D_KV_HEADS * hd
    half = hd // 2
    w = jnp.concatenate([wqkv, _swap_halves_cols(wqkv[:, :nq + nk], half)], axis=1).astype(BF16)
    rows, cols = pos // GRID_W, pos % GRID_W

    def tables(g, scale):
        cr, sr = _rope_tables(rows, g[:half], scale)
        cc, sc = _rope_tables(cols, g[half:], scale)
        return jnp.concatenate([cr, cc], axis=-1), jnp.concatenate([sr, sc], axis=-1)

    cq, sq = tables(gq, hd ** -0.5 * LOG2E)
    ck, sk = tables(gk, 1.0)
    head_major = lambda n: pl.BlockSpec((n, tm, hd), lambda i: (0, i, 0))
    tab = pl.BlockSpec((tm, hd), lambda i: (i, 0))
    return pl.pallas_call(
        _proj_d_kernel,
        grid=(t_rows // tm,),
        in_specs=[pl.BlockSpec((tm, D_MODEL), lambda i: (i, 0)), _const_spec((1, D_MODEL)),
                  _const_spec(w.shape), tab, tab, tab, tab],
        out_specs=[head_major(D_HEADS), head_major(D_KV_HEADS), head_major(D_KV_HEADS)],
        out_shape=[jax.ShapeDtypeStruct((D_HEADS, t_rows, hd), BF16),
                   jax.ShapeDtypeStruct((D_KV_HEADS, t_rows, hd), BF16),
                   jax.ShapeDtypeStruct((D_KV_HEADS, t_rows, hd), BF16)],
        compiler_params=_params(("parallel",)),
        name="proj_d",
    )(x, gn.reshape(1, D_MODEL), w, cq, sq, ck, sk)


def _proj_b_kernel(x_ref, gn_ref, w1_ref, gcq_ref, gckv_ref, wq_ref, wqs_ref, wk_ref, wv_ref,
                   cq_ref, sq_ref, ck_ref, sk_ref, q_ref, k_ref, v_ref):
    h = _rms(x_ref[...], gn_ref[...]).astype(BF16)
    y1 = _dot(h, w1_ref[...])
    c_q = _rms(y1[:, :B_Q_LORA], gcq_ref[...]).astype(BF16)
    c_kv = _rms(y1[:, B_Q_LORA:B_Q_LORA + B_KV_LORA], gckv_ref[...]).astype(BF16)
    off = B_Q_LORA + B_KV_LORA
    kr = y1[:, off:off + B_PAD]
    krs = y1[:, off + B_PAD:off + 2 * B_PAD]
    q = _dot(c_q, wq_ref[...])
    qs = _dot(c_q, wqs_ref[...])
    kn = _dot(c_kv, wk_ref[...])
    v = _dot(c_kv, wv_ref[...])
    cq, sq, ck, sk = cq_ref[...], sq_ref[...], ck_ref[...], sk_ref[...]
    krs_s = krs * sk
    for n in range(B_HEADS):
        t = q[:, n * B_PAD:(n + 1) * B_PAD]
        ts = qs[:, n * B_PAD:(n + 1) * B_PAD]
        q_ref[n] = ((t * cq + ts * sq) * _head_norm(t, B_QK)).astype(BF16)
        t = kn[:, n * B_PAD:(n + 1) * B_PAD] + kr
        k_ref[n] = ((t * ck + krs_s) * _head_norm(t, B_QK)).astype(BF16)
        v_ref[n] = v[:, n * B_V:(n + 1) * B_V].astype(BF16)


def _pad_heads(w, lo, width):
    k, nh, _ = w.shape
    out = jnp.zeros((k, nh, B_PAD), w.dtype).at[:, :, lo:lo + width].set(w)
    return out.reshape(k, nh * B_PAD)


def _proj_b_call(x, gn, wdq, gcq, wuq, wdkv, gckv, wukv, gq, gk, pos):
    t_rows = x.shape[0]
    tm = TOKEN_TILE
    wuq3 = wuq.reshape(B_Q_LORA, B_HEADS, B_QK)
    wq = _pad_heads(wuq3, 0, B_QK).astype(BF16)
    wuq_rope_sw = _swap_halves_cols(wuq3[:, :, B_NOPE:].reshape(B_Q_LORA, -1), B_ROPE)
    wqs = _pad_heads(wuq_rope_sw.reshape(B_Q_LORA, B_HEADS, B_ROPE), B_NOPE, B_ROPE).astype(BF16)
    wukv3 = wukv.reshape(B_KV_LORA, B_HEADS, B_NOPE + B_V)
    wk = _pad_heads(wukv3[:, :, :B_NOPE], 0, B_NOPE).astype(BF16)
    wv = wukv3[:, :, B_NOPE:].reshape(B_KV_LORA, B_HEADS * B_V).astype(BF16)
    w_kr = wdkv[:, B_KV_LORA:]
    kr_pad = _pad_heads(w_kr[:, None, :], B_NOPE, B_ROPE)
    krs_pad = _pad_heads(_swap_halves_cols(w_kr, B_ROPE)[:, None, :], B_NOPE, B_ROPE)
    w1 = jnp.concatenate([wdq, wdkv[:, :B_KV_LORA], kr_pad, krs_pad], axis=1).astype(BF16)

    def tables(g, scale):
        c, s = _rope_tables(pos, g[B_NOPE:], scale)
        zeros = jnp.zeros((t_rows, B_PAD - B_QK), F32)
        c_full = jnp.concatenate(
            [jnp.broadcast_to(g[None, :B_NOPE] * scale, (t_rows, B_NOPE)), c, zeros], axis=-1)
        s_full = jnp.concatenate([jnp.zeros((t_rows, B_NOPE), F32), s, zeros], axis=-1)
        return c_full, s_full

    cq, sq = tables(gq, B_QK ** -0.5 * LOG2E)
    ck, sk = tables(gk, 1.0)
    tab = pl.BlockSpec((tm, B_PAD), lambda i: (i, 0))
    head_major = lambda w_: pl.BlockSpec((B_HEADS, tm, w_), lambda i: (0, i, 0))
    return pl.pallas_call(
        _proj_b_kernel,
        grid=(t_rows // tm,),
        in_specs=[pl.BlockSpec((tm, D_MODEL), lambda i: (i, 0)), _const_spec((1, D_MODEL)),
                  _const_spec(w1.shape), _const_spec((1, B_Q_LORA)), _const_spec((1, B_KV_LORA)),
                  _const_spec(wq.shape), _const_spec(wqs.shape), _const_spec(wk.shape),
                  _const_spec(wv.shape), tab, tab, tab, tab],
        out_specs=[head_major(B_PAD), head_major(B_PAD), head_major(B_V)],
        out_shape=[jax.ShapeDtypeStruct((B_HEADS, t_rows, B_PAD), BF16),
                   jax.ShapeDtypeStruct((B_HEADS, t_rows, B_PAD), BF16),
                   jax.ShapeDtypeStruct((B_HEADS, t_rows, B_V), BF16)],
        compiler_params=_params(("parallel",)),
        name="proj_b",
    )(x, gn.reshape(1, D_MODEL), w1, gcq.reshape(1, -1), gckv.reshape(1, -1), wq, wqs, wk, wv,
      cq, sq, ck, sk)


def _proj_c_kernel(x_ref, gn_ref, w_ref, gq_ref, gk_ref, q_ref, k_ref, v_ref):
    hd = HEAD_DIM
    width = C_HEADS * hd
    h = _rms(x_ref[...], gn_ref[...]).astype(BF16)
    y = _dot(h, w_ref[...])
    gains = (gq_ref[...], gk_ref[...])
    outs = (q_ref, k_ref)
    for g in range(len(C_PATTERNS)):
        for j in range(2):
            base = (g * 3 + j) * width
            heads = []
            for n in range(C_HEADS):
                t = y[:, base + n * hd: base + (n + 1) * hd]
                heads.append(t * _head_norm(t, hd) * gains[j])
            outs[j][g] = jnp.concatenate(heads, axis=-1).astype(BF16)
        base = (g * 3 + 2) * width
        v_ref[g] = y[:, base: base + width].astype(BF16)


def _proj_c_call(x, gn, wqkv, gq, gk):
    t_rows = x.shape[0]
    tm, hd = TOKEN_TILE // 2, HEAD_DIM
    ng, width = len(C_PATTERNS), C_HEADS * hd
    w = wqkv.astype(BF16)
    out = pl.BlockSpec((ng, tm, width), lambda i: (0, i, 0))
    shape = jax.ShapeDtypeStruct((ng, t_rows, width), BF16)
    return pl.pallas_call(
        _proj_c_kernel,
        grid=(t_rows // tm,),
        in_specs=[pl.BlockSpec((tm, D_MODEL), lambda i: (i, 0)), _const_spec((1, D_MODEL)),
                  _const_spec(w.shape), _const_spec((1, hd)), _const_spec((1, hd))],
        out_specs=[out, out, out],
        out_shape=[shape, shape, shape],
        compiler_params=_params(("parallel",)),
        name="proj_c",
    )(x, gn.reshape(1, D_MODEL), w, (gq * hd ** -0.5).reshape(1, hd), gk.reshape(1, hd))


def _band_geometry(row0, lo, hi, tq, w):
    nk = tq + 2 * w
    r = lax.broadcasted_iota(jnp.int32, (tq, nk), 0)
    c = lax.broadcasted_iota(jnp.int32, (tq, nk), 1)
    dist = jnp.abs(r + w - c)
    jpos = row0 - w + lax.broadcasted_iota(jnp.int32, (1, nk), 1)
    outside = jnp.where(jpos < lo, w + 1, 0) + jnp.where(jpos >= hi, w + 1, 0)
    valid = (dist + outside) <= w
    return dist.astype(F32), valid


def _band_head(q, kk, vv, bias, valid, sink):
    s = jnp.where(valid, _dot_nt(q, kk) + bias, NEG_INF)
    m = jnp.max(s, axis=-1, keepdims=True)
    if sink is not None:
        m = jnp.maximum(m, sink)
    p = jnp.exp(s - m)
    den = jnp.sum(p, axis=-1, keepdims=True)
    if sink is not None:
        den = den + jnp.exp(sink - m)
    o = _dot(p.astype(BF16), vv) / den
    return o, m + jnp.log(den)


def _attn_a_kernel(slope_ref, sink_ref, q_ref, kp_ref, kc_ref, kn_ref, vp_ref, vc_ref, vn_ref,
                   o_ref, *, tq, p_len, s_len):
    w = A_HALF_WINDOW
    group = A_HEADS // A_KV_HEADS
    kvh = pl.program_id(0)
    row0 = pl.program_id(1) * tq
    lo, hi = _seq_bounds(row0, p_len, s_len)
    dist, valid = _band_geometry(row0, lo, hi, tq, w)
    kk = jnp.concatenate([kp_ref[0], kc_ref[0], kn_ref[0]], axis=0)
    vv = jnp.concatenate([vp_ref[0], vc_ref[0], vn_ref[0]], axis=0)
    outs = []
    for g in range(group):
        n = kvh * group + g
        o, _ = _band_head(q_ref[g], kk, vv, dist * (-slope_ref[n]), valid, sink_ref[n])
        outs.append(o)
    o_ref[...] = jnp.concatenate(outs, axis=-1).astype(BF16)


def _attn_a_call(q, k, v, sink, geo):
    p_len, s_len, t_rows = geo
    tq, w, hd = 512, A_HALF_WINDOW, HEAD_DIM
    group = A_HEADS // A_KV_HEADS
    per = tq // w
    last = t_rows // w - 1
    prev = pl.BlockSpec((1, w, hd), lambda h, i: (h, jnp.maximum(i * per - 1, 0), 0))
    cur = pl.BlockSpec((1, tq, hd), lambda h, i: (h, i, 0))
    nxt = pl.BlockSpec((1, w, hd), lambda h, i: (h, jnp.minimum((i + 1) * per, last), 0))
    smem = pl.BlockSpec(memory_space=pltpu.SMEM)
    slopes = jnp.asarray(_alibi_slopes(A_HEADS), F32)
    return pl.pallas_call(
        functools.partial(_attn_a_kernel, tq=tq, p_len=p_len, s_len=s_len),
        grid=(A_KV_HEADS, t_rows // tq),
        in_specs=[smem, smem, pl.BlockSpec((group, tq, hd), lambda h, i: (h, i, 0)),
                  prev, cur, nxt, prev, cur, nxt],
        out_specs=pl.BlockSpec((tq, group * hd), lambda h, i: (i, h)),
        out_shape=jax.ShapeDtypeStruct((t_rows, A_HEADS * hd), BF16),
        compiler_params=_params(("parallel", "parallel")),
        name="attn_a",
    )(slopes, sink.astype(F32), q, k, k, k, v, v, v)


def _attn_c_kernel(q_ref, kp_ref, kc_ref, kn_ref, vp_ref, vc_ref, vn_ref, o_ref, lse_ref,
                   *, tq, dil, p_len, s_len):
    w, hd = C_STEPS, HEAD_DIM
    row0 = pl.program_id(0) * tq
    lo, hi = _seq_bounds(row0, p_len // dil, s_len // dil)
    dist, valid = _band_geometry(row0, lo, hi, tq, w)
    slopes = _alibi_slopes(C_HEADS)
    outs, lses = [], []
    for n in range(C_HEADS):
        hs = slice(n * hd, (n + 1) * hd)
        kk = jnp.concatenate([kp_ref[:, hs], kc_ref[:, hs], kn_ref[:, hs]], axis=0)
        vv = jnp.concatenate([vp_ref[:, hs], vc_ref[:, hs], vn_ref[:, hs]], axis=0)
        o, lse = _band_head(q_ref[:, hs], kk, vv, dist * float(-slopes[n] * dil), valid, None)
        outs.append(o)
        lses.append(jnp.broadcast_to(lse, (tq, hd)))
    o_ref[...] = jnp.concatenate(outs, axis=-1)
    lse_ref[...] = jnp.concatenate(lses, axis=-1)


def _attn_c_call(q, k, v, dil, geo):
    p_len, s_len, t_rows = geo
    w, width = C_STEPS, C_HEADS * HEAD_DIM
    rows = t_rows // dil
    tq = min(512, s_len // dil)
    per = tq // w
    last = rows // w - 1
    view = lambda a: a.reshape(rows, dil * width)
    prev = pl.BlockSpec((w, width), lambda i, r: (jnp.maximum(i * per - 1, 0), r))
    cur = pl.BlockSpec((tq, width), lambda i, r: (i, r))
    nxt = pl.BlockSpec((w, width), lambda i, r: (jnp.minimum((i + 1) * per, last), r))
    shape = jax.ShapeDtypeStruct((rows, dil * width), F32)
    qv, kv, vv = view(q), view(k), view(v)
    o, lse = pl.pallas_call(
        functools.partial(_attn_c_kernel, tq=tq, dil=dil, p_len=p_len, s_len=s_len),
        grid=(rows // tq, dil),
        in_specs=[cur, prev, cur, nxt, prev, cur, nxt],
        out_specs=[cur, cur],
        out_shape=[shape, shape],
        compiler_params=_params(("parallel", "parallel")),
        name="attn_c_d%d" % dil,
    )(qv, kv, kv, kv, vv, vv, vv)
    return o.reshape(t_rows, width), lse.reshape(t_rows, width)


def _attn_dense_kernel(q_ref, k_ref, vt_ref, o_ref, sa_ref, sb_ref, *, nq, tq, tk, p_len, s_len):
    dq = q_ref.shape[-1]
    dv = vt_ref.shape[-2]
    width = nq * tq
    row0 = pl.program_id(1) * tq
    lo, hi = _seq_bounds(row0, p_len, s_len)
    shift = int(math.log2(tk))
    first = lax.shift_right_logical(lo, shift)
    n_chunks = lax.shift_right_logical(hi - lo, shift)
    q = q_ref[...].reshape(width, dq)

    def scores(c, st_ref):
        start = pl.multiple_of(c * tk, tk)
        st = _dot_nt(k_ref[0, pl.ds(start, tk), :], q)
        st_ref[...] = st
        return jnp.max(st, axis=0, keepdims=True)

    def update(c, st_ref, mc, m, l, acc):
        m_new = jnp.maximum(m, mc)
        alpha = jnp.exp2(m - m_new)
        pt = jnp.exp2(st_ref[...] - m_new)
        l = alpha * l + jnp.sum(pt, axis=0, keepdims=True)
        acc = alpha * acc + _dot(vt_ref[0, c], pt.astype(BF16))
        return m_new, l, acc

    def pair(i, carry):
        m, l, acc, mc_a = carry
        c = first + 2 * i
        mc_b = scores(c + 1, sb_ref)
        m, l, acc = update(c, sa_ref, mc_a, m, l, acc)
        mc_a = scores(c + 2, sa_ref)
        m, l, acc = update(c + 1, sb_ref, mc_b, m, l, acc)
        return m, l, acc, mc_a

    init = (jnp.full((1, width), NEG_INF, F32), jnp.zeros((1, width), F32),
            jnp.zeros((dv, width), F32), scores(first, sa_ref))
    n_pairs = lax.shift_right_logical(n_chunks, 1)
    m, l, acc, mc_a = lax.fori_loop(0, n_pairs - 1, pair, init)
    c = first + n_chunks - 2
    mc_b = scores(c + 1, sb_ref)
    m, l, acc = update(c, sa_ref, mc_a, m, l, acc)
    m, l, acc = update(c + 1, sb_ref, mc_b, m, l, acc)
    o = (acc / l).astype(BF16)
    for g in range(nq):
        o_ref[g] = o[:, g * tq:(g + 1) * tq]


def _attn_dense_call(q, k, v, geo, tq, tk=512):
    p_len, s_len, t_rows = geo
    hq, _, dq = q.shape
    hk, _, dv = v.shape
    nq = hq // hk
    n_chunks = t_rows // tk
    vt = v.reshape(hk, n_chunks, tk, dv).transpose(0, 1, 3, 2)
    o = pl.pallas_call(
        functools.partial(_attn_dense_kernel, nq=nq, tq=tq, tk=tk, p_len=p_len, s_len=s_len),
        grid=(hk, t_rows // tq),
        in_specs=[pl.BlockSpec((nq, tq, dq), lambda h, i: (h, i, 0)),
                  pl.BlockSpec((1, t_rows, dq), lambda h, i: (h, 0, 0), pipeline_mode=pl.Buffered(1)),
                  pl.BlockSpec((1, n_chunks, dv, tk), lambda h, i: (h, 0, 0, 0),
                               pipeline_mode=pl.Buffered(1))],
        out_specs=pl.BlockSpec((nq, dv, tq), lambda h, i: (h, 0, i)),
        out_shape=jax.ShapeDtypeStruct((hq, dv, t_rows), BF16),
        scratch_shapes=[pltpu.VMEM((tk, nq * tq), F32), pltpu.VMEM((tk, nq * tq), F32)],
        compiler_params=_params(("parallel", "parallel")),
        name="attn_dense_%d" % dq,
    )(q, k, vt)
    return o.transpose(2, 0, 1).reshape(t_rows, hq * dv)


def _trunk(x, p, geo, pos):
    for i in range(DEPTH):
        mixer, j = i % 4, i // 4
        x = _ffn_call(x, p["norm_ffn1"][i], p["ffn1_wg"][i], p["ffn1_wu"][i], p["ffn1_wd"][i])
        gn = p["norm_mix"][i]
        if mixer == 0:
            q, k, v = _proj_a_call(x, gn, p["a_wqkv"][j], p["a_gq"][j], p["a_gk"][j])
            pre = (_attn_a_call(q, k, v, p["a_sink"][j], geo), p["a_wo"][j])
        elif mixer == 1:
            q, k, v = _proj_b_call(x, gn, p["b_wdq"][j], p["b_gcq"][j], p["b_wuq"][j], p["b_wdkv"][j],
                                   p["b_gckv"][j], p["b_wukv"][j], p["b_gq"][j], p["b_gk"][j], pos)
            pre = (_attn_dense_call(q, k, v, geo, tq=512), p["b_wo"][j])
        elif mixer == 2:
            q, k, v = _proj_c_call(x, gn, p["c_wqkv"][j], p["c_gq"][j], p["c_gk"][j])
            res = [_attn_c_call(q[g], k[g], v[g], dil, geo) for g, (_, dil) in enumerate(C_PATTERNS)]
            pre = ([r[0] for r in res], [r[1] for r in res], p["c_wo"][j])
        else:
            q, k, v = _proj_d_call(x, gn, p["d_wqkv"][j], p["d_gq"][j], p["d_gk"][j], pos)
            pre = (_attn_dense_call(q, k, v, geo, tq=128), p["d_wo"][j])
        x = _ffn_call(x, p["norm_ffn2"][i], p["ffn2_wg"][i], p["ffn2_wu"][i], p["ffn2_wd"][i], pre=pre)
    return x


def kernel(x_prompt, x_sample, norm_ffn1, ffn1_wg, ffn1_wu, ffn1_wd, norm_mix, norm_ffn2, ffn2_wg, ffn2_wu, ffn2_wd, a_wqkv, a_gq, a_gk, a_sink, a_wo, b_wdq, b_gcq, b_wuq, b_wdkv, b_gckv, b_wukv, b_gq, b_gk, b_wo, c_wqkv, c_gq, c_gk, c_wo, d_wqkv, d_gq, d_gk, d_wo):
    p = dict(norm_ffn1=norm_ffn1, ffn1_wg=ffn1_wg, ffn1_wu=ffn1_wu, ffn1_wd=ffn1_wd,
             norm_mix=norm_mix, norm_ffn2=norm_ffn2, ffn2_wg=ffn2_wg, ffn2_wu=ffn2_wu,
             ffn2_wd=ffn2_wd, a_wqkv=a_wqkv, a_gq=a_gq, a_gk=a_gk, a_sink=a_sink, a_wo=a_wo,
             b_wdq=b_wdq, b_gcq=b_gcq, b_wuq=b_wuq, b_wdkv=b_wdkv, b_gckv=b_gckv,
             b_wukv=b_wukv, b_gq=b_gq, b_gk=b_gk, b_wo=b_wo, c_wqkv=c_wqkv, c_gq=c_gq,
             c_gk=c_gk, c_wo=c_wo, d_wqkv=d_wqkv, d_gq=d_gq, d_gk=d_gk, d_wo=d_wo)
    n_p, p_seq, _ = x_prompt.shape
    n_s, s_len, _ = x_sample.shape
    assert n_p == 1 and s_len & (s_len - 1) == 0 and p_seq % s_len == 0
    p_len = n_p * p_seq
    t_rows = p_len + n_s * s_len
    geo = (p_len, s_len, t_rows)
    pos = jnp.concatenate([jnp.arange(p_len, dtype=jnp.int32),
                           jnp.tile(jnp.arange(s_len, dtype=jnp.int32), n_s)])
    x = jnp.concatenate([x_prompt.reshape(p_len, D_MODEL), x_sample.reshape(n_s * s_len, D_MODEL)])
    y = _trunk(x, p, geo, pos)
    return y[:p_len].reshape(x_prompt.shape), y[p_len:].reshape(x_sample.shape)
```

```python
import functools
import math

import numpy as np
import jax
import jax.numpy as jnp
from jax import lax
from jax.experimental import pallas as pl
from jax.experimental.pallas import tpu as pltpu

F32 = jnp.float32
BF16 = jnp.bfloat16

D_MODEL = 1024
DEPTH = 4
HEAD_DIM = 64
D_FF = 2816
NORM_EPS = 1e-6
NEG_INF = -1e30
GRID_W = 64
ROPE_THETA = 10000.0
LOG2E = math.log2(math.e)
BF16_SUBLANES = 16

A_HEADS, A_KV_HEADS, A_HALF_WINDOW = 16, 4, 128
B_HEADS, B_Q_LORA, B_KV_LORA, B_NOPE, B_ROPE, B_V = 16, 512, 256, 64, 32, 64
B_QK = B_NOPE + B_ROPE
B_PAD = 128
C_HEADS = 8
C_PATTERNS = ((128, 1), (512, 4), (2048, 16))
C_STEPS = 64
D_HEADS, D_KV_HEADS = 16, 4

FF_CHUNK = 256
N_FF_CHUNKS = D_FF // FF_CHUNK
TOKEN_TILE = 512
VMEM_LIMIT = 56 * 1024 * 1024


def _alibi_slopes(n):
    return 2.0 ** (-8.0 * np.arange(1, n + 1) / n)


def _params(sem, vmem=VMEM_LIMIT):
    return pltpu.CompilerParams(dimension_semantics=sem, vmem_limit_bytes=vmem)


def _const_spec(shape):
    nd = len(shape)
    return pl.BlockSpec(tuple(shape), lambda *_: (0,) * nd, pipeline_mode=pl.Buffered(1))


def _seq_bounds(row0, p_len, s_len):
    in_prompt = row0 < p_len
    b = lax.shift_right_logical(jnp.maximum(row0 - p_len, 0), int(math.log2(s_len)))
    lo = jnp.where(in_prompt, 0, p_len + b * s_len)
    hi = jnp.where(in_prompt, p_len, lo + s_len)
    return lo, hi


def _rms(x, g):
    return x * lax.rsqrt(jnp.mean(x * x, axis=-1, keepdims=True) + NORM_EPS) * g


def _dot(a, b):
    return jnp.dot(a, b, preferred_element_type=F32)


def _dot_nt(a, b):
    return lax.dot_general(a, b, (((1,), (1,)), ((), ())), preferred_element_type=F32)


def _ffn_kernel(*refs, mode):
    if mode == "plain":
        x_ref, gn_ref, wg_ref, wu_ref, wd_ref, out_ref, acc_ref = refs
        x = x_ref[...]
    elif mode == "proj":
        x_ref, o_ref, wo_ref, gn_ref, wg_ref, wu_ref, wd_ref, out_ref, acc_ref = refs
        x = x_ref[...] + _dot(o_ref[...], wo_ref[...])
    else:
        (x_ref, o0, o1, o2, l0, l1, l2, wo_ref, gn_ref, wg_ref, wu_ref, wd_ref,
         out_ref, acc_ref) = refs
        la, lb, lc = l0[...], l1[...], l2[...]
        m = jnp.maximum(jnp.maximum(la, lb), lc)
        ea, eb, ec = jnp.exp(la - m), jnp.exp(lb - m), jnp.exp(lc - m)
        o = (ea * o0[...] + eb * o1[...] + ec * o2[...]) / (ea + eb + ec)
        x = x_ref[...] + _dot(o.astype(BF16), wo_ref[...])

    h = _rms(x, gn_ref[...]).astype(BF16)
    acc_ref[...] = jnp.zeros_like(acc_ref)

    def chunk(c, carry):
        g = _dot(h, wg_ref[c])
        u = _dot(h, wu_ref[c])
        a = (g * jax.nn.sigmoid(g) * u).astype(BF16)
        acc_ref[...] += _dot(a, wd_ref[c])
        return carry

    lax.fori_loop(0, N_FF_CHUNKS, chunk, 0)
    out_ref[...] = x + 0.5 * acc_ref[...]


def _ffn_weights(wg, wu, wd):
    wg3 = wg.astype(BF16).reshape(D_MODEL, N_FF_CHUNKS, FF_CHUNK).transpose(1, 0, 2)
    wu3 = wu.astype(BF16).reshape(D_MODEL, N_FF_CHUNKS, FF_CHUNK).transpose(1, 0, 2)
    wd3 = wd.astype(BF16).reshape(N_FF_CHUNKS, FF_CHUNK, D_MODEL)
    return wg3, wu3, wd3


def _ffn_call(x, gn, wg, wu, wd, pre=None):
    t_rows = x.shape[0]
    tm = TOKEN_TILE
    wg3, wu3, wd3 = _ffn_weights(wg, wu, wd)
    row = lambda w: pl.BlockSpec((tm, w), lambda i: (i, 0))
    ffn_args = [gn.reshape(1, D_MODEL), wg3, wu3, wd3]
    ffn_specs = [_const_spec((1, D_MODEL)), _const_spec(wg3.shape), _const_spec(wu3.shape),
                 _const_spec(wd3.shape)]
    if pre is None:
        mode, args, specs = "plain", [x], [row(D_MODEL)]
    elif len(pre) == 2:
        o, wo = pre
        wo = wo.astype(BF16)
        mode = "proj"
        args = [x, o, wo]
        specs = [row(D_MODEL), row(o.shape[1]), _const_spec(wo.shape)]
    else:
        outs, lses, wo = pre
        wo = wo.astype(BF16)
        mode = "merge"
        args = [x, *outs, *lses, wo]
        specs = [row(D_MODEL)] + [row(outs[0].shape[1])] * 6 + [_const_spec(wo.shape)]
    return pl.pallas_call(
        functools.partial(_ffn_kernel, mode=mode),
        grid=(t_rows // tm,),
        in_specs=specs + ffn_specs,
        out_specs=row(D_MODEL),
        out_shape=jax.ShapeDtypeStruct((t_rows, D_MODEL), F32),
        scratch_shapes=[pltpu.VMEM((tm, D_MODEL), F32)],
        compiler_params=_params(("parallel",)),
        name="ffn_" + mode,
    )(*args, *ffn_args)


def _head_norm(t, width):
    return lax.rsqrt(jnp.sum(t * t, axis=-1, keepdims=True) * (1.0 / width) + NORM_EPS)


def _proj_a_kernel(x_ref, gn_ref, w_ref, gq_ref, gk_ref, q_ref, k_ref, v_ref):
    hd = HEAD_DIM
    h = _rms(x_ref[...], gn_ref[...]).astype(BF16)
    y = _dot(h, w_ref[...])
    gq, gk = gq_ref[...], gk_ref[...]
    for n in range(A_HEADS):
        t = y[:, n * hd:(n + 1) * hd]
        q_ref[n] = (t * _head_norm(t, hd) * gq).astype(BF16)
    base = A_HEADS * hd
    for n in range(A_KV_HEADS):
        t = y[:, base + n * hd: base + (n + 1) * hd]
        k_ref[n] = (t * _head_norm(t, hd) * gk).astype(BF16)
    base += A_KV_HEADS * hd
    for n in range(A_KV_HEADS):
        v_ref[n] = y[:, base + n * hd: base + (n + 1) * hd].astype(BF16)


def _proj_a_call(x, gn, wqkv, gq, gk):
    t_rows = x.shape[0]
    tm, hd = TOKEN_TILE, HEAD_DIM
    w = wqkv.astype(BF16)
    head_major = lambda n: pl.BlockSpec((n, tm, hd), lambda i: (0, i, 0))
    return pl.pallas_call(
        _proj_a_kernel,
        grid=(t_rows // tm,),
        in_specs=[pl.BlockSpec((tm, D_MODEL), lambda i: (i, 0)), _const_spec((1, D_MODEL)),
                  _const_spec(w.shape), _const_spec((1, hd)), _const_spec((1, hd))],
        out_specs=[head_major(A_HEADS), head_major(A_KV_HEADS), head_major(A_KV_HEADS)],
        out_shape=[jax.ShapeDtypeStruct((A_HEADS, t_rows, hd), BF16),
                   jax.ShapeDtypeStruct((A_KV_HEADS, t_rows, hd), BF16),
                   jax.ShapeDtypeStruct((A_KV_HEADS, t_rows, hd), BF16)],
        compiler_params=_params(("parallel",)),
        name="proj_a",
    )(x, gn.reshape(1, D_MODEL), w, (gq * hd ** -0.5).reshape(1, hd), gk.reshape(1, hd))


def _proj_d_kernel(x_ref, gn_ref, w_ref, cq_ref, sq_ref, ck_ref, sk_ref, q_ref, k_ref, v_ref):
    hd = HEAD_DIM
    nq, nk = D_HEADS * hd, D_KV_HEADS * hd
    h = _rms(x_ref[...], gn_ref[...]).astype(BF16)
    y = _dot(h, w_ref[...])
    cq, sq, ck, sk = cq_ref[...], sq_ref[...], ck_ref[...], sk_ref[...]
    swapped = nq + 2 * nk
    for n in range(D_HEADS):
        t = y[:, n * hd:(n + 1) * hd]
        ts = y[:, swapped + n * hd: swapped + (n + 1) * hd]
        q_ref[n] = ((t * cq + ts * sq) * _head_norm(t, hd)).astype(BF16)
    for n in range(D_KV_HEADS):
        t = y[:, nq + n * hd: nq + (n + 1) * hd]
        ts = y[:, swapped + nq + n * hd: swapped + nq + (n + 1) * hd]
        k_ref[n] = ((t * ck + ts * sk) * _head_norm(t, hd)).astype(BF16)
    for n in range(D_KV_HEADS):
        v_ref[n] = y[:, nq + nk + n * hd: nq + nk + (n + 1) * hd].astype(BF16)


def _swap_halves_cols(w, group):
    k, n = w.shape
    return w.reshape(k, n // group, 2, group // 2)[:, :, ::-1, :].reshape(k, n)


def _rope_tables(pos, gain, scale):
    half = gain.shape[-1] // 2
    inv = ROPE_THETA ** (-jnp.arange(half, dtype=F32) / half)
    ang = pos.astype(F32)[:, None] * inv[None, :]
    cos, sin = jnp.cos(ang), jnp.sin(ang)
    g_sw = jnp.concatenate([gain[half:], gain[:half]])
    c = jnp.concatenate([cos, cos], axis=-1) * gain[None, :] * scale
    s = jnp.concatenate([-sin, sin], axis=-1) * g_sw[None, :] * scale
    return c, s


def _proj_d_call(x, gn, wqkv, gq, gk, pos):
    t_rows = x.shape[0]
    tm, hd = TOKEN_TILE, HEAD_DIM
    nq, nk = D_HEADS * hd, D_KV_HEADS * hd
    half = hd // 2
    w = jnp.concatenate([wqkv, _swap_halves_cols(wqkv[:, :nq + nk], half)], axis=1).astype(BF16)
    rows, cols = pos // GRID_W, pos % GRID_W

    def tables(g, scale):
        cr, sr = _rope_tables(rows, g[:half], scale)
        cc, sc = _rope_tables(cols, g[half:], scale)
        return jnp.concatenate([cr, cc], axis=-1), jnp.concatenate([sr, sc], axis=-1)

    cq, sq = tables(gq, hd ** -0.5 * LOG2E)
    ck, sk = tables(gk, 1.0)
    head_major = lambda n: pl.BlockSpec((n, tm, hd), lambda i: (0, i, 0))
    tab = pl.BlockSpec((tm, hd), lambda i: (i, 0))
    return pl.pallas_call(
        _proj_d_kernel,
        grid=(t_rows // tm,),
        in_specs=[pl.BlockSpec((tm, D_MODEL), lambda i: (i, 0)), _const_spec((1, D_MODEL)),
                  _const_spec(w.shape), tab, tab, tab, tab],
        out_specs=[head_major(D_HEADS), head_major(D_KV_HEADS), head_major(D_KV_HEADS)],
        out_shape=[jax.ShapeDtypeStruct((D_HEADS, t_rows, hd), BF16),
                   jax.ShapeDtypeStruct((D_KV_HEADS, t_rows, hd), BF16),
                   jax.ShapeDtypeStruct((D_KV_HEADS, t_rows, hd), BF16)],
        compiler_params=_params(("parallel",)),
        name="proj_d",
    )(x, gn.reshape(1, D_MODEL), w, cq, sq, ck, sk)


def _proj_b_kernel(x_ref, gn_ref, w1_ref, gcq_ref, gckv_ref, wq_ref, wqs_ref, wk_ref, wv_ref,
                   cq_ref, sq_ref, ck_ref, sk_ref, q_ref, k_ref, v_ref):
    h = _rms(x_ref[...], gn_ref[...]).astype(BF16)
    y1 = _dot(h, w1_ref[...])
    c_q = _rms(y1[:, :B_Q_LORA], gcq_ref[...]).astype(BF16)
    c_kv = _rms(y1[:, B_Q_LORA:B_Q_LORA + B_KV_LORA], gckv_ref[...]).astype(BF16)
    off = B_Q_LORA + B_KV_LORA
    kr = y1[:, off:off + B_PAD]
    krs = y1[:, off + B_PAD:off + 2 * B_PAD]
    q = _dot(c_q, wq_ref[...])
    qs = _dot(c_q, wqs_ref[...])
    kn = _dot(c_kv, wk_ref[...])
    v = _dot(c_kv, wv_ref[...])
    cq, sq, ck, sk = cq_ref[...], sq_ref[...], ck_ref[...], sk_ref[...]
    krs_s = krs * sk
    for n in range(B_HEADS):
        t = q[:, n * B_PAD:(n + 1) * B_PAD]
        ts = qs[:, n * B_PAD:(n + 1) * B_PAD]
        q_ref[n] = ((t * cq + ts * sq) * _head_norm(t, B_QK)).astype(BF16)
        t = kn[:, n * B_PAD:(n + 1) * B_PAD] + kr
        k_ref[n] = ((t * ck + krs_s) * _head_norm(t, B_QK)).astype(BF16)
        v_ref[n] = v[:, n * B_V:(n + 1) * B_V].astype(BF16)


def _pad_heads(w, lo, width):
    k, nh, _ = w.shape
    out = jnp.zeros((k, nh, B_PAD), w.dtype).at[:, :, lo:lo + width].set(w)
    return out.reshape(k, nh * B_PAD)


def _proj_b_call(x, gn, wdq, gcq, wuq, wdkv, gckv, wukv, gq, gk, pos):
    t_rows = x.shape[0]
    tm = TOKEN_TILE
    wuq3 = wuq.reshape(B_Q_LORA, B_HEADS, B_QK)
    wq = _pad_heads(wuq3, 0, B_QK).astype(BF16)
    wuq_rope_sw = _swap_halves_cols(wuq3[:, :, B_NOPE:].reshape(B_Q_LORA, -1), B_ROPE)
    wqs = _pad_heads(wuq_rope_sw.reshape(B_Q_LORA, B_HEADS, B_ROPE), B_NOPE, B_ROPE).astype(BF16)
    wukv3 = wukv.reshape(B_KV_LORA, B_HEADS, B_NOPE + B_V)
    wk = _pad_heads(wukv3[:, :, :B_NOPE], 0, B_NOPE).astype(BF16)
    wv = wukv3[:, :, B_NOPE:].reshape(B_KV_LORA, B_HEADS * B_V).astype(BF16)
    w_kr = wdkv[:, B_KV_LORA:]
    kr_pad = _pad_heads(w_kr[:, None, :], B_NOPE, B_ROPE)
    krs_pad = _pad_heads(_swap_halves_cols(w_kr, B_ROPE)[:, None, :], B_NOPE, B_ROPE)
    w1 = jnp.concatenate([wdq, wdkv[:, :B_KV_LORA], kr_pad, krs_pad], axis=1).astype(BF16)

    def tables(g, scale):
        c, s = _rope_tables(pos, g[B_NOPE:], scale)
        zeros = jnp.zeros((t_rows, B_PAD - B_QK), F32)
        c_full = jnp.concatenate(
            [jnp.broadcast_to(g[None, :B_NOPE] * scale, (t_rows, B_NOPE)), c, zeros], axis=-1)
        s_full = jnp.concatenate([jnp.zeros((t_rows, B_NOPE), F32), s, zeros], axis=-1)
        return c_full, s_full

    cq, sq = tables(gq, B_QK ** -0.5 * LOG2E)
    ck, sk = tables(gk, 1.0)
    tab = pl.BlockSpec((tm, B_PAD), lambda i: (i, 0))
    head_major = lambda w_: pl.BlockSpec((B_HEADS, tm, w_), lambda i: (0, i, 0))
    return pl.pallas_call(
        _proj_b_kernel,
        grid=(t_rows // tm,),
        in_specs=[pl.BlockSpec((tm, D_MODEL), lambda i: (i, 0)), _const_spec((1, D_MODEL)),
                  _const_spec(w1.shape), _const_spec((1, B_Q_LORA)), _const_spec((1, B_KV_LORA)),
                  _const_spec(wq.shape), _const_spec(wqs.shape), _const_spec(wk.shape),
                  _const_spec(wv.shape), tab, tab, tab, tab],
        out_specs=[head_major(B_PAD), head_major(B_PAD), head_major(B_V)],
        out_shape=[jax.ShapeDtypeStruct((B_HEADS, t_rows, B_PAD), BF16),
                   jax.ShapeDtypeStruct((B_HEADS, t_rows, B_PAD), BF16),
                   jax.ShapeDtypeStruct((B_HEADS, t_rows, B_V), BF16)],
        compiler_params=_params(("parallel",)),
        name="proj_b",
    )(x, gn.reshape(1, D_MODEL), w1, gcq.reshape(1, -1), gckv.reshape(1, -1), wq, wqs, wk, wv,
      cq, sq, ck, sk)


def _proj_c_kernel(x_ref, gn_ref, w_ref, gq_ref, gk_ref, q_ref, k_ref, v_ref):
    hd = HEAD_DIM
    width = C_HEADS * hd
    h = _rms(x_ref[...], gn_ref[...]).astype(BF16)
    y = _dot(h, w_ref[...])
    gains = (gq_ref[...], gk_ref[...])
    outs = (q_ref, k_ref)
    for g in range(len(C_PATTERNS)):
        for j in range(2):
            base = (g * 3 + j) * width
            heads = []
            for n in range(C_HEADS):
                t = y[:, base + n * hd: base + (n + 1) * hd]
                heads.append(t * _head_norm(t, hd) * gains[j])
            outs[j][g] = jnp.concatenate(heads, axis=-1).astype(BF16)
        base = (g * 3 + 2) * width
        v_ref[g] = y[:, base: base + width].astype(BF16)


def _proj_c_call(x, gn, wqkv, gq, gk):
    t_rows = x.shape[0]
    tm, hd = TOKEN_TILE // 2, HEAD_DIM
    ng, width = len(C_PATTERNS), C_HEADS * hd
    w = wqkv.astype(BF16)
    out = pl.BlockSpec((ng, tm, width), lambda i: (0, i, 0))
    shape = jax.ShapeDtypeStruct((ng, t_rows, width), BF16)
    return pl.pallas_call(
        _proj_c_kernel,
        grid=(t_rows // tm,),
        in_specs=[pl.BlockSpec((tm, D_MODEL), lambda i: (i, 0)), _const_spec((1, D_MODEL)),
                  _const_spec(w.shape), _const_spec((1, hd)), _const_spec((1, hd))],
        out_specs=[out, out, out],
        out_shape=[shape, shape, shape],
        compiler_params=_params(("parallel",)),
        name="proj_c",
    )(x, gn.reshape(1, D_MODEL), w, (gq * hd ** -0.5).reshape(1, hd), gk.reshape(1, hd))


def _band_geometry(row0, lo, hi, tq, w):
    nk = tq + 2 * w
    r = lax.broadcasted_iota(jnp.int32, (tq, nk), 0)
    c = lax.broadcasted_iota(jnp.int32, (tq, nk), 1)
    dist = jnp.abs(r + w - c)
    jpos = row0 - w + lax.broadcasted_iota(jnp.int32, (1, nk), 1)
    outside = jnp.where(jpos < lo, w + 1, 0) + jnp.where(jpos >= hi, w + 1, 0)
    valid = (dist + outside) <= w
    return dist.astype(F32), valid


def _band_head(q, kk, vv, bias, valid, sink):
    s = jnp.where(valid, _dot_nt(q, kk) + bias, NEG_INF)
    m = jnp.max(s, axis=-1, keepdims=True)
    if sink is not None:
        m = jnp.maximum(m, sink)
    p = jnp.exp(s - m)
    den = jnp.sum(p, axis=-1, keepdims=True)
    if sink is not None:
        den = den + jnp.exp(sink - m)
    o = _dot(p.astype(BF16), vv) / den
    return o, m + jnp.log(den)


def _attn_a_kernel(slope_ref, sink_ref, q_ref, kp_ref, kc_ref, kn_ref, vp_ref, vc_ref, vn_ref,
                   o_ref, *, tq, p_len, s_len):
    w = A_HALF_WINDOW
    group = A_HEADS // A_KV_HEADS
    kvh = pl.program_id(0)
    row0 = pl.program_id(1) * tq
    lo, hi = _seq_bounds(row0, p_len, s_len)
    dist, valid = _band_geometry(row0, lo, hi, tq, w)
    kk = jnp.concatenate([kp_ref[0], kc_ref[0], kn_ref[0]], axis=0)
    vv = jnp.concatenate([vp_ref[0], vc_ref[0], vn_ref[0]], axis=0)
    outs = []
    for g in range(group):
        n = kvh * group + g
        o, _ = _band_head(q_ref[g], kk, vv, dist * (-slope_ref[n]), valid, sink_ref[n])
        outs.append(o)
    o_ref[...] = jnp.concatenate(outs, axis=-1).astype(BF16)


def _attn_a_call(q, k, v, sink, geo):
    p_len, s_len, t_rows = geo
    tq, w, hd = 512, A_HALF_WINDOW, HEAD_DIM
    group = A_HEADS // A_KV_HEADS
    per = tq // w
    last = t_rows // w - 1
    prev = pl.BlockSpec((1, w, hd), lambda h, i: (h, jnp.maximum(i * per - 1, 0), 0))
    cur = pl.BlockSpec((1, tq, hd), lambda h, i: (h, i, 0))
    nxt = pl.BlockSpec((1, w, hd), lambda h, i: (h, jnp.minimum((i + 1) * per, last), 0))
    smem = pl.BlockSpec(memory_space=pltpu.SMEM)
    slopes = jnp.asarray(_alibi_slopes(A_HEADS), F32)
    return pl.pallas_call(
        functools.partial(_attn_a_kernel, tq=tq, p_len=p_len, s_len=s_len),
        grid=(A_KV_HEADS, t_rows // tq),
        in_specs=[smem, smem, pl.BlockSpec((group, tq, hd), lambda h, i: (h, i, 0)),
                  prev, cur, nxt, prev, cur, nxt],
        out_specs=pl.BlockSpec((tq, group * hd), lambda h, i: (i, h)),
        out_shape=jax.ShapeDtypeStruct((t_rows, A_HEADS * hd), BF16),
        compiler_params=_params(("parallel", "parallel")),
        name="attn_a",
    )(slopes, sink.astype(F32), q, k, k, k, v, v, v)


def _attn_c_kernel(q_ref, kp_ref, kc_ref, kn_ref, vp_ref, vc_ref, vn_ref, o_ref, lse_ref,
                   *, tq, dil, p_len, s_len):
    w, hd = C_STEPS, HEAD_DIM
    row0 = pl.program_id(0) * tq
    lo, hi = _seq_bounds(row0, p_len // dil, s_len // dil)
    dist, valid = _band_geometry(row0, lo, hi, tq, w)
    slopes = _alibi_slopes(C_HEADS)
    outs, lses = [], []
    for n in range(C_HEADS):
        hs = slice(n * hd, (n + 1) * hd)
        kk = jnp.concatenate([kp_ref[:, hs], kc_ref[:, hs], kn_ref[:, hs]], axis=0)
        vv = jnp.concatenate([vp_ref[:, hs], vc_ref[:, hs], vn_ref[:, hs]], axis=0)
        o, lse = _band_head(q_ref[:, hs], kk, vv, dist * float(-slopes[n] * dil), valid, None)
        outs.append(o)
        lses.append(jnp.broadcast_to(lse, (tq, hd)))
    o_ref[...] = jnp.concatenate(outs, axis=-1)
    lse_ref[...] = jnp.concatenate(lses, axis=-1)


def _attn_c_call(q, k, v, dil, geo):
    p_len, s_len, t_rows = geo
    w, width = C_STEPS, C_HEADS * HEAD_DIM
    rows = t_rows // dil
    tq = min(512, s_len // dil)
    per = tq // w
    last = rows // w - 1
    view = lambda a: a.reshape(rows, dil * width)
    prev = pl.BlockSpec((w, width), lambda i, r: (jnp.maximum(i * per - 1, 0), r))
    cur = pl.BlockSpec((tq, width), lambda i, r: (i, r))
    nxt = pl.BlockSpec((w, width), lambda i, r: (jnp.minimum((i + 1) * per, last), r))
    shape = jax.ShapeDtypeStruct((rows, dil * width), F32)
    qv, kv, vv = view(q), view(k), view(v)
    o, lse = pl.pallas_call(
        functools.partial(_attn_c_kernel, tq=tq, dil=dil, p_len=p_len, s_len=s_len),
        grid=(rows // tq, dil),
        in_specs=[cur, prev, cur, nxt, prev, cur, nxt],
        out_specs=[cur, cur],
        out_shape=[shape, shape],
        compiler_params=_params(("parallel", "parallel")),
        name="attn_c_d%d" % dil,
    )(qv, kv, kv, kv, vv, vv, vv)
    return o.reshape(t_rows, width), lse.reshape(t_rows, width)


def _attn_dense_kernel(q_ref, k_ref, vt_ref, o_ref, sa_ref, sb_ref, sc_ref, sd_ref,
                       *, nq, tq, tk, p_len, s_len):
    dq = q_ref.shape[-1]
    dv = o_ref.shape[-2]
    width = nq * tq
    row0 = pl.program_id(1) * tq
    lo, hi = _seq_bounds(row0, p_len, s_len)
    shift = int(math.log2(tk))
    first = lax.shift_right_logical(lo, shift)
    n_chunks = lax.shift_right_logical(hi - lo, shift)
    q = q_ref[...].reshape(width, dq)

    def scores(c, st_ref):
        start = pl.multiple_of(c * tk, tk)
        st = _dot_nt(k_ref[0, pl.ds(start, tk), :], q).astype(BF16)
        st_ref[...] = st
        return jnp.max(st, axis=0, keepdims=True).astype(F32)

    def update(c, st_ref, mc, m, acc):
        m_new = jnp.maximum(m, mc)
        alpha = jnp.exp2(m - m_new)
        pt = jnp.exp2(st_ref[...] - m_new.astype(BF16))
        return m_new, alpha * acc + _dot(vt_ref[0, c], pt)

    def quad(i, carry):
        m, acc, mc_a, mc_b = carry
        c = first + 4 * i
        mc_c = scores(c + 2, sc_ref)
        m, acc = update(c, sa_ref, mc_a, m, acc)
        mc_d = scores(c + 3, sd_ref)
        m, acc = update(c + 1, sb_ref, mc_b, m, acc)
        mc_a = scores(c + 4, sa_ref)
        m, acc = update(c + 2, sc_ref, mc_c, m, acc)
        mc_b = scores(c + 5, sb_ref)
        m, acc = update(c + 3, sd_ref, mc_d, m, acc)
        return m, acc, mc_a, mc_b

    init = (jnp.full((1, width), NEG_INF, F32), jnp.zeros((vt_ref.shape[-2], width), F32),
            scores(first, sa_ref), scores(first + 1, sb_ref))
    n_quads = lax.shift_right_logical(n_chunks, 2)
    m, acc, mc_a, mc_b = lax.fori_loop(0, n_quads - 1, quad, init)
    c = first + n_chunks - 4
    mc_c = scores(c + 2, sc_ref)
    mc_d = scores(c + 3, sd_ref)
    m, acc = update(c, sa_ref, mc_a, m, acc)
    m, acc = update(c + 1, sb_ref, mc_b, m, acc)
    m, acc = update(c + 2, sc_ref, mc_c, m, acc)
    m, acc = update(c + 3, sd_ref, mc_d, m, acc)
    o = (acc[:dv] / acc[dv:dv + 1]).astype(BF16)
    for g in range(nq):
        o_ref[g] = o[:, g * tq:(g + 1) * tq]


def _attn_dense_call(q, k, v, geo, tq, tk=512):
    p_len, s_len, t_rows = geo
    hq, _, dq = q.shape
    hk, _, dv = v.shape
    nq = hq // hk
    n_chunks = t_rows // tk
    assert s_len % (4 * tk) == 0 and p_len % (4 * tk) == 0
    vt = v.reshape(hk, n_chunks, tk, dv).transpose(0, 1, 3, 2)
    ones_rows = jnp.zeros((hk, n_chunks, BF16_SUBLANES, tk), BF16).at[:, :, 0].set(1.0)
    vt = jnp.concatenate([vt, ones_rows], axis=2)
    o = pl.pallas_call(
        functools.partial(_attn_dense_kernel, nq=nq, tq=tq, tk=tk, p_len=p_len, s_len=s_len),
        grid=(hk, t_rows // tq),
        in_specs=[pl.BlockSpec((nq, tq, dq), lambda h, i: (h, i, 0)),
                  pl.BlockSpec((1, t_rows, dq), lambda h, i: (h, 0, 0), pipeline_mode=pl.Buffered(1)),
                  pl.BlockSpec((1, n_chunks, dv + BF16_SUBLANES, tk), lambda h, i: (h, 0, 0, 0),
                               pipeline_mode=pl.Buffered(1))],
        out_specs=pl.BlockSpec((nq, dv, tq), lambda h, i: (h, 0, i)),
        out_shape=jax.ShapeDtypeStruct((hq, dv, t_rows), BF16),
        scratch_shapes=[pltpu.VMEM((tk, nq * tq), BF16)] * 4,
        compiler_params=_params(("parallel", "parallel")),
        name="attn_dense_%d" % dq,
    )(q, k, vt)
    return o.transpose(2, 0, 1).reshape(t_rows, hq * dv)


def _trunk(x, p, geo, pos):
    for i in range(DEPTH):
        mixer, j = i % 4, i // 4
        x = _ffn_call(x, p["norm_ffn1"][i], p["ffn1_wg"][i], p["ffn1_wu"][i], p["ffn1_wd"][i])
        gn = p["norm_mix"][i]
        if mixer == 0:
            q, k, v = _proj_a_call(x, gn, p["a_wqkv"][j], p["a_gq"][j], p["a_gk"][j])
            pre = (_attn_a_call(q, k, v, p["a_sink"][j], geo), p["a_wo"][j])
        elif mixer == 1:
            q, k, v = _proj_b_call(x, gn, p["b_wdq"][j], p["b_gcq"][j], p["b_wuq"][j], p["b_wdkv"][j],
                                   p["b_gckv"][j], p["b_wukv"][j], p["b_gq"][j], p["b_gk"][j], pos)
            pre = (_attn_dense_call(q, k, v, geo, tq=512), p["b_wo"][j])
        elif mixer == 2:
            q, k, v = _proj_c_call(x, gn, p["c_wqkv"][j], p["c_gq"][j], p["c_gk"][j])
            res = [_attn_c_call(q[g], k[g], v[g], dil, geo) for g, (_, dil) in enumerate(C_PATTERNS)]
            pre = ([r[0] for r in res], [r[1] for r in res], p["c_wo"][j])
        else:
            q, k, v = _proj_d_call(x, gn, p["d_wqkv"][j], p["d_gq"][j], p["d_gk"][j], pos)
            pre = (_attn_dense_call(q, k, v, geo, tq=128), p["d_wo"][j])
        x = _ffn_call(x, p["norm_ffn2"][i], p["ffn2_wg"][i], p["ffn2_wu"][i], p["ffn2_wd"][i], pre=pre)
    return x


def kernel(x_prompt, x_sample, norm_ffn1, ffn1_wg, ffn1_wu, ffn1_wd, norm_mix, norm_ffn2, ffn2_wg, ffn2_wu, ffn2_wd, a_wqkv, a_gq, a_gk, a_sink, a_wo, b_wdq, b_gcq, b_wuq, b_wdkv, b_gckv, b_wukv, b_gq, b_gk, b_wo, c_wqkv, c_gq, c_gk, c_wo, d_wqkv, d_gq, d_gk, d_wo):
    p = dict(norm_ffn1=norm_ffn1, ffn1_wg=ffn1_wg, ffn1_wu=ffn1_wu, ffn1_wd=ffn1_wd,
             norm_mix=norm_mix, norm_ffn2=norm_ffn2, ffn2_wg=ffn2_wg, ffn2_wu=ffn2_wu,
             ffn2_wd=ffn2_wd, a_wqkv=a_wqkv, a_gq=a_gq, a_gk=a_gk, a_sink=a_sink, a_wo=a_wo,
             b_wdq=b_wdq, b_gcq=b_gcq, b_wuq=b_wuq, b_wdkv=b_wdkv, b_gckv=b_gckv,
             b_wukv=b_wukv, b_gq=b_gq, b_gk=b_gk, b_wo=b_wo, c_wqkv=c_wqkv, c_gq=c_gq,
             c_gk=c_gk, c_wo=c_wo, d_wqkv=d_wqkv, d_gq=d_gq, d_gk=d_gk, d_wo=d_wo)
    n_p, p_seq, _ = x_prompt.shape
    n_s, s_len, _ = x_sample.shape
    assert n_p == 1 and s_len & (s_len - 1) == 0 and p_seq % s_len == 0
    p_len = n_p * p_seq
    t_rows = p_len + n_s * s_len
    geo = (p_len, s_len, t_rows)
    pos = jnp.concatenate([jnp.arange(p_len, dtype=jnp.int32),
                           jnp.tile(jnp.arange(s_len, dtype=jnp.int32), n_s)])
    x = jnp.concatenate([x_prompt.reshape(p_len, D_MODEL), x_sample.reshape(n_s * s_len, D_MODEL)])
    y = _trunk(x, p, geo, pos)
    return y[:p_len].reshape(x_prompt.shape), y[p_len:].reshape(x_sample.shape)
```

```python
import functools
import math

import numpy as np
import jax
import jax.numpy as jnp
from jax import lax
from jax.experimental import pallas as pl
from jax.experimental.pallas import tpu as pltpu

F32 = jnp.float32
BF16 = jnp.bfloat16

D_MODEL = 1024
DEPTH = 4
HEAD_DIM = 64
D_FF = 2816
NORM_EPS = 1e-6
NEG_INF = -1e30
GRID_W = 64
ROPE_THETA = 10000.0
LOG2E = math.log2(math.e)
LANES = 128
BF16_SUBLANES = 16
SCORES_AHEAD = 2

A_HEADS, A_KV_HEADS, A_HALF_WINDOW = 16, 4, 128
B_HEADS, B_Q_LORA, B_KV_LORA, B_NOPE, B_ROPE, B_V = 16, 512, 256, 64, 32, 64
B_QK = B_NOPE + B_ROPE
B_PAD = 128
C_HEADS = 8
C_PATTERNS = ((128, 1), (512, 4), (2048, 16))
C_STEPS = 64
D_HEADS, D_KV_HEADS = 16, 4

FF_CHUNK = 256
N_FF_CHUNKS = D_FF // FF_CHUNK
TOKEN_TILE = 512
VMEM_LIMIT = 56 * 1024 * 1024


def _alibi_slopes(n):
    return 2.0 ** (-8.0 * np.arange(1, n + 1) / n)


def _params(sem, vmem=VMEM_LIMIT):
    return pltpu.CompilerParams(dimension_semantics=sem, vmem_limit_bytes=vmem)


def _const_spec(shape):
    nd = len(shape)
    return pl.BlockSpec(tuple(shape), lambda *_: (0,) * nd, pipeline_mode=pl.Buffered(1))


def _seq_bounds(row0, p_len, s_len):
    in_prompt = row0 < p_len
    b = lax.shift_right_logical(jnp.maximum(row0 - p_len, 0), int(math.log2(s_len)))
    lo = jnp.where(in_prompt, 0, p_len + b * s_len)
    hi = jnp.where(in_prompt, p_len, lo + s_len)
    return lo, hi


def _rms(x, g):
    return x * lax.rsqrt(jnp.mean(x * x, axis=-1, keepdims=True) + NORM_EPS) * g


def _dot(a, b):
    return jnp.dot(a, b, preferred_element_type=F32)


def _dot_nt(a, b):
    return lax.dot_general(a, b, (((1,), (1,)), ((), ())), preferred_element_type=F32)


def _ffn_kernel(*refs, mode):
    if mode == "plain":
        x_ref, gn_ref, wg_ref, wu_ref, wd_ref, out_ref, acc_ref = refs
        x = x_ref[...]
    elif mode == "proj":
        x_ref, o_ref, wo_ref, gn_ref, wg_ref, wu_ref, wd_ref, out_ref, acc_ref = refs
        x = x_ref[...] + _dot(o_ref[...], wo_ref[...])
    else:
        (x_ref, o0, o1, o2, l0, l1, l2, wo_ref, gn_ref, wg_ref, wu_ref, wd_ref,
         out_ref, acc_ref, *scr) = refs
        tm, width = x_ref.shape[0], wo_ref.shape[0]

        def token_major(ref, dil, scr_ref):
            slabs = width // LANES
            for r in range(dil):
                for s in range(slabs):
                    col = r * width + s * LANES
                    scr_ref[s, pl.ds(r, tm // dil, stride=dil), :] = ref[:, col:col + LANES]
            return jnp.concatenate([scr_ref[s] for s in range(slabs)], axis=-1)

        dils = [d for _, d in C_PATTERNS]
        assert dils[0] == 1
        oa, la = o0[...], l0[...]
        ob, lb = token_major(o1, dils[1], scr[0]), token_major(l1, dils[1], scr[1])
        oc, lc = token_major(o2, dils[2], scr[2]), token_major(l2, dils[2], scr[3])
        m = jnp.maximum(jnp.maximum(la, lb), lc)
        ea, eb, ec = jnp.exp(la - m), jnp.exp(lb - m), jnp.exp(lc - m)
        o = (ea * oa + eb * ob + ec * oc) / (ea + eb + ec)
        x = x_ref[...] + _dot(o.astype(BF16), wo_ref[...])

    h = _rms(x, gn_ref[...]).astype(BF16)
    acc_ref[...] = jnp.zeros_like(acc_ref)

    def chunk(c, carry):
        g = _dot(h, wg_ref[c])
        u = _dot(h, wu_ref[c])
        a = (g * jax.nn.sigmoid(g) * u).astype(BF16)
        acc_ref[...] += _dot(a, wd_ref[c])
        return carry

    lax.fori_loop(0, N_FF_CHUNKS, chunk, 0)
    out_ref[...] = x + 0.5 * acc_ref[...]


def _ffn_weights(wg, wu, wd):
    wg3 = wg.astype(BF16).reshape(D_MODEL, N_FF_CHUNKS, FF_CHUNK).transpose(1, 0, 2)
    wu3 = wu.astype(BF16).reshape(D_MODEL, N_FF_CHUNKS, FF_CHUNK).transpose(1, 0, 2)
    wd3 = wd.astype(BF16).reshape(N_FF_CHUNKS, FF_CHUNK, D_MODEL)
    return wg3, wu3, wd3


def _ffn_call(x, gn, wg, wu, wd, pre=None):
    t_rows = x.shape[0]
    tm = TOKEN_TILE
    wg3, wu3, wd3 = _ffn_weights(wg, wu, wd)
    row = lambda w: pl.BlockSpec((tm, w), lambda i: (i, 0))
    ffn_args = [gn.reshape(1, D_MODEL), wg3, wu3, wd3]
    ffn_specs = [_const_spec((1, D_MODEL)), _const_spec(wg3.shape), _const_spec(wu3.shape),
                 _const_spec(wd3.shape)]
    scratch = []
    if pre is None:
        mode, args, specs = "plain", [x], [row(D_MODEL)]
    elif len(pre) == 2:
        o, wo = pre
        wo = wo.astype(BF16)
        mode = "proj"
        args = [x, o, wo]
        specs = [row(D_MODEL), row(o.shape[1]), _const_spec(wo.shape)]
    else:
        outs, lses, wo = pre
        wo = wo.astype(BF16)
        mode = "merge"
        args = [x, *outs, *lses, wo]
        width = wo.shape[0]
        dilated = [pl.BlockSpec((tm // d, d * width), lambda i: (i, 0)) for _, d in C_PATTERNS]
        specs = [row(D_MODEL)] + dilated * 2 + [_const_spec(wo.shape)]
        scratch = [pltpu.VMEM((width // LANES, tm, LANES), F32)] * 4
    return pl.pallas_call(
        functools.partial(_ffn_kernel, mode=mode),
        grid=(t_rows // tm,),
        in_specs=specs + ffn_specs,
        out_specs=row(D_MODEL),
        out_shape=jax.ShapeDtypeStruct((t_rows, D_MODEL), F32),
        scratch_shapes=[pltpu.VMEM((tm, D_MODEL), F32)] + scratch,
        compiler_params=_params(("parallel",)),
        name="ffn_" + mode,
    )(*args, *ffn_args)


def _head_norm(t, width):
    return lax.rsqrt(jnp.sum(t * t, axis=-1, keepdims=True) * (1.0 / width) + NORM_EPS)


def _proj_a_kernel(x_ref, gn_ref, w_ref, gq_ref, gk_ref, q_ref, k_ref, v_ref):
    hd = HEAD_DIM
    h = _rms(x_ref[...], gn_ref[...]).astype(BF16)
    y = _dot(h, w_ref[...])
    gq, gk = gq_ref[...], gk_ref[...]
    for n in range(A_HEADS):
        t = y[:, n * hd:(n + 1) * hd]
        q_ref[n] = (t * _head_norm(t, hd) * gq).astype(BF16)
    base = A_HEADS * hd
    for n in range(A_KV_HEADS):
        t = y[:, base + n * hd: base + (n + 1) * hd]
        k_ref[n] = (t * _head_norm(t, hd) * gk).astype(BF16)
    base += A_KV_HEADS * hd
    for n in range(A_KV_HEADS):
        v_ref[n] = y[:, base + n * hd: base + (n + 1) * hd].astype(BF16)


def _proj_a_call(x, gn, wqkv, gq, gk):
    t_rows = x.shape[0]
    tm, hd = TOKEN_TILE, HEAD_DIM
    w = wqkv.astype(BF16)
    head_major = lambda n: pl.BlockSpec((n, tm, hd), lambda i: (0, i, 0))
    return pl.pallas_call(
        _proj_a_kernel,
        grid=(t_rows // tm,),
        in_specs=[pl.BlockSpec((tm, D_MODEL), lambda i: (i, 0)), _const_spec((1, D_MODEL)),
                  _const_spec(w.shape), _const_spec((1, hd)), _const_spec((1, hd))],
        out_specs=[head_major(A_HEADS), head_major(A_KV_HEADS), head_major(A_KV_HEADS)],
        out_shape=[jax.ShapeDtypeStruct((A_HEADS, t_rows, hd), BF16),
                   jax.ShapeDtypeStruct((A_KV_HEADS, t_rows, hd), BF16),
                   jax.ShapeDtypeStruct((A_KV_HEADS, t_rows, hd), BF16)],
        compiler_params=_params(("parallel",)),
        name="proj_a",
    )(x, gn.reshape(1, D_MODEL), w, (gq * hd ** -0.5).reshape(1, hd), gk.reshape(1, hd))


def _proj_d_kernel(x_ref, gn_ref, w_ref, cq_ref, sq_ref, ck_ref, sk_ref, q_ref, k_ref, v_ref):
    hd = HEAD_DIM
    nq, nk = D_HEADS * hd, D_KV_HEADS * hd
    h = _rms(x_ref[...], gn_ref[...]).astype(BF16)
    y = _dot(h, w_ref[...])
    cq, sq, ck, sk = cq_ref[...], sq_ref[...], ck_ref[...], sk_ref[...]
    swapped = nq + 2 * nk
    for n in range(D_HEADS):
        t = y[:, n * hd:(n + 1) * hd]
        ts = y[:, swapped + n * hd: swapped + (n + 1) * hd]
        q_ref[n] = ((t * cq + ts * sq) * _head_norm(t, hd)).astype(BF16)
    for n in range(D_KV_HEADS):
        t = y[:, nq + n * hd: nq + (n + 1) * hd]
        ts = y[:, swapped + nq + n * hd: swapped + nq + (n + 1) * hd]
        k_ref[n] = ((t * ck + ts * sk) * _head_norm(t, hd)).astype(BF16)
    for n in range(D_KV_HEADS):
        v_ref[n] = y[:, nq + nk + n * hd: nq + nk + (n + 1) * hd].astype(BF16)


def _swap_halves_cols(w, group):
    k, n = w.shape
    return w.reshape(k, n // group, 2, group // 2)[:, :, ::-1, :].reshape(k, n)


def _rope_tables(pos, gain, scale):
    half = gain.shape[-1] // 2
    inv = ROPE_THETA ** (-jnp.arange(half, dtype=F32) / half)
    ang = pos.astype(F32)[:, None] * inv[None, :]
    cos, sin = jnp.cos(ang), jnp.sin(ang)
    g_sw = jnp.concatenate([gain[half:], gain[:half]])
    c = jnp.concatenate([cos, cos], axis=-1) * gain[None, :] * scale
    s = jnp.concatenate([-sin, sin], axis=-1) * g_sw[None, :] * scale
    return c, s


def _proj_d_call(x, gn, wqkv, gq, gk, pos):
    t_rows = x.shape[0]
    tm, hd = TOKEN_TILE, HEAD_DIM
    nq, nk = D_HEADS * hd, D_KV_HEADS * hd
    half = hd // 2
    w = jnp.concatenate([wqkv, _swap_halves_cols(wqkv[:, :nq + nk], half)], axis=1).astype(BF16)
    rows, cols = pos // GRID_W, pos % GRID_W

    def tables(g, scale):
        cr, sr = _rope_tables(rows, g[:half], scale)
        cc, sc = _rope_tables(cols, g[half:], scale)
        return jnp.concatenate([cr, cc], axis=-1), jnp.concatenate([sr, sc], axis=-1)

    cq, sq = tables(gq, hd ** -0.5 * LOG2E)
    ck, sk = tables(gk, 1.0)
    head_major = lambda n: pl.BlockSpec((n, tm, hd), lambda i: (0, i, 0))
    tab = pl.BlockSpec((tm, hd), lambda i: (i, 0))
    return pl.pallas_call(
        _proj_d_kernel,
        grid=(t_rows // tm,),
        in_specs=[pl.BlockSpec((tm, D_MODEL), lambda i: (i, 0)), _const_spec((1, D_MODEL)),
                  _const_spec(w.shape), tab, tab, tab, tab],
        out_specs=[head_major(D_HEADS), head_major(D_KV_HEADS), head_major(D_KV_HEADS)],
        out_shape=[jax.ShapeDtypeStruct((D_HEADS, t_rows, hd), BF16),
                   jax.ShapeDtypeStruct((D_KV_HEADS, t_rows, hd), BF16),
                   jax.ShapeDtypeStruct((D_KV_HEADS, t_rows, hd), BF16)],
        compiler_params=_params(("parallel",)),
        name="proj_d",
    )(x, gn.reshape(1, D_MODEL), w, cq, sq, ck, sk)


def _proj_b_kernel(x_ref, gn_ref, w1_ref, gcq_ref, gckv_ref, wq_ref, wqs_ref, wk_ref, wv_ref,
                   cq_ref, sq_ref, ck_ref, sk_ref, q_ref, k_ref, v_ref):
    h = _rms(x_ref[...], gn_ref[...]).astype(BF16)
    y1 = _dot(h, w1_ref[...])
    c_q = _rms(y1[:, :B_Q_LORA], gcq_ref[...]).astype(BF16)
    c_kv = _rms(y1[:, B_Q_LORA:B_Q_LORA + B_KV_LORA], gckv_ref[...]).astype(BF16)
    off = B_Q_LORA + B_KV_LORA
    kr = y1[:, off:off + B_PAD]
    krs = y1[:, off + B_PAD:off + 2 * B_PAD]
    q = _dot(c_q, wq_ref[...])
    qs = _dot(c_q, wqs_ref[...])
    kn = _dot(c_kv, wk_ref[...])
    v = _dot(c_kv, wv_ref[...])
    cq, sq, ck, sk = cq_ref[...], sq_ref[...], ck_ref[...], sk_ref[...]
    krs_s = krs * sk
    for n in range(B_HEADS):
        t = q[:, n * B_PAD:(n + 1) * B_PAD]
        ts = qs[:, n * B_PAD:(n + 1) * B_PAD]
        q_ref[n] = ((t * cq + ts * sq) * _head_norm(t, B_QK)).astype(BF16)
        t = kn[:, n * B_PAD:(n + 1) * B_PAD] + kr
        k_ref[n] = ((t * ck + krs_s) * _head_norm(t, B_QK)).astype(BF16)
        v_ref[n] = v[:, n * B_V:(n + 1) * B_V].astype(BF16)


def _pad_heads(w, lo, width):
    k, nh, _ = w.shape
    out = jnp.zeros((k, nh, B_PAD), w.dtype).at[:, :, lo:lo + width].set(w)
    return out.reshape(k, nh * B_PAD)


def _proj_b_call(x, gn, wdq, gcq, wuq, wdkv, gckv, wukv, gq, gk, pos):
    t_rows = x.shape[0]
    tm = TOKEN_TILE
    wuq3 = wuq.reshape(B_Q_LORA, B_HEADS, B_QK)
    wq = _pad_heads(wuq3, 0, B_QK).astype(BF16)
    wuq_rope_sw = _swap_halves_cols(wuq3[:, :, B_NOPE:].reshape(B_Q_LORA, -1), B_ROPE)
    wqs = _pad_heads(wuq_rope_sw.reshape(B_Q_LORA, B_HEADS, B_ROPE), B_NOPE, B_ROPE).astype(BF16)
    wukv3 = wukv.reshape(B_KV_LORA, B_HEADS, B_NOPE + B_V)
    wk = _pad_heads(wukv3[:, :, :B_NOPE], 0, B_NOPE).astype(BF16)
    wv = wukv3[:, :, B_NOPE:].reshape(B_KV_LORA, B_HEADS * B_V).astype(BF16)
    w_kr = wdkv[:, B_KV_LORA:]
    kr_pad = _pad_heads(w_kr[:, None, :], B_NOPE, B_ROPE)
    krs_pad = _pad_heads(_swap_halves_cols(w_kr, B_ROPE)[:, None, :], B_NOPE, B_ROPE)
    w1 = jnp.concatenate([wdq, wdkv[:, :B_KV_LORA], kr_pad, krs_pad], axis=1).astype(BF16)

    def tables(g, scale):
        c, s = _rope_tables(pos, g[B_NOPE:], scale)
        zeros = jnp.zeros((t_rows, B_PAD - B_QK), F32)
        c_full = jnp.concatenate(
            [jnp.broadcast_to(g[None, :B_NOPE] * scale, (t_rows, B_NOPE)), c, zeros], axis=-1)
        s_full = jnp.concatenate([jnp.zeros((t_rows, B_NOPE), F32), s, zeros], axis=-1)
        return c_full, s_full

    cq, sq = tables(gq, B_QK ** -0.5 * LOG2E)
    ck, sk = tables(gk, 1.0)
    tab = pl.BlockSpec((tm, B_PAD), lambda i: (i, 0))
    head_major = lambda w_: pl.BlockSpec((B_HEADS, tm, w_), lambda i: (0, i, 0))
    return pl.pallas_call(
        _proj_b_kernel,
        grid=(t_rows // tm,),
        in_specs=[pl.BlockSpec((tm, D_MODEL), lambda i: (i, 0)), _const_spec((1, D_MODEL)),
                  _const_spec(w1.shape), _const_spec((1, B_Q_LORA)), _const_spec((1, B_KV_LORA)),
                  _const_spec(wq.shape), _const_spec(wqs.shape), _const_spec(wk.shape),
                  _const_spec(wv.shape), tab, tab, tab, tab],
        out_specs=[head_major(B_PAD), head_major(B_PAD), head_major(B_V)],
        out_shape=[jax.ShapeDtypeStruct((B_HEADS, t_rows, B_PAD), BF16),
                   jax.ShapeDtypeStruct((B_HEADS, t_rows, B_PAD), BF16),
                   jax.ShapeDtypeStruct((B_HEADS, t_rows, B_V), BF16)],
        compiler_params=_params(("parallel",)),
        name="proj_b",
    )(x, gn.reshape(1, D_MODEL), w1, gcq.reshape(1, -1), gckv.reshape(1, -1), wq, wqs, wk, wv,
      cq, sq, ck, sk)


def _proj_c_kernel(x_ref, gn_ref, w_ref, gq_ref, gk_ref, *refs):
    hd = HEAD_DIM
    width = C_HEADS * hd
    outs, scr_ref = refs[:-1], refs[-1]
    tm = x_ref.shape[0]
    h = _rms(x_ref[...], gn_ref[...]).astype(BF16)
    y = _dot(h, w_ref[...])
    gains = (gq_ref[...], gk_ref[...])
    for g, (_, dil) in enumerate(C_PATTERNS):
        for j in range(3):
            base = (g * 3 + j) * width
            if j < 2:
                heads = []
                for n in range(C_HEADS):
                    t = y[:, base + n * hd: base + (n + 1) * hd]
                    heads.append(t * _head_norm(t, hd) * gains[j])
                val = jnp.concatenate(heads, axis=-1)
            else:
                val = y[:, base: base + width]
            out_ref = outs[g * 3 + j]
            if dil == 1:
                out_ref[...] = val.astype(BF16)
            else:
                for s in range(width // LANES):
                    scr_ref[s] = val[:, s * LANES:(s + 1) * LANES]
                for r in range(dil):
                    for s in range(width // LANES):
                        rows = scr_ref[s, pl.ds(r, tm // dil, stride=dil), :]
                        col = r * width + s * LANES
                        out_ref[:, col:col + LANES] = rows.astype(BF16)


def _proj_c_call(x, gn, wqkv, gq, gk):
    t_rows = x.shape[0]
    tm, hd = TOKEN_TILE // 2, HEAD_DIM
    width = C_HEADS * hd
    w = wqkv.astype(BF16)
    specs, shapes = [], []
    for _, dil in C_PATTERNS:
        specs += [pl.BlockSpec((tm // dil, dil * width), lambda i: (i, 0))] * 3
        shapes += [jax.ShapeDtypeStruct((t_rows // dil, dil * width), BF16)] * 3
    outs = pl.pallas_call(
        _proj_c_kernel,
        grid=(t_rows // tm,),
        in_specs=[pl.BlockSpec((tm, D_MODEL), lambda i: (i, 0)), _const_spec((1, D_MODEL)),
                  _const_spec(w.shape), _const_spec((1, hd)), _const_spec((1, hd))],
        out_specs=specs,
        out_shape=shapes,
        scratch_shapes=[pltpu.VMEM((width // LANES, tm, LANES), F32)],
        compiler_params=_params(("parallel",)),
        name="proj_c",
    )(x, gn.reshape(1, D_MODEL), w, (gq * hd ** -0.5).reshape(1, hd), gk.reshape(1, hd))
    return [outs[3 * g:3 * g + 3] for g in range(len(C_PATTERNS))]


def _band_geometry(row0, lo, hi, tq, w):
    nk = tq + 2 * w
    r = lax.broadcasted_iota(jnp.int32, (tq, nk), 0)
    c = lax.broadcasted_iota(jnp.int32, (tq, nk), 1)
    dist = jnp.abs(r + w - c)
    jpos = row0 - w + lax.broadcasted_iota(jnp.int32, (1, nk), 1)
    outside = jnp.where(jpos < lo, w + 1, 0) + jnp.where(jpos >= hi, w + 1, 0)
    valid = (dist + outside) <= w
    return dist.astype(F32), valid


def _band_head(q, kk, vv, bias, valid, sink):
    s = jnp.where(valid, _dot_nt(q, kk) + bias, NEG_INF)
    m = jnp.max(s, axis=-1, keepdims=True)
    if sink is not None:
        m = jnp.maximum(m, sink)
    p = jnp.exp(s - m)
    den = jnp.sum(p, axis=-1, keepdims=True)
    if sink is not None:
        den = den + jnp.exp(sink - m)
    o = _dot(p.astype(BF16), vv) / den
    return o, m + jnp.log(den)


def _attn_a_kernel(slope_ref, sink_ref, q_ref, kp_ref, kc_ref, kn_ref, vp_ref, vc_ref, vn_ref,
                   o_ref, *, tq, p_len, s_len):
    w = A_HALF_WINDOW
    group = A_HEADS // A_KV_HEADS
    kvh = pl.program_id(0)
    row0 = pl.program_id(1) * tq
    lo, hi = _seq_bounds(row0, p_len, s_len)
    dist, valid = _band_geometry(row0, lo, hi, tq, w)
    kk = jnp.concatenate([kp_ref[0], kc_ref[0], kn_ref[0]], axis=0)
    vv = jnp.concatenate([vp_ref[0], vc_ref[0], vn_ref[0]], axis=0)
    outs = []
    for g in range(group):
        n = kvh * group + g
        o, _ = _band_head(q_ref[g], kk, vv, dist * (-slope_ref[n]), valid, sink_ref[n])
        outs.append(o)
    o_ref[...] = jnp.concatenate(outs, axis=-1).astype(BF16)


def _attn_a_call(q, k, v, sink, geo):
    p_len, s_len, t_rows = geo
    tq, w, hd = 512, A_HALF_WINDOW, HEAD_DIM
    group = A_HEADS // A_KV_HEADS
    per = tq // w
    last = t_rows // w - 1
    prev = pl.BlockSpec((1, w, hd), lambda h, i: (h, jnp.maximum(i * per - 1, 0), 0))
    cur = pl.BlockSpec((1, tq, hd), lambda h, i: (h, i, 0))
    nxt = pl.BlockSpec((1, w, hd), lambda h, i: (h, jnp.minimum((i + 1) * per, last), 0))
    smem = pl.BlockSpec(memory_space=pltpu.SMEM)
    slopes = jnp.asarray(_alibi_slopes(A_HEADS), F32)
    return pl.pallas_call(
        functools.partial(_attn_a_kernel, tq=tq, p_len=p_len, s_len=s_len),
        grid=(A_KV_HEADS, t_rows // tq),
        in_specs=[smem, smem, pl.BlockSpec((group, tq, hd), lambda h, i: (h, i, 0)),
                  prev, cur, nxt, prev, cur, nxt],
        out_specs=pl.BlockSpec((tq, group * hd), lambda h, i: (i, h)),
        out_shape=jax.ShapeDtypeStruct((t_rows, A_HEADS * hd), BF16),
        compiler_params=_params(("parallel", "parallel")),
        name="attn_a",
    )(slopes, sink.astype(F32), q, k, k, k, v, v, v)


def _attn_c_kernel(q_ref, kp_ref, kc_ref, kn_ref, vp_ref, vc_ref, vn_ref, o_ref, lse_ref,
                   *, tq, dil, p_len, s_len):
    w, hd = C_STEPS, HEAD_DIM
    row0 = pl.program_id(0) * tq
    lo, hi = _seq_bounds(row0, p_len // dil, s_len // dil)
    dist, valid = _band_geometry(row0, lo, hi, tq, w)
    slopes = _alibi_slopes(C_HEADS)
    outs, lses = [], []
    for n in range(C_HEADS):
        hs = slice(n * hd, (n + 1) * hd)
        kk = jnp.concatenate([kp_ref[:, hs], kc_ref[:, hs], kn_ref[:, hs]], axis=0)
        vv = jnp.concatenate([vp_ref[:, hs], vc_ref[:, hs], vn_ref[:, hs]], axis=0)
        o, lse = _band_head(q_ref[:, hs], kk, vv, dist * float(-slopes[n] * dil), valid, None)
        outs.append(o)
        lses.append(jnp.broadcast_to(lse, (tq, hd)))
    o_ref[...] = jnp.concatenate(outs, axis=-1)
    lse_ref[...] = jnp.concatenate(lses, axis=-1)


def _attn_c_call(q, k, v, dil, geo):
    p_len, s_len, t_rows = geo
    w, width = C_STEPS, C_HEADS * HEAD_DIM
    rows = t_rows // dil
    tq = min(512, s_len // dil)
    per = tq // w
    last = rows // w - 1
    prev = pl.BlockSpec((w, width), lambda i, r: (jnp.maximum(i * per - 1, 0), r))
    cur = pl.BlockSpec((tq, width), lambda i, r: (i, r))
    nxt = pl.BlockSpec((w, width), lambda i, r: (jnp.minimum((i + 1) * per, last), r))
    shape = jax.ShapeDtypeStruct((rows, dil * width), F32)
    return pl.pallas_call(
        functools.partial(_attn_c_kernel, tq=tq, dil=dil, p_len=p_len, s_len=s_len),
        grid=(rows // tq, dil),
        in_specs=[cur, prev, cur, nxt, prev, cur, nxt],
        out_specs=[cur, cur],
        out_shape=[shape, shape],
        compiler_params=_params(("parallel", "parallel")),
        name="attn_c_d%d" % dil,
    )(q, k, k, k, v, v, v)


def _attn_dense_kernel(q_ref, k_ref, vt_ref, o_ref, acc_ref, *bufs, nq, tq, tk):
    dq = q_ref.shape[-1]
    dv = o_ref.shape[-2]
    width = nq * tq
    n_chunks = vt_ref.shape[1]

    def scores(c):
        q = q_ref[...].reshape(width, dq)
        st = _dot_nt(k_ref[0, c * tk:(c + 1) * tk, :], q).astype(BF16)
        bufs[c % len(bufs)][...] = st
        return jnp.max(st, axis=0, keepdims=True).astype(F32)

    def update(c, mc, m):
        m_new = jnp.maximum(m, mc)
        pt = jnp.exp2(bufs[c % len(bufs)][...] - m_new.astype(BF16))
        pv = _dot(vt_ref[0, c], pt)
        acc_ref[...] = pv if c == 0 else jnp.exp2(m - m_new) * acc_ref[...] + pv
        return m_new

    m = jnp.full((1, width), NEG_INF, F32)
    mcs = {c: scores(c) for c in range(min(SCORES_AHEAD, n_chunks))}
    for c in range(n_chunks):
        if c + SCORES_AHEAD < n_chunks:
            mcs[c + SCORES_AHEAD] = scores(c + SCORES_AHEAD)
        m = update(c, mcs.pop(c), m)
    o = (acc_ref[:dv] / acc_ref[dv:dv + 1]).astype(BF16)
    for g in range(nq):
        o_ref[g] = o[:, g * tq:(g + 1) * tq]


def _attn_dense_group(q, k, vt, base, n_seq, seq_len, tq, tk):
    hq, _, dq = q.shape
    hk, _, rows_v, _ = vt.shape
    dv = rows_v - BF16_SUBLANES
    nq = hq // hk
    assert base % seq_len == 0 and seq_len % tk == 0 and seq_len % tq == 0
    seq0, tiles = base // seq_len, seq_len // tq
    return pl.pallas_call(
        functools.partial(_attn_dense_kernel, nq=nq, tq=tq, tk=tk),
        grid=(hk, n_seq, tiles),
        in_specs=[pl.BlockSpec((nq, tq, dq), lambda h, b, i: (h, (seq0 + b) * tiles + i, 0)),
                  pl.BlockSpec((1, seq_len, dq), lambda h, b, i: (h, seq0 + b, 0)),
                  pl.BlockSpec((1, seq_len // tk, rows_v, tk), lambda h, b, i: (h, seq0 + b, 0, 0))],
        out_specs=pl.BlockSpec((nq, dv, tq), lambda h, b, i: (h, 0, b * tiles + i)),
        out_shape=jax.ShapeDtypeStruct((hq, dv, n_seq * seq_len), BF16),
        scratch_shapes=[pltpu.VMEM((rows_v, nq * tq), F32)]
        + [pltpu.VMEM((tk, nq * tq), BF16)] * (2 * SCORES_AHEAD),
        compiler_params=_params(("parallel", "parallel", "parallel")),
        name="attn_dense_%d_%d" % (dq, seq_len),
    )(q, k, vt)


def _attn_dense_call(q, k, v, geo, tq, tk=512):
    p_len, s_len, t_rows = geo
    hq = q.shape[0]
    hk, _, dv = v.shape
    n_chunks = t_rows // tk
    vt = v.reshape(hk, n_chunks, tk, dv).transpose(0, 1, 3, 2)
    ones_rows = jnp.zeros((hk, n_chunks, BF16_SUBLANES, tk), BF16).at[:, :, 0].set(1.0)
    vt = jnp.concatenate([vt, ones_rows], axis=2)
    o_p = _attn_dense_group(q, k, vt, 0, 1, p_len, tq, tk)
    o_s = _attn_dense_group(q, k, vt, p_len, (t_rows - p_len) // s_len, s_len, tq, tk)
    o = jnp.concatenate([o_p, o_s], axis=-1)
    return o.transpose(2, 0, 1).reshape(t_rows, hq * dv)


def _trunk(x, p, geo, pos):
    for i in range(DEPTH):
        mixer, j = i % 4, i // 4
        x = _ffn_call(x, p["norm_ffn1"][i], p["ffn1_wg"][i], p["ffn1_wu"][i], p["ffn1_wd"][i])
        gn = p["norm_mix"][i]
        if mixer == 0:
            q, k, v = _proj_a_call(x, gn, p["a_wqkv"][j], p["a_gq"][j], p["a_gk"][j])
            pre = (_attn_a_call(q, k, v, p["a_sink"][j], geo), p["a_wo"][j])
        elif mixer == 1:
            q, k, v = _proj_b_call(x, gn, p["b_wdq"][j], p["b_gcq"][j], p["b_wuq"][j], p["b_wdkv"][j],
                                   p["b_gckv"][j], p["b_wukv"][j], p["b_gq"][j], p["b_gk"][j], pos)
            pre = (_attn_dense_call(q, k, v, geo, tq=512), p["b_wo"][j])
        elif mixer == 2:
            qkv = _proj_c_call(x, gn, p["c_wqkv"][j], p["c_gq"][j], p["c_gk"][j])
            res = [_attn_c_call(*qkv[g], dil, geo) for g, (_, dil) in enumerate(C_PATTERNS)]
            pre = ([r[0] for r in res], [r[1] for r in res], p["c_wo"][j])
        else:
            q, k, v = _proj_d_call(x, gn, p["d_wqkv"][j], p["d_gq"][j], p["d_gk"][j], pos)
            pre = (_attn_dense_call(q, k, v, geo, tq=128), p["d_wo"][j])
        x = _ffn_call(x, p["norm_ffn2"][i], p["ffn2_wg"][i], p["ffn2_wu"][i], p["ffn2_wd"][i], pre=pre)
    return x


def kernel(x_prompt, x_sample, norm_ffn1, ffn1_wg, ffn1_wu, ffn1_wd, norm_mix, norm_ffn2, ffn2_wg, ffn2_wu, ffn2_wd, a_wqkv, a_gq, a_gk, a_sink, a_wo, b_wdq, b_gcq, b_wuq, b_wdkv, b_gckv, b_wukv, b_gq, b_gk, b_wo, c_wqkv, c_gq, c_gk, c_wo, d_wqkv, d_gq, d_gk, d_wo):
    p = dict(norm_ffn1=norm_ffn1, ffn1_wg=ffn1_wg, ffn1_wu=ffn1_wu, ffn1_wd=ffn1_wd,
             norm_mix=norm_mix, norm_ffn2=norm_ffn2, ffn2_wg=ffn2_wg, ffn2_wu=ffn2_wu,
             ffn2_wd=ffn2_wd, a_wqkv=a_wqkv, a_gq=a_gq, a_gk=a_gk, a_sink=a_sink, a_wo=a_wo,
             b_wdq=b_wdq, b_gcq=b_gcq, b_wuq=b_wuq, b_wdkv=b_wdkv, b_gckv=b_gckv,
             b_wukv=b_wukv, b_gq=b_gq, b_gk=b_gk, b_wo=b_wo, c_wqkv=c_wqkv, c_gq=c_gq,
             c_gk=c_gk, c_wo=c_wo, d_wqkv=d_wqkv, d_gq=d_gq, d_gk=d_gk, d_wo=d_wo)
    n_p, p_seq, _ = x_prompt.shape
    n_s, s_len, _ = x_sample.shape
    assert n_p == 1 and s_len & (s_len - 1) == 0 and p_seq % s_len == 0
    p_len = n_p * p_seq
    t_rows = p_len + n_s * s_len
    geo = (p_len, s_len, t_rows)
    pos = jnp.concatenate([jnp.arange(p_len, dtype=jnp.int32),
                           jnp.tile(jnp.arange(s_len, dtype=jnp.int32), n_s)])
    x = jnp.concatenate([x_prompt.reshape(p_len, D_MODEL), x_sample.reshape(n_s * s_len, D_MODEL)])
    y = _trunk(x, p, geo, pos)
    return y[:p_len].reshape(x_prompt.shape), y[p_len:].reshape(x_sample.shape)
```

```python
import functools
import math

import numpy as np
import jax
import jax.numpy as jnp
from jax import lax
from jax.experimental import pallas as pl
from jax.experimental.pallas import tpu as pltpu

F32 = jnp.float32
BF16 = jnp.bfloat16

D_MODEL = 1024
DEPTH = 4
HEAD_DIM = 64
D_FF = 2816
NORM_EPS = 1e-6
NEG_INF = -1e30
GRID_W = 64
ROPE_THETA = 10000.0
LOG2E = math.log2(math.e)
LANES = 128
BF16_SUBLANES = 16
BAND_SUB = 128
MASK_DIST = 1e33

A_HEADS, A_KV_HEADS, A_HALF_WINDOW = 16, 4, 128
B_HEADS, B_Q_LORA, B_KV_LORA, B_NOPE, B_ROPE, B_V = 16, 512, 256, 64, 32, 64
B_QK = B_NOPE + B_ROPE
B_PAD = 128
C_HEADS = 8
C_PATTERNS = ((128, 1), (512, 4), (2048, 16))
C_STEPS = 64
D_HEADS, D_KV_HEADS = 16, 4

FF_CHUNK = 256
N_FF_CHUNKS = D_FF // FF_CHUNK
TOKEN_TILE = 512
VMEM_LIMIT = 56 * 1024 * 1024


def _alibi_slopes(n):
    return 2.0 ** (-8.0 * np.arange(1, n + 1) / n)


def _params(sem, vmem=VMEM_LIMIT):
    return pltpu.CompilerParams(dimension_semantics=sem, vmem_limit_bytes=vmem)


def _const_spec(shape):
    nd = len(shape)
    return pl.BlockSpec(tuple(shape), lambda *_: (0,) * nd, pipeline_mode=pl.Buffered(1))


def _seq_bounds(row0, p_len, s_len):
    in_prompt = row0 < p_len
    b = lax.shift_right_logical(jnp.maximum(row0 - p_len, 0), int(math.log2(s_len)))
    lo = jnp.where(in_prompt, 0, p_len + b * s_len)
    hi = jnp.where(in_prompt, p_len, lo + s_len)
    return lo, hi


def _rms(x, g):
    return x * lax.rsqrt(jnp.mean(x * x, axis=-1, keepdims=True) + NORM_EPS) * g


def _dot(a, b):
    return jnp.dot(a, b, preferred_element_type=F32)


def _dot_nt(a, b):
    return lax.dot_general(a, b, (((1,), (1,)), ((), ())), preferred_element_type=F32)


def _ffn_kernel(*refs, mode):
    if mode == "plain":
        x_ref, gn_ref, wg_ref, wu_ref, wd_ref, out_ref, acc_ref = refs
        x = x_ref[...]
    elif mode == "proj":
        x_ref, o_ref, wo_ref, gn_ref, wg_ref, wu_ref, wd_ref, out_ref, acc_ref = refs
        x = x_ref[...] + _dot(o_ref[...], wo_ref[...])
    else:
        (x_ref, o0, o1, o2, l0, l1, l2, wo_ref, gn_ref, wg_ref, wu_ref, wd_ref,
         out_ref, acc_ref, *scr) = refs
        tm, width = x_ref.shape[0], wo_ref.shape[0]

        def token_major(ref, dil, scr_ref):
            slabs = width // LANES
            for r in range(dil):
                for s in range(slabs):
                    col = r * width + s * LANES
                    scr_ref[s, pl.ds(r, tm // dil, stride=dil), :] = ref[:, col:col + LANES]
            return jnp.concatenate([scr_ref[s] for s in range(slabs)], axis=-1)

        dils = [d for _, d in C_PATTERNS]
        assert dils[0] == 1
        oa, la = o0[...], l0[...]
        ob, lb = token_major(o1, dils[1], scr[0]), token_major(l1, dils[1], scr[1])
        oc, lc = token_major(o2, dils[2], scr[2]), token_major(l2, dils[2], scr[3])
        m = jnp.maximum(jnp.maximum(la, lb), lc)
        ea, eb, ec = jnp.exp(la - m), jnp.exp(lb - m), jnp.exp(lc - m)
        o = (ea * oa + eb * ob + ec * oc) / (ea + eb + ec)
        x = x_ref[...] + _dot(o.astype(BF16), wo_ref[...])

    h = _rms(x, gn_ref[...]).astype(BF16)
    acc_ref[...] = jnp.zeros_like(acc_ref)

    def chunk(c, carry):
        g = _dot(h, wg_ref[c])
        u = _dot(h, wu_ref[c])
        a = (g * jax.nn.sigmoid(g) * u).astype(BF16)
        acc_ref[...] += _dot(a, wd_ref[c])
        return carry

    lax.fori_loop(0, N_FF_CHUNKS, chunk, 0)
    out_ref[...] = x + 0.5 * acc_ref[...]


def _ffn_weights(wg, wu, wd):
    wg3 = wg.astype(BF16).reshape(D_MODEL, N_FF_CHUNKS, FF_CHUNK).transpose(1, 0, 2)
    wu3 = wu.astype(BF16).reshape(D_MODEL, N_FF_CHUNKS, FF_CHUNK).transpose(1, 0, 2)
    wd3 = wd.astype(BF16).reshape(N_FF_CHUNKS, FF_CHUNK, D_MODEL)
    return wg3, wu3, wd3


def _ffn_call(x, gn, wg, wu, wd, pre=None):
    t_rows = x.shape[0]
    tm = TOKEN_TILE
    wg3, wu3, wd3 = _ffn_weights(wg, wu, wd)
    row = lambda w: pl.BlockSpec((tm, w), lambda i: (i, 0))
    ffn_args = [gn.reshape(1, D_MODEL), wg3, wu3, wd3]
    ffn_specs = [_const_spec((1, D_MODEL)), _const_spec(wg3.shape), _const_spec(wu3.shape),
                 _const_spec(wd3.shape)]
    scratch = []
    if pre is None:
        mode, args, specs = "plain", [x], [row(D_MODEL)]
    elif len(pre) == 2:
        o, wo = pre
        wo = wo.astype(BF16)
        mode = "proj"
        args = [x, o, wo]
        specs = [row(D_MODEL), row(o.shape[1]), _const_spec(wo.shape)]
    else:
        outs, lses, wo = pre
        wo = wo.astype(BF16)
        mode = "merge"
        args = [x, *outs, *lses, wo]
        width = wo.shape[0]
        dilated = [pl.BlockSpec((tm // d, d * width), lambda i: (i, 0)) for _, d in C_PATTERNS]
        specs = [row(D_MODEL)] + dilated * 2 + [_const_spec(wo.shape)]
        scratch = [pltpu.VMEM((width // LANES, tm, LANES), F32)] * 4
    return pl.pallas_call(
        functools.partial(_ffn_kernel, mode=mode),
        grid=(t_rows // tm,),
        in_specs=specs + ffn_specs,
        out_specs=row(D_MODEL),
        out_shape=jax.ShapeDtypeStruct((t_rows, D_MODEL), F32),
        scratch_shapes=[pltpu.VMEM((tm, D_MODEL), F32)] + scratch,
        compiler_params=_params(("parallel",)),
        name="ffn_" + mode,
    )(*args, *ffn_args)


def _head_norm(t, width):
    return lax.rsqrt(jnp.sum(t * t, axis=-1, keepdims=True) * (1.0 / width) + NORM_EPS)


def _proj_a_kernel(x_ref, gn_ref, w_ref, gq_ref, gk_ref, q_ref, k_ref, v_ref):
    hd = HEAD_DIM
    h = _rms(x_ref[...], gn_ref[...]).astype(BF16)
    y = _dot(h, w_ref[...])
    gq, gk = gq_ref[...], gk_ref[...]
    for n in range(A_HEADS):
        t = y[:, n * hd:(n + 1) * hd]
        q_ref[n] = (t * _head_norm(t, hd) * gq).astype(BF16)
    base = A_HEADS * hd
    for n in range(A_KV_HEADS):
        t = y[:, base + n * hd: base + (n + 1) * hd]
        k_ref[n] = (t * _head_norm(t, hd) * gk).astype(BF16)
    base += A_KV_HEADS * hd
    for n in range(A_KV_HEADS):
        v_ref[n] = y[:, base + n * hd: base + (n + 1) * hd].astype(BF16)


def _proj_a_call(x, gn, wqkv, gq, gk):
    t_rows = x.shape[0]
    tm, hd = TOKEN_TILE, HEAD_DIM
    w = wqkv.astype(BF16)
    head_major = lambda n: pl.BlockSpec((n, tm, hd), lambda i: (0, i, 0))
    return pl.pallas_call(
        _proj_a_kernel,
        grid=(t_rows // tm,),
        in_specs=[pl.BlockSpec((tm, D_MODEL), lambda i: (i, 0)), _const_spec((1, D_MODEL)),
                  _const_spec(w.shape), _const_spec((1, hd)), _const_spec((1, hd))],
        out_specs=[head_major(A_HEADS), head_major(A_KV_HEADS), head_major(A_KV_HEADS)],
        out_shape=[jax.ShapeDtypeStruct((A_HEADS, t_rows, hd), BF16),
                   jax.ShapeDtypeStruct((A_KV_HEADS, t_rows, hd), BF16),
                   jax.ShapeDtypeStruct((A_KV_HEADS, t_rows, hd), BF16)],
        compiler_params=_params(("parallel",)),
        name="proj_a",
    )(x, gn.reshape(1, D_MODEL), w, (gq * (hd ** -0.5 * LOG2E)).reshape(1, hd), gk.reshape(1, hd))


def _proj_d_kernel(x_ref, gn_ref, w_ref, cq_ref, sq_ref, ck_ref, sk_ref, q_ref, k_ref, v_ref):
    hd = HEAD_DIM
    nq, nk = D_HEADS * hd, D_KV_HEADS * hd
    h = _rms(x_ref[...], gn_ref[...]).astype(BF16)
    y = _dot(h, w_ref[...])
    cq, sq, ck, sk = cq_ref[...], sq_ref[...], ck_ref[...], sk_ref[...]
    swapped = nq + 2 * nk
    for n in range(D_HEADS):
        t = y[:, n * hd:(n + 1) * hd]
        ts = y[:, swapped + n * hd: swapped + (n + 1) * hd]
        q_ref[n] = ((t * cq + ts * sq) * _head_norm(t, hd)).astype(BF16)
    for n in range(D_KV_HEADS):
        t = y[:, nq + n * hd: nq + (n + 1) * hd]
        ts = y[:, swapped + nq + n * hd: swapped + nq + (n + 1) * hd]
        k_ref[n] = ((t * ck + ts * sk) * _head_norm(t, hd)).astype(BF16)
    for n in range(D_KV_HEADS):
        v_ref[n] = y[:, nq + nk + n * hd: nq + nk + (n + 1) * hd].astype(BF16)


def _swap_halves_cols(w, group):
    k, n = w.shape
    return w.reshape(k, n // group, 2, group // 2)[:, :, ::-1, :].reshape(k, n)


def _rope_tables(pos, gain, scale):
    half = gain.shape[-1] // 2
    inv = ROPE_THETA ** (-jnp.arange(half, dtype=F32) / half)
    ang = pos.astype(F32)[:, None] * inv[None, :]
    cos, sin = jnp.cos(ang), jnp.sin(ang)
    g_sw = jnp.concatenate([gain[half:], gain[:half]])
    c = jnp.concatenate([cos, cos], axis=-1) * gain[None, :] * scale
    s = jnp.concatenate([-sin, sin], axis=-1) * g_sw[None, :] * scale
    return c, s


def _proj_d_call(x, gn, wqkv, gq, gk, pos):
    t_rows = x.shape[0]
    tm, hd = TOKEN_TILE, HEAD_DIM
    nq, nk = D_HEADS * hd, D_KV_HEADS * hd
    half = hd // 2
    w = jnp.concatenate([wqkv, _swap_halves_cols(wqkv[:, :nq + nk], half)], axis=1).astype(BF16)
    rows, cols = pos // GRID_W, pos % GRID_W

    def tables(g, scale):
        cr, sr = _rope_tables(rows, g[:half], scale)
        cc, sc = _rope_tables(cols, g[half:], scale)
        return jnp.concatenate([cr, cc], axis=-1), jnp.concatenate([sr, sc], axis=-1)

    cq, sq = tables(gq, hd ** -0.5 * LOG2E)
    ck, sk = tables(gk, 1.0)
    head_major = lambda n: pl.BlockSpec((n, tm, hd), lambda i: (0, i, 0))
    tab = pl.BlockSpec((tm, hd), lambda i: (i, 0))
    return pl.pallas_call(
        _proj_d_kernel,
        grid=(t_rows // tm,),
        in_specs=[pl.BlockSpec((tm, D_MODEL), lambda i: (i, 0)), _const_spec((1, D_MODEL)),
                  _const_spec(w.shape), tab, tab, tab, tab],
        out_specs=[head_major(D_HEADS), head_major(D_KV_HEADS), head_major(D_KV_HEADS)],
        out_shape=[jax.ShapeDtypeStruct((D_HEADS, t_rows, hd), BF16),
                   jax.ShapeDtypeStruct((D_KV_HEADS, t_rows, hd), BF16),
                   jax.ShapeDtypeStruct((D_KV_HEADS, t_rows, hd), BF16)],
        compiler_params=_params(("parallel",)),
        name="proj_d",
    )(x, gn.reshape(1, D_MODEL), w, cq, sq, ck, sk)


def _proj_b_kernel(x_ref, gn_ref, w1_ref, gcq_ref, gckv_ref, wq_ref, wqs_ref, wk_ref, wv_ref,
                   cq_ref, sq_ref, ck_ref, sk_ref, q_ref, k_ref, v_ref):
    h = _rms(x_ref[...], gn_ref[...]).astype(BF16)
    y1 = _dot(h, w1_ref[...])
    c_q = _rms(y1[:, :B_Q_LORA], gcq_ref[...]).astype(BF16)
    c_kv = _rms(y1[:, B_Q_LORA:B_Q_LORA + B_KV_LORA], gckv_ref[...]).astype(BF16)
    off = B_Q_LORA + B_KV_LORA
    kr = y1[:, off:off + B_PAD]
    krs = y1[:, off + B_PAD:off + 2 * B_PAD]
    q = _dot(c_q, wq_ref[...])
    qs = _dot(c_q, wqs_ref[...])
    kn = _dot(c_kv, wk_ref[...])
    v = _dot(c_kv, wv_ref[...])
    cq, sq, ck, sk = cq_ref[...], sq_ref[...], ck_ref[...], sk_ref[...]
    krs_s = krs * sk
    for n in range(B_HEADS):
        t = q[:, n * B_PAD:(n + 1) * B_PAD]
        ts = qs[:, n * B_PAD:(n + 1) * B_PAD]
        q_ref[n] = ((t * cq + ts * sq) * _head_norm(t, B_QK)).astype(BF16)
        t = kn[:, n * B_PAD:(n + 1) * B_PAD] + kr
        k_ref[n] = ((t * ck + krs_s) * _head_norm(t, B_QK)).astype(BF16)
        v_ref[n] = v[:, n * B_V:(n + 1) * B_V].astype(BF16)


def _pad_heads(w, lo, width):
    k, nh, _ = w.shape
    out = jnp.zeros((k, nh, B_PAD), w.dtype).at[:, :, lo:lo + width].set(w)
    return out.reshape(k, nh * B_PAD)


def _proj_b_call(x, gn, wdq, gcq, wuq, wdkv, gckv, wukv, gq, gk, pos):
    t_rows = x.shape[0]
    tm = TOKEN_TILE
    wuq3 = wuq.reshape(B_Q_LORA, B_HEADS, B_QK)
    wq = _pad_heads(wuq3, 0, B_QK).astype(BF16)
    wuq_rope_sw = _swap_halves_cols(wuq3[:, :, B_NOPE:].reshape(B_Q_LORA, -1), B_ROPE)
    wqs = _pad_heads(wuq_rope_sw.reshape(B_Q_LORA, B_HEADS, B_ROPE), B_NOPE, B_ROPE).astype(BF16)
    wukv3 = wukv.reshape(B_KV_LORA, B_HEADS, B_NOPE + B_V)
    wk = _pad_heads(wukv3[:, :, :B_NOPE], 0, B_NOPE).astype(BF16)
    wv = wukv3[:, :, B_NOPE:].reshape(B_KV_LORA, B_HEADS * B_V).astype(BF16)
    w_kr = wdkv[:, B_KV_LORA:]
    kr_pad = _pad_heads(w_kr[:, None, :], B_NOPE, B_ROPE)
    krs_pad = _pad_heads(_swap_halves_cols(w_kr, B_ROPE)[:, None, :], B_NOPE, B_ROPE)
    w1 = jnp.concatenate([wdq, wdkv[:, :B_KV_LORA], kr_pad, krs_pad], axis=1).astype(BF16)

    def tables(g, scale):
        c, s = _rope_tables(pos, g[B_NOPE:], scale)
        zeros = jnp.zeros((t_rows, B_PAD - B_QK), F32)
        c_full = jnp.concatenate(
            [jnp.broadcast_to(g[None, :B_NOPE] * scale, (t_rows, B_NOPE)), c, zeros], axis=-1)
        s_full = jnp.concatenate([jnp.zeros((t_rows, B_NOPE), F32), s, zeros], axis=-1)
        return c_full, s_full

    cq, sq = tables(gq, B_QK ** -0.5 * LOG2E)
    ck, sk = tables(gk, 1.0)
    tab = pl.BlockSpec((tm, B_PAD), lambda i: (i, 0))
    head_major = lambda w_: pl.BlockSpec((B_HEADS, tm, w_), lambda i: (0, i, 0))
    return pl.pallas_call(
        _proj_b_kernel,
        grid=(t_rows // tm,),
        in_specs=[pl.BlockSpec((tm, D_MODEL), lambda i: (i, 0)), _const_spec((1, D_MODEL)),
                  _const_spec(w1.shape), _const_spec((1, B_Q_LORA)), _const_spec((1, B_KV_LORA)),
                  _const_spec(wq.shape), _const_spec(wqs.shape), _const_spec(wk.shape),
                  _const_spec(wv.shape), tab, tab, tab, tab],
        out_specs=[head_major(B_PAD), head_major(B_PAD), head_major(B_V)],
        out_shape=[jax.ShapeDtypeStruct((B_HEADS, t_rows, B_PAD), BF16),
                   jax.ShapeDtypeStruct((B_HEADS, t_rows, B_PAD), BF16),
                   jax.ShapeDtypeStruct((B_HEADS, t_rows, B_V), BF16)],
        compiler_params=_params(("parallel",)),
        name="proj_b",
    )(x, gn.reshape(1, D_MODEL), w1, gcq.reshape(1, -1), gckv.reshape(1, -1), wq, wqs, wk, wv,
      cq, sq, ck, sk)


def _proj_c_kernel(x_ref, gn_ref, w_ref, gq_ref, gk_ref, *refs):
    hd = HEAD_DIM
    width = C_HEADS * hd
    outs, scr_ref = refs[:-1], refs[-1]
    tm = x_ref.shape[0]
    h = _rms(x_ref[...], gn_ref[...]).astype(BF16)
    y = _dot(h, w_ref[...])
    gains = (gq_ref[...], gk_ref[...])
    for g, (_, dil) in enumerate(C_PATTERNS):
        for j in range(3):
            base = (g * 3 + j) * width
            if j < 2:
                heads = []
                for n in range(C_HEADS):
                    t = y[:, base + n * hd: base + (n + 1) * hd]
                    heads.append(t * _head_norm(t, hd) * gains[j])
                val = jnp.concatenate(heads, axis=-1)
            else:
                val = y[:, base: base + width]
            out_ref = outs[g * 3 + j]
            if dil == 1:
                out_ref[...] = val.astype(BF16)
            else:
                for s in range(width // LANES):
                    scr_ref[s] = val[:, s * LANES:(s + 1) * LANES]
                for r in range(dil):
                    for s in range(width // LANES):
                        rows = scr_ref[s, pl.ds(r, tm // dil, stride=dil), :]
                        col = r * width + s * LANES
                        out_ref[:, col:col + LANES] = rows.astype(BF16)


def _proj_c_call(x, gn, wqkv, gq, gk):
    t_rows = x.shape[0]
    tm, hd = TOKEN_TILE // 2, HEAD_DIM
    width = C_HEADS * hd
    w = wqkv.astype(BF16)
    specs, shapes = [], []
    for _, dil in C_PATTERNS:
        specs += [pl.BlockSpec((tm // dil, dil * width), lambda i: (i, 0))] * 3
        shapes += [jax.ShapeDtypeStruct((t_rows // dil, dil * width), BF16)] * 3
    outs = pl.pallas_call(
        _proj_c_kernel,
        grid=(t_rows // tm,),
        in_specs=[pl.BlockSpec((tm, D_MODEL), lambda i: (i, 0)), _const_spec((1, D_MODEL)),
                  _const_spec(w.shape), _const_spec((1, hd)), _const_spec((1, hd))],
        out_specs=specs,
        out_shape=shapes,
        scratch_shapes=[pltpu.VMEM((width // LANES, tm, LANES), F32)],
        compiler_params=_params(("parallel",)),
        name="proj_c",
    )(x, gn.reshape(1, D_MODEL), w, (gq * (hd ** -0.5 * LOG2E)).reshape(1, hd), gk.reshape(1, hd))
    return [outs[3 * g:3 * g + 3] for g in range(len(C_PATTERNS))]


def _band_masked_dist(row0, lo, hi, sub, w):
    nk = sub + 2 * w
    r = lax.broadcasted_iota(jnp.int32, (sub, nk), 0)
    c = lax.broadcasted_iota(jnp.int32, (sub, nk), 1)
    dist = jnp.abs(r + w - c)
    jpos = row0 - w + lax.broadcasted_iota(jnp.int32, (1, nk), 1)
    outside = jnp.where(jpos < lo, w + 1, 0) + jnp.where(jpos >= hi, w + 1, 0)
    return jnp.where(dist + outside <= w, dist.astype(F32), MASK_DIST)


def _two_stage(n_items, first, second, ahead=2):
    pending = {i: first(i) for i in range(min(ahead, n_items))}
    results = []
    for i in range(n_items):
        if i + ahead < n_items:
            pending[i + ahead] = first(i + ahead)
        results.append(second(i, pending.pop(i)))
    return results


def _band_softmax_pv(s, vv, sink):
    m = jnp.max(s, axis=-1, keepdims=True)
    if sink is not None:
        m = jnp.maximum(m, sink)
    p = jnp.exp2(s - m)
    den = jnp.sum(p, axis=-1, keepdims=True)
    if sink is not None:
        den = den + jnp.exp2(sink - m)
    return _dot(p.astype(BF16), vv) / den, m + jnp.log2(den)


def _attn_a_kernel(slope_ref, sink_ref, q_ref, kp_ref, kc_ref, kn_ref, vp_ref, vc_ref, vn_ref,
                   o_ref, *, tq, p_len, s_len):
    w, sub = A_HALF_WINDOW, BAND_SUB
    group = A_HEADS // A_KV_HEADS
    kvh = pl.program_id(0)
    row0 = pl.program_id(1) * tq
    lo, hi = _seq_bounds(row0, p_len, s_len)
    kk = jnp.concatenate([kp_ref[0], kc_ref[0], kn_ref[0]], axis=0)
    vv = jnp.concatenate([vp_ref[0], vc_ref[0], vn_ref[0]], axis=0)
    sink = jnp.concatenate(
        [jnp.full((sub, 1), sink_ref[kvh * group + g], F32) for g in range(group)], axis=0)

    def keys(r):
        return slice(r * sub, (r + 1) * sub + 2 * w)

    def logits(r):
        md = _band_masked_dist(row0 + r * sub, lo, hi, sub, w)
        bias = jnp.concatenate([md * slope_ref[kvh * group + g] for g in range(group)], axis=0)
        q = q_ref[:, r * sub:(r + 1) * sub, :].reshape(group * sub, HEAD_DIM)
        return _dot_nt(q, kk[keys(r)]) + bias

    tiles = _two_stage(tq // sub, logits, lambda r, s: _band_softmax_pv(s, vv[keys(r)], sink)[0])
    outs = [[o[g * sub:(g + 1) * sub] for o in tiles] for g in range(group)]
    o_ref[...] = jnp.concatenate(
        [jnp.concatenate(rows, axis=0) for rows in outs], axis=-1).astype(BF16)


def _attn_a_call(q, k, v, sink, geo):
    p_len, s_len, t_rows = geo
    tq, w, hd = 512, A_HALF_WINDOW, HEAD_DIM
    group = A_HEADS // A_KV_HEADS
    per = tq // w
    last = t_rows // w - 1
    prev = pl.BlockSpec((1, w, hd), lambda h, i: (h, jnp.maximum(i * per - 1, 0), 0))
    cur = pl.BlockSpec((1, tq, hd), lambda h, i: (h, i, 0))
    nxt = pl.BlockSpec((1, w, hd), lambda h, i: (h, jnp.minimum((i + 1) * per, last), 0))
    smem = pl.BlockSpec(memory_space=pltpu.SMEM)
    slopes = jnp.asarray(-LOG2E * _alibi_slopes(A_HEADS), F32)
    return pl.pallas_call(
        functools.partial(_attn_a_kernel, tq=tq, p_len=p_len, s_len=s_len),
        grid=(A_KV_HEADS, t_rows // tq),
        in_specs=[smem, smem, pl.BlockSpec((group, tq, hd), lambda h, i: (h, i, 0)),
                  prev, cur, nxt, prev, cur, nxt],
        out_specs=pl.BlockSpec((tq, group * hd), lambda h, i: (i, h)),
        out_shape=jax.ShapeDtypeStruct((t_rows, A_HEADS * hd), BF16),
        compiler_params=_params(("parallel", "parallel")),
        name="attn_a",
    )(slopes, sink.astype(F32) * LOG2E, q, k, k, k, v, v, v)


def _attn_c_kernel(q_ref, kp_ref, kc_ref, kn_ref, vp_ref, vc_ref, vn_ref, o_ref, lse_ref,
                   *, tq, dil, p_len, s_len):
    w, hd, sub = C_STEPS, HEAD_DIM, min(BAND_SUB, tq)
    row0 = pl.program_id(0) * tq
    lo, hi = _seq_bounds(row0, p_len // dil, s_len // dil)
    slopes = -LOG2E * dil * _alibi_slopes(C_HEADS)
    mds = [_band_masked_dist(row0 + r * sub, lo, hi, sub, w) for r in range(tq // sub)]
    n_sub = tq // sub

    def window(prev_ref, cur_ref, next_ref, i):
        n, r = divmod(i, n_sub)
        hs = slice(n * hd, (n + 1) * hd)
        rows = jnp.concatenate([prev_ref[:, hs], cur_ref[:, hs], next_ref[:, hs]], axis=0)
        return rows[r * sub:(r + 1) * sub + 2 * w]

    def logits(i):
        n, r = divmod(i, n_sub)
        q = q_ref[r * sub:(r + 1) * sub, n * hd:(n + 1) * hd]
        return _dot_nt(q, window(kp_ref, kc_ref, kn_ref, i)) + mds[r] * float(slopes[n])

    def attend(i, s):
        o, lse2 = _band_softmax_pv(s, window(vp_ref, vc_ref, vn_ref, i), None)
        return o, jnp.broadcast_to(lse2 * (1.0 / LOG2E), (sub, hd))

    tiles = _two_stage(C_HEADS * n_sub, logits, attend)
    outs = [jnp.concatenate([t[0] for t in tiles[n * n_sub:(n + 1) * n_sub]], axis=0)
            for n in range(C_HEADS)]
    lses = [jnp.concatenate([t[1] for t in tiles[n * n_sub:(n + 1) * n_sub]], axis=0)
            for n in range(C_HEADS)]
    o_ref[...] = jnp.concatenate(outs, axis=-1)
    lse_ref[...] = jnp.concatenate(lses, axis=-1)


def _attn_c_call(q, k, v, dil, geo):
    p_len, s_len, t_rows = geo
    w, width = C_STEPS, C_HEADS * HEAD_DIM
    rows = t_rows // dil
    tq = min(512, s_len // dil)
    per = tq // w
    last = rows // w - 1
    prev = pl.BlockSpec((w, width), lambda i, r: (jnp.maximum(i * per - 1, 0), r))
    cur = pl.BlockSpec((tq, width), lambda i, r: (i, r))
    nxt = pl.BlockSpec((w, width), lambda i, r: (jnp.minimum((i + 1) * per, last), r))
    shape = jax.ShapeDtypeStruct((rows, dil * width), F32)
    return pl.pallas_call(
        functools.partial(_attn_c_kernel, tq=tq, dil=dil, p_len=p_len, s_len=s_len),
        grid=(rows // tq, dil),
        in_specs=[cur, prev, cur, nxt, prev, cur, nxt],
        out_specs=[cur, cur],
        out_shape=[shape, shape],
        compiler_params=_params(("parallel", "parallel")),
        name="attn_c_d%d" % dil,
    )(q, k, k, k, v, v, v)


def _attn_dense_kernel(qt_ref, k_ref, vt_ref, o_ref, sa_ref, sb_ref, sc_ref, sd_ref,
                       *, nq, tq, tk, p_len, s_len):
    dv = o_ref.shape[-2]
    width = nq * tq
    row0 = pl.program_id(1) * tq
    lo, hi = _seq_bounds(row0, p_len, s_len)
    shift = int(math.log2(tk))
    first = lax.shift_right_logical(lo, shift)
    n_chunks = lax.shift_right_logical(hi - lo, shift)

    def scores(c, st_ref):
        start = pl.multiple_of(c * tk, tk)
        qt = jnp.concatenate([qt_ref[g] for g in range(nq)], axis=-1)
        st = _dot(k_ref[0, pl.ds(start, tk), :], qt).astype(BF16)
        st_ref[...] = st
        return jnp.max(st, axis=0, keepdims=True).astype(F32)

    def update(c, st_ref, mc, m, acc):
        m_new = jnp.maximum(m, mc)
        alpha = jnp.exp2(m - m_new)
        pt = jnp.exp2(st_ref[...] - m_new.astype(BF16))
        return m_new, alpha * acc + _dot(vt_ref[0, c], pt)

    def quad(i, carry):
        m, acc, mc_a, mc_b = carry
        c = first + 4 * i
        mc_c = scores(c + 2, sc_ref)
        m, acc = update(c, sa_ref, mc_a, m, acc)
        mc_d = scores(c + 3, sd_ref)
        m, acc = update(c + 1, sb_ref, mc_b, m, acc)
        mc_a = scores(c + 4, sa_ref)
        m, acc = update(c + 2, sc_ref, mc_c, m, acc)
        mc_b = scores(c + 5, sb_ref)
        m, acc = update(c + 3, sd_ref, mc_d, m, acc)
        return m, acc, mc_a, mc_b

    init = (jnp.full((1, width), NEG_INF, F32), jnp.zeros((vt_ref.shape[-2], width), F32),
            scores(first, sa_ref), scores(first + 1, sb_ref))
    n_quads = lax.shift_right_logical(n_chunks, 2)
    m, acc, mc_a, mc_b = lax.fori_loop(0, n_quads - 1, quad, init)
    c = first + n_chunks - 4
    mc_c = scores(c + 2, sc_ref)
    m, acc = update(c, sa_ref, mc_a, m, acc)
    mc_d = scores(c + 3, sd_ref)
    m, acc = update(c + 1, sb_ref, mc_b, m, acc)
    m, acc = update(c + 2, sc_ref, mc_c, m, acc)
    m, acc = update(c + 3, sd_ref, mc_d, m, acc)
    o = (acc[:dv] / acc[dv:dv + 1]).astype(BF16)
    for g in range(nq):
        o_ref[g] = o[:, g * tq:(g + 1) * tq]


def _attn_dense_call(q, k, v, geo, tq, tk=512):
    p_len, s_len, t_rows = geo
    hq, _, dq = q.shape
    hk, _, dv = v.shape
    nq = hq // hk
    n_chunks = t_rows // tk
    assert s_len % (4 * tk) == 0 and p_len % (4 * tk) == 0
    qt = q.transpose(0, 2, 1)
    vt = v.reshape(hk, n_chunks, tk, dv).transpose(0, 1, 3, 2)
    ones_rows = jnp.zeros((hk, n_chunks, BF16_SUBLANES, tk), BF16).at[:, :, 0].set(1.0)
    vt = jnp.concatenate([vt, ones_rows], axis=2)
    o = pl.pallas_call(
        functools.partial(_attn_dense_kernel, nq=nq, tq=tq, tk=tk, p_len=p_len, s_len=s_len),
        grid=(hk, t_rows // tq),
        in_specs=[pl.BlockSpec((nq, dq, tq), lambda h, i: (h, 0, i)),
                  pl.BlockSpec((1, t_rows, dq), lambda h, i: (h, 0, 0), pipeline_mode=pl.Buffered(1)),
                  pl.BlockSpec((1, n_chunks, dv + BF16_SUBLANES, tk), lambda h, i: (h, 0, 0, 0),
                               pipeline_mode=pl.Buffered(1))],
        out_specs=pl.BlockSpec((nq, dv, tq), lambda h, i: (h, 0, i)),
        out_shape=jax.ShapeDtypeStruct((hq, dv, t_rows), BF16),
        scratch_shapes=[pltpu.VMEM((tk, nq * tq), BF16)] * 4,
        compiler_params=_params(("parallel", "parallel")),
        name="attn_dense_%d" % dq,
    )(qt, k, vt)
    return o.transpose(2, 0, 1).reshape(t_rows, hq * dv)


def _trunk(x, p, geo, pos):
    for i in range(DEPTH):
        mixer, j = i % 4, i // 4
        x = _ffn_call(x, p["norm_ffn1"][i], p["ffn1_wg"][i], p["ffn1_wu"][i], p["ffn1_wd"][i])
        gn = p["norm_mix"][i]
        if mixer == 0:
            q, k, v = _proj_a_call(x, gn, p["a_wqkv"][j], p["a_gq"][j], p["a_gk"][j])
            pre = (_attn_a_call(q, k, v, p["a_sink"][j], geo), p["a_wo"][j])
        elif mixer == 1:
            q, k, v = _proj_b_call(x, gn, p["b_wdq"][j], p["b_gcq"][j], p["b_wuq"][j], p["b_wdkv"][j],
                                   p["b_gckv"][j], p["b_wukv"][j], p["b_gq"][j], p["b_gk"][j], pos)
            pre = (_attn_dense_call(q, k, v, geo, tq=512), p["b_wo"][j])
        elif mixer == 2:
            qkv = _proj_c_call(x, gn, p["c_wqkv"][j], p["c_gq"][j], p["c_gk"][j])
            res = [_attn_c_call(*qkv[g], dil, geo) for g, (_, dil) in enumerate(C_PATTERNS)]
            pre = ([r[0] for r in res], [r[1] for r in res], p["c_wo"][j])
        else:
            q, k, v = _proj_d_call(x, gn, p["d_wqkv"][j], p["d_gq"][j], p["d_gk"][j], pos)
            pre = (_attn_dense_call(q, k, v, geo, tq=128), p["d_wo"][j])
        x = _ffn_call(x, p["norm_ffn2"][i], p["ffn2_wg"][i], p["ffn2_wu"][i], p["ffn2_wd"][i], pre=pre)
    return x


def kernel(x_prompt, x_sample, norm_ffn1, ffn1_wg, ffn1_wu, ffn1_wd, norm_mix, norm_ffn2, ffn2_wg, ffn2_wu, ffn2_wd, a_wqkv, a_gq, a_gk, a_sink, a_wo, b_wdq, b_gcq, b_wuq, b_wdkv, b_gckv, b_wukv, b_gq, b_gk, b_wo, c_wqkv, c_gq, c_gk, c_wo, d_wqkv, d_gq, d_gk, d_wo):
    p = dict(norm_ffn1=norm_ffn1, ffn1_wg=ffn1_wg, ffn1_wu=ffn1_wu, ffn1_wd=ffn1_wd,
             norm_mix=norm_mix, norm_ffn2=norm_ffn2, ffn2_wg=ffn2_wg, ffn2_wu=ffn2_wu,
             ffn2_wd=ffn2_wd, a_wqkv=a_wqkv, a_gq=a_gq, a_gk=a_gk, a_sink=a_sink, a_wo=a_wo,
             b_wdq=b_wdq, b_gcq=b_gcq, b_wuq=b_wuq, b_wdkv=b_wdkv, b_gckv=b_gckv,
             b_wukv=b_wukv, b_gq=b_gq, b_gk=b_gk, b_wo=b_wo, c_wqkv=c_wqkv, c_gq=c_gq,
             c_gk=c_gk, c_wo=c_wo, d_wqkv=d_wqkv, d_gq=d_gq, d_gk=d_gk, d_wo=d_wo)
    n_p, p_seq, _ = x_prompt.shape
    n_s, s_len, _ = x_sample.shape
    assert n_p == 1 and s_len & (s_len - 1) == 0 and p_seq % s_len == 0
    p_len = n_p * p_seq
    t_rows = p_len + n_s * s_len
    geo = (p_len, s_len, t_rows)
    pos = jnp.concatenate([jnp.arange(p_len, dtype=jnp.int32),
                           jnp.tile(jnp.arange(s_len, dtype=jnp.int32), n_s)])
    x = jnp.concatenate([x_prompt.reshape(p_len, D_MODEL), x_sample.reshape(n_s * s_len, D_MODEL)])
    y = _trunk(x, p, geo, pos)
    return y[:p_len].reshape(x_prompt.shape), y[p_len:].reshape(x_sample.shape)
```

```python
import functools
import math

import numpy as np
import jax
import jax.numpy as jnp
from jax import lax
from jax.experimental import pallas as pl
from jax.experimental.pallas import tpu as pltpu

F32 = jnp.float32
BF16 = jnp.bfloat16

D_MODEL = 1024
DEPTH = 4
HEAD_DIM = 64
D_FF = 2816
NORM_EPS = 1e-6
NEG_INF = -1e30
GRID_W = 64
ROPE_THETA = 10000.0
LOG2E = math.log2(math.e)
LANES = 128
MXU_TILE = 256
BF16_SUBLANES = 16
BAND_SUB = 128
MASK_DIST = 1e33

A_HEADS, A_KV_HEADS, A_HALF_WINDOW = 16, 4, 128
B_HEADS, B_Q_LORA, B_KV_LORA, B_NOPE, B_ROPE, B_V = 16, 512, 256, 64, 32, 64
B_QK = B_NOPE + B_ROPE
B_PAD = 128
C_HEADS = 8
C_PATTERNS = ((128, 1), (512, 4), (2048, 16))
C_STEPS = 64
D_HEADS, D_KV_HEADS = 16, 4

FF_CHUNK = 256
N_FF_CHUNKS = D_FF // FF_CHUNK
TOKEN_TILE = 512
VMEM_LIMIT = 56 * 1024 * 1024


def _alibi_slopes(n):
    return 2.0 ** (-8.0 * np.arange(1, n + 1) / n)


def _params(sem, vmem=VMEM_LIMIT):
    return pltpu.CompilerParams(dimension_semantics=sem, vmem_limit_bytes=vmem)


def _const_spec(shape):
    nd = len(shape)
    return pl.BlockSpec(tuple(shape), lambda *_: (0,) * nd, pipeline_mode=pl.Buffered(1))


def _seq_bounds(row0, p_len, s_len):
    in_prompt = row0 < p_len
    b = lax.shift_right_logical(jnp.maximum(row0 - p_len, 0), int(math.log2(s_len)))
    lo = jnp.where(in_prompt, 0, p_len + b * s_len)
    hi = jnp.where(in_prompt, p_len, lo + s_len)
    return lo, hi


def _rms(x, g):
    return x * lax.rsqrt(jnp.mean(x * x, axis=-1, keepdims=True) + NORM_EPS) * g


def _dot(a, b):
    return jnp.dot(a, b, preferred_element_type=F32)


def _dot_nt(a, b):
    return lax.dot_general(a, b, (((1,), (1,)), ((), ())), preferred_element_type=F32)


def _ffn_kernel(*refs, mode):
    if mode == "plain":
        x_ref, gn_ref, wg_ref, wu_ref, wd_ref, out_ref, acc_ref, act_a, act_b = refs
        x = x_ref[...]
    elif mode == "proj":
        (x_ref, o_ref, wo_ref, gn_ref, wg_ref, wu_ref, wd_ref, out_ref, acc_ref, act_a,
         act_b) = refs
        x = x_ref[...] + _dot(o_ref[...], wo_ref[...])
    else:
        (x_ref, o0, o1, o2, l0, l1, l2, wo_ref, gn_ref, wg_ref, wu_ref, wd_ref,
         out_ref, acc_ref, act_a, act_b, *scr) = refs
        tm, width = x_ref.shape[0], wo_ref.shape[0]

        def token_major(ref, dil, scr_ref):
            slabs = width // LANES
            for r in range(dil):
                for s in range(slabs):
                    col = r * width + s * LANES
                    scr_ref[s, pl.ds(r, tm // dil, stride=dil), :] = ref[:, col:col + LANES]
            return jnp.concatenate([scr_ref[s] for s in range(slabs)], axis=-1)

        dils = [d for _, d in C_PATTERNS]
        assert dils[0] == 1
        oa, la = o0[...], l0[...]
        ob, lb = token_major(o1, dils[1], scr[0]), token_major(l1, dils[1], scr[1])
        oc, lc = token_major(o2, dils[2], scr[2]), token_major(l2, dils[2], scr[3])
        m = jnp.maximum(jnp.maximum(la, lb), lc)
        ea, eb, ec = jnp.exp(la - m), jnp.exp(lb - m), jnp.exp(lc - m)
        o = (ea * oa + eb * ob + ec * oc) / (ea + eb + ec)
        x = x_ref[...] + _dot(o.astype(BF16), wo_ref[...])

    h = _rms(x, gn_ref[...]).astype(BF16)

    def gate_up(c, a_ref):
        g = _dot(h, wg_ref[c])
        u = _dot(h, wu_ref[c])
        a_ref[...] = (g * jax.nn.sigmoid(g) * u).astype(BF16)

    def down(c, a_ref):
        acc_ref[...] += _dot(a_ref[...], wd_ref[c])

    assert N_FF_CHUNKS % 2 == 1
    acc_ref[...] = jnp.zeros_like(acc_ref)
    gate_up(0, act_a)

    def pair(i, carry):
        c = 2 * i
        gate_up(c + 1, act_b)
        down(c, act_a)
        gate_up(c + 2, act_a)
        down(c + 1, act_b)
        return carry

    lax.fori_loop(0, N_FF_CHUNKS // 2, pair, 0)
    down(N_FF_CHUNKS - 1, act_a)
    out_ref[...] = x + 0.5 * acc_ref[...]


def _ffn_weights(wg, wu, wd):
    wg3 = wg.astype(BF16).reshape(D_MODEL, N_FF_CHUNKS, FF_CHUNK).transpose(1, 0, 2)
    wu3 = wu.astype(BF16).reshape(D_MODEL, N_FF_CHUNKS, FF_CHUNK).transpose(1, 0, 2)
    wd3 = wd.astype(BF16).reshape(N_FF_CHUNKS, FF_CHUNK, D_MODEL)
    return wg3, wu3, wd3


def _ffn_call(x, gn, wg, wu, wd, pre=None):
    t_rows = x.shape[0]
    tm = TOKEN_TILE
    wg3, wu3, wd3 = _ffn_weights(wg, wu, wd)
    row = lambda w: pl.BlockSpec((tm, w), lambda i: (i, 0))
    ffn_args = [gn.reshape(1, D_MODEL), wg3, wu3, wd3]
    ffn_specs = [_const_spec((1, D_MODEL)), _const_spec(wg3.shape), _const_spec(wu3.shape),
                 _const_spec(wd3.shape)]
    scratch = []
    if pre is None:
        mode, args, specs = "plain", [x], [row(D_MODEL)]
    elif len(pre) == 2:
        o, wo = pre
        wo = wo.astype(BF16)
        mode = "proj"
        args = [x, o, wo]
        specs = [row(D_MODEL), row(o.shape[1]), _const_spec(wo.shape)]
    else:
        outs, lses, wo = pre
        wo = wo.astype(BF16)
        mode = "merge"
        args = [x, *outs, *lses, wo]
        width = wo.shape[0]
        dilated = [pl.BlockSpec((tm // d, d * width), lambda i: (i, 0)) for _, d in C_PATTERNS]
        specs = [row(D_MODEL)] + dilated * 2 + [_const_spec(wo.shape)]
        scratch = [pltpu.VMEM((width // LANES, tm, LANES), F32)] * 4
    return pl.pallas_call(
        functools.partial(_ffn_kernel, mode=mode),
        grid=(t_rows // tm,),
        in_specs=specs + ffn_specs,
        out_specs=row(D_MODEL),
        out_shape=jax.ShapeDtypeStruct((t_rows, D_MODEL), F32),
        scratch_shapes=[pltpu.VMEM((tm, D_MODEL), F32)] + [pltpu.VMEM((tm, FF_CHUNK), BF16)] * 2
        + scratch,
        compiler_params=_params(("parallel",)),
        name="ffn_" + mode,
    )(*args, *ffn_args)


def _head_norm(t, width):
    return lax.rsqrt(jnp.sum(t * t, axis=-1, keepdims=True) * (1.0 / width) + NORM_EPS)


def _segment_ones(seg):
    n = MXU_TILE // seg
    return jnp.kron(jnp.eye(n, dtype=F32), jnp.ones((seg, seg), F32)).astype(BF16)


def _segment_rsqrt(t, e_ref, width):
    sq = t * t
    hi = sq.astype(BF16)
    lo = (sq - hi.astype(F32)).astype(BF16)
    e = e_ref[...]
    sums = [_dot(hi[:, b:b + MXU_TILE], e) + _dot(lo[:, b:b + MXU_TILE], e)
            for b in range(0, t.shape[-1], MXU_TILE)]
    ssq = sums[0] if len(sums) == 1 else jnp.concatenate(sums, axis=-1)
    return lax.rsqrt(ssq * (1.0 / width) + NORM_EPS)


def _proj_a_kernel(x_ref, gn_ref, w_ref, gain_ref, e_ref, q_ref, k_ref, v_ref):
    hd = HEAD_DIM
    nqk = (A_HEADS + A_KV_HEADS) * hd
    h = _rms(x_ref[...], gn_ref[...]).astype(BF16)
    y = _dot(h, w_ref[...])
    qk = y[:, :nqk]
    qk = (qk * _segment_rsqrt(qk, e_ref, hd) * gain_ref[...]).astype(BF16)
    for n in range(A_HEADS):
        q_ref[n] = qk[:, n * hd:(n + 1) * hd]
    for n in range(A_KV_HEADS):
        k_ref[n] = qk[:, (A_HEADS + n) * hd:(A_HEADS + n + 1) * hd]
        v_ref[n] = y[:, nqk + n * hd: nqk + (n + 1) * hd].astype(BF16)


def _proj_a_call(x, gn, wqkv, gq, gk):
    t_rows = x.shape[0]
    tm, hd = TOKEN_TILE, HEAD_DIM
    w = wqkv.astype(BF16)
    gain = jnp.concatenate([jnp.tile(gq * (hd ** -0.5 * LOG2E), A_HEADS), jnp.tile(gk, A_KV_HEADS)])
    ones = _segment_ones(hd)
    head_major = lambda n: pl.BlockSpec((n, tm, hd), lambda i: (0, i, 0))
    return pl.pallas_call(
        _proj_a_kernel,
        grid=(t_rows // tm,),
        in_specs=[pl.BlockSpec((tm, D_MODEL), lambda i: (i, 0)), _const_spec((1, D_MODEL)),
                  _const_spec(w.shape), _const_spec((1, gain.shape[0])), _const_spec(ones.shape)],
        out_specs=[head_major(A_HEADS), head_major(A_KV_HEADS), head_major(A_KV_HEADS)],
        out_shape=[jax.ShapeDtypeStruct((A_HEADS, t_rows, hd), BF16),
                   jax.ShapeDtypeStruct((A_KV_HEADS, t_rows, hd), BF16),
                   jax.ShapeDtypeStruct((A_KV_HEADS, t_rows, hd), BF16)],
        compiler_params=_params(("parallel",)),
        name="proj_a",
    )(x, gn.reshape(1, D_MODEL), w, gain.reshape(1, -1), ones)


def _proj_d_kernel(x_ref, gn_ref, w_ref, e_ref, cq_ref, sq_ref, ck_ref, sk_ref, q_ref, k_ref, v_ref):
    hd = HEAD_DIM
    nq, nk = D_HEADS * hd, D_KV_HEADS * hd
    h = _rms(x_ref[...], gn_ref[...]).astype(BF16)
    y = _dot(h, w_ref[...])
    swapped = nq + 2 * nk
    t, ts = y[:, :nq + nk], y[:, swapped:swapped + nq + nk]
    r = _segment_rsqrt(t, e_ref, hd)
    for b in range((nq + nk) // LANES):
        c_ref, s_ref = (cq_ref, sq_ref) if b < nq // LANES else (ck_ref, sk_ref)
        sl = slice(b * LANES, (b + 1) * LANES)
        blk = ((t[:, sl] * c_ref[...] + ts[:, sl] * s_ref[...]) * r[:, sl]).astype(BF16)
        for half in range(LANES // hd):
            n = b * (LANES // hd) + half
            dst, idx = (q_ref, n) if n < D_HEADS else (k_ref, n - D_HEADS)
            dst[idx] = blk[:, half * hd:(half + 1) * hd]
    for n in range(D_KV_HEADS):
        v_ref[n] = y[:, nq + nk + n * hd: nq + nk + (n + 1) * hd].astype(BF16)


def _swap_halves_cols(w, group):
    k, n = w.shape
    return w.reshape(k, n // group, 2, group // 2)[:, :, ::-1, :].reshape(k, n)


def _rope_tables(pos, gain, scale):
    half = gain.shape[-1] // 2
    inv = ROPE_THETA ** (-jnp.arange(half, dtype=F32) / half)
    ang = pos.astype(F32)[:, None] * inv[None, :]
    cos, sin = jnp.cos(ang), jnp.sin(ang)
    g_sw = jnp.concatenate([gain[half:], gain[:half]])
    c = jnp.concatenate([cos, cos], axis=-1) * gain[None, :] * scale
    s = jnp.concatenate([-sin, sin], axis=-1) * g_sw[None, :] * scale
    return c, s


def _proj_d_call(x, gn, wqkv, gq, gk, pos):
    t_rows = x.shape[0]
    tm, hd = TOKEN_TILE, HEAD_DIM
    nq, nk = D_HEADS * hd, D_KV_HEADS * hd
    half = hd // 2
    w = jnp.concatenate([wqkv, _swap_halves_cols(wqkv[:, :nq + nk], half)], axis=1).astype(BF16)
    rows, cols = pos // GRID_W, pos % GRID_W

    def tables(g, scale):
        cr, sr = _rope_tables(rows, g[:half], scale)
        cc, sc = _rope_tables(cols, g[half:], scale)
        reps = LANES // hd
        return jnp.concatenate([cr, cc] * reps, axis=-1), jnp.concatenate([sr, sc] * reps, axis=-1)

    cq, sq = tables(gq, hd ** -0.5 * LOG2E)
    ck, sk = tables(gk, 1.0)
    ones = _segment_ones(hd)
    head_major = lambda n: pl.BlockSpec((n, tm, hd), lambda i: (0, i, 0))
    tab = pl.BlockSpec((tm, LANES), lambda i: (i, 0))
    return pl.pallas_call(
        _proj_d_kernel,
        grid=(t_rows // tm,),
        in_specs=[pl.BlockSpec((tm, D_MODEL), lambda i: (i, 0)), _const_spec((1, D_MODEL)),
                  _const_spec(w.shape), _const_spec(ones.shape), tab, tab, tab, tab],
        out_specs=[head_major(D_HEADS), head_major(D_KV_HEADS), head_major(D_KV_HEADS)],
        out_shape=[jax.ShapeDtypeStruct((D_HEADS, t_rows, hd), BF16),
                   jax.ShapeDtypeStruct((D_KV_HEADS, t_rows, hd), BF16),
                   jax.ShapeDtypeStruct((D_KV_HEADS, t_rows, hd), BF16)],
        compiler_params=_params(("parallel",)),
        name="proj_d",
    )(x, gn.reshape(1, D_MODEL), w, ones, cq, sq, ck, sk)


def _proj_b_kernel(x_ref, gn_ref, w1_ref, gcq_ref, gckv_ref, wq_ref, wqs_ref, wk_ref, wv_ref,
                   cq_ref, sq_ref, ck_ref, sk_ref, q_ref, k_ref, v_ref):
    h = _rms(x_ref[...], gn_ref[...]).astype(BF16)
    y1 = _dot(h, w1_ref[...])
    c_q = _rms(y1[:, :B_Q_LORA], gcq_ref[...]).astype(BF16)
    c_kv = _rms(y1[:, B_Q_LORA:B_Q_LORA + B_KV_LORA], gckv_ref[...]).astype(BF16)
    off = B_Q_LORA + B_KV_LORA
    kr = y1[:, off:off + B_PAD]
    krs = y1[:, off + B_PAD:off + 2 * B_PAD]
    q = _dot(c_q, wq_ref[...])
    qs = _dot(c_q, wqs_ref[...])
    kn = _dot(c_kv, wk_ref[...])
    v = _dot(c_kv, wv_ref[...])
    cq, sq, ck, sk = cq_ref[...], sq_ref[...], ck_ref[...], sk_ref[...]
    krs_s = krs * sk
    for n in range(B_HEADS):
        t = q[:, n * B_PAD:(n + 1) * B_PAD]
        ts = qs[:, n * B_PAD:(n + 1) * B_PAD]
        q_ref[n] = ((t * cq + ts * sq) * _head_norm(t, B_QK)).astype(BF16)
        t = kn[:, n * B_PAD:(n + 1) * B_PAD] + kr
        k_ref[n] = ((t * ck + krs_s) * _head_norm(t, B_QK)).astype(BF16)
        v_ref[n] = v[:, n * B_V:(n + 1) * B_V].astype(BF16)


def _pad_heads(w, lo, width):
    k, nh, _ = w.shape
    out = jnp.zeros((k, nh, B_PAD), w.dtype).at[:, :, lo:lo + width].set(w)
    return out.reshape(k, nh * B_PAD)


def _proj_b_call(x, gn, wdq, gcq, wuq, wdkv, gckv, wukv, gq, gk, pos):
    t_rows = x.shape[0]
    tm = TOKEN_TILE
    wuq3 = wuq.reshape(B_Q_LORA, B_HEADS, B_QK)
    wq = _pad_heads(wuq3, 0, B_QK).astype(BF16)
    wuq_rope_sw = _swap_halves_cols(wuq3[:, :, B_NOPE:].reshape(B_Q_LORA, -1), B_ROPE)
    wqs = _pad_heads(wuq_rope_sw.reshape(B_Q_LORA, B_HEADS, B_ROPE), B_NOPE, B_ROPE).astype(BF16)
    wukv3 = wukv.reshape(B_KV_LORA, B_HEADS, B_NOPE + B_V)
    wk = _pad_heads(wukv3[:, :, :B_NOPE], 0, B_NOPE).astype(BF16)
    wv = wukv3[:, :, B_NOPE:].reshape(B_KV_LORA, B_HEADS * B_V).astype(BF16)
    w_kr = wdkv[:, B_KV_LORA:]
    kr_pad = _pad_heads(w_kr[:, None, :], B_NOPE, B_ROPE)
    krs_pad = _pad_heads(_swap_halves_cols(w_kr, B_ROPE)[:, None, :], B_NOPE, B_ROPE)
    w1 = jnp.concatenate([wdq, wdkv[:, :B_KV_LORA], kr_pad, krs_pad], axis=1).astype(BF16)

    def tables(g, scale):
        c, s = _rope_tables(pos, g[B_NOPE:], scale)
        zeros = jnp.zeros((t_rows, B_PAD - B_QK), F32)
        c_full = jnp.concatenate(
            [jnp.broadcast_to(g[None, :B_NOPE] * scale, (t_rows, B_NOPE)), c, zeros], axis=-1)
        s_full = jnp.concatenate([jnp.zeros((t_rows, B_NOPE), F32), s, zeros], axis=-1)
        return c_full, s_full

    cq, sq = tables(gq, B_QK ** -0.5 * LOG2E)
    ck, sk = tables(gk, 1.0)
    tab = pl.BlockSpec((tm, B_PAD), lambda i: (i, 0))
    head_major = lambda w_: pl.BlockSpec((B_HEADS, tm, w_), lambda i: (0, i, 0))
    return pl.pallas_call(
        _proj_b_kernel,
        grid=(t_rows // tm,),
        in_specs=[pl.BlockSpec((tm, D_MODEL), lambda i: (i, 0)), _const_spec((1, D_MODEL)),
                  _const_spec(w1.shape), _const_spec((1, B_Q_LORA)), _const_spec((1, B_KV_LORA)),
                  _const_spec(wq.shape), _const_spec(wqs.shape), _const_spec(wk.shape),
                  _const_spec(wv.shape), tab, tab, tab, tab],
        out_specs=[head_major(B_PAD), head_major(B_PAD), head_major(B_V)],
        out_shape=[jax.ShapeDtypeStruct((B_HEADS, t_rows, B_PAD), BF16),
                   jax.ShapeDtypeStruct((B_HEADS, t_rows, B_PAD), BF16),
                   jax.ShapeDtypeStruct((B_HEADS, t_rows, B_V), BF16)],
        compiler_params=_params(("parallel",)),
        name="proj_b",
    )(x, gn.reshape(1, D_MODEL), w1, gcq.reshape(1, -1), gckv.reshape(1, -1), wq, wqs, wk, wv,
      cq, sq, ck, sk)


def _proj_c_kernel(x_ref, gn_ref, w_ref, gq_ref, gk_ref, e_ref, *refs):
    hd = HEAD_DIM
    width = C_HEADS * hd
    outs, scr_ref = refs[:-1], refs[-1]
    tm = x_ref.shape[0]
    h = _rms(x_ref[...], gn_ref[...]).astype(BF16)
    y = _dot(h, w_ref[...])
    gains = (gq_ref[...], gk_ref[...])
    for g, (_, dil) in enumerate(C_PATTERNS):
        for j in range(3):
            base = (g * 3 + j) * width
            val = y[:, base: base + width]
            if j < 2:
                val = val * _segment_rsqrt(val, e_ref, hd) * gains[j]
            out_ref = outs[g * 3 + j]
            if dil == 1:
                out_ref[...] = val.astype(BF16)
            else:
                for s in range(width // LANES):
                    scr_ref[s] = val[:, s * LANES:(s + 1) * LANES]
                for r in range(dil):
                    for s in range(width // LANES):
                        rows = scr_ref[s, pl.ds(r, tm // dil, stride=dil), :]
                        col = r * width + s * LANES
                        out_ref[:, col:col + LANES] = rows.astype(BF16)


def _proj_c_call(x, gn, wqkv, gq, gk):
    t_rows = x.shape[0]
    tm, hd = TOKEN_TILE // 2, HEAD_DIM
    width = C_HEADS * hd
    w = wqkv.astype(BF16)
    ones = _segment_ones(hd)
    specs, shapes = [], []
    for _, dil in C_PATTERNS:
        specs += [pl.BlockSpec((tm // dil, dil * width), lambda i: (i, 0))] * 3
        shapes += [jax.ShapeDtypeStruct((t_rows // dil, dil * width), BF16)] * 3
    outs = pl.pallas_call(
        _proj_c_kernel,
        grid=(t_rows // tm,),
        in_specs=[pl.BlockSpec((tm, D_MODEL), lambda i: (i, 0)), _const_spec((1, D_MODEL)),
                  _const_spec(w.shape), _const_spec((1, width)), _const_spec((1, width)),
                  _const_spec(ones.shape)],
        out_specs=specs,
        out_shape=shapes,
        scratch_shapes=[pltpu.VMEM((width // LANES, tm, LANES), F32)],
        compiler_params=_params(("parallel",)),
        name="proj_c",
    )(x, gn.reshape(1, D_MODEL), w, jnp.tile(gq * (hd ** -0.5 * LOG2E), C_HEADS).reshape(1, width),
      jnp.tile(gk, C_HEADS).reshape(1, width), ones)
    return [outs[3 * g:3 * g + 3] for g in range(len(C_PATTERNS))]


def _band_masked_dist(row0, lo, hi, sub, w):
    nk = sub + 2 * w
    r = lax.broadcasted_iota(jnp.int32, (sub, nk), 0)
    c = lax.broadcasted_iota(jnp.int32, (sub, nk), 1)
    dist = jnp.abs(r + w - c)
    jpos = row0 - w + lax.broadcasted_iota(jnp.int32, (1, nk), 1)
    outside = jnp.where(jpos < lo, w + 1, 0) + jnp.where(jpos >= hi, w + 1, 0)
    return jnp.where(dist + outside <= w, dist.astype(F32), MASK_DIST)


def _two_stage(n_items, first, second, ahead=2):
    pending = {i: first(i) for i in range(min(ahead, n_items))}
    results = []
    for i in range(n_items):
        if i + ahead < n_items:
            pending[i + ahead] = first(i + ahead)
        results.append(second(i, pending.pop(i)))
    return results


def _band_softmax_pv(s, vv, sink):
    m = jnp.max(s, axis=-1, keepdims=True)
    if sink is not None:
        m = jnp.maximum(m, sink)
    p = jnp.exp2(s - m)
    den = jnp.sum(p, axis=-1, keepdims=True)
    if sink is not None:
        den = den + jnp.exp2(sink - m)
    return _dot(p.astype(BF16), vv) / den, m + jnp.log2(den)


def _attn_a_kernel(slope_ref, sink_ref, q_ref, kp_ref, kc_ref, kn_ref, vp_ref, vc_ref, vn_ref,
                   o_ref, *, tq, p_len, s_len):
    w, sub = A_HALF_WINDOW, BAND_SUB
    group = A_HEADS // A_KV_HEADS
    kvh = pl.program_id(0)
    row0 = pl.program_id(1) * tq
    lo, hi = _seq_bounds(row0, p_len, s_len)
    kk = jnp.concatenate([kp_ref[0], kc_ref[0], kn_ref[0]], axis=0)
    vv = jnp.concatenate([vp_ref[0], vc_ref[0], vn_ref[0]], axis=0)
    sink = jnp.concatenate(
        [jnp.full((sub, 1), sink_ref[kvh * group + g], F32) for g in range(group)], axis=0)

    def keys(r):
        return slice(r * sub, (r + 1) * sub + 2 * w)

    def logits(r):
        md = _band_masked_dist(row0 + r * sub, lo, hi, sub, w)
        bias = jnp.concatenate([md * slope_ref[kvh * group + g] for g in range(group)], axis=0)
        q = q_ref[:, r * sub:(r + 1) * sub, :].reshape(group * sub, HEAD_DIM)
        return _dot_nt(q, kk[keys(r)]) + bias

    tiles = _two_stage(tq // sub, logits, lambda r, s: _band_softmax_pv(s, vv[keys(r)], sink)[0])
    outs = [[o[g * sub:(g + 1) * sub] for o in tiles] for g in range(group)]
    o_ref[...] = jnp.concatenate(
        [jnp.concatenate(rows, axis=0) for rows in outs], axis=-1).astype(BF16)


def _attn_a_call(q, k, v, sink, geo):
    p_len, s_len, t_rows = geo
    tq, w, hd = 512, A_HALF_WINDOW, HEAD_DIM
    group = A_HEADS // A_KV_HEADS
    per = tq // w
    last = t_rows // w - 1
    prev = pl.BlockSpec((1, w, hd), lambda h, i: (h, jnp.maximum(i * per - 1, 0), 0))
    cur = pl.BlockSpec((1, tq, hd), lambda h, i: (h, i, 0))
    nxt = pl.BlockSpec((1, w, hd), lambda h, i: (h, jnp.minimum((i + 1) * per, last), 0))
    smem = pl.BlockSpec(memory_space=pltpu.SMEM)
    slopes = jnp.asarray(-LOG2E * _alibi_slopes(A_HEADS), F32)
    return pl.pallas_call(
        functools.partial(_attn_a_kernel, tq=tq, p_len=p_len, s_len=s_len),
        grid=(A_KV_HEADS, t_rows // tq),
        in_specs=[smem, smem, pl.BlockSpec((group, tq, hd), lambda h, i: (h, i, 0)),
                  prev, cur, nxt, prev, cur, nxt],
        out_specs=pl.BlockSpec((tq, group * hd), lambda h, i: (i, h)),
        out_shape=jax.ShapeDtypeStruct((t_rows, A_HEADS * hd), BF16),
        compiler_params=_params(("parallel", "parallel")),
        name="attn_a",
    )(slopes, sink.astype(F32) * LOG2E, q, k, k, k, v, v, v)


def _attn_c_kernel(q_ref, kp_ref, kc_ref, kn_ref, vp_ref, vc_ref, vn_ref, o_ref, lse_ref,
                   *, tq, dil, p_len, s_len):
    w, hd, sub = C_STEPS, HEAD_DIM, min(BAND_SUB, tq)
    row0 = pl.program_id(0) * tq
    lo, hi = _seq_bounds(row0, p_len // dil, s_len // dil)
    slopes = -LOG2E * dil * _alibi_slopes(C_HEADS)
    mds = [_band_masked_dist(row0 + r * sub, lo, hi, sub, w) for r in range(tq // sub)]
    n_sub = tq // sub

    def window(prev_ref, cur_ref, next_ref, i):
        n, r = divmod(i, n_sub)
        hs = slice(n * hd, (n + 1) * hd)
        rows = jnp.concatenate([prev_ref[:, hs], cur_ref[:, hs], next_ref[:, hs]], axis=0)
        return rows[r * sub:(r + 1) * sub + 2 * w]

    def logits(i):
        n, r = divmod(i, n_sub)
        q = q_ref[r * sub:(r + 1) * sub, n * hd:(n + 1) * hd]
        return _dot_nt(q, window(kp_ref, kc_ref, kn_ref, i)) + mds[r] * float(slopes[n])

    def attend(i, s):
        o, lse2 = _band_softmax_pv(s, window(vp_ref, vc_ref, vn_ref, i), None)
        return o, jnp.broadcast_to(lse2 * (1.0 / LOG2E), (sub, hd))

    tiles = _two_stage(C_HEADS * n_sub, logits, attend)
    outs = [jnp.concatenate([t[0] for t in tiles[n * n_sub:(n + 1) * n_sub]], axis=0)
            for n in range(C_HEADS)]
    lses = [jnp.concatenate([t[1] for t in tiles[n * n_sub:(n + 1) * n_sub]], axis=0)
            for n in range(C_HEADS)]
    o_ref[...] = jnp.concatenate(outs, axis=-1)
    lse_ref[...] = jnp.concatenate(lses, axis=-1)


def _attn_c_call(q, k, v, dil, geo):
    p_len, s_len, t_rows = geo
    w, width = C_STEPS, C_HEADS * HEAD_DIM
    rows = t_rows // dil
    tq = min(512, s_len // dil)
    per = tq // w
    last = rows // w - 1
    prev = pl.BlockSpec((w, width), lambda i, r: (jnp.maximum(i * per - 1, 0), r))
    cur = pl.BlockSpec((tq, width), lambda i, r: (i, r))
    nxt = pl.BlockSpec((w, width), lambda i, r: (jnp.minimum((i + 1) * per, last), r))
    shape = jax.ShapeDtypeStruct((rows, dil * width), F32)
    return pl.pallas_call(
        functools.partial(_attn_c_kernel, tq=tq, dil=dil, p_len=p_len, s_len=s_len),
        grid=(rows // tq, dil),
        in_specs=[cur, prev, cur, nxt, prev, cur, nxt],
        out_specs=[cur, cur],
        out_shape=[shape, shape],
        compiler_params=_params(("parallel", "parallel")),
        name="attn_c_d%d" % dil,
    )(q, k, k, k, v, v, v)


def _attn_dense_kernel(qt_ref, k_ref, vt_ref, o_ref, sa_ref, sb_ref, sc_ref, sd_ref,
                       *, nq, tq, tk, p_len, s_len):
    dv = o_ref.shape[-2]
    width = nq * tq
    row0 = pl.program_id(1) * tq
    lo, hi = _seq_bounds(row0, p_len, s_len)
    shift = int(math.log2(tk))
    first = lax.shift_right_logical(lo, shift)
    n_chunks = lax.shift_right_logical(hi - lo, shift)

    def scores(c, st_ref):
        start = pl.multiple_of(c * tk, tk)
        qt = jnp.concatenate([qt_ref[g] for g in range(nq)], axis=-1)
        st = _dot(k_ref[0, pl.ds(start, tk), :], qt).astype(BF16)
        st_ref[...] = st
        return jnp.max(st, axis=0, keepdims=True).astype(F32)

    def update(c, st_ref, mc, m, acc):
        m_new = jnp.maximum(m, mc)
        alpha = jnp.exp2(m - m_new)
        pt = jnp.exp2(st_ref[...] - m_new.astype(BF16))
        return m_new, alpha * acc + _dot(vt_ref[0, c], pt)

    def quad(i, carry):
        m, acc, mc_a, mc_b = carry
        c = first + 4 * i
        mc_c = scores(c + 2, sc_ref)
        m, acc = update(c, sa_ref, mc_a, m, acc)
        mc_d = scores(c + 3, sd_ref)
        m, acc = update(c + 1, sb_ref, mc_b, m, acc)
        mc_a = scores(c + 4, sa_ref)
        m, acc = update(c + 2, sc_ref, mc_c, m, acc)
        mc_b = scores(c + 5, sb_ref)
        m, acc = update(c + 3, sd_ref, mc_d, m, acc)
        return m, acc, mc_a, mc_b

    init = (jnp.full((1, width), NEG_INF, F32), jnp.zeros((vt_ref.shape[-2], width), F32),
            scores(first, sa_ref), scores(first + 1, sb_ref))
    n_quads = lax.shift_right_logical(n_chunks, 2)
    m, acc, mc_a, mc_b = lax.fori_loop(0, n_quads - 1, quad, init)
    c = first + n_chunks - 4
    mc_c = scores(c + 2, sc_ref)
    m, acc = update(c, sa_ref, mc_a, m, acc)
    mc_d = scores(c + 3, sd_ref)
    m, acc = update(c + 1, sb_ref, mc_b, m, acc)
    m, acc = update(c + 2, sc_ref, mc_c, m, acc)
    m, acc = update(c + 3, sd_ref, mc_d, m, acc)
    o = (acc[:dv] / acc[dv:dv + 1]).astype(BF16)
    for g in range(nq):
        o_ref[g] = o[:, g * tq:(g + 1) * tq]


def _attn_dense_call(q, k, v, geo, tq, tk=512):
    p_len, s_len, t_rows = geo
    hq, _, dq = q.shape
    hk, _, dv = v.shape
    nq = hq // hk
    n_chunks = t_rows // tk
    assert s_len % (4 * tk) == 0 and p_len % (4 * tk) == 0
    qt = q.transpose(0, 2, 1)
    vt = v.reshape(hk, n_chunks, tk, dv).transpose(0, 1, 3, 2)
    ones_rows = jnp.zeros((hk, n_chunks, BF16_SUBLANES, tk), BF16).at[:, :, 0].set(1.0)
    vt = jnp.concatenate([vt, ones_rows], axis=2)
    o = pl.pallas_call(
        functools.partial(_attn_dense_kernel, nq=nq, tq=tq, tk=tk, p_len=p_len, s_len=s_len),
        grid=(hk, t_rows // tq),
        in_specs=[pl.BlockSpec((nq, dq, tq), lambda h, i: (h, 0, i)),
                  pl.BlockSpec((1, t_rows, dq), lambda h, i: (h, 0, 0), pipeline_mode=pl.Buffered(1)),
                  pl.BlockSpec((1, n_chunks, dv + BF16_SUBLANES, tk), lambda h, i: (h, 0, 0, 0),
                               pipeline_mode=pl.Buffered(1))],
        out_specs=pl.BlockSpec((nq, dv, tq), lambda h, i: (h, 0, i)),
        out_shape=jax.ShapeDtypeStruct((hq, dv, t_rows), BF16),
        scratch_shapes=[pltpu.VMEM((tk, nq * tq), BF16)] * 4,
        compiler_params=_params(("parallel", "parallel")),
        name="attn_dense_%d" % dq,
    )(qt, k, vt)
    return o.transpose(2, 0, 1).reshape(t_rows, hq * dv)


def _trunk(x, p, geo, pos):
    for i in range(DEPTH):
        mixer, j = i % 4, i // 4
        x = _ffn_call(x, p["norm_ffn1"][i], p["ffn1_wg"][i], p["ffn1_wu"][i], p["ffn1_wd"][i])
        gn = p["norm_mix"][i]
        if mixer == 0:
            q, k, v = _proj_a_call(x, gn, p["a_wqkv"][j], p["a_gq"][j], p["a_gk"][j])
            pre = (_attn_a_call(q, k, v, p["a_sink"][j], geo), p["a_wo"][j])
        elif mixer == 1:
            q, k, v = _proj_b_call(x, gn, p["b_wdq"][j], p["b_gcq"][j], p["b_wuq"][j], p["b_wdkv"][j],
                                   p["b_gckv"][j], p["b_wukv"][j], p["b_gq"][j], p["b_gk"][j], pos)
            pre = (_attn_dense_call(q, k, v, geo, tq=512), p["b_wo"][j])
        elif mixer == 2:
            qkv = _proj_c_call(x, gn, p["c_wqkv"][j], p["c_gq"][j], p["c_gk"][j])
            res = [_attn_c_call(*qkv[g], dil, geo) for g, (_, dil) in enumerate(C_PATTERNS)]
            pre = ([r[0] for r in res], [r[1] for r in res], p["c_wo"][j])
        else:
            q, k, v = _proj_d_call(x, gn, p["d_wqkv"][j], p["d_gq"][j], p["d_gk"][j], pos)
            pre = (_attn_dense_call(q, k, v, geo, tq=128), p["d_wo"][j])
        x = _ffn_call(x, p["norm_ffn2"][i], p["ffn2_wg"][i], p["ffn2_wu"][i], p["ffn2_wd"][i], pre=pre)
    return x


def kernel(x_prompt, x_sample, norm_ffn1, ffn1_wg, ffn1_wu, ffn1_wd, norm_mix, norm_ffn2, ffn2_wg, ffn2_wu, ffn2_wd, a_wqkv, a_gq, a_gk, a_sink, a_wo, b_wdq, b_gcq, b_wuq, b_wdkv, b_gckv, b_wukv, b_gq, b_gk, b_wo, c_wqkv, c_gq, c_gk, c_wo, d_wqkv, d_gq, d_gk, d_wo):
    p = dict(norm_ffn1=norm_ffn1, ffn1_wg=ffn1_wg, ffn1_wu=ffn1_wu, ffn1_wd=ffn1_wd,
             norm_mix=norm_mix, norm_ffn2=norm_ffn2, ffn2_wg=ffn2_wg, ffn2_wu=ffn2_wu,
             ffn2_wd=ffn2_wd, a_wqkv=a_wqkv, a_gq=a_gq, a_gk=a_gk, a_sink=a_sink, a_wo=a_wo,
             b_wdq=b_wdq, b_gcq=b_gcq, b_wuq=b_wuq, b_wdkv=b_wdkv, b_gckv=b_gckv,
             b_wukv=b_wukv, b_gq=b_gq, b_gk=b_gk, b_wo=b_wo, c_wqkv=c_wqkv, c_gq=c_gq,
             c_gk=c_gk, c_wo=c_wo, d_wqkv=d_wqkv, d_gq=d_gq, d_gk=d_gk, d_wo=d_wo)
    n_p, p_seq, _ = x_prompt.shape
    n_s, s_len, _ = x_sample.shape
    assert n_p == 1 and s_len & (s_len - 1) == 0 and p_seq % s_len == 0
    p_len = n_p * p_seq
    t_rows = p_len + n_s * s_len
    geo = (p_len, s_len, t_rows)
    pos = jnp.concatenate([jnp.arange(p_len, dtype=jnp.int32),
                           jnp.tile(jnp.arange(s_len, dtype=jnp.int32), n_s)])
    x = jnp.concatenate([x_prompt.reshape(p_len, D_MODEL), x_sample.reshape(n_s * s_len, D_MODEL)])
    y = _trunk(x, p, geo, pos)
    return y[:p_len].reshape(x_prompt.shape), y[p_len:].reshape(x_sample.shape)
```

```python
import functools
import math

import numpy as np
import jax
import jax.numpy as jnp
from jax import lax
from jax.experimental import pallas as pl
from jax.experimental.pallas import tpu as pltpu

F32 = jnp.float32
BF16 = jnp.bfloat16

D_MODEL = 1024
DEPTH = 4
HEAD_DIM = 64
D_FF = 2816
NORM_EPS = 1e-6
NEG_INF = -1e30
GRID_W = 64
ROPE_THETA = 10000.0
LOG2E = math.log2(math.e)
LANES = 128
MXU_TILE = 256
BF16_SUBLANES = 16
BAND_SUB = 128
Q_TILES_PER_STEP = 4
MASK_DIST = 1e33

A_HEADS, A_KV_HEADS, A_HALF_WINDOW = 16, 4, 128
B_HEADS, B_Q_LORA, B_KV_LORA, B_NOPE, B_ROPE, B_V = 16, 512, 256, 64, 32, 64
B_QK = B_NOPE + B_ROPE
B_PAD = 128
C_HEADS = 8
C_PATTERNS = ((128, 1), (512, 4), (2048, 16))
C_STEPS = 64
D_HEADS, D_KV_HEADS = 16, 4

FF_CHUNK = 256
N_FF_CHUNKS = D_FF // FF_CHUNK
TOKEN_TILE = 512
VMEM_LIMIT = 56 * 1024 * 1024


def _alibi_slopes(n):
    return 2.0 ** (-8.0 * np.arange(1, n + 1) / n)


def _params(sem, vmem=VMEM_LIMIT):
    return pltpu.CompilerParams(dimension_semantics=sem, vmem_limit_bytes=vmem)


def _const_spec(shape):
    nd = len(shape)
    return pl.BlockSpec(tuple(shape), lambda *_: (0,) * nd, pipeline_mode=pl.Buffered(1))


def _seq_bounds(row0, p_len, s_len):
    in_prompt = row0 < p_len
    b = lax.shift_right_logical(jnp.maximum(row0 - p_len, 0), int(math.log2(s_len)))
    lo = jnp.where(in_prompt, 0, p_len + b * s_len)
    hi = jnp.where(in_prompt, p_len, lo + s_len)
    return lo, hi


def _rms(x, g):
    return x * lax.rsqrt(jnp.mean(x * x, axis=-1, keepdims=True) + NORM_EPS) * g


def _dot(a, b):
    return jnp.dot(a, b, preferred_element_type=F32)


def _dot_nt(a, b):
    return lax.dot_general(a, b, (((1,), (1,)), ((), ())), preferred_element_type=F32)


def _ffn_kernel(*refs, mode):
    if mode == "plain":
        x_ref, gn_ref, wg_ref, wu_ref, wd_ref, out_ref, acc_ref, act_a, act_b = refs
        x = x_ref[...]
    elif mode == "proj":
        (x_ref, o_ref, wo_ref, gn_ref, wg_ref, wu_ref, wd_ref, out_ref, acc_ref, act_a,
         act_b) = refs
        x = x_ref[...] + _dot(o_ref[...], wo_ref[...])
    else:
        (x_ref, o0, o1, o2, l0, l1, l2, wo_ref, gn_ref, wg_ref, wu_ref, wd_ref,
         out_ref, acc_ref, act_a, act_b, *scr) = refs
        tm, width = x_ref.shape[0], wo_ref.shape[0]

        def token_major(ref, dil, scr_ref):
            slabs = width // LANES
            for r in range(dil):
                for s in range(slabs):
                    col = r * width + s * LANES
                    scr_ref[s, pl.ds(r, tm // dil, stride=dil), :] = ref[:, col:col + LANES]
            return jnp.concatenate([scr_ref[s] for s in range(slabs)], axis=-1)

        dils = [d for _, d in C_PATTERNS]
        assert dils[0] == 1
        oa, la = o0[...], l0[...]
        ob, lb = token_major(o1, dils[1], scr[0]), token_major(l1, dils[1], scr[1])
        oc, lc = token_major(o2, dils[2], scr[2]), token_major(l2, dils[2], scr[3])
        m = jnp.maximum(jnp.maximum(la, lb), lc)
        ea, eb, ec = jnp.exp(la - m), jnp.exp(lb - m), jnp.exp(lc - m)
        o = (ea * oa + eb * ob + ec * oc) / (ea + eb + ec)
        x = x_ref[...] + _dot(o.astype(BF16), wo_ref[...])

    h = _rms(x, gn_ref[...]).astype(BF16)

    def gate_up(c, a_ref):
        g = _dot(h, wg_ref[c])
        u = _dot(h, wu_ref[c])
        a_ref[...] = (g * jax.nn.sigmoid(g) * u).astype(BF16)

    def down(c, a_ref):
        acc_ref[...] += _dot(a_ref[...], wd_ref[c])

    assert N_FF_CHUNKS % 2 == 1
    acc_ref[...] = jnp.zeros_like(acc_ref)
    gate_up(0, act_a)

    def pair(i, carry):
        c = 2 * i
        gate_up(c + 1, act_b)
        down(c, act_a)
        gate_up(c + 2, act_a)
        down(c + 1, act_b)
        return carry

    lax.fori_loop(0, N_FF_CHUNKS // 2, pair, 0)
    down(N_FF_CHUNKS - 1, act_a)
    out_ref[...] = x + 0.5 * acc_ref[...]


def _ffn_weights(wg, wu, wd):
    wg3 = wg.astype(BF16).reshape(D_MODEL, N_FF_CHUNKS, FF_CHUNK).transpose(1, 0, 2)
    wu3 = wu.astype(BF16).reshape(D_MODEL, N_FF_CHUNKS, FF_CHUNK).transpose(1, 0, 2)
    wd3 = wd.astype(BF16).reshape(N_FF_CHUNKS, FF_CHUNK, D_MODEL)
    return wg3, wu3, wd3


def _ffn_call(x, gn, wg, wu, wd, pre=None):
    t_rows = x.shape[0]
    tm = TOKEN_TILE
    wg3, wu3, wd3 = _ffn_weights(wg, wu, wd)
    row = lambda w: pl.BlockSpec((tm, w), lambda i: (i, 0))
    ffn_args = [gn.reshape(1, D_MODEL), wg3, wu3, wd3]
    ffn_specs = [_const_spec((1, D_MODEL)), _const_spec(wg3.shape), _const_spec(wu3.shape),
                 _const_spec(wd3.shape)]
    scratch = []
    if pre is None:
        mode, args, specs = "plain", [x], [row(D_MODEL)]
    elif len(pre) == 2:
        o, wo = pre
        wo = wo.astype(BF16)
        mode = "proj"
        args = [x, o, wo]
        specs = [row(D_MODEL), row(o.shape[1]), _const_spec(wo.shape)]
    else:
        outs, lses, wo = pre
        wo = wo.astype(BF16)
        mode = "merge"
        args = [x, *outs, *lses, wo]
        width = wo.shape[0]
        dilated = [pl.BlockSpec((tm // d, d * width), lambda i: (i, 0)) for _, d in C_PATTERNS]
        specs = [row(D_MODEL)] + dilated * 2 + [_const_spec(wo.shape)]
        scratch = [pltpu.VMEM((width // LANES, tm, LANES), F32)] * 4
    return pl.pallas_call(
        functools.partial(_ffn_kernel, mode=mode),
        grid=(t_rows // tm,),
        in_specs=specs + ffn_specs,
        out_specs=row(D_MODEL),
        out_shape=jax.ShapeDtypeStruct((t_rows, D_MODEL), F32),
        scratch_shapes=[pltpu.VMEM((tm, D_MODEL), F32)] + [pltpu.VMEM((tm, FF_CHUNK), BF16)] * 2
        + scratch,
        compiler_params=_params(("parallel",)),
        name="ffn_" + mode,
    )(*args, *ffn_args)


def _head_norm(t, width):
    return lax.rsqrt(jnp.sum(t * t, axis=-1, keepdims=True) * (1.0 / width) + NORM_EPS)


def _segment_ones(seg):
    n = MXU_TILE // seg
    return jnp.kron(jnp.eye(n, dtype=F32), jnp.ones((seg, seg), F32)).astype(BF16)


def _segment_rsqrt(t, e_ref, width):
    sq = t * t
    hi = sq.astype(BF16)
    lo = (sq - hi.astype(F32)).astype(BF16)
    e = e_ref[...]
    sums = [_dot(hi[:, b:b + MXU_TILE], e) + _dot(lo[:, b:b + MXU_TILE], e)
            for b in range(0, t.shape[-1], MXU_TILE)]
    ssq = sums[0] if len(sums) == 1 else jnp.concatenate(sums, axis=-1)
    return lax.rsqrt(ssq * (1.0 / width) + NORM_EPS)


def _proj_a_kernel(x_ref, gn_ref, w_ref, gain_ref, e_ref, q_ref, k_ref, v_ref):
    hd = HEAD_DIM
    nqk = (A_HEADS + A_KV_HEADS) * hd
    h = _rms(x_ref[...], gn_ref[...]).astype(BF16)
    y = _dot(h, w_ref[...])
    qk = y[:, :nqk]
    qk = (qk * _segment_rsqrt(qk, e_ref, hd) * gain_ref[...]).astype(BF16)
    for n in range(A_HEADS):
        q_ref[n] = qk[:, n * hd:(n + 1) * hd]
    for n in range(A_KV_HEADS):
        k_ref[n] = qk[:, (A_HEADS + n) * hd:(A_HEADS + n + 1) * hd]
        v_ref[n] = y[:, nqk + n * hd: nqk + (n + 1) * hd].astype(BF16)


def _proj_a_call(x, gn, wqkv, gq, gk):
    t_rows = x.shape[0]
    tm, hd = TOKEN_TILE, HEAD_DIM
    w = wqkv.astype(BF16)
    gain = jnp.concatenate([jnp.tile(gq * (hd ** -0.5 * LOG2E), A_HEADS), jnp.tile(gk, A_KV_HEADS)])
    ones = _segment_ones(hd)
    head_major = lambda n: pl.BlockSpec((n, tm, hd), lambda i: (0, i, 0))
    return pl.pallas_call(
        _proj_a_kernel,
        grid=(t_rows // tm,),
        in_specs=[pl.BlockSpec((tm, D_MODEL), lambda i: (i, 0)), _const_spec((1, D_MODEL)),
                  _const_spec(w.shape), _const_spec((1, gain.shape[0])), _const_spec(ones.shape)],
        out_specs=[head_major(A_HEADS), head_major(A_KV_HEADS), head_major(A_KV_HEADS)],
        out_shape=[jax.ShapeDtypeStruct((A_HEADS, t_rows, hd), BF16),
                   jax.ShapeDtypeStruct((A_KV_HEADS, t_rows, hd), BF16),
                   jax.ShapeDtypeStruct((A_KV_HEADS, t_rows, hd), BF16)],
        compiler_params=_params(("parallel",)),
        name="proj_a",
    )(x, gn.reshape(1, D_MODEL), w, gain.reshape(1, -1), ones)


def _proj_d_kernel(x_ref, gn_ref, w_ref, e_ref, cq_ref, sq_ref, ck_ref, sk_ref, q_ref, k_ref, v_ref):
    hd = HEAD_DIM
    nq, nk = D_HEADS * hd, D_KV_HEADS * hd
    h = _rms(x_ref[...], gn_ref[...]).astype(BF16)
    y = _dot(h, w_ref[...])
    swapped = nq + 2 * nk
    t, ts = y[:, :nq + nk], y[:, swapped:swapped + nq + nk]
    r = _segment_rsqrt(t, e_ref, hd)
    for b in range((nq + nk) // LANES):
        c_ref, s_ref = (cq_ref, sq_ref) if b < nq // LANES else (ck_ref, sk_ref)
        sl = slice(b * LANES, (b + 1) * LANES)
        blk = ((t[:, sl] * c_ref[...] + ts[:, sl] * s_ref[...]) * r[:, sl]).astype(BF16)
        for half in range(LANES // hd):
            n = b * (LANES // hd) + half
            dst, idx = (q_ref, n) if n < D_HEADS else (k_ref, n - D_HEADS)
            dst[idx] = blk[:, half * hd:(half + 1) * hd]
    for n in range(D_KV_HEADS):
        v_ref[n] = y[:, nq + nk + n * hd: nq + nk + (n + 1) * hd].astype(BF16)


def _swap_halves_cols(w, group):
    k, n = w.shape
    return w.reshape(k, n // group, 2, group // 2)[:, :, ::-1, :].reshape(k, n)


def _rope_tables(pos, gain, scale):
    half = gain.shape[-1] // 2
    inv = ROPE_THETA ** (-jnp.arange(half, dtype=F32) / half)
    ang = pos.astype(F32)[:, None] * inv[None, :]
    cos, sin = jnp.cos(ang), jnp.sin(ang)
    g_sw = jnp.concatenate([gain[half:], gain[:half]])
    c = jnp.concatenate([cos, cos], axis=-1) * gain[None, :] * scale
    s = jnp.concatenate([-sin, sin], axis=-1) * g_sw[None, :] * scale
    return c, s


def _proj_d_call(x, gn, wqkv, gq, gk, pos):
    t_rows = x.shape[0]
    tm, hd = TOKEN_TILE, HEAD_DIM
    nq, nk = D_HEADS * hd, D_KV_HEADS * hd
    half = hd // 2
    w = jnp.concatenate([wqkv, _swap_halves_cols(wqkv[:, :nq + nk], half)], axis=1).astype(BF16)
    rows, cols = pos // GRID_W, pos % GRID_W

    def tables(g, scale):
        cr, sr = _rope_tables(rows, g[:half], scale)
        cc, sc = _rope_tables(cols, g[half:], scale)
        reps = LANES // hd
        return jnp.concatenate([cr, cc] * reps, axis=-1), jnp.concatenate([sr, sc] * reps, axis=-1)

    cq, sq = tables(gq, hd ** -0.5 * LOG2E)
    ck, sk = tables(gk, 1.0)
    ones = _segment_ones(hd)
    head_major = lambda n: pl.BlockSpec((n, tm, hd), lambda i: (0, i, 0))
    tab = pl.BlockSpec((tm, LANES), lambda i: (i, 0))
    return pl.pallas_call(
        _proj_d_kernel,
        grid=(t_rows // tm,),
        in_specs=[pl.BlockSpec((tm, D_MODEL), lambda i: (i, 0)), _const_spec((1, D_MODEL)),
                  _const_spec(w.shape), _const_spec(ones.shape), tab, tab, tab, tab],
        out_specs=[head_major(D_HEADS), head_major(D_KV_HEADS), head_major(D_KV_HEADS)],
        out_shape=[jax.ShapeDtypeStruct((D_HEADS, t_rows, hd), BF16),
                   jax.ShapeDtypeStruct((D_KV_HEADS, t_rows, hd), BF16),
                   jax.ShapeDtypeStruct((D_KV_HEADS, t_rows, hd), BF16)],
        compiler_params=_params(("parallel",)),
        name="proj_d",
    )(x, gn.reshape(1, D_MODEL), w, ones, cq, sq, ck, sk)


def _proj_b_kernel(x_ref, gn_ref, w1_ref, gcq_ref, gckv_ref, wq_ref, wqs_ref, wk_ref, wv_ref,
                   cq_ref, sq_ref, ck_ref, sk_ref, q_ref, k_ref, v_ref):
    h = _rms(x_ref[...], gn_ref[...]).astype(BF16)
    y1 = _dot(h, w1_ref[...])
    c_q = _rms(y1[:, :B_Q_LORA], gcq_ref[...]).astype(BF16)
    c_kv = _rms(y1[:, B_Q_LORA:B_Q_LORA + B_KV_LORA], gckv_ref[...]).astype(BF16)
    off = B_Q_LORA + B_KV_LORA
    kr = y1[:, off:off + B_PAD]
    krs = y1[:, off + B_PAD:off + 2 * B_PAD]
    q = _dot(c_q, wq_ref[...])
    qs = _dot(c_q, wqs_ref[...])
    kn = _dot(c_kv, wk_ref[...])
    v = _dot(c_kv, wv_ref[...])
    cq, sq, ck, sk = cq_ref[...], sq_ref[...], ck_ref[...], sk_ref[...]
    krs_s = krs * sk
    for n in range(B_HEADS):
        t = q[:, n * B_PAD:(n + 1) * B_PAD]
        ts = qs[:, n * B_PAD:(n + 1) * B_PAD]
        q_ref[n] = ((t * cq + ts * sq) * _head_norm(t, B_QK)).astype(BF16)
        t = kn[:, n * B_PAD:(n + 1) * B_PAD] + kr
        k_ref[n] = ((t * ck + krs_s) * _head_norm(t, B_QK)).astype(BF16)
        v_ref[n] = v[:, n * B_V:(n + 1) * B_V].astype(BF16)


def _pad_heads(w, lo, width):
    k, nh, _ = w.shape
    out = jnp.zeros((k, nh, B_PAD), w.dtype).at[:, :, lo:lo + width].set(w)
    return out.reshape(k, nh * B_PAD)


def _proj_b_call(x, gn, wdq, gcq, wuq, wdkv, gckv, wukv, gq, gk, pos):
    t_rows = x.shape[0]
    tm = TOKEN_TILE
    wuq3 = wuq.reshape(B_Q_LORA, B_HEADS, B_QK)
    wq = _pad_heads(wuq3, 0, B_QK).astype(BF16)
    wuq_rope_sw = _swap_halves_cols(wuq3[:, :, B_NOPE:].reshape(B_Q_LORA, -1), B_ROPE)
    wqs = _pad_heads(wuq_rope_sw.reshape(B_Q_LORA, B_HEADS, B_ROPE), B_NOPE, B_ROPE).astype(BF16)
    wukv3 = wukv.reshape(B_KV_LORA, B_HEADS, B_NOPE + B_V)
    wk = _pad_heads(wukv3[:, :, :B_NOPE], 0, B_NOPE).astype(BF16)
    wv = wukv3[:, :, B_NOPE:].reshape(B_KV_LORA, B_HEADS * B_V).astype(BF16)
    w_kr = wdkv[:, B_KV_LORA:]
    kr_pad = _pad_heads(w_kr[:, None, :], B_NOPE, B_ROPE)
    krs_pad = _pad_heads(_swap_halves_cols(w_kr, B_ROPE)[:, None, :], B_NOPE, B_ROPE)
    w1 = jnp.concatenate([wdq, wdkv[:, :B_KV_LORA], kr_pad, krs_pad], axis=1).astype(BF16)

    def tables(g, scale):
        c, s = _rope_tables(pos, g[B_NOPE:], scale)
        zeros = jnp.zeros((t_rows, B_PAD - B_QK), F32)
        c_full = jnp.concatenate(
            [jnp.broadcast_to(g[None, :B_NOPE] * scale, (t_rows, B_NOPE)), c, zeros], axis=-1)
        s_full = jnp.concatenate([jnp.zeros((t_rows, B_NOPE), F32), s, zeros], axis=-1)
        return c_full, s_full

    cq, sq = tables(gq, B_QK ** -0.5 * LOG2E)
    ck, sk = tables(gk, 1.0)
    tab = pl.BlockSpec((tm, B_PAD), lambda i: (i, 0))
    head_major = lambda w_: pl.BlockSpec((B_HEADS, tm, w_), lambda i: (0, i, 0))
    return pl.pallas_call(
        _proj_b_kernel,
        grid=(t_rows // tm,),
        in_specs=[pl.BlockSpec((tm, D_MODEL), lambda i: (i, 0)), _const_spec((1, D_MODEL)),
                  _const_spec(w1.shape), _const_spec((1, B_Q_LORA)), _const_spec((1, B_KV_LORA)),
                  _const_spec(wq.shape), _const_spec(wqs.shape), _const_spec(wk.shape),
                  _const_spec(wv.shape), tab, tab, tab, tab],
        out_specs=[head_major(B_PAD), head_major(B_PAD), head_major(B_V)],
        out_shape=[jax.ShapeDtypeStruct((B_HEADS, t_rows, B_PAD), BF16),
                   jax.ShapeDtypeStruct((B_HEADS, t_rows, B_PAD), BF16),
                   jax.ShapeDtypeStruct((B_HEADS, t_rows, B_V), BF16)],
        compiler_params=_params(("parallel",)),
        name="proj_b",
    )(x, gn.reshape(1, D_MODEL), w1, gcq.reshape(1, -1), gckv.reshape(1, -1), wq, wqs, wk, wv,
      cq, sq, ck, sk)


def _proj_c_kernel(x_ref, gn_ref, w_ref, gq_ref, gk_ref, e_ref, *refs):
    hd = HEAD_DIM
    width = C_HEADS * hd
    outs, scr_ref = refs[:-1], refs[-1]
    tm = x_ref.shape[0]
    h = _rms(x_ref[...], gn_ref[...]).astype(BF16)
    y = _dot(h, w_ref[...])
    gains = (gq_ref[...], gk_ref[...])
    for g, (_, dil) in enumerate(C_PATTERNS):
        for j in range(3):
            base = (g * 3 + j) * width
            val = y[:, base: base + width]
            if j < 2:
                val = val * _segment_rsqrt(val, e_ref, hd) * gains[j]
            out_ref = outs[g * 3 + j]
            if dil == 1:
                out_ref[...] = val.astype(BF16)
            else:
                for s in range(width // LANES):
                    scr_ref[s] = val[:, s * LANES:(s + 1) * LANES]
                for r in range(dil):
                    for s in range(width // LANES):
                        rows = scr_ref[s, pl.ds(r, tm // dil, stride=dil), :]
                        col = r * width + s * LANES
                        out_ref[:, col:col + LANES] = rows.astype(BF16)


def _proj_c_call(x, gn, wqkv, gq, gk):
    t_rows = x.shape[0]
    tm, hd = TOKEN_TILE // 2, HEAD_DIM
    width = C_HEADS * hd
    w = wqkv.astype(BF16)
    ones = _segment_ones(hd)
    specs, shapes = [], []
    for _, dil in C_PATTERNS:
        specs += [pl.BlockSpec((tm // dil, dil * width), lambda i: (i, 0))] * 3
        shapes += [jax.ShapeDtypeStruct((t_rows // dil, dil * width), BF16)] * 3
    outs = pl.pallas_call(
        _proj_c_kernel,
        grid=(t_rows // tm,),
        in_specs=[pl.BlockSpec((tm, D_MODEL), lambda i: (i, 0)), _const_spec((1, D_MODEL)),
                  _const_spec(w.shape), _const_spec((1, width)), _const_spec((1, width)),
                  _const_spec(ones.shape)],
        out_specs=specs,
        out_shape=shapes,
        scratch_shapes=[pltpu.VMEM((width // LANES, tm, LANES), F32)],
        compiler_params=_params(("parallel",)),
        name="proj_c",
    )(x, gn.reshape(1, D_MODEL), w, jnp.tile(gq * (hd ** -0.5 * LOG2E), C_HEADS).reshape(1, width),
      jnp.tile(gk, C_HEADS).reshape(1, width), ones)
    return [outs[3 * g:3 * g + 3] for g in range(len(C_PATTERNS))]


def _band_masked_dist(row0, lo, hi, sub, w):
    nk = sub + 2 * w
    r = lax.broadcasted_iota(jnp.int32, (sub, nk), 0)
    c = lax.broadcasted_iota(jnp.int32, (sub, nk), 1)
    dist = jnp.abs(r + w - c)
    jpos = row0 - w + lax.broadcasted_iota(jnp.int32, (1, nk), 1)
    outside = jnp.where(jpos < lo, w + 1, 0) + jnp.where(jpos >= hi, w + 1, 0)
    return jnp.where(dist + outside <= w, dist.astype(F32), MASK_DIST)


def _two_stage(n_items, first, second, ahead=2):
    pending = {i: first(i) for i in range(min(ahead, n_items))}
    results = []
    for i in range(n_items):
        if i + ahead < n_items:
            pending[i + ahead] = first(i + ahead)
        results.append(second(i, pending.pop(i)))
    return results


def _band_softmax_pv(s, vv, sink):
    m = jnp.max(s, axis=-1, keepdims=True)
    if sink is not None:
        m = jnp.maximum(m, sink)
    p = jnp.exp2(s - m)
    den = jnp.sum(p, axis=-1, keepdims=True)
    if sink is not None:
        den = den + jnp.exp2(sink - m)
    return _dot(p.astype(BF16), vv) / den, m + jnp.log2(den)


def _attn_a_kernel(slope_ref, sink_ref, q_ref, kp_ref, kc_ref, kn_ref, vp_ref, vc_ref, vn_ref,
                   o_ref, *, tq, p_len, s_len):
    w, sub = A_HALF_WINDOW, BAND_SUB
    group = A_HEADS // A_KV_HEADS
    kvh = pl.program_id(0)
    row0 = pl.program_id(1) * tq
    lo, hi = _seq_bounds(row0, p_len, s_len)
    kk = jnp.concatenate([kp_ref[0], kc_ref[0], kn_ref[0]], axis=0)
    vv = jnp.concatenate([vp_ref[0], vc_ref[0], vn_ref[0]], axis=0)
    sink = jnp.concatenate(
        [jnp.full((sub, 1), sink_ref[kvh * group + g], F32) for g in range(group)], axis=0)

    def keys(r):
        return slice(r * sub, (r + 1) * sub + 2 * w)

    def logits(r):
        md = _band_masked_dist(row0 + r * sub, lo, hi, sub, w)
        bias = jnp.concatenate([md * slope_ref[kvh * group + g] for g in range(group)], axis=0)
        q = q_ref[:, r * sub:(r + 1) * sub, :].reshape(group * sub, HEAD_DIM)
        return _dot_nt(q, kk[keys(r)]) + bias

    tiles = _two_stage(tq // sub, logits, lambda r, s: _band_softmax_pv(s, vv[keys(r)], sink)[0])
    outs = [[o[g * sub:(g + 1) * sub] for o in tiles] for g in range(group)]
    o_ref[...] = jnp.concatenate(
        [jnp.concatenate(rows, axis=0) for rows in outs], axis=-1).astype(BF16)


def _attn_a_call(q, k, v, sink, geo):
    p_len, s_len, t_rows = geo
    tq, w, hd = 512, A_HALF_WINDOW, HEAD_DIM
    group = A_HEADS // A_KV_HEADS
    per = tq // w
    last = t_rows // w - 1
    prev = pl.BlockSpec((1, w, hd), lambda h, i: (h, jnp.maximum(i * per - 1, 0), 0))
    cur = pl.BlockSpec((1, tq, hd), lambda h, i: (h, i, 0))
    nxt = pl.BlockSpec((1, w, hd), lambda h, i: (h, jnp.minimum((i + 1) * per, last), 0))
    smem = pl.BlockSpec(memory_space=pltpu.SMEM)
    slopes = jnp.asarray(-LOG2E * _alibi_slopes(A_HEADS), F32)
    return pl.pallas_call(
        functools.partial(_attn_a_kernel, tq=tq, p_len=p_len, s_len=s_len),
        grid=(A_KV_HEADS, t_rows // tq),
        in_specs=[smem, smem, pl.BlockSpec((group, tq, hd), lambda h, i: (h, i, 0)),
                  prev, cur, nxt, prev, cur, nxt],
        out_specs=pl.BlockSpec((tq, group * hd), lambda h, i: (i, h)),
        out_shape=jax.ShapeDtypeStruct((t_rows, A_HEADS * hd), BF16),
        compiler_params=_params(("parallel", "parallel")),
        name="attn_a",
    )(slopes, sink.astype(F32) * LOG2E, q, k, k, k, v, v, v)


def _attn_c_kernel(q_ref, kp_ref, kc_ref, kn_ref, vp_ref, vc_ref, vn_ref, o_ref, lse_ref,
                   *, tq, dil, p_len, s_len):
    w, hd, sub = C_STEPS, HEAD_DIM, min(BAND_SUB, tq)
    row0 = pl.program_id(0) * tq
    lo, hi = _seq_bounds(row0, p_len // dil, s_len // dil)
    slopes = -LOG2E * dil * _alibi_slopes(C_HEADS)
    mds = [_band_masked_dist(row0 + r * sub, lo, hi, sub, w) for r in range(tq // sub)]
    n_sub = tq // sub

    def window(prev_ref, cur_ref, next_ref, i):
        n, r = divmod(i, n_sub)
        hs = slice(n * hd, (n + 1) * hd)
        rows = jnp.concatenate([prev_ref[:, hs], cur_ref[:, hs], next_ref[:, hs]], axis=0)
        return rows[r * sub:(r + 1) * sub + 2 * w]

    def logits(i):
        n, r = divmod(i, n_sub)
        q = q_ref[r * sub:(r + 1) * sub, n * hd:(n + 1) * hd]
        return _dot_nt(q, window(kp_ref, kc_ref, kn_ref, i)) + mds[r] * float(slopes[n])

    def attend(i, s):
        o, lse2 = _band_softmax_pv(s, window(vp_ref, vc_ref, vn_ref, i), None)
        return o, jnp.broadcast_to(lse2 * (1.0 / LOG2E), (sub, hd))

    tiles = _two_stage(C_HEADS * n_sub, logits, attend)
    outs = [jnp.concatenate([t[0] for t in tiles[n * n_sub:(n + 1) * n_sub]], axis=0)
            for n in range(C_HEADS)]
    lses = [jnp.concatenate([t[1] for t in tiles[n * n_sub:(n + 1) * n_sub]], axis=0)
            for n in range(C_HEADS)]
    o_ref[...] = jnp.concatenate(outs, axis=-1)
    lse_ref[...] = jnp.concatenate(lses, axis=-1)


def _attn_c_call(q, k, v, dil, geo):
    p_len, s_len, t_rows = geo
    w, width = C_STEPS, C_HEADS * HEAD_DIM
    rows = t_rows // dil
    tq = min(512, s_len // dil)
    per = tq // w
    last = rows // w - 1
    prev = pl.BlockSpec((w, width), lambda i, r: (jnp.maximum(i * per - 1, 0), r))
    cur = pl.BlockSpec((tq, width), lambda i, r: (i, r))
    nxt = pl.BlockSpec((w, width), lambda i, r: (jnp.minimum((i + 1) * per, last), r))
    shape = jax.ShapeDtypeStruct((rows, dil * width), F32)
    return pl.pallas_call(
        functools.partial(_attn_c_kernel, tq=tq, dil=dil, p_len=p_len, s_len=s_len),
        grid=(rows // tq, dil),
        in_specs=[cur, prev, cur, nxt, prev, cur, nxt],
        out_specs=[cur, cur],
        out_shape=[shape, shape],
        compiler_params=_params(("parallel", "parallel")),
        name="attn_c_d%d" % dil,
    )(q, k, k, k, v, v, v)


def _attn_dense_kernel(qt_ref, k_ref, vt_ref, o_ref, sa_ref, sb_ref, sc_ref, sd_ref,
                       *, nq, tq, tk, p_len, s_len):
    dv = o_ref.shape[-2]
    width = nq * tq
    shift = int(math.log2(tk))

    def q_tile(j, _):
        row0 = (pl.program_id(1) * Q_TILES_PER_STEP + j) * tq
        lo, hi = _seq_bounds(row0, p_len, s_len)
        first = lax.shift_right_logical(lo, shift)
        n_chunks = lax.shift_right_logical(hi - lo, shift)

        def scores(c, st_ref):
            start = pl.multiple_of(c * tk, tk)
            qt = jnp.concatenate([qt_ref[g, j] for g in range(nq)], axis=-1)
            st = _dot(k_ref[0, pl.ds(start, tk), :], qt).astype(BF16)
            st_ref[...] = st
            return jnp.max(st, axis=0, keepdims=True).astype(F32)

        def update(c, st_ref, mc, m, acc):
            m_new = jnp.maximum(m, mc)
            alpha = jnp.exp2(m - m_new)
            pt = jnp.exp2(st_ref[...] - m_new.astype(BF16))
            return m_new, alpha * acc + _dot(vt_ref[0, c], pt)

        def quad(i, carry):
            m, acc, mc_a, mc_b = carry
            c = first + 4 * i
            mc_c = scores(c + 2, sc_ref)
            m, acc = update(c, sa_ref, mc_a, m, acc)
            mc_d = scores(c + 3, sd_ref)
            m, acc = update(c + 1, sb_ref, mc_b, m, acc)
            mc_a = scores(c + 4, sa_ref)
            m, acc = update(c + 2, sc_ref, mc_c, m, acc)
            mc_b = scores(c + 5, sb_ref)
            m, acc = update(c + 3, sd_ref, mc_d, m, acc)
            return m, acc, mc_a, mc_b

        carry = (jnp.full((1, width), NEG_INF, F32), jnp.zeros((vt_ref.shape[-2], width), F32),
                 scores(first, sa_ref), scores(first + 1, sb_ref))
        n_body = lax.shift_right_logical(n_chunks, 2) - 1
        carry = lax.fori_loop(0, lax.shift_right_logical(n_body, 1),
                              lambda i, cr: quad(2 * i + 1, quad(2 * i, cr)), carry)
        carry = lax.fori_loop(0, n_body & 1, lambda _, cr: quad(n_body - 1, cr), carry)
        m, acc, mc_a, mc_b = carry
        c = first + n_chunks - 4
        mc_c = scores(c + 2, sc_ref)
        m, acc = update(c, sa_ref, mc_a, m, acc)
        mc_d = scores(c + 3, sd_ref)
        m, acc = update(c + 1, sb_ref, mc_b, m, acc)
        m, acc = update(c + 2, sc_ref, mc_c, m, acc)
        m, acc = update(c + 3, sd_ref, mc_d, m, acc)
        o = (acc[:dv] / acc[dv:dv + 1]).astype(BF16)
        for g in range(nq):
            o_ref[g, j] = o[:, g * tq:(g + 1) * tq]
        return 0

    lax.fori_loop(0, Q_TILES_PER_STEP, q_tile, 0)


def _attn_dense_call(q, k, v, geo, tq, tk=512):
    p_len, s_len, t_rows = geo
    hq, _, dq = q.shape
    hk, _, dv = v.shape
    nq = hq // hk
    n_chunks, n_tiles, per = t_rows // tk, t_rows // tq, Q_TILES_PER_STEP
    assert s_len % (4 * tk) == 0 and p_len % (4 * tk) == 0 and n_tiles % per == 0
    qt = q.reshape(hq, n_tiles, tq, dq).transpose(0, 1, 3, 2)
    vt = v.reshape(hk, n_chunks, tk, dv).transpose(0, 1, 3, 2)
    ones_rows = jnp.zeros((hk, n_chunks, BF16_SUBLANES, tk), BF16).at[:, :, 0].set(1.0)
    vt = jnp.concatenate([vt, ones_rows], axis=2)
    o = pl.pallas_call(
        functools.partial(_attn_dense_kernel, nq=nq, tq=tq, tk=tk, p_len=p_len, s_len=s_len),
        grid=(hk, n_tiles // per),
        in_specs=[pl.BlockSpec((nq, per, dq, tq), lambda h, i: (h, i, 0, 0)),
                  pl.BlockSpec((1, t_rows, dq), lambda h, i: (h, 0, 0), pipeline_mode=pl.Buffered(1)),
                  pl.BlockSpec((1, n_chunks, dv + BF16_SUBLANES, tk), lambda h, i: (h, 0, 0, 0),
                               pipeline_mode=pl.Buffered(1))],
        out_specs=pl.BlockSpec((nq, per, dv, tq), lambda h, i: (h, i, 0, 0)),
        out_shape=jax.ShapeDtypeStruct((hq, n_tiles, dv, tq), BF16),
        scratch_shapes=[pltpu.VMEM((tk, nq * tq), BF16)] * 4,
        compiler_params=_params(("parallel", "parallel")),
        name="attn_dense_%d" % dq,
    )(qt, k, vt)
    return o.transpose(1, 3, 0, 2).reshape(t_rows, hq * dv)


def _trunk(x, p, geo, pos):
    for i in range(DEPTH):
        mixer, j = i % 4, i // 4
        x = _ffn_call(x, p["norm_ffn1"][i], p["ffn1_wg"][i], p["ffn1_wu"][i], p["ffn1_wd"][i])
        gn = p["norm_mix"][i]
        if mixer == 0:
            q, k, v = _proj_a_call(x, gn, p["a_wqkv"][j], p["a_gq"][j], p["a_gk"][j])
            pre = (_attn_a_call(q, k, v, p["a_sink"][j], geo), p["a_wo"][j])
        elif mixer == 1:
            q, k, v = _proj_b_call(x, gn, p["b_wdq"][j], p["b_gcq"][j], p["b_wuq"][j], p["b_wdkv"][j],
                                   p["b_gckv"][j], p["b_wukv"][j], p["b_gq"][j], p["b_gk"][j], pos)
            pre = (_attn_dense_call(q, k, v, geo, tq=512), p["b_wo"][j])
        elif mixer == 2:
            qkv = _proj_c_call(x, gn, p["c_wqkv"][j], p["c_gq"][j], p["c_gk"][j])
            res = [_attn_c_call(*qkv[g], dil, geo) for g, (_, dil) in enumerate(C_PATTERNS)]
            pre = ([r[0] for r in res], [r[1] for r in res], p["c_wo"][j])
        else:
            q, k, v = _proj_d_call(x, gn, p["d_wqkv"][j], p["d_gq"][j], p["d_gk"][j], pos)
            pre = (_attn_dense_call(q, k, v, geo, tq=128), p["d_wo"][j])
        x = _ffn_call(x, p["norm_ffn2"][i], p["ffn2_wg"][i], p["ffn2_wu"][i], p["ffn2_wd"][i], pre=pre)
    return x


def kernel(x_prompt, x_sample, norm_ffn1, ffn1_wg, ffn1_wu, ffn1_wd, norm_mix, norm_ffn2, ffn2_wg, ffn2_wu, ffn2_wd, a_wqkv, a_gq, a_gk, a_sink, a_wo, b_wdq, b_gcq, b_wuq, b_wdkv, b_gckv, b_wukv, b_gq, b_gk, b_wo, c_wqkv, c_gq, c_gk, c_wo, d_wqkv, d_gq, d_gk, d_wo):
    p = dict(norm_ffn1=norm_ffn1, ffn1_wg=ffn1_wg, ffn1_wu=ffn1_wu, ffn1_wd=ffn1_wd,
             norm_mix=norm_mix, norm_ffn2=norm_ffn2, ffn2_wg=ffn2_wg, ffn2_wu=ffn2_wu,
             ffn2_wd=ffn2_wd, a_wqkv=a_wqkv, a_gq=a_gq, a_gk=a_gk, a_sink=a_sink, a_wo=a_wo,
             b_wdq=b_wdq, b_gcq=b_gcq, b_wuq=b_wuq, b_wdkv=b_wdkv, b_gckv=b_gckv,
             b_wukv=b_wukv, b_gq=b_gq, b_gk=b_gk, b_wo=b_wo, c_wqkv=c_wqkv, c_gq=c_gq,
             c_gk=c_gk, c_wo=c_wo, d_wqkv=d_wqkv, d_gq=d_gq, d_gk=d_gk, d_wo=d_wo)
    n_p, p_seq, _ = x_prompt.shape
    n_s, s_len, _ = x_sample.shape
    assert n_p == 1 and s_len & (s_len - 1) == 0 and p_seq % s_len == 0
    p_len = n_p * p_seq
    t_rows = p_len + n_s * s_len
    geo = (p_len, s_len, t_rows)
    pos = jnp.concatenate([jnp.arange(p_len, dtype=jnp.int32),
                           jnp.tile(jnp.arange(s_len, dtype=jnp.int32), n_s)])
    x = jnp.concatenate([x_prompt.reshape(p_len, D_MODEL), x_sample.reshape(n_s * s_len, D_MODEL)])
    y = _trunk(x, p, geo, pos)
    return y[:p_len].reshape(x_prompt.shape), y[p_len:].reshape(x_sample.shape)
```

```python
import functools
import math

import numpy as np
import jax
import jax.numpy as jnp
from jax import lax
from jax.experimental import pallas as pl
from jax.experimental.pallas import tpu as pltpu

F32 = jnp.float32
BF16 = jnp.bfloat16

D_MODEL = 1024
DEPTH = 4
HEAD_DIM = 64
D_FF = 2816
NORM_EPS = 1e-6
NEG_INF = -1e30
GRID_W = 64
ROPE_THETA = 10000.0
LOG2E = math.log2(math.e)
LANES = 128
MXU_TILE = 256
BF16_SUBLANES = 16
BAND_SUB = 128
Q_TILES_PER_STEP = 4
MASK_DIST = 1e33

A_HEADS, A_KV_HEADS, A_HALF_WINDOW = 16, 4, 128
B_HEADS, B_Q_LORA, B_KV_LORA, B_NOPE, B_ROPE, B_V = 16, 512, 256, 64, 32, 64
B_QK = B_NOPE + B_ROPE
B_PAD = 128
C_HEADS = 8
C_PATTERNS = ((128, 1), (512, 4), (2048, 16))
C_STEPS = 64
D_HEADS, D_KV_HEADS = 16, 4

FF_CHUNK = 256
N_FF_CHUNKS = D_FF // FF_CHUNK
TOKEN_TILE = 512
VMEM_LIMIT = 56 * 1024 * 1024


def _alibi_slopes(n):
    return 2.0 ** (-8.0 * np.arange(1, n + 1) / n)


def _params(sem, vmem=VMEM_LIMIT):
    return pltpu.CompilerParams(dimension_semantics=sem, vmem_limit_bytes=vmem)


def _const_spec(shape):
    nd = len(shape)
    return pl.BlockSpec(tuple(shape), lambda *_: (0,) * nd, pipeline_mode=pl.Buffered(1))


def _seq_bounds(row0, p_len, s_len):
    in_prompt = row0 < p_len
    b = lax.shift_right_logical(jnp.maximum(row0 - p_len, 0), int(math.log2(s_len)))
    lo = jnp.where(in_prompt, 0, p_len + b * s_len)
    hi = jnp.where(in_prompt, p_len, lo + s_len)
    return lo, hi


def _rms(x, g):
    return x * lax.rsqrt(jnp.mean(x * x, axis=-1, keepdims=True) + NORM_EPS) * g


def _dot(a, b):
    return jnp.dot(a, b, preferred_element_type=F32)


def _dot_nt(a, b):
    return lax.dot_general(a, b, (((1,), (1,)), ((), ())), preferred_element_type=F32)


def _ffn_kernel(*refs, mode):
    if mode == "plain":
        x_ref, gn_ref, wg_ref, wu_ref, wd_ref, out_ref, acc_ref, act_a, act_b = refs
        x = x_ref[...]
    elif mode == "proj":
        (x_ref, o_ref, wo_ref, gn_ref, wg_ref, wu_ref, wd_ref, out_ref, acc_ref, act_a,
         act_b) = refs
        if len(o_ref.shape) == 3:
            o = jnp.concatenate([o_ref[n] for n in range(o_ref.shape[0])], axis=-1)
        else:
            o = o_ref[...]
        x = x_ref[...] + _dot(o, wo_ref[...])
    else:
        (x_ref, o0, o1, o2, l0, l1, l2, wo_ref, gn_ref, wg_ref, wu_ref, wd_ref,
         out_ref, acc_ref, act_a, act_b, *scr) = refs
        tm, width = x_ref.shape[0], wo_ref.shape[0]

        def token_major(ref, dil, scr_ref):
            slabs = width // LANES
            for r in range(dil):
                for s in range(slabs):
                    col = r * width + s * LANES
                    scr_ref[s, pl.ds(r, tm // dil, stride=dil), :] = ref[:, col:col + LANES]
            return jnp.concatenate([scr_ref[s] for s in range(slabs)], axis=-1)

        dils = [d for _, d in C_PATTERNS]
        assert dils[0] == 1
        oa, la = o0[...], l0[...]
        ob, lb = token_major(o1, dils[1], scr[0]), token_major(l1, dils[1], scr[1])
        oc, lc = token_major(o2, dils[2], scr[2]), token_major(l2, dils[2], scr[3])
        m = jnp.maximum(jnp.maximum(la, lb), lc)
        ea, eb, ec = jnp.exp(la - m), jnp.exp(lb - m), jnp.exp(lc - m)
        o = (ea * oa + eb * ob + ec * oc) / (ea + eb + ec)
        x = x_ref[...] + _dot(o.astype(BF16), wo_ref[...])

    h = _rms(x, gn_ref[...]).astype(BF16)

    def gate_up(c, a_ref):
        g = _dot(h, wg_ref[c])
        u = _dot(h, wu_ref[c])
        a_ref[...] = (g * jax.nn.sigmoid(g) * u).astype(BF16)

    def down(c, a_ref):
        acc_ref[...] += _dot(a_ref[...], wd_ref[c])

    assert N_FF_CHUNKS % 2 == 1
    acc_ref[...] = jnp.zeros_like(acc_ref)
    gate_up(0, act_a)

    def pair(i, carry):
        c = 2 * i
        gate_up(c + 1, act_b)
        down(c, act_a)
        gate_up(c + 2, act_a)
        down(c + 1, act_b)
        return carry

    lax.fori_loop(0, N_FF_CHUNKS // 2, pair, 0)
    down(N_FF_CHUNKS - 1, act_a)
    out_ref[...] = x + 0.5 * acc_ref[...]


def _ffn_weights(wg, wu, wd):
    wg3 = wg.astype(BF16).reshape(D_MODEL, N_FF_CHUNKS, FF_CHUNK).transpose(1, 0, 2)
    wu3 = wu.astype(BF16).reshape(D_MODEL, N_FF_CHUNKS, FF_CHUNK).transpose(1, 0, 2)
    wd3 = wd.astype(BF16).reshape(N_FF_CHUNKS, FF_CHUNK, D_MODEL)
    return wg3, wu3, wd3


def _ffn_call(x, gn, wg, wu, wd, pre=None):
    t_rows = x.shape[0]
    tm = TOKEN_TILE
    wg3, wu3, wd3 = _ffn_weights(wg, wu, wd)
    row = lambda w: pl.BlockSpec((tm, w), lambda i: (i, 0))
    ffn_args = [gn.reshape(1, D_MODEL), wg3, wu3, wd3]
    ffn_specs = [_const_spec((1, D_MODEL)), _const_spec(wg3.shape), _const_spec(wu3.shape),
                 _const_spec(wd3.shape)]
    scratch = []
    if pre is None:
        mode, args, specs = "plain", [x], [row(D_MODEL)]
    elif len(pre) == 2:
        o, wo = pre
        wo = wo.astype(BF16)
        mode = "proj"
        args = [x, o, wo]
        o_spec = row(o.shape[1]) if o.ndim == 2 else pl.BlockSpec(
            (o.shape[0], tm, o.shape[2]), lambda i: (0, i, 0))
        specs = [row(D_MODEL), o_spec, _const_spec(wo.shape)]
    else:
        outs, lses, wo = pre
        wo = wo.astype(BF16)
        mode = "merge"
        args = [x, *outs, *lses, wo]
        width = wo.shape[0]
        dilated = [pl.BlockSpec((tm // d, d * width), lambda i: (i, 0)) for _, d in C_PATTERNS]
        specs = [row(D_MODEL)] + dilated * 2 + [_const_spec(wo.shape)]
        scratch = [pltpu.VMEM((width // LANES, tm, LANES), F32)] * 4
    return pl.pallas_call(
        functools.partial(_ffn_kernel, mode=mode),
        grid=(t_rows // tm,),
        in_specs=specs + ffn_specs,
        out_specs=row(D_MODEL),
        out_shape=jax.ShapeDtypeStruct((t_rows, D_MODEL), F32),
        scratch_shapes=[pltpu.VMEM((tm, D_MODEL), F32)] + [pltpu.VMEM((tm, FF_CHUNK), BF16)] * 2
        + scratch,
        compiler_params=_params(("parallel",)),
        name="ffn_" + mode,
    )(*args, *ffn_args)


def _head_norm(t, width):
    return lax.rsqrt(jnp.sum(t * t, axis=-1, keepdims=True) * (1.0 / width) + NORM_EPS)


def _segment_ones(seg):
    n = MXU_TILE // seg
    return jnp.kron(jnp.eye(n, dtype=F32), jnp.ones((seg, seg), F32)).astype(BF16)


def _segment_rsqrt(t, e_ref, width):
    sq = t * t
    hi = sq.astype(BF16)
    lo = (sq - hi.astype(F32)).astype(BF16)
    e = e_ref[...]
    sums = [_dot(hi[:, b:b + MXU_TILE], e) + _dot(lo[:, b:b + MXU_TILE], e)
            for b in range(0, t.shape[-1], MXU_TILE)]
    ssq = sums[0] if len(sums) == 1 else jnp.concatenate(sums, axis=-1)
    return lax.rsqrt(ssq * (1.0 / width) + NORM_EPS)


def _store_vt(vt_ref, n, v_t):
    dv, rows = v_t.shape
    vt_ref[n, 0, :dv, :] = v_t.astype(BF16)
    first = lax.broadcasted_iota(jnp.int32, (BF16_SUBLANES, rows), 0) == 0
    vt_ref[n, 0, dv:, :] = jnp.where(first, 1.0, 0.0).astype(BF16)


def _proj_a_kernel(x_ref, gn_ref, w_ref, gain_ref, e_ref, q_ref, k_ref, v_ref):
    hd = HEAD_DIM
    nqk = (A_HEADS + A_KV_HEADS) * hd
    h = _rms(x_ref[...], gn_ref[...]).astype(BF16)
    y = _dot(h, w_ref[...])
    qk = y[:, :nqk]
    qk = (qk * _segment_rsqrt(qk, e_ref, hd) * gain_ref[...]).astype(BF16)
    for n in range(A_HEADS):
        q_ref[n] = qk[:, n * hd:(n + 1) * hd]
    for n in range(A_KV_HEADS):
        k_ref[n] = qk[:, (A_HEADS + n) * hd:(A_HEADS + n + 1) * hd]
        v_ref[n] = y[:, nqk + n * hd: nqk + (n + 1) * hd].astype(BF16)


def _proj_a_call(x, gn, wqkv, gq, gk):
    t_rows = x.shape[0]
    tm, hd = TOKEN_TILE, HEAD_DIM
    w = wqkv.astype(BF16)
    gain = jnp.concatenate([jnp.tile(gq * (hd ** -0.5 * LOG2E), A_HEADS), jnp.tile(gk, A_KV_HEADS)])
    ones = _segment_ones(hd)
    head_major = lambda n: pl.BlockSpec((n, tm, hd), lambda i: (0, i, 0))
    return pl.pallas_call(
        _proj_a_kernel,
        grid=(t_rows // tm,),
        in_specs=[pl.BlockSpec((tm, D_MODEL), lambda i: (i, 0)), _const_spec((1, D_MODEL)),
                  _const_spec(w.shape), _const_spec((1, gain.shape[0])), _const_spec(ones.shape)],
        out_specs=[head_major(A_HEADS), head_major(A_KV_HEADS), head_major(A_KV_HEADS)],
        out_shape=[jax.ShapeDtypeStruct((A_HEADS, t_rows, hd), BF16),
                   jax.ShapeDtypeStruct((A_KV_HEADS, t_rows, hd), BF16),
                   jax.ShapeDtypeStruct((A_KV_HEADS, t_rows, hd), BF16)],
        compiler_params=_params(("parallel",)),
        name="proj_a",
    )(x, gn.reshape(1, D_MODEL), w, gain.reshape(1, -1), ones)


def _proj_d_kernel(x_ref, gn_ref, w_ref, e_ref, cq_ref, sq_ref, ck_ref, sk_ref, qt_ref, k_ref, vt_ref):
    hd = HEAD_DIM
    nq, nk = D_HEADS * hd, D_KV_HEADS * hd
    tq = qt_ref.shape[-1]
    h = _rms(x_ref[...], gn_ref[...]).astype(BF16)
    y = _dot(h, w_ref[...])
    swapped = nq + 2 * nk
    t, ts = y[:, :nq + nk], y[:, swapped:swapped + nq + nk]
    r = _segment_rsqrt(t, e_ref, hd)
    for b in range((nq + nk) // LANES):
        is_q = b < nq // LANES
        c_ref, s_ref = (cq_ref, sq_ref) if is_q else (ck_ref, sk_ref)
        sl = slice(b * LANES, (b + 1) * LANES)
        blk = (t[:, sl] * c_ref[...] + ts[:, sl] * s_ref[...]) * r[:, sl]
        blk = blk.T.astype(BF16) if is_q else blk.astype(BF16)
        for half in range(LANES // hd):
            n = b * (LANES // hd) + half
            if is_q:
                for j in range(qt_ref.shape[1]):
                    qt_ref[n, j] = blk[half * hd:(half + 1) * hd, j * tq:(j + 1) * tq]
            else:
                k_ref[n - D_HEADS] = blk[:, half * hd:(half + 1) * hd]
    v_t = y[:, nq + nk: nq + 2 * nk].T
    for n in range(D_KV_HEADS):
        _store_vt(vt_ref, n, v_t[n * hd:(n + 1) * hd])


def _swap_halves_cols(w, group):
    k, n = w.shape
    return w.reshape(k, n // group, 2, group // 2)[:, :, ::-1, :].reshape(k, n)


def _rope_tables(pos, gain, scale):
    half = gain.shape[-1] // 2
    inv = ROPE_THETA ** (-jnp.arange(half, dtype=F32) / half)
    ang = pos.astype(F32)[:, None] * inv[None, :]
    cos, sin = jnp.cos(ang), jnp.sin(ang)
    g_sw = jnp.concatenate([gain[half:], gain[:half]])
    c = jnp.concatenate([cos, cos], axis=-1) * gain[None, :] * scale
    s = jnp.concatenate([-sin, sin], axis=-1) * g_sw[None, :] * scale
    return c, s


def _proj_d_call(x, gn, wqkv, gq, gk, pos, tq):
    t_rows = x.shape[0]
    tm, hd = TOKEN_TILE, HEAD_DIM
    nq, nk = D_HEADS * hd, D_KV_HEADS * hd
    half = hd // 2
    w = jnp.concatenate([wqkv, _swap_halves_cols(wqkv[:, :nq + nk], half)], axis=1).astype(BF16)
    rows, cols = pos // GRID_W, pos % GRID_W

    def tables(g, scale):
        cr, sr = _rope_tables(rows, g[:half], scale)
        cc, sc = _rope_tables(cols, g[half:], scale)
        reps = LANES // hd
        return jnp.concatenate([cr, cc] * reps, axis=-1), jnp.concatenate([sr, sc] * reps, axis=-1)

    cq, sq = tables(gq, hd ** -0.5 * LOG2E)
    ck, sk = tables(gk, 1.0)
    ones = _segment_ones(hd)
    head_major = lambda n: pl.BlockSpec((n, tm, hd), lambda i: (0, i, 0))
    tab = pl.BlockSpec((tm, LANES), lambda i: (i, 0))
    return pl.pallas_call(
        _proj_d_kernel,
        grid=(t_rows // tm,),
        in_specs=[pl.BlockSpec((tm, D_MODEL), lambda i: (i, 0)), _const_spec((1, D_MODEL)),
                  _const_spec(w.shape), _const_spec(ones.shape), tab, tab, tab, tab],
        out_specs=[pl.BlockSpec((D_HEADS, tm // tq, hd, tq), lambda i: (0, i, 0, 0)),
                   head_major(D_KV_HEADS),
                   pl.BlockSpec((D_KV_HEADS, 1, hd + BF16_SUBLANES, tm), lambda i: (0, i, 0, 0))],
        out_shape=[jax.ShapeDtypeStruct((D_HEADS, t_rows // tq, hd, tq), BF16),
                   jax.ShapeDtypeStruct((D_KV_HEADS, t_rows, hd), BF16),
                   jax.ShapeDtypeStruct((D_KV_HEADS, t_rows // tm, hd + BF16_SUBLANES, tm), BF16)],
        compiler_params=_params(("parallel",)),
        name="proj_d",
    )(x, gn.reshape(1, D_MODEL), w, ones, cq, sq, ck, sk)


def _proj_b_kernel(x_ref, gn_ref, w1_ref, gcq_ref, gckv_ref, wq_ref, wqs_ref, wk_ref, wv_ref,
                   cq_ref, sq_ref, ck_ref, sk_ref, qt_ref, k_ref, vt_ref):
    h = _rms(x_ref[...], gn_ref[...]).astype(BF16)
    y1 = _dot(h, w1_ref[...])
    c_q = _rms(y1[:, :B_Q_LORA], gcq_ref[...]).astype(BF16)
    c_kv = _rms(y1[:, B_Q_LORA:B_Q_LORA + B_KV_LORA], gckv_ref[...]).astype(BF16)
    off = B_Q_LORA + B_KV_LORA
    kr = y1[:, off:off + B_PAD]
    krs = y1[:, off + B_PAD:off + 2 * B_PAD]
    q = _dot(c_q, wq_ref[...])
    qs = _dot(c_q, wqs_ref[...])
    kn = _dot(c_kv, wk_ref[...])
    v = _dot(c_kv, wv_ref[...])
    cq, sq, ck, sk = cq_ref[...], sq_ref[...], ck_ref[...], sk_ref[...]
    krs_s = krs * sk
    for n in range(B_HEADS):
        t = q[:, n * B_PAD:(n + 1) * B_PAD]
        ts = qs[:, n * B_PAD:(n + 1) * B_PAD]
        qt_ref[n, 0] = ((t * cq + ts * sq) * _head_norm(t, B_QK)).T.astype(BF16)
        t = kn[:, n * B_PAD:(n + 1) * B_PAD] + kr
        k_ref[n] = ((t * ck + krs_s) * _head_norm(t, B_QK)).astype(BF16)
    v_t = v.T
    for n in range(B_HEADS):
        _store_vt(vt_ref, n, v_t[n * B_V:(n + 1) * B_V])


def _pad_heads(w, lo, width):
    k, nh, _ = w.shape
    out = jnp.zeros((k, nh, B_PAD), w.dtype).at[:, :, lo:lo + width].set(w)
    return out.reshape(k, nh * B_PAD)


def _proj_b_call(x, gn, wdq, gcq, wuq, wdkv, gckv, wukv, gq, gk, pos):
    t_rows = x.shape[0]
    tm = TOKEN_TILE
    wuq3 = wuq.reshape(B_Q_LORA, B_HEADS, B_QK)
    wq = _pad_heads(wuq3, 0, B_QK).astype(BF16)
    wuq_rope_sw = _swap_halves_cols(wuq3[:, :, B_NOPE:].reshape(B_Q_LORA, -1), B_ROPE)
    wqs = _pad_heads(wuq_rope_sw.reshape(B_Q_LORA, B_HEADS, B_ROPE), B_NOPE, B_ROPE).astype(BF16)
    wukv3 = wukv.reshape(B_KV_LORA, B_HEADS, B_NOPE + B_V)
    wk = _pad_heads(wukv3[:, :, :B_NOPE], 0, B_NOPE).astype(BF16)
    wv = wukv3[:, :, B_NOPE:].reshape(B_KV_LORA, B_HEADS * B_V).astype(BF16)
    w_kr = wdkv[:, B_KV_LORA:]
    kr_pad = _pad_heads(w_kr[:, None, :], B_NOPE, B_ROPE)
    krs_pad = _pad_heads(_swap_halves_cols(w_kr, B_ROPE)[:, None, :], B_NOPE, B_ROPE)
    w1 = jnp.concatenate([wdq, wdkv[:, :B_KV_LORA], kr_pad, krs_pad], axis=1).astype(BF16)

    def tables(g, scale):
        c, s = _rope_tables(pos, g[B_NOPE:], scale)
        zeros = jnp.zeros((t_rows, B_PAD - B_QK), F32)
        c_full = jnp.concatenate(
            [jnp.broadcast_to(g[None, :B_NOPE] * scale, (t_rows, B_NOPE)), c, zeros], axis=-1)
        s_full = jnp.concatenate([jnp.zeros((t_rows, B_NOPE), F32), s, zeros], axis=-1)
        return c_full, s_full

    cq, sq = tables(gq, B_QK ** -0.5 * LOG2E)
    ck, sk = tables(gk, 1.0)
    tab = pl.BlockSpec((tm, B_PAD), lambda i: (i, 0))
    head_major = lambda w_: pl.BlockSpec((B_HEADS, tm, w_), lambda i: (0, i, 0))
    return pl.pallas_call(
        _proj_b_kernel,
        grid=(t_rows // tm,),
        in_specs=[pl.BlockSpec((tm, D_MODEL), lambda i: (i, 0)), _const_spec((1, D_MODEL)),
                  _const_spec(w1.shape), _const_spec((1, B_Q_LORA)), _const_spec((1, B_KV_LORA)),
                  _const_spec(wq.shape), _const_spec(wqs.shape), _const_spec(wk.shape),
                  _const_spec(wv.shape), tab, tab, tab, tab],
        out_specs=[pl.BlockSpec((B_HEADS, 1, B_PAD, tm), lambda i: (0, i, 0, 0)),
                   head_major(B_PAD),
                   pl.BlockSpec((B_HEADS, 1, B_V + BF16_SUBLANES, tm), lambda i: (0, i, 0, 0))],
        out_shape=[jax.ShapeDtypeStruct((B_HEADS, t_rows // tm, B_PAD, tm), BF16),
                   jax.ShapeDtypeStruct((B_HEADS, t_rows, B_PAD), BF16),
                   jax.ShapeDtypeStruct((B_HEADS, t_rows // tm, B_V + BF16_SUBLANES, tm), BF16)],
        compiler_params=_params(("parallel",)),
        name="proj_b",
    )(x, gn.reshape(1, D_MODEL), w1, gcq.reshape(1, -1), gckv.reshape(1, -1), wq, wqs, wk, wv,
      cq, sq, ck, sk)


def _proj_c_kernel(x_ref, gn_ref, w_ref, gq_ref, gk_ref, e_ref, *refs):
    hd = HEAD_DIM
    width = C_HEADS * hd
    outs, scr_ref = refs[:-1], refs[-1]
    tm = x_ref.shape[0]
    h = _rms(x_ref[...], gn_ref[...]).astype(BF16)
    y = _dot(h, w_ref[...])
    gains = (gq_ref[...], gk_ref[...])
    for g, (_, dil) in enumerate(C_PATTERNS):
        for j in range(3):
            base = (g * 3 + j) * width
            val = y[:, base: base + width]
            if j < 2:
                val = val * _segment_rsqrt(val, e_ref, hd) * gains[j]
            out_ref = outs[g * 3 + j]
            if dil == 1:
                out_ref[...] = val.astype(BF16)
            else:
                for s in range(width // LANES):
                    scr_ref[s] = val[:, s * LANES:(s + 1) * LANES]
                for r in range(dil):
                    for s in range(width // LANES):
                        rows = scr_ref[s, pl.ds(r, tm // dil, stride=dil), :]
                        col = r * width + s * LANES
                        out_ref[:, col:col + LANES] = rows.astype(BF16)


def _proj_c_call(x, gn, wqkv, gq, gk):
    t_rows = x.shape[0]
    tm, hd = TOKEN_TILE // 2, HEAD_DIM
    width = C_HEADS * hd
    w = wqkv.astype(BF16)
    ones = _segment_ones(hd)
    specs, shapes = [], []
    for _, dil in C_PATTERNS:
        specs += [pl.BlockSpec((tm // dil, dil * width), lambda i: (i, 0))] * 3
        shapes += [jax.ShapeDtypeStruct((t_rows // dil, dil * width), BF16)] * 3
    outs = pl.pallas_call(
        _proj_c_kernel,
        grid=(t_rows // tm,),
        in_specs=[pl.BlockSpec((tm, D_MODEL), lambda i: (i, 0)), _const_spec((1, D_MODEL)),
                  _const_spec(w.shape), _const_spec((1, width)), _const_spec((1, width)),
                  _const_spec(ones.shape)],
        out_specs=specs,
        out_shape=shapes,
        scratch_shapes=[pltpu.VMEM((width // LANES, tm, LANES), F32)],
        compiler_params=_params(("parallel",)),
        name="proj_c",
    )(x, gn.reshape(1, D_MODEL), w, jnp.tile(gq * (hd ** -0.5 * LOG2E), C_HEADS).reshape(1, width),
      jnp.tile(gk, C_HEADS).reshape(1, width), ones)
    return [outs[3 * g:3 * g + 3] for g in range(len(C_PATTERNS))]


def _band_masked_dist(row0, lo, hi, sub, w):
    nk = sub + 2 * w
    r = lax.broadcasted_iota(jnp.int32, (sub, nk), 0)
    c = lax.broadcasted_iota(jnp.int32, (sub, nk), 1)
    dist = jnp.abs(r + w - c)
    jpos = row0 - w + lax.broadcasted_iota(jnp.int32, (1, nk), 1)
    outside = jnp.where(jpos < lo, w + 1, 0) + jnp.where(jpos >= hi, w + 1, 0)
    return jnp.where(dist + outside <= w, dist.astype(F32), MASK_DIST)


def _two_stage(n_items, first, second, ahead=2):
    pending = {i: first(i) for i in range(min(ahead, n_items))}
    results = []
    for i in range(n_items):
        if i + ahead < n_items:
            pending[i + ahead] = first(i + ahead)
        results.append(second(i, pending.pop(i)))
    return results


def _band_softmax_pv(s, vv, sink):
    m = jnp.max(s, axis=-1, keepdims=True)
    if sink is not None:
        m = jnp.maximum(m, sink)
    p = jnp.exp2(s - m)
    den = jnp.sum(p, axis=-1, keepdims=True)
    if sink is not None:
        den = den + jnp.exp2(sink - m)
    return _dot(p.astype(BF16), vv) / den, m + jnp.log2(den)


def _attn_a_kernel(slope_ref, sink_ref, q_ref, kp_ref, kc_ref, kn_ref, vp_ref, vc_ref, vn_ref,
                   o_ref, *, tq, p_len, s_len):
    w, sub = A_HALF_WINDOW, BAND_SUB
    group = A_HEADS // A_KV_HEADS
    kvh = pl.program_id(0)
    row0 = pl.program_id(1) * tq
    lo, hi = _seq_bounds(row0, p_len, s_len)
    kk = jnp.concatenate([kp_ref[0], kc_ref[0], kn_ref[0]], axis=0)
    vv = jnp.concatenate([vp_ref[0], vc_ref[0], vn_ref[0]], axis=0)
    sink = jnp.concatenate(
        [jnp.full((sub, 1), sink_ref[kvh * group + g], F32) for g in range(group)], axis=0)

    def keys(r):
        return slice(r * sub, (r + 1) * sub + 2 * w)

    def logits(r):
        md = _band_masked_dist(row0 + r * sub, lo, hi, sub, w)
        bias = jnp.concatenate([md * slope_ref[kvh * group + g] for g in range(group)], axis=0)
        q = q_ref[:, r * sub:(r + 1) * sub, :].reshape(group * sub, HEAD_DIM)
        return _dot_nt(q, kk[keys(r)]) + bias

    tiles = _two_stage(tq // sub, logits, lambda r, s: _band_softmax_pv(s, vv[keys(r)], sink)[0])
    outs = [[o[g * sub:(g + 1) * sub] for o in tiles] for g in range(group)]
    o_ref[...] = jnp.concatenate(
        [jnp.concatenate(rows, axis=0) for rows in outs], axis=-1).astype(BF16)


def _attn_a_call(q, k, v, sink, geo):
    p_len, s_len, t_rows = geo
    tq, w, hd = 512, A_HALF_WINDOW, HEAD_DIM
    group = A_HEADS // A_KV_HEADS
    per = tq // w
    last = t_rows // w - 1
    prev = pl.BlockSpec((1, w, hd), lambda h, i: (h, jnp.maximum(i * per - 1, 0), 0))
    cur = pl.BlockSpec((1, tq, hd), lambda h, i: (h, i, 0))
    nxt = pl.BlockSpec((1, w, hd), lambda h, i: (h, jnp.minimum((i + 1) * per, last), 0))
    smem = pl.BlockSpec(memory_space=pltpu.SMEM)
    slopes = jnp.asarray(-LOG2E * _alibi_slopes(A_HEADS), F32)
    return pl.pallas_call(
        functools.partial(_attn_a_kernel, tq=tq, p_len=p_len, s_len=s_len),
        grid=(A_KV_HEADS, t_rows // tq),
        in_specs=[smem, smem, pl.BlockSpec((group, tq, hd), lambda h, i: (h, i, 0)),
                  prev, cur, nxt, prev, cur, nxt],
        out_specs=pl.BlockSpec((tq, group * hd), lambda h, i: (i, h)),
        out_shape=jax.ShapeDtypeStruct((t_rows, A_HEADS * hd), BF16),
        compiler_params=_params(("parallel", "parallel")),
        name="attn_a",
    )(slopes, sink.astype(F32) * LOG2E, q, k, k, k, v, v, v)


def _attn_c_kernel(q_ref, kp_ref, kc_ref, kn_ref, vp_ref, vc_ref, vn_ref, o_ref, lse_ref,
                   *, tq, dil, p_len, s_len):
    w, hd, sub = C_STEPS, HEAD_DIM, min(BAND_SUB, tq)
    row0 = pl.program_id(0) * tq
    lo, hi = _seq_bounds(row0, p_len // dil, s_len // dil)
    slopes = -LOG2E * dil * _alibi_slopes(C_HEADS)
    mds = [_band_masked_dist(row0 + r * sub, lo, hi, sub, w) for r in range(tq // sub)]
    n_sub = tq // sub

    def window(prev_ref, cur_ref, next_ref, i):
        n, r = divmod(i, n_sub)
        hs = slice(n * hd, (n + 1) * hd)
        rows = jnp.concatenate([prev_ref[:, hs], cur_ref[:, hs], next_ref[:, hs]], axis=0)
        return rows[r * sub:(r + 1) * sub + 2 * w]

    def logits(i):
        n, r = divmod(i, n_sub)
        q = q_ref[r * sub:(r + 1) * sub, n * hd:(n + 1) * hd]
        return _dot_nt(q, window(kp_ref, kc_ref, kn_ref, i)) + mds[r] * float(slopes[n])

    def attend(i, s):
        o, lse2 = _band_softmax_pv(s, window(vp_ref, vc_ref, vn_ref, i), None)
        return o, jnp.broadcast_to(lse2 * (1.0 / LOG2E), (sub, hd))

    tiles = _two_stage(C_HEADS * n_sub, logits, attend)
    outs = [jnp.concatenate([t[0] for t in tiles[n * n_sub:(n + 1) * n_sub]], axis=0)
            for n in range(C_HEADS)]
    lses = [jnp.concatenate([t[1] for t in tiles[n * n_sub:(n + 1) * n_sub]], axis=0)
            for n in range(C_HEADS)]
    o_ref[...] = jnp.concatenate(outs, axis=-1)
    lse_ref[...] = jnp.concatenate(lses, axis=-1)


def _attn_c_call(q, k, v, dil, geo):
    p_len, s_len, t_rows = geo
    w, width = C_STEPS, C_HEADS * HEAD_DIM
    rows = t_rows // dil
    tq = min(512, s_len // dil)
    per = tq // w
    last = rows // w - 1
    prev = pl.BlockSpec((w, width), lambda i, r: (jnp.maximum(i * per - 1, 0), r))
    cur = pl.BlockSpec((tq, width), lambda i, r: (i, r))
    nxt = pl.BlockSpec((w, width), lambda i, r: (jnp.minimum((i + 1) * per, last), r))
    shape = jax.ShapeDtypeStruct((rows, dil * width), F32)
    return pl.pallas_call(
        functools.partial(_attn_c_kernel, tq=tq, dil=dil, p_len=p_len, s_len=s_len),
        grid=(rows // tq, dil),
        in_specs=[cur, prev, cur, nxt, prev, cur, nxt],
        out_specs=[cur, cur],
        out_shape=[shape, shape],
        compiler_params=_params(("parallel", "parallel")),
        name="attn_c_d%d" % dil,
    )(q, k, k, k, v, v, v)


def _attn_dense_kernel(qt_ref, k_ref, vt_ref, o_ref, sa_ref, sb_ref, sc_ref, sd_ref,
                       *, nq, tq, tk, p_len, s_len):
    dv = o_ref.shape[-1]
    width = nq * tq
    shift = int(math.log2(tk))

    def q_tile(j, _):
        row0 = (pl.program_id(1) * Q_TILES_PER_STEP + j) * tq
        lo, hi = _seq_bounds(row0, p_len, s_len)
        first = lax.shift_right_logical(lo, shift)
        n_chunks = lax.shift_right_logical(hi - lo, shift)

        def scores(c, st_ref):
            start = pl.multiple_of(c * tk, tk)
            qt = jnp.concatenate([qt_ref[g, j] for g in range(nq)], axis=-1)
            st = _dot(k_ref[0, pl.ds(start, tk), :], qt).astype(BF16)
            st_ref[...] = st
            return jnp.max(st, axis=0, keepdims=True).astype(F32)

        def update(c, st_ref, mc, m, acc):
            m_new = jnp.maximum(m, mc)
            alpha = jnp.exp2(m - m_new)
            pt = jnp.exp2(st_ref[...] - m_new.astype(BF16))
            return m_new, alpha * acc + _dot(vt_ref[0, c], pt)

        def quad(i, carry):
            m, acc, mc_a, mc_b = carry
            c = first + 4 * i
            mc_c = scores(c + 2, sc_ref)
            m, acc = update(c, sa_ref, mc_a, m, acc)
            mc_d = scores(c + 3, sd_ref)
            m, acc = update(c + 1, sb_ref, mc_b, m, acc)
            mc_a = scores(c + 4, sa_ref)
            m, acc = update(c + 2, sc_ref, mc_c, m, acc)
            mc_b = scores(c + 5, sb_ref)
            m, acc = update(c + 3, sd_ref, mc_d, m, acc)
            return m, acc, mc_a, mc_b

        carry = (jnp.full((1, width), NEG_INF, F32), jnp.zeros((vt_ref.shape[-2], width), F32),
                 scores(first, sa_ref), scores(first + 1, sb_ref))
        n_body = lax.shift_right_logical(n_chunks, 2) - 1
        carry = lax.fori_loop(0, lax.shift_right_logical(n_body, 1),
                              lambda i, cr: quad(2 * i + 1, quad(2 * i, cr)), carry)
        carry = lax.fori_loop(0, n_body & 1, lambda _, cr: quad(n_body - 1, cr), carry)
        m, acc, mc_a, mc_b = carry
        c = first + n_chunks - 4
        mc_c = scores(c + 2, sc_ref)
        m, acc = update(c, sa_ref, mc_a, m, acc)
        mc_d = scores(c + 3, sd_ref)
        m, acc = update(c + 1, sb_ref, mc_b, m, acc)
        m, acc = update(c + 2, sc_ref, mc_c, m, acc)
        m, acc = update(c + 3, sd_ref, mc_d, m, acc)
        o = acc[:dv] / acc[dv:dv + 1]
        rows = pl.ds(pl.multiple_of(j * tq, tq), tq)
        for g in range(nq):
            o_ref[g, rows, :] = o[:, g * tq:(g + 1) * tq].T.astype(BF16)
        return 0

    lax.fori_loop(0, Q_TILES_PER_STEP, q_tile, 0)


def _attn_dense_call(qt, k, vt, geo):
    p_len, s_len, t_rows = geo
    hq, n_tiles, dq, tq = qt.shape
    hk, n_chunks, rows_v, tk = vt.shape
    dv = rows_v - BF16_SUBLANES
    nq, per = hq // hk, Q_TILES_PER_STEP
    assert s_len % (4 * tk) == 0 and p_len % (4 * tk) == 0 and n_tiles % per == 0
    return pl.pallas_call(
        functools.partial(_attn_dense_kernel, nq=nq, tq=tq, tk=tk, p_len=p_len, s_len=s_len),
        grid=(hk, n_tiles // per),
        in_specs=[pl.BlockSpec((nq, per, dq, tq), lambda h, i: (h, i, 0, 0)),
                  pl.BlockSpec((1, t_rows, dq), lambda h, i: (h, 0, 0), pipeline_mode=pl.Buffered(1)),
                  pl.BlockSpec((1, n_chunks, dv + BF16_SUBLANES, tk), lambda h, i: (h, 0, 0, 0),
                               pipeline_mode=pl.Buffered(1))],
        out_specs=pl.BlockSpec((nq, per * tq, dv), lambda h, i: (h, i, 0)),
        out_shape=jax.ShapeDtypeStruct((hq, t_rows, dv), BF16),
        scratch_shapes=[pltpu.VMEM((tk, nq * tq), BF16)] * 4,
        compiler_params=_params(("parallel", "parallel")),
        name="attn_dense_%d" % dq,
    )(qt, k, vt)


def _trunk(x, p, geo, pos):
    for i in range(DEPTH):
        mixer, j = i % 4, i // 4
        x = _ffn_call(x, p["norm_ffn1"][i], p["ffn1_wg"][i], p["ffn1_wu"][i], p["ffn1_wd"][i])
        gn = p["norm_mix"][i]
        if mixer == 0:
            q, k, v = _proj_a_call(x, gn, p["a_wqkv"][j], p["a_gq"][j], p["a_gk"][j])
            pre = (_attn_a_call(q, k, v, p["a_sink"][j], geo), p["a_wo"][j])
        elif mixer == 1:
            qt, k, vt = _proj_b_call(x, gn, p["b_wdq"][j], p["b_gcq"][j], p["b_wuq"][j], p["b_wdkv"][j],
                                     p["b_gckv"][j], p["b_wukv"][j], p["b_gq"][j], p["b_gk"][j], pos)
            pre = (_attn_dense_call(qt, k, vt, geo), p["b_wo"][j])
        elif mixer == 2:
            qkv = _proj_c_call(x, gn, p["c_wqkv"][j], p["c_gq"][j], p["c_gk"][j])
            res = [_attn_c_call(*qkv[g], dil, geo) for g, (_, dil) in enumerate(C_PATTERNS)]
            pre = ([r[0] for r in res], [r[1] for r in res], p["c_wo"][j])
        else:
            qt, k, vt = _proj_d_call(x, gn, p["d_wqkv"][j], p["d_gq"][j], p["d_gk"][j], pos, tq=128)
            pre = (_attn_dense_call(qt, k, vt, geo), p["d_wo"][j])
        x = _ffn_call(x, p["norm_ffn2"][i], p["ffn2_wg"][i], p["ffn2_wu"][i], p["ffn2_wd"][i], pre=pre)
    return x


def kernel(x_prompt, x_sample, norm_ffn1, ffn1_wg, ffn1_wu, ffn1_wd, norm_mix, norm_ffn2, ffn2_wg, ffn2_wu, ffn2_wd, a_wqkv, a_gq, a_gk, a_sink, a_wo, b_wdq, b_gcq, b_wuq, b_wdkv, b_gckv, b_wukv, b_gq, b_gk, b_wo, c_wqkv, c_gq, c_gk, c_wo, d_wqkv, d_gq, d_gk, d_wo):
    p = dict(norm_ffn1=norm_ffn1, ffn1_wg=ffn1_wg, ffn1_wu=ffn1_wu, ffn1_wd=ffn1_wd,
             norm_mix=norm_mix, norm_ffn2=norm_ffn2, ffn2_wg=ffn2_wg, ffn2_wu=ffn2_wu,
             ffn2_wd=ffn2_wd, a_wqkv=a_wqkv, a_gq=a_gq, a_gk=a_gk, a_sink=a_sink, a_wo=a_wo,
             b_wdq=b_wdq, b_gcq=b_gcq, b_wuq=b_wuq, b_wdkv=b_wdkv, b_gckv=b_gckv,
             b_wukv=b_wukv, b_gq=b_gq, b_gk=b_gk, b_wo=b_wo, c_wqkv=c_wqkv, c_gq=c_gq,
             c_gk=c_gk, c_wo=c_wo, d_wqkv=d_wqkv, d_gq=d_gq, d_gk=d_gk, d_wo=d_wo)
    n_p, p_seq, _ = x_prompt.shape
    n_s, s_len, _ = x_sample.shape
    assert n_p == 1 and s_len & (s_len - 1) == 0 and p_seq % s_len == 0
    p_len = n_p * p_seq
    t_rows = p_len + n_s * s_len
    geo = (p_len, s_len, t_rows)
    pos = jnp.concatenate([jnp.arange(p_len, dtype=jnp.int32),
                           jnp.tile(jnp.arange(s_len, dtype=jnp.int32), n_s)])
    x = jnp.concatenate([x_prompt.reshape(p_len, D_MODEL), x_sample.reshape(n_s * s_len, D_MODEL)])
    y = _trunk(x, p, geo, pos)
    return y[:p_len].reshape(x_prompt.shape), y[p_len:].reshape(x_sample.shape)
```

```python
import functools
import math

import numpy as np
import jax
import jax.numpy as jnp
from jax import lax
from jax.experimental import pallas as pl
from jax.experimental.pallas import tpu as pltpu

F32 = jnp.float32
BF16 = jnp.bfloat16

D_MODEL = 1024
DEPTH = 4
HEAD_DIM = 64
D_FF = 2816
NORM_EPS = 1e-6
NEG_INF = -1e30
GRID_W = 64
ROPE_THETA = 10000.0
LOG2E = math.log2(math.e)
LANES = 128
MXU_TILE = 256
BF16_SUBLANES = 16
BAND_SUB = 128
Q_TILES_PER_STEP = 4
MASK_DIST = 1e33

A_HEADS, A_KV_HEADS, A_HALF_WINDOW = 16, 4, 128
B_HEADS, B_Q_LORA, B_KV_LORA, B_NOPE, B_ROPE, B_V = 16, 512, 256, 64, 32, 64
B_QK = B_NOPE + B_ROPE
B_PAD = 128
C_HEADS = 8
C_PATTERNS = ((128, 1), (512, 4), (2048, 16))
C_STEPS = 64
D_HEADS, D_KV_HEADS = 16, 4

FF_CHUNK = 256
N_FF_CHUNKS = D_FF // FF_CHUNK
TOKEN_TILE = 512
VMEM_LIMIT = 56 * 1024 * 1024


def _alibi_slopes(n):
    return 2.0 ** (-8.0 * np.arange(1, n + 1) / n)


def _params(sem, vmem=VMEM_LIMIT):
    return pltpu.CompilerParams(dimension_semantics=sem, vmem_limit_bytes=vmem)


def _const_spec(shape):
    nd = len(shape)
    return pl.BlockSpec(tuple(shape), lambda *_: (0,) * nd, pipeline_mode=pl.Buffered(1))


def _seq_bounds(row0, p_len, s_len):
    in_prompt = row0 < p_len
    b = lax.shift_right_logical(jnp.maximum(row0 - p_len, 0), int(math.log2(s_len)))
    lo = jnp.where(in_prompt, 0, p_len + b * s_len)
    hi = jnp.where(in_prompt, p_len, lo + s_len)
    return lo, hi


def _rms(x, g):
    return x * lax.rsqrt(jnp.mean(x * x, axis=-1, keepdims=True) + NORM_EPS) * g


def _dot(a, b):
    return jnp.dot(a, b, preferred_element_type=F32)


def _dot_nt(a, b):
    return lax.dot_general(a, b, (((1,), (1,)), ((), ())), preferred_element_type=F32)


def _ffn_kernel(*refs, mode, first_tiles=0):
    if mode == "plain":
        x_ref, gn_ref, wg_ref, wu_ref, wd_ref, out_ref, acc_ref, act_a, act_b = refs
        x = x_ref[...]
    elif mode == "plain2":
        xa_ref, xb_ref, gn_ref, wg_ref, wu_ref, wd_ref, out_ref, acc_ref, act_a, act_b = refs
        x = jnp.where(pl.program_id(0) < first_tiles, xa_ref[...], xb_ref[...])
    elif mode == "proj":
        (x_ref, o_ref, wo_ref, gn_ref, wg_ref, wu_ref, wd_ref, out_ref, acc_ref, act_a,
         act_b) = refs
        if len(o_ref.shape) == 3:
            o = jnp.concatenate([o_ref[n] for n in range(o_ref.shape[0])], axis=-1)
        else:
            o = o_ref[...]
        x = x_ref[...] + _dot(o, wo_ref[...])
    else:
        (x_ref, o0, o1, o2, l0, l1, l2, wo_ref, gn_ref, wg_ref, wu_ref, wd_ref,
         out_ref, acc_ref, act_a, act_b, *scr) = refs
        tm, width = x_ref.shape[0], wo_ref.shape[0]

        def token_major(ref, dil, scr_ref):
            slabs = width // LANES
            for r in range(dil):
                for s in range(slabs):
                    col = r * width + s * LANES
                    scr_ref[s, pl.ds(r, tm // dil, stride=dil), :] = ref[:, col:col + LANES]
            return jnp.concatenate([scr_ref[s] for s in range(slabs)], axis=-1)

        dils = [d for _, d in C_PATTERNS]
        assert dils[0] == 1
        oa, la = o0[...], l0[...]
        ob, lb = token_major(o1, dils[1], scr[0]), token_major(l1, dils[1], scr[1])
        oc, lc = token_major(o2, dils[2], scr[2]), token_major(l2, dils[2], scr[3])
        m = jnp.maximum(jnp.maximum(la, lb), lc)
        ea, eb, ec = jnp.exp(la - m), jnp.exp(lb - m), jnp.exp(lc - m)
        o = (ea * oa + eb * ob + ec * oc) / (ea + eb + ec)
        x = x_ref[...] + _dot(o.astype(BF16), wo_ref[...])

    h = _rms(x, gn_ref[...]).astype(BF16)

    def gate_up(c, a_ref):
        g = _dot(h, wg_ref[c])
        u = _dot(h, wu_ref[c])
        a_ref[...] = (g * jax.nn.sigmoid(g) * u).astype(BF16)

    def down(c, a_ref):
        acc_ref[...] += _dot(a_ref[...], wd_ref[c])

    assert N_FF_CHUNKS % 2 == 1
    acc_ref[...] = jnp.zeros_like(acc_ref)
    gate_up(0, act_a)

    def pair(i, carry):
        c = 2 * i
        gate_up(c + 1, act_b)
        down(c, act_a)
        gate_up(c + 2, act_a)
        down(c + 1, act_b)
        return carry

    lax.fori_loop(0, N_FF_CHUNKS // 2, pair, 0)
    down(N_FF_CHUNKS - 1, act_a)
    out_ref[...] = x + 0.5 * acc_ref[...]


def _ffn_weights(wg, wu, wd):
    wg3 = wg.astype(BF16).reshape(D_MODEL, N_FF_CHUNKS, FF_CHUNK).transpose(1, 0, 2)
    wu3 = wu.astype(BF16).reshape(D_MODEL, N_FF_CHUNKS, FF_CHUNK).transpose(1, 0, 2)
    wd3 = wd.astype(BF16).reshape(N_FF_CHUNKS, FF_CHUNK, D_MODEL)
    return wg3, wu3, wd3


def _ffn_call(x, gn, wg, wu, wd, pre=None, tiles=None):
    tm = TOKEN_TILE
    xs = x if isinstance(x, tuple) else (x,)
    off, n_tiles = tiles if tiles is not None else (0, sum(a.shape[0] for a in xs) // tm)
    wg3, wu3, wd3 = _ffn_weights(wg, wu, wd)
    row = lambda w: pl.BlockSpec((tm, w), lambda i: (i + off, 0))
    ffn_args = [gn.reshape(1, D_MODEL), wg3, wu3, wd3]
    ffn_specs = [_const_spec((1, D_MODEL)), _const_spec(wg3.shape), _const_spec(wu3.shape),
                 _const_spec(wd3.shape)]
    scratch = []
    first_tiles = 0
    if len(xs) == 2:
        assert pre is None and tiles is None
        first_tiles = xs[0].shape[0] // tm
        last = xs[1].shape[0] // tm - 1
        mode, args = "plain2", list(xs)
        specs = [pl.BlockSpec((tm, D_MODEL), lambda i: (jnp.minimum(i, first_tiles - 1), 0)),
                 pl.BlockSpec((tm, D_MODEL), lambda i: (jnp.clip(i - first_tiles, 0, last), 0))]
    elif pre is None:
        mode, args, specs = "plain", [x], [row(D_MODEL)]
    elif len(pre) == 2:
        o, wo = pre
        wo = wo.astype(BF16)
        mode = "proj"
        args = [x, o, wo]
        o_spec = row(o.shape[1]) if o.ndim == 2 else pl.BlockSpec(
            (o.shape[0], tm, o.shape[2]), lambda i: (0, i + off, 0))
        specs = [row(D_MODEL), o_spec, _const_spec(wo.shape)]
    else:
        outs, lses, wo = pre
        wo = wo.astype(BF16)
        mode = "merge"
        args = [x, *outs, *lses, wo]
        width = wo.shape[0]
        dilated = [pl.BlockSpec((tm // d, d * width), lambda i: (i + off, 0)) for _, d in C_PATTERNS]
        specs = [row(D_MODEL)] + dilated * 2 + [_const_spec(wo.shape)]
        scratch = [pltpu.VMEM((width // LANES, tm, LANES), F32)] * 4
    return pl.pallas_call(
        functools.partial(_ffn_kernel, mode=mode, first_tiles=first_tiles),
        grid=(n_tiles,),
        in_specs=specs + ffn_specs,
        out_specs=pl.BlockSpec((tm, D_MODEL), lambda i: (i, 0)),
        out_shape=jax.ShapeDtypeStruct((n_tiles * tm, D_MODEL), F32),
        scratch_shapes=[pltpu.VMEM((tm, D_MODEL), F32)] + [pltpu.VMEM((tm, FF_CHUNK), BF16)] * 2
        + scratch,
        compiler_params=_params(("parallel",)),
        name="ffn_" + mode,
    )(*args, *ffn_args)


def _head_norm(t, width):
    return lax.rsqrt(jnp.sum(t * t, axis=-1, keepdims=True) * (1.0 / width) + NORM_EPS)


def _segment_ones(seg):
    n = MXU_TILE // seg
    return jnp.kron(jnp.eye(n, dtype=F32), jnp.ones((seg, seg), F32)).astype(BF16)


def _segment_rsqrt(t, e_ref, width):
    sq = t * t
    hi = sq.astype(BF16)
    lo = (sq - hi.astype(F32)).astype(BF16)
    e = e_ref[...]
    sums = [_dot(hi[:, b:b + MXU_TILE], e) + _dot(lo[:, b:b + MXU_TILE], e)
            for b in range(0, t.shape[-1], MXU_TILE)]
    ssq = sums[0] if len(sums) == 1 else jnp.concatenate(sums, axis=-1)
    return lax.rsqrt(ssq * (1.0 / width) + NORM_EPS)


def _store_vt(vt_ref, n, v_t):
    dv, rows = v_t.shape
    vt_ref[n, 0, :dv, :] = v_t.astype(BF16)
    first = lax.broadcasted_iota(jnp.int32, (BF16_SUBLANES, rows), 0) == 0
    vt_ref[n, 0, dv:, :] = jnp.where(first, 1.0, 0.0).astype(BF16)


def _proj_a_kernel(x_ref, gn_ref, w_ref, gain_ref, e_ref, q_ref, k_ref, v_ref):
    hd = HEAD_DIM
    nqk = (A_HEADS + A_KV_HEADS) * hd
    h = _rms(x_ref[...], gn_ref[...]).astype(BF16)
    y = _dot(h, w_ref[...])
    qk = y[:, :nqk]
    qk = (qk * _segment_rsqrt(qk, e_ref, hd) * gain_ref[...]).astype(BF16)
    for n in range(A_HEADS):
        q_ref[n] = qk[:, n * hd:(n + 1) * hd]
    for n in range(A_KV_HEADS):
        k_ref[n] = qk[:, (A_HEADS + n) * hd:(A_HEADS + n + 1) * hd]
        v_ref[n] = y[:, nqk + n * hd: nqk + (n + 1) * hd].astype(BF16)


def _proj_a_call(x, gn, wqkv, gq, gk):
    t_rows = x.shape[0]
    tm, hd = TOKEN_TILE, HEAD_DIM
    w = wqkv.astype(BF16)
    gain = jnp.concatenate([jnp.tile(gq * (hd ** -0.5 * LOG2E), A_HEADS), jnp.tile(gk, A_KV_HEADS)])
    ones = _segment_ones(hd)
    head_major = lambda n: pl.BlockSpec((n, tm, hd), lambda i: (0, i, 0))
    return pl.pallas_call(
        _proj_a_kernel,
        grid=(t_rows // tm,),
        in_specs=[pl.BlockSpec((tm, D_MODEL), lambda i: (i, 0)), _const_spec((1, D_MODEL)),
                  _const_spec(w.shape), _const_spec((1, gain.shape[0])), _const_spec(ones.shape)],
        out_specs=[head_major(A_HEADS), head_major(A_KV_HEADS), head_major(A_KV_HEADS)],
        out_shape=[jax.ShapeDtypeStruct((A_HEADS, t_rows, hd), BF16),
                   jax.ShapeDtypeStruct((A_KV_HEADS, t_rows, hd), BF16),
                   jax.ShapeDtypeStruct((A_KV_HEADS, t_rows, hd), BF16)],
        compiler_params=_params(("parallel",)),
        name="proj_a",
    )(x, gn.reshape(1, D_MODEL), w, gain.reshape(1, -1), ones)


def _proj_d_kernel(x_ref, gn_ref, w_ref, e_ref, cq_ref, sq_ref, ck_ref, sk_ref, qt_ref, k_ref, vt_ref):
    hd = HEAD_DIM
    nq, nk = D_HEADS * hd, D_KV_HEADS * hd
    tq = qt_ref.shape[-1]
    h = _rms(x_ref[...], gn_ref[...]).astype(BF16)
    y = _dot(h, w_ref[...])
    swapped = nq + 2 * nk
    t, ts = y[:, :nq + nk], y[:, swapped:swapped + nq + nk]
    r = _segment_rsqrt(t, e_ref, hd)
    for b in range((nq + nk) // LANES):
        is_q = b < nq // LANES
        c_ref, s_ref = (cq_ref, sq_ref) if is_q else (ck_ref, sk_ref)
        sl = slice(b * LANES, (b + 1) * LANES)
        blk = (t[:, sl] * c_ref[...] + ts[:, sl] * s_ref[...]) * r[:, sl]
        blk = blk.T.astype(BF16) if is_q else blk.astype(BF16)
        for half in range(LANES // hd):
            n = b * (LANES // hd) + half
            if is_q:
                for j in range(qt_ref.shape[1]):
                    qt_ref[n, j] = blk[half * hd:(half + 1) * hd, j * tq:(j + 1) * tq]
            else:
                k_ref[n - D_HEADS] = blk[:, half * hd:(half + 1) * hd]
    v_t = y[:, nq + nk: nq + 2 * nk].T
    for n in range(D_KV_HEADS):
        _store_vt(vt_ref, n, v_t[n * hd:(n + 1) * hd])


def _swap_halves_cols(w, group):
    k, n = w.shape
    return w.reshape(k, n // group, 2, group // 2)[:, :, ::-1, :].reshape(k, n)


def _rope_tables(pos, gain, scale):
    half = gain.shape[-1] // 2
    inv = ROPE_THETA ** (-jnp.arange(half, dtype=F32) / half)
    ang = pos.astype(F32)[:, None] * inv[None, :]
    cos, sin = lax.optimization_barrier((jnp.cos(ang), jnp.sin(ang)))
    g_sw = jnp.concatenate([gain[half:], gain[:half]])
    c = jnp.concatenate([cos, cos], axis=-1) * gain[None, :] * scale
    s = jnp.concatenate([-sin, sin], axis=-1) * g_sw[None, :] * scale
    return c, s


def _proj_d_call(x, gn, wqkv, gq, gk, pos, tq):
    t_rows = x.shape[0]
    tm, hd = TOKEN_TILE, HEAD_DIM
    nq, nk = D_HEADS * hd, D_KV_HEADS * hd
    half = hd // 2
    w = jnp.concatenate([wqkv, _swap_halves_cols(wqkv[:, :nq + nk], half)], axis=1).astype(BF16)
    rows, cols = pos // GRID_W, pos % GRID_W

    def tables(g, scale):
        cr, sr = _rope_tables(rows, g[:half], scale)
        cc, sc = _rope_tables(cols, g[half:], scale)
        reps = LANES // hd
        return jnp.concatenate([cr, cc] * reps, axis=-1), jnp.concatenate([sr, sc] * reps, axis=-1)

    cq, sq = tables(gq, hd ** -0.5 * LOG2E)
    ck, sk = tables(gk, 1.0)
    ones = _segment_ones(hd)
    head_major = lambda n: pl.BlockSpec((n, tm, hd), lambda i: (0, i, 0))
    tab = pl.BlockSpec((tm, LANES), lambda i: (i, 0))
    return pl.pallas_call(
        _proj_d_kernel,
        grid=(t_rows // tm,),
        in_specs=[pl.BlockSpec((tm, D_MODEL), lambda i: (i, 0)), _const_spec((1, D_MODEL)),
                  _const_spec(w.shape), _const_spec(ones.shape), tab, tab, tab, tab],
        out_specs=[pl.BlockSpec((D_HEADS, tm // tq, hd, tq), lambda i: (0, i, 0, 0)),
                   head_major(D_KV_HEADS),
                   pl.BlockSpec((D_KV_HEADS, 1, hd + BF16_SUBLANES, tm), lambda i: (0, i, 0, 0))],
        out_shape=[jax.ShapeDtypeStruct((D_HEADS, t_rows // tq, hd, tq), BF16),
                   jax.ShapeDtypeStruct((D_KV_HEADS, t_rows, hd), BF16),
                   jax.ShapeDtypeStruct((D_KV_HEADS, t_rows // tm, hd + BF16_SUBLANES, tm), BF16)],
        compiler_params=_params(("parallel",)),
        name="proj_d",
    )(x, gn.reshape(1, D_MODEL), w, ones, cq, sq, ck, sk)


def _proj_b_kernel(x_ref, gn_ref, w1_ref, gcq_ref, gckv_ref, wq_ref, wqs_ref, wk_ref, wv_ref,
                   cq_ref, sq_ref, ck_ref, sk_ref, qt_ref, k_ref, vt_ref):
    h = _rms(x_ref[...], gn_ref[...]).astype(BF16)
    y1 = _dot(h, w1_ref[...])
    c_q = _rms(y1[:, :B_Q_LORA], gcq_ref[...]).astype(BF16)
    c_kv = _rms(y1[:, B_Q_LORA:B_Q_LORA + B_KV_LORA], gckv_ref[...]).astype(BF16)
    off = B_Q_LORA + B_KV_LORA
    kr = y1[:, off:off + B_PAD]
    krs = y1[:, off + B_PAD:off + 2 * B_PAD]
    q = _dot(c_q, wq_ref[...])
    qs = _dot(c_q, wqs_ref[...])
    kn = _dot(c_kv, wk_ref[...])
    v = _dot(c_kv, wv_ref[...])
    cq, sq, ck, sk = cq_ref[...], sq_ref[...], ck_ref[...], sk_ref[...]
    krs_s = krs * sk
    for n in range(B_HEADS):
        t = q[:, n * B_PAD:(n + 1) * B_PAD]
        ts = qs[:, n * B_PAD:(n + 1) * B_PAD]
        qt_ref[n, 0] = ((t * cq + ts * sq) * _head_norm(t, B_QK)).T.astype(BF16)
        t = kn[:, n * B_PAD:(n + 1) * B_PAD] + kr
        k_ref[n] = ((t * ck + krs_s) * _head_norm(t, B_QK)).astype(BF16)
    v_t = v.T
    for n in range(B_HEADS):
        _store_vt(vt_ref, n, v_t[n * B_V:(n + 1) * B_V])


def _pad_heads(w, lo, width):
    k, nh, _ = w.shape
    out = jnp.zeros((k, nh, B_PAD), w.dtype).at[:, :, lo:lo + width].set(w)
    return out.reshape(k, nh * B_PAD)


def _proj_b_call(x, gn, wdq, gcq, wuq, wdkv, gckv, wukv, gq, gk, pos):
    t_rows = x.shape[0]
    tm = TOKEN_TILE
    wuq3 = wuq.reshape(B_Q_LORA, B_HEADS, B_QK)
    wq = _pad_heads(wuq3, 0, B_QK).astype(BF16)
    wuq_rope_sw = _swap_halves_cols(wuq3[:, :, B_NOPE:].reshape(B_Q_LORA, -1), B_ROPE)
    wqs = _pad_heads(wuq_rope_sw.reshape(B_Q_LORA, B_HEADS, B_ROPE), B_NOPE, B_ROPE).astype(BF16)
    wukv3 = wukv.reshape(B_KV_LORA, B_HEADS, B_NOPE + B_V)
    wk = _pad_heads(wukv3[:, :, :B_NOPE], 0, B_NOPE).astype(BF16)
    wv = wukv3[:, :, B_NOPE:].reshape(B_KV_LORA, B_HEADS * B_V).astype(BF16)
    w_kr = wdkv[:, B_KV_LORA:]
    kr_pad = _pad_heads(w_kr[:, None, :], B_NOPE, B_ROPE)
    krs_pad = _pad_heads(_swap_halves_cols(w_kr, B_ROPE)[:, None, :], B_NOPE, B_ROPE)
    w1 = jnp.concatenate([wdq, wdkv[:, :B_KV_LORA], kr_pad, krs_pad], axis=1).astype(BF16)

    def tables(g, scale):
        c, s = _rope_tables(pos, g[B_NOPE:], scale)
        zeros = jnp.zeros((t_rows, B_PAD - B_QK), F32)
        c_full = jnp.concatenate(
            [jnp.broadcast_to(g[None, :B_NOPE] * scale, (t_rows, B_NOPE)), c, zeros], axis=-1)
        s_full = jnp.concatenate([jnp.zeros((t_rows, B_NOPE), F32), s, zeros], axis=-1)
        return c_full, s_full

    cq, sq = tables(gq, B_QK ** -0.5 * LOG2E)
    ck, sk = tables(gk, 1.0)
    tab = pl.BlockSpec((tm, B_PAD), lambda i: (i, 0))
    head_major = lambda w_: pl.BlockSpec((B_HEADS, tm, w_), lambda i: (0, i, 0))
    return pl.pallas_call(
        _proj_b_kernel,
        grid=(t_rows // tm,),
        in_specs=[pl.BlockSpec((tm, D_MODEL), lambda i: (i, 0)), _const_spec((1, D_MODEL)),
                  _const_spec(w1.shape), _const_spec((1, B_Q_LORA)), _const_spec((1, B_KV_LORA)),
                  _const_spec(wq.shape), _const_spec(wqs.shape), _const_spec(wk.shape),
                  _const_spec(wv.shape), tab, tab, tab, tab],
        out_specs=[pl.BlockSpec((B_HEADS, 1, B_PAD, tm), lambda i: (0, i, 0, 0)),
                   head_major(B_PAD),
                   pl.BlockSpec((B_HEADS, 1, B_V + BF16_SUBLANES, tm), lambda i: (0, i, 0, 0))],
        out_shape=[jax.ShapeDtypeStruct((B_HEADS, t_rows // tm, B_PAD, tm), BF16),
                   jax.ShapeDtypeStruct((B_HEADS, t_rows, B_PAD), BF16),
                   jax.ShapeDtypeStruct((B_HEADS, t_rows // tm, B_V + BF16_SUBLANES, tm), BF16)],
        compiler_params=_params(("parallel",)),
        name="proj_b",
    )(x, gn.reshape(1, D_MODEL), w1, gcq.reshape(1, -1), gckv.reshape(1, -1), wq, wqs, wk, wv,
      cq, sq, ck, sk)


def _proj_c_kernel(x_ref, gn_ref, w_ref, gq_ref, gk_ref, e_ref, *refs):
    hd = HEAD_DIM
    width = C_HEADS * hd
    outs, scr_ref = refs[:-1], refs[-1]
    tm = x_ref.shape[0]
    h = _rms(x_ref[...], gn_ref[...]).astype(BF16)
    y = _dot(h, w_ref[...])
    gains = (gq_ref[...], gk_ref[...])
    for g, (_, dil) in enumerate(C_PATTERNS):
        for j in range(3):
            base = (g * 3 + j) * width
            val = y[:, base: base + width]
            if j < 2:
                val = val * _segment_rsqrt(val, e_ref, hd) * gains[j]
            out_ref = outs[g * 3 + j]
            if dil == 1:
                out_ref[...] = val.astype(BF16)
            else:
                for s in range(width // LANES):
                    scr_ref[s] = val[:, s * LANES:(s + 1) * LANES]
                for r in range(dil):
                    for s in range(width // LANES):
                        rows = scr_ref[s, pl.ds(r, tm // dil, stride=dil), :]
                        col = r * width + s * LANES
                        out_ref[:, col:col + LANES] = rows.astype(BF16)


def _proj_c_call(x, gn, wqkv, gq, gk):
    t_rows = x.shape[0]
    tm, hd = TOKEN_TILE // 2, HEAD_DIM
    width = C_HEADS * hd
    w = wqkv.astype(BF16)
    ones = _segment_ones(hd)
    specs, shapes = [], []
    for _, dil in C_PATTERNS:
        specs += [pl.BlockSpec((tm // dil, dil * width), lambda i: (i, 0))] * 3
        shapes += [jax.ShapeDtypeStruct((t_rows // dil, dil * width), BF16)] * 3
    outs = pl.pallas_call(
        _proj_c_kernel,
        grid=(t_rows // tm,),
        in_specs=[pl.BlockSpec((tm, D_MODEL), lambda i: (i, 0)), _const_spec((1, D_MODEL)),
                  _const_spec(w.shape), _const_spec((1, width)), _const_spec((1, width)),
                  _const_spec(ones.shape)],
        out_specs=specs,
        out_shape=shapes,
        scratch_shapes=[pltpu.VMEM((width // LANES, tm, LANES), F32)],
        compiler_params=_params(("parallel",)),
        name="proj_c",
    )(x, gn.reshape(1, D_MODEL), w, jnp.tile(gq * (hd ** -0.5 * LOG2E), C_HEADS).reshape(1, width),
      jnp.tile(gk, C_HEADS).reshape(1, width), ones)
    return [outs[3 * g:3 * g + 3] for g in range(len(C_PATTERNS))]


def _band_masked_dist(row0, lo, hi, sub, w):
    nk = sub + 2 * w
    r = lax.broadcasted_iota(jnp.int32, (sub, nk), 0)
    c = lax.broadcasted_iota(jnp.int32, (sub, nk), 1)
    dist = jnp.abs(r + w - c)
    jpos = row0 - w + lax.broadcasted_iota(jnp.int32, (1, nk), 1)
    outside = jnp.where(jpos < lo, w + 1, 0) + jnp.where(jpos >= hi, w + 1, 0)
    return jnp.where(dist + outside <= w, dist.astype(F32), MASK_DIST)


def _two_stage(n_items, first, second, ahead=2):
    pending = {i: first(i) for i in range(min(ahead, n_items))}
    results = []
    for i in range(n_items):
        if i + ahead < n_items:
            pending[i + ahead] = first(i + ahead)
        results.append(second(i, pending.pop(i)))
    return results


def _band_softmax_pv(s, vv, sink):
    m = jnp.max(s, axis=-1, keepdims=True)
    if sink is not None:
        m = jnp.maximum(m, sink)
    p = jnp.exp2(s - m)
    den = jnp.sum(p, axis=-1, keepdims=True)
    if sink is not None:
        den = den + jnp.exp2(sink - m)
    return _dot(p.astype(BF16), vv) / den, m + jnp.log2(den)


def _attn_a_kernel(slope_ref, sink_ref, q_ref, kp_ref, kc_ref, kn_ref, vp_ref, vc_ref, vn_ref,
                   o_ref, *, tq, p_len, s_len):
    w, sub = A_HALF_WINDOW, BAND_SUB
    group = A_HEADS // A_KV_HEADS
    kvh = pl.program_id(0)
    row0 = pl.program_id(1) * tq
    lo, hi = _seq_bounds(row0, p_len, s_len)
    kk = jnp.concatenate([kp_ref[0], kc_ref[0], kn_ref[0]], axis=0)
    vv = jnp.concatenate([vp_ref[0], vc_ref[0], vn_ref[0]], axis=0)
    sink = jnp.concatenate(
        [jnp.full((sub, 1), sink_ref[kvh * group + g], F32) for g in range(group)], axis=0)

    def keys(r):
        return slice(r * sub, (r + 1) * sub + 2 * w)

    def logits(r):
        md = _band_masked_dist(row0 + r * sub, lo, hi, sub, w)
        bias = jnp.concatenate([md * slope_ref[kvh * group + g] for g in range(group)], axis=0)
        q = q_ref[:, r * sub:(r + 1) * sub, :].reshape(group * sub, HEAD_DIM)
        return _dot_nt(q, kk[keys(r)]) + bias

    tiles = _two_stage(tq // sub, logits, lambda r, s: _band_softmax_pv(s, vv[keys(r)], sink)[0])
    outs = [[o[g * sub:(g + 1) * sub] for o in tiles] for g in range(group)]
    o_ref[...] = jnp.concatenate(
        [jnp.concatenate(rows, axis=0) for rows in outs], axis=-1).astype(BF16)


def _attn_a_call(q, k, v, sink, geo):
    p_len, s_len, t_rows = geo
    tq, w, hd = 512, A_HALF_WINDOW, HEAD_DIM
    group = A_HEADS // A_KV_HEADS
    per = tq // w
    last = t_rows // w - 1
    prev = pl.BlockSpec((1, w, hd), lambda h, i: (h, jnp.maximum(i * per - 1, 0), 0))
    cur = pl.BlockSpec((1, tq, hd), lambda h, i: (h, i, 0))
    nxt = pl.BlockSpec((1, w, hd), lambda h, i: (h, jnp.minimum((i + 1) * per, last), 0))
    smem = pl.BlockSpec(memory_space=pltpu.SMEM)
    slopes = jnp.asarray(-LOG2E * _alibi_slopes(A_HEADS), F32)
    return pl.pallas_call(
        functools.partial(_attn_a_kernel, tq=tq, p_len=p_len, s_len=s_len),
        grid=(A_KV_HEADS, t_rows // tq),
        in_specs=[smem, smem, pl.BlockSpec((group, tq, hd), lambda h, i: (h, i, 0)),
                  prev, cur, nxt, prev, cur, nxt],
        out_specs=pl.BlockSpec((tq, group * hd), lambda h, i: (i, h)),
        out_shape=jax.ShapeDtypeStruct((t_rows, A_HEADS * hd), BF16),
        compiler_params=_params(("parallel", "parallel")),
        name="attn_a",
    )(slopes, sink.astype(F32) * LOG2E, q, k, k, k, v, v, v)


def _attn_c_kernel(q_ref, kp_ref, kc_ref, kn_ref, vp_ref, vc_ref, vn_ref, o_ref, lse_ref,
                   *, tq, dil, p_len, s_len):
    w, hd, sub = C_STEPS, HEAD_DIM, min(BAND_SUB, tq)
    row0 = pl.program_id(0) * tq
    lo, hi = _seq_bounds(row0, p_len // dil, s_len // dil)
    slopes = -LOG2E * dil * _alibi_slopes(C_HEADS)
    mds = [_band_masked_dist(row0 + r * sub, lo, hi, sub, w) for r in range(tq // sub)]
    n_sub = tq // sub

    def window(prev_ref, cur_ref, next_ref, i):
        n, r = divmod(i, n_sub)
        hs = slice(n * hd, (n + 1) * hd)
        rows = jnp.concatenate([prev_ref[:, hs], cur_ref[:, hs], next_ref[:, hs]], axis=0)
        return rows[r * sub:(r + 1) * sub + 2 * w]

    def logits(i):
        n, r = divmod(i, n_sub)
        q = q_ref[r * sub:(r + 1) * sub, n * hd:(n + 1) * hd]
        return _dot_nt(q, window(kp_ref, kc_ref, kn_ref, i)) + mds[r] * float(slopes[n])

    def attend(i, s):
        o, lse2 = _band_softmax_pv(s, window(vp_ref, vc_ref, vn_ref, i), None)
        return o, jnp.broadcast_to(lse2 * (1.0 / LOG2E), (sub, hd))

    tiles = _two_stage(C_HEADS * n_sub, logits, attend)
    outs = [jnp.concatenate([t[0] for t in tiles[n * n_sub:(n + 1) * n_sub]], axis=0)
            for n in range(C_HEADS)]
    lses = [jnp.concatenate([t[1] for t in tiles[n * n_sub:(n + 1) * n_sub]], axis=0)
            for n in range(C_HEADS)]
    o_ref[...] = jnp.concatenate(outs, axis=-1)
    lse_ref[...] = jnp.concatenate(lses, axis=-1)


def _attn_c_call(q, k, v, dil, geo):
    p_len, s_len, t_rows = geo
    w, width = C_STEPS, C_HEADS * HEAD_DIM
    rows = t_rows // dil
    tq = min(512, s_len // dil)
    per = tq // w
    last = rows // w - 1
    prev = pl.BlockSpec((w, width), lambda i, r: (jnp.maximum(i * per - 1, 0), r))
    cur = pl.BlockSpec((tq, width), lambda i, r: (i, r))
    nxt = pl.BlockSpec((w, width), lambda i, r: (jnp.minimum((i + 1) * per, last), r))
    shape = jax.ShapeDtypeStruct((rows, dil * width), F32)
    return pl.pallas_call(
        functools.partial(_attn_c_kernel, tq=tq, dil=dil, p_len=p_len, s_len=s_len),
        grid=(rows // tq, dil),
        in_specs=[cur, prev, cur, nxt, prev, cur, nxt],
        out_specs=[cur, cur],
        out_shape=[shape, shape],
        compiler_params=_params(("parallel", "parallel")),
        name="attn_c_d%d" % dil,
    )(q, k, k, k, v, v, v)


def _attn_dense_kernel(qt_ref, k_ref, vt_ref, o_ref, sa_ref, sb_ref, sc_ref, sd_ref,
                       *, nq, tq, tk, p_len, s_len):
    dv = vt_ref.shape[-2] - BF16_SUBLANES
    width = nq * tq
    shift = int(math.log2(tk))

    def q_tile(j, _):
        row0 = (pl.program_id(1) * Q_TILES_PER_STEP + j) * tq
        lo, hi = _seq_bounds(row0, p_len, s_len)
        first = lax.shift_right_logical(lo, shift)
        n_chunks = lax.shift_right_logical(hi - lo, shift)

        def scores(c, st_ref):
            start = pl.multiple_of(c * tk, tk)
            qt = jnp.concatenate([qt_ref[g, j] for g in range(nq)], axis=-1)
            st = _dot(k_ref[0, pl.ds(start, tk), :], qt).astype(BF16)
            st_ref[...] = st
            return jnp.max(st, axis=0, keepdims=True).astype(F32)

        def update(c, st_ref, mc, m, acc):
            m_new = jnp.maximum(m, mc)
            alpha = jnp.exp2(m - m_new)
            pt = jnp.exp2(st_ref[...] - m_new.astype(BF16))
            return m_new, alpha * acc + _dot(vt_ref[0, c], pt)

        def quad(i, carry):
            m, acc, mc_a, mc_b = carry
            c = first + 4 * i
            mc_c = scores(c + 2, sc_ref)
            m, acc = update(c, sa_ref, mc_a, m, acc)
            mc_d = scores(c + 3, sd_ref)
            m, acc = update(c + 1, sb_ref, mc_b, m, acc)
            mc_a = scores(c + 4, sa_ref)
            m, acc = update(c + 2, sc_ref, mc_c, m, acc)
            mc_b = scores(c + 5, sb_ref)
            m, acc = update(c + 3, sd_ref, mc_d, m, acc)
            return m, acc, mc_a, mc_b

        carry = (jnp.full((1, width), NEG_INF, F32), jnp.zeros((vt_ref.shape[-2], width), F32),
                 scores(first, sa_ref), scores(first + 1, sb_ref))
        n_body = lax.shift_right_logical(n_chunks, 2) - 1
        carry = lax.fori_loop(0, lax.shift_right_logical(n_body, 1),
                              lambda i, cr: quad(2 * i + 1, quad(2 * i, cr)), carry)
        carry = lax.fori_loop(0, n_body & 1, lambda _, cr: quad(n_body - 1, cr), carry)
        m, acc, mc_a, mc_b = carry
        c = first + n_chunks - 4
        mc_c = scores(c + 2, sc_ref)
        m, acc = update(c, sa_ref, mc_a, m, acc)
        mc_d = scores(c + 3, sd_ref)
        m, acc = update(c + 1, sb_ref, mc_b, m, acc)
        m, acc = update(c + 2, sc_ref, mc_c, m, acc)
        m, acc = update(c + 3, sd_ref, mc_d, m, acc)
        o = acc[:dv] / acc[dv:dv + 1]
        rows = pl.ds(pl.multiple_of(j * tq, tq), tq)
        heads = [o[:, g * tq:(g + 1) * tq].T.astype(BF16) for g in range(nq)]
        if nq == 1:
            o_ref[0, rows, :] = heads[0]
        else:
            o_ref[rows, :] = jnp.concatenate(heads, axis=-1)
        return 0

    lax.fori_loop(0, Q_TILES_PER_STEP, q_tile, 0)


def _attn_dense_call(qt, k, vt, geo):
    p_len, s_len, t_rows = geo
    hq, n_tiles, dq, tq = qt.shape
    hk, n_chunks, rows_v, tk = vt.shape
    dv = rows_v - BF16_SUBLANES
    nq, per = hq // hk, Q_TILES_PER_STEP
    assert s_len % (4 * tk) == 0 and p_len % (4 * tk) == 0 and n_tiles % per == 0
    return pl.pallas_call(
        functools.partial(_attn_dense_kernel, nq=nq, tq=tq, tk=tk, p_len=p_len, s_len=s_len),
        grid=(hk, n_tiles // per),
        in_specs=[pl.BlockSpec((nq, per, dq, tq), lambda h, i: (h, i, 0, 0)),
                  pl.BlockSpec((1, t_rows, dq), lambda h, i: (h, 0, 0), pipeline_mode=pl.Buffered(1)),
                  pl.BlockSpec((1, n_chunks, dv + BF16_SUBLANES, tk), lambda h, i: (h, 0, 0, 0),
                               pipeline_mode=pl.Buffered(1))],
        out_specs=(pl.BlockSpec((1, per * tq, dv), lambda h, i: (h, i, 0)) if nq == 1
                   else pl.BlockSpec((per * tq, nq * dv), lambda h, i: (i, h))),
        out_shape=jax.ShapeDtypeStruct((hq, t_rows, dv) if nq == 1 else (t_rows, hq * dv), BF16),
        scratch_shapes=[pltpu.VMEM((tk, nq * tq), BF16)] * 4,
        compiler_params=_params(("parallel", "parallel")),
        name="attn_dense_%d" % dq,
    )(qt, k, vt)


def _trunk(x, p, geo, pos):
    p_tiles = geo[0] // TOKEN_TILE
    all_tiles = geo[2] // TOKEN_TILE
    for i in range(DEPTH):
        mixer, j = i % 4, i // 4
        x = _ffn_call(x, p["norm_ffn1"][i], p["ffn1_wg"][i], p["ffn1_wu"][i], p["ffn1_wd"][i])
        gn = p["norm_mix"][i]
        if mixer == 0:
            q, k, v = _proj_a_call(x, gn, p["a_wqkv"][j], p["a_gq"][j], p["a_gk"][j])
            pre = (_attn_a_call(q, k, v, p["a_sink"][j], geo), p["a_wo"][j])
        elif mixer == 1:
            qt, k, vt = _proj_b_call(x, gn, p["b_wdq"][j], p["b_gcq"][j], p["b_wuq"][j], p["b_wdkv"][j],
                                     p["b_gckv"][j], p["b_wukv"][j], p["b_gq"][j], p["b_gk"][j], pos)
            pre = (_attn_dense_call(qt, k, vt, geo), p["b_wo"][j])
        elif mixer == 2:
            qkv = _proj_c_call(x, gn, p["c_wqkv"][j], p["c_gq"][j], p["c_gk"][j])
            res = [_attn_c_call(*qkv[g], dil, geo) for g, (_, dil) in enumerate(C_PATTERNS)]
            pre = ([r[0] for r in res], [r[1] for r in res], p["c_wo"][j])
        else:
            qt, k, vt = _proj_d_call(x, gn, p["d_wqkv"][j], p["d_gq"][j], p["d_gk"][j], pos, tq=128)
            pre = (_attn_dense_call(qt, k, vt, geo), p["d_wo"][j])
        ffn2 = (p["norm_ffn2"][i], p["ffn2_wg"][i], p["ffn2_wu"][i], p["ffn2_wd"][i])
        if i < DEPTH - 1:
            x = _ffn_call(x, *ffn2, pre=pre)
        else:
            x = (_ffn_call(x, *ffn2, pre=pre, tiles=(0, p_tiles)),
                 _ffn_call(x, *ffn2, pre=pre, tiles=(p_tiles, all_tiles - p_tiles)))
    return x


def kernel(x_prompt, x_sample, norm_ffn1, ffn1_wg, ffn1_wu, ffn1_wd, norm_mix, norm_ffn2, ffn2_wg, ffn2_wu, ffn2_wd, a_wqkv, a_gq, a_gk, a_sink, a_wo, b_wdq, b_gcq, b_wuq, b_wdkv, b_gckv, b_wukv, b_gq, b_gk, b_wo, c_wqkv, c_gq, c_gk, c_wo, d_wqkv, d_gq, d_gk, d_wo):
    p = dict(norm_ffn1=norm_ffn1, ffn1_wg=ffn1_wg, ffn1_wu=ffn1_wu, ffn1_wd=ffn1_wd,
             norm_mix=norm_mix, norm_ffn2=norm_ffn2, ffn2_wg=ffn2_wg, ffn2_wu=ffn2_wu,
             ffn2_wd=ffn2_wd, a_wqkv=a_wqkv, a_gq=a_gq, a_gk=a_gk, a_sink=a_sink, a_wo=a_wo,
             b_wdq=b_wdq, b_gcq=b_gcq, b_wuq=b_wuq, b_wdkv=b_wdkv, b_gckv=b_gckv,
             b_wukv=b_wukv, b_gq=b_gq, b_gk=b_gk, b_wo=b_wo, c_wqkv=c_wqkv, c_gq=c_gq,
             c_gk=c_gk, c_wo=c_wo, d_wqkv=d_wqkv, d_gq=d_gq, d_gk=d_gk, d_wo=d_wo)
    n_p, p_seq, _ = x_prompt.shape
    n_s, s_len, _ = x_sample.shape
    assert n_p == 1 and s_len & (s_len - 1) == 0 and p_seq % s_len == 0
    p_len = n_p * p_seq
    t_rows = p_len + n_s * s_len
    geo = (p_len, s_len, t_rows)
    pos = jnp.concatenate([jnp.arange(p_len, dtype=jnp.int32),
                           jnp.tile(jnp.arange(s_len, dtype=jnp.int32), n_s)])
    x = (x_prompt.reshape(p_len, D_MODEL), x_sample.reshape(n_s * s_len, D_MODEL))
    y_p, y_s = _trunk(x, p, geo, pos)
    return y_p.reshape(x_prompt.shape), y_s.reshape(x_sample.shape)
```

```python
import functools
import math

import numpy as np
import jax
import jax.numpy as jnp
from jax import lax
from jax.experimental import pallas as pl
from jax.experimental.pallas import tpu as pltpu

F32 = jnp.float32
BF16 = jnp.bfloat16

D_MODEL = 1024
DEPTH = 4
HEAD_DIM = 64
D_FF = 2816
NORM_EPS = 1e-6
NEG_INF = -1e30
GRID_W = 64
ROPE_THETA = 10000.0
LOG2E = math.log2(math.e)
LANES = 128
MXU_TILE = 256
BF16_SUBLANES = 16
BAND_SUB = 128
Q_TILES_PER_STEP = 8
MASK_DIST = 1e33

A_HEADS, A_KV_HEADS, A_HALF_WINDOW = 16, 4, 128
B_HEADS, B_Q_LORA, B_KV_LORA, B_NOPE, B_ROPE, B_V = 16, 512, 256, 64, 32, 64
B_QK = B_NOPE + B_ROPE
B_PAD = 128
C_HEADS = 8
C_PATTERNS = ((128, 1), (512, 4), (2048, 16))
C_STEPS = 64
D_HEADS, D_KV_HEADS = 16, 4

FF_CHUNK = 256
N_FF_CHUNKS = D_FF // FF_CHUNK
TOKEN_TILE = 512
VMEM_LIMIT = 56 * 1024 * 1024


def _alibi_slopes(n):
    return 2.0 ** (-8.0 * np.arange(1, n + 1) / n)


def _params(sem, vmem=VMEM_LIMIT):
    return pltpu.CompilerParams(dimension_semantics=sem, vmem_limit_bytes=vmem)


def _const_spec(shape):
    nd = len(shape)
    return pl.BlockSpec(tuple(shape), lambda *_: (0,) * nd, pipeline_mode=pl.Buffered(1))


def _seq_bounds(row0, p_len, s_len):
    in_prompt = row0 < p_len
    b = lax.shift_right_logical(jnp.maximum(row0 - p_len, 0), int(math.log2(s_len)))
    lo = jnp.where(in_prompt, 0, p_len + b * s_len)
    hi = jnp.where(in_prompt, p_len, lo + s_len)
    return lo, hi


def _rms(x, g):
    return x * lax.rsqrt(jnp.mean(x * x, axis=-1, keepdims=True) + NORM_EPS) * g


def _dot(a, b):
    return jnp.dot(a, b, preferred_element_type=F32)


def _dot_nt(a, b):
    return lax.dot_general(a, b, (((1,), (1,)), ((), ())), preferred_element_type=F32)


def _ffn_kernel(*refs, mode, first_tiles=0):
    if mode == "plain":
        x_ref, gn_ref, wg_ref, wu_ref, wd_ref, out_ref, acc_ref, act_a, act_b = refs
        x = x_ref[...]
    elif mode == "plain2":
        xa_ref, xb_ref, gn_ref, wg_ref, wu_ref, wd_ref, out_ref, acc_ref, act_a, act_b = refs
        x = jnp.where(pl.program_id(0) < first_tiles, xa_ref[...], xb_ref[...])
    elif mode == "proj":
        (x_ref, o_ref, wo_ref, gn_ref, wg_ref, wu_ref, wd_ref, out_ref, acc_ref, act_a,
         act_b) = refs
        if len(o_ref.shape) == 3:
            o = jnp.concatenate([o_ref[n] for n in range(o_ref.shape[0])], axis=-1)
        else:
            o = o_ref[...]
        x = x_ref[...] + _dot(o, wo_ref[...])
    else:
        (x_ref, o0, o1, o2, l0, l1, l2, wo_ref, gn_ref, wg_ref, wu_ref, wd_ref,
         out_ref, acc_ref, act_a, act_b, *scr) = refs
        tm, width = x_ref.shape[0], wo_ref.shape[0]

        def token_major(ref, dil, scr_ref):
            slabs = width // LANES
            for r in range(dil):
                for s in range(slabs):
                    col = r * width + s * LANES
                    scr_ref[s, pl.ds(r, tm // dil, stride=dil), :] = ref[:, col:col + LANES]
            return jnp.concatenate([scr_ref[s] for s in range(slabs)], axis=-1)

        dils = [d for _, d in C_PATTERNS]
        assert dils[0] == 1
        oa, la = o0[...], l0[...]
        ob, lb = token_major(o1, dils[1], scr[0]), token_major(l1, dils[1], scr[1])
        oc, lc = token_major(o2, dils[2], scr[2]), token_major(l2, dils[2], scr[3])
        m = jnp.maximum(jnp.maximum(la, lb), lc)
        ea, eb, ec = jnp.exp(la - m), jnp.exp(lb - m), jnp.exp(lc - m)
        o = (ea * oa + eb * ob + ec * oc) / (ea + eb + ec)
        x = x_ref[...] + _dot(o.astype(BF16), wo_ref[...])

    h = _rms(x, gn_ref[...]).astype(BF16)

    def gate_up(c, a_ref):
        cols = pl.ds(pl.multiple_of(c * FF_CHUNK, FF_CHUNK), FF_CHUNK)
        g = _dot(h, wg_ref[:, cols])
        u = _dot(h, wu_ref[:, cols])
        a_ref[...] = (g * jax.nn.sigmoid(g) * u).astype(BF16)

    def down(c, a_ref):
        acc_ref[...] += _dot(a_ref[...], wd_ref[c])

    assert N_FF_CHUNKS % 2 == 1
    acc_ref[...] = jnp.zeros_like(acc_ref)
    gate_up(0, act_a)

    def pair(i, carry):
        c = 2 * i
        gate_up(c + 1, act_b)
        down(c, act_a)
        gate_up(c + 2, act_a)
        down(c + 1, act_b)
        return carry

    lax.fori_loop(0, N_FF_CHUNKS // 2, pair, 0)
    down(N_FF_CHUNKS - 1, act_a)
    out_ref[...] = x + 0.5 * acc_ref[...]


def _ffn_weights(wg, wu, wd):
    wg3 = wg.astype(BF16)
    wu3 = wu.astype(BF16)
    wd3 = wd.astype(BF16).reshape(N_FF_CHUNKS, FF_CHUNK, D_MODEL)
    return wg3, wu3, wd3


def _ffn_call(x, gn, wg, wu, wd, pre=None, tiles=None):
    tm = TOKEN_TILE
    xs = x if isinstance(x, tuple) else (x,)
    off, n_tiles = tiles if tiles is not None else (0, sum(a.shape[0] for a in xs) // tm)
    wg3, wu3, wd3 = _ffn_weights(wg, wu, wd)
    row = lambda w: pl.BlockSpec((tm, w), lambda i: (i + off, 0))
    ffn_args = [gn.reshape(1, D_MODEL), wg3, wu3, wd3]
    ffn_specs = [_const_spec((1, D_MODEL)), _const_spec(wg3.shape), _const_spec(wu3.shape),
                 _const_spec(wd3.shape)]
    scratch = []
    first_tiles = 0
    if len(xs) == 2:
        assert pre is None and tiles is None
        first_tiles = xs[0].shape[0] // tm
        last = xs[1].shape[0] // tm - 1
        mode, args = "plain2", list(xs)
        specs = [pl.BlockSpec((tm, D_MODEL), lambda i: (jnp.minimum(i, first_tiles - 1), 0)),
                 pl.BlockSpec((tm, D_MODEL), lambda i: (jnp.clip(i - first_tiles, 0, last), 0))]
    elif pre is None:
        mode, args, specs = "plain", [x], [row(D_MODEL)]
    elif len(pre) == 2:
        o, wo = pre
        wo = wo.astype(BF16)
        mode = "proj"
        args = [x, o, wo]
        o_spec = row(o.shape[1]) if o.ndim == 2 else pl.BlockSpec(
            (o.shape[0], tm, o.shape[2]), lambda i: (0, i + off, 0))
        specs = [row(D_MODEL), o_spec, _const_spec(wo.shape)]
    else:
        outs, lses, wo = pre
        wo = wo.astype(BF16)
        mode = "merge"
        args = [x, *outs, *lses, wo]
        width = wo.shape[0]
        dilated = [pl.BlockSpec((tm // d, d * width), lambda i: (i + off, 0)) for _, d in C_PATTERNS]
        specs = [row(D_MODEL)] + dilated * 2 + [_const_spec(wo.shape)]
        scratch = [pltpu.VMEM((width // LANES, tm, LANES), F32)] * 4
    return pl.pallas_call(
        functools.partial(_ffn_kernel, mode=mode, first_tiles=first_tiles),
        grid=(n_tiles,),
        in_specs=specs + ffn_specs,
        out_specs=pl.BlockSpec((tm, D_MODEL), lambda i: (i, 0)),
        out_shape=jax.ShapeDtypeStruct((n_tiles * tm, D_MODEL), F32),
        scratch_shapes=[pltpu.VMEM((tm, D_MODEL), F32)] + [pltpu.VMEM((tm, FF_CHUNK), BF16)] * 2
        + scratch,
        compiler_params=_params(("parallel",)),
        name="ffn_" + mode,
    )(*args, *ffn_args)


def _head_norm(t, width):
    return lax.rsqrt(jnp.sum(t * t, axis=-1, keepdims=True) * (1.0 / width) + NORM_EPS)


def _segment_ones(seg):
    n = MXU_TILE // seg
    return jnp.kron(jnp.eye(n, dtype=F32), jnp.ones((seg, seg), F32)).astype(BF16)


def _segment_rsqrt(t, e_ref, width):
    sq = t * t
    hi = sq.astype(BF16)
    lo = (sq - hi.astype(F32)).astype(BF16)
    e = e_ref[...]
    sums = [_dot(hi[:, b:b + MXU_TILE], e) + _dot(lo[:, b:b + MXU_TILE], e)
            for b in range(0, t.shape[-1], MXU_TILE)]
    ssq = sums[0] if len(sums) == 1 else jnp.concatenate(sums, axis=-1)
    return lax.rsqrt(ssq * (1.0 / width) + NORM_EPS)


def _store_vt(vt_ref, n, v_t):
    dv, rows = v_t.shape
    vt_ref[n, 0, :dv, :] = v_t.astype(BF16)
    first = lax.broadcasted_iota(jnp.int32, (BF16_SUBLANES, rows), 0) == 0
    vt_ref[n, 0, dv:, :] = jnp.where(first, 1.0, 0.0).astype(BF16)


def _proj_a_kernel(x_ref, gn_ref, w_ref, gain_ref, e_ref, q_ref, k_ref, v_ref):
    hd = HEAD_DIM
    nqk = (A_HEADS + A_KV_HEADS) * hd
    h = _rms(x_ref[...], gn_ref[...]).astype(BF16)
    y = _dot(h, w_ref[...])
    qk = y[:, :nqk]
    qk = (qk * _segment_rsqrt(qk, e_ref, hd) * gain_ref[...]).astype(BF16)
    for n in range(A_HEADS):
        q_ref[n] = qk[:, n * hd:(n + 1) * hd]
    for n in range(A_KV_HEADS):
        k_ref[n] = qk[:, (A_HEADS + n) * hd:(A_HEADS + n + 1) * hd]
        v_ref[n] = y[:, nqk + n * hd: nqk + (n + 1) * hd].astype(BF16)


def _proj_a_call(x, gn, wqkv, gq, gk):
    t_rows = x.shape[0]
    tm, hd = TOKEN_TILE, HEAD_DIM
    w = wqkv.astype(BF16)
    gain = jnp.concatenate([jnp.tile(gq * (hd ** -0.5 * LOG2E), A_HEADS), jnp.tile(gk, A_KV_HEADS)])
    ones = _segment_ones(hd)
    head_major = lambda n: pl.BlockSpec((n, tm, hd), lambda i: (0, i, 0))
    return pl.pallas_call(
        _proj_a_kernel,
        grid=(t_rows // tm,),
        in_specs=[pl.BlockSpec((tm, D_MODEL), lambda i: (i, 0)), _const_spec((1, D_MODEL)),
                  _const_spec(w.shape), _const_spec((1, gain.shape[0])), _const_spec(ones.shape)],
        out_specs=[head_major(A_HEADS), head_major(A_KV_HEADS), head_major(A_KV_HEADS)],
        out_shape=[jax.ShapeDtypeStruct((A_HEADS, t_rows, hd), BF16),
                   jax.ShapeDtypeStruct((A_KV_HEADS, t_rows, hd), BF16),
                   jax.ShapeDtypeStruct((A_KV_HEADS, t_rows, hd), BF16)],
        compiler_params=_params(("parallel",)),
        name="proj_a",
    )(x, gn.reshape(1, D_MODEL), w, gain.reshape(1, -1), ones)


def _proj_d_kernel(x_ref, gn_ref, w_ref, e_ref, cq_ref, sq_ref, ck_ref, sk_ref, qt_ref, k_ref, vt_ref):
    hd = HEAD_DIM
    nq, nk = D_HEADS * hd, D_KV_HEADS * hd
    tq = qt_ref.shape[-1]
    h = _rms(x_ref[...], gn_ref[...]).astype(BF16)
    y = _dot(h, w_ref[...])
    swapped = nq + 2 * nk
    t, ts = y[:, :nq + nk], y[:, swapped:swapped + nq + nk]
    r = _segment_rsqrt(t, e_ref, hd)
    for b in range((nq + nk) // LANES):
        is_q = b < nq // LANES
        c_ref, s_ref = (cq_ref, sq_ref) if is_q else (ck_ref, sk_ref)
        sl = slice(b * LANES, (b + 1) * LANES)
        blk = (t[:, sl] * c_ref[...] + ts[:, sl] * s_ref[...]) * r[:, sl]
        blk = blk.T.astype(BF16) if is_q else blk.astype(BF16)
        for half in range(LANES // hd):
            n = b * (LANES // hd) + half
            if is_q:
                for j in range(qt_ref.shape[1]):
                    qt_ref[n, j] = blk[half * hd:(half + 1) * hd, j * tq:(j + 1) * tq]
            else:
                k_ref[n - D_HEADS] = blk[:, half * hd:(half + 1) * hd]
    v_t = y[:, nq + nk: nq + 2 * nk].T
    for n in range(D_KV_HEADS):
        _store_vt(vt_ref, n, v_t[n * hd:(n + 1) * hd])


def _swap_halves_cols(w, group):
    k, n = w.shape
    return w.reshape(k, n // group, 2, group // 2)[:, :, ::-1, :].reshape(k, n)


def _position_spec(geo, tm, width):
    p_len, s_len, _ = geo
    p_tiles, s_tiles = p_len // tm, s_len // tm
    return pl.BlockSpec(
        (tm, width), lambda i: (jnp.where(i < p_tiles, i, (i - p_tiles) % s_tiles), 0))


def _rope_tables(pos, gain, scale):
    half = gain.shape[-1] // 2
    inv = ROPE_THETA ** (-jnp.arange(half, dtype=F32) / half)
    ang = pos.astype(F32)[:, None] * inv[None, :]
    cos, sin = lax.optimization_barrier((jnp.cos(ang), jnp.sin(ang)))
    g_sw = jnp.concatenate([gain[half:], gain[:half]])
    c = jnp.concatenate([cos, cos], axis=-1) * gain[None, :] * scale
    s = jnp.concatenate([-sin, sin], axis=-1) * g_sw[None, :] * scale
    return c, s


def _proj_d_call(x, gn, wqkv, gq, gk, pos, geo, tq):
    t_rows = x.shape[0]
    tm, hd = TOKEN_TILE, HEAD_DIM
    nq, nk = D_HEADS * hd, D_KV_HEADS * hd
    half = hd // 2
    w = jnp.concatenate([wqkv, _swap_halves_cols(wqkv[:, :nq + nk], half)], axis=1).astype(BF16)
    rows, cols = pos // GRID_W, pos % GRID_W

    def tables(g, scale):
        cr, sr = _rope_tables(rows, g[:half], scale)
        cc, sc = _rope_tables(cols, g[half:], scale)
        reps = LANES // hd
        return jnp.concatenate([cr, cc] * reps, axis=-1), jnp.concatenate([sr, sc] * reps, axis=-1)

    cq, sq = tables(gq, hd ** -0.5 * LOG2E)
    ck, sk = tables(gk, 1.0)
    ones = _segment_ones(hd)
    head_major = lambda n: pl.BlockSpec((n, tm, hd), lambda i: (0, i, 0))
    tab = _position_spec(geo, tm, LANES)
    return pl.pallas_call(
        _proj_d_kernel,
        grid=(t_rows // tm,),
        in_specs=[pl.BlockSpec((tm, D_MODEL), lambda i: (i, 0)), _const_spec((1, D_MODEL)),
                  _const_spec(w.shape), _const_spec(ones.shape), tab, tab, tab, tab],
        out_specs=[pl.BlockSpec((D_HEADS, tm // tq, hd, tq), lambda i: (0, i, 0, 0)),
                   head_major(D_KV_HEADS),
                   pl.BlockSpec((D_KV_HEADS, 1, hd + BF16_SUBLANES, tm), lambda i: (0, i, 0, 0))],
        out_shape=[jax.ShapeDtypeStruct((D_HEADS, t_rows // tq, hd, tq), BF16),
                   jax.ShapeDtypeStruct((D_KV_HEADS, t_rows, hd), BF16),
                   jax.ShapeDtypeStruct((D_KV_HEADS, t_rows // tm, hd + BF16_SUBLANES, tm), BF16)],
        compiler_params=_params(("parallel",)),
        name="proj_d",
    )(x, gn.reshape(1, D_MODEL), w, ones, cq, sq, ck, sk)


def _proj_b_kernel(x_ref, gn_ref, w1_ref, gcq_ref, gckv_ref, wq_ref, wqs_ref, wk_ref, wv_ref,
                   cq_ref, sq_ref, ck_ref, sk_ref, qt_ref, k_ref, vt_ref):
    h = _rms(x_ref[...], gn_ref[...]).astype(BF16)
    y1 = _dot(h, w1_ref[...])
    c_q = _rms(y1[:, :B_Q_LORA], gcq_ref[...]).astype(BF16)
    c_kv = _rms(y1[:, B_Q_LORA:B_Q_LORA + B_KV_LORA], gckv_ref[...]).astype(BF16)
    off = B_Q_LORA + B_KV_LORA
    kr = y1[:, off:off + B_PAD]
    krs = y1[:, off + B_PAD:off + 2 * B_PAD]
    q = _dot(c_q, wq_ref[...])
    qs = _dot(c_q, wqs_ref[...])
    kn = _dot(c_kv, wk_ref[...])
    v = _dot(c_kv, wv_ref[...])
    cq, sq, ck, sk = cq_ref[...], sq_ref[...], ck_ref[...], sk_ref[...]
    krs_s = krs * sk
    for n in range(B_HEADS):
        t = q[:, n * B_PAD:(n + 1) * B_PAD]
        ts = qs[:, n * B_PAD:(n + 1) * B_PAD]
        qt_ref[n, 0] = ((t * cq + ts * sq) * _head_norm(t, B_QK)).T.astype(BF16)
        t = kn[:, n * B_PAD:(n + 1) * B_PAD] + kr
        k_ref[n] = ((t * ck + krs_s) * _head_norm(t, B_QK)).astype(BF16)
    v_t = v.T
    for n in range(B_HEADS):
        _store_vt(vt_ref, n, v_t[n * B_V:(n + 1) * B_V])


def _pad_heads(w, lo, width):
    k, nh, _ = w.shape
    out = jnp.zeros((k, nh, B_PAD), w.dtype).at[:, :, lo:lo + width].set(w)
    return out.reshape(k, nh * B_PAD)


def _proj_b_call(x, gn, wdq, gcq, wuq, wdkv, gckv, wukv, gq, gk, pos, geo):
    t_rows = x.shape[0]
    tm = TOKEN_TILE
    wuq3 = wuq.reshape(B_Q_LORA, B_HEADS, B_QK)
    wq = _pad_heads(wuq3, 0, B_QK).astype(BF16)
    wuq_rope_sw = _swap_halves_cols(wuq3[:, :, B_NOPE:].reshape(B_Q_LORA, -1), B_ROPE)
    wqs = _pad_heads(wuq_rope_sw.reshape(B_Q_LORA, B_HEADS, B_ROPE), B_NOPE, B_ROPE).astype(BF16)
    wukv3 = wukv.reshape(B_KV_LORA, B_HEADS, B_NOPE + B_V)
    wk = _pad_heads(wukv3[:, :, :B_NOPE], 0, B_NOPE).astype(BF16)
    wv = wukv3[:, :, B_NOPE:].reshape(B_KV_LORA, B_HEADS * B_V).astype(BF16)
    w_kr = wdkv[:, B_KV_LORA:]
    kr_pad = _pad_heads(w_kr[:, None, :], B_NOPE, B_ROPE)
    krs_pad = _pad_heads(_swap_halves_cols(w_kr, B_ROPE)[:, None, :], B_NOPE, B_ROPE)
    w1 = jnp.concatenate([wdq, wdkv[:, :B_KV_LORA], kr_pad, krs_pad], axis=1).astype(BF16)

    def tables(g, scale):
        c, s = _rope_tables(pos, g[B_NOPE:], scale)
        n_pos = pos.shape[0]
        zeros = jnp.zeros((n_pos, B_PAD - B_QK), F32)
        c_full = jnp.concatenate(
            [jnp.broadcast_to(g[None, :B_NOPE] * scale, (n_pos, B_NOPE)), c, zeros], axis=-1)
        s_full = jnp.concatenate([jnp.zeros((n_pos, B_NOPE), F32), s, zeros], axis=-1)
        return c_full, s_full

    cq, sq = tables(gq, B_QK ** -0.5 * LOG2E)
    ck, sk = tables(gk, 1.0)
    tab = _position_spec(geo, tm, B_PAD)
    head_major = lambda w_: pl.BlockSpec((B_HEADS, tm, w_), lambda i: (0, i, 0))
    return pl.pallas_call(
        _proj_b_kernel,
        grid=(t_rows // tm,),
        in_specs=[pl.BlockSpec((tm, D_MODEL), lambda i: (i, 0)), _const_spec((1, D_MODEL)),
                  _const_spec(w1.shape), _const_spec((1, B_Q_LORA)), _const_spec((1, B_KV_LORA)),
                  _const_spec(wq.shape), _const_spec(wqs.shape), _const_spec(wk.shape),
                  _const_spec(wv.shape), tab, tab, tab, tab],
        out_specs=[pl.BlockSpec((B_HEADS, 1, B_PAD, tm), lambda i: (0, i, 0, 0)),
                   head_major(B_PAD),
                   pl.BlockSpec((B_HEADS, 1, B_V + BF16_SUBLANES, tm), lambda i: (0, i, 0, 0))],
        out_shape=[jax.ShapeDtypeStruct((B_HEADS, t_rows // tm, B_PAD, tm), BF16),
                   jax.ShapeDtypeStruct((B_HEADS, t_rows, B_PAD), BF16),
                   jax.ShapeDtypeStruct((B_HEADS, t_rows // tm, B_V + BF16_SUBLANES, tm), BF16)],
        compiler_params=_params(("parallel",)),
        name="proj_b",
    )(x, gn.reshape(1, D_MODEL), w1, gcq.reshape(1, -1), gckv.reshape(1, -1), wq, wqs, wk, wv,
      cq, sq, ck, sk)


def _proj_c_kernel(x_ref, gn_ref, w_ref, gq_ref, gk_ref, e_ref, *refs):
    hd = HEAD_DIM
    width = C_HEADS * hd
    outs, scr_ref = refs[:-1], refs[-1]
    tm = x_ref.shape[0]
    h = _rms(x_ref[...], gn_ref[...]).astype(BF16)
    y = _dot(h, w_ref[...])
    gains = (gq_ref[...], gk_ref[...])
    for g, (_, dil) in enumerate(C_PATTERNS):
        for j in range(3):
            base = (g * 3 + j) * width
            val = y[:, base: base + width]
            if j < 2:
                val = val * _segment_rsqrt(val, e_ref, hd) * gains[j]
            out_ref = outs[g * 3 + j]
            if dil == 1:
                out_ref[...] = val.astype(BF16)
            else:
                for s in range(width // LANES):
                    scr_ref[s] = val[:, s * LANES:(s + 1) * LANES]
                for r in range(dil):
                    for s in range(width // LANES):
                        rows = scr_ref[s, pl.ds(r, tm // dil, stride=dil), :]
                        col = r * width + s * LANES
                        out_ref[:, col:col + LANES] = rows.astype(BF16)


def _proj_c_call(x, gn, wqkv, gq, gk):
    t_rows = x.shape[0]
    tm, hd = TOKEN_TILE // 2, HEAD_DIM
    width = C_HEADS * hd
    w = wqkv.astype(BF16)
    ones = _segment_ones(hd)
    specs, shapes = [], []
    for _, dil in C_PATTERNS:
        specs += [pl.BlockSpec((tm // dil, dil * width), lambda i: (i, 0))] * 3
        shapes += [jax.ShapeDtypeStruct((t_rows // dil, dil * width), BF16)] * 3
    outs = pl.pallas_call(
        _proj_c_kernel,
        grid=(t_rows // tm,),
        in_specs=[pl.BlockSpec((tm, D_MODEL), lambda i: (i, 0)), _const_spec((1, D_MODEL)),
                  _const_spec(w.shape), _const_spec((1, width)), _const_spec((1, width)),
                  _const_spec(ones.shape)],
        out_specs=specs,
        out_shape=shapes,
        scratch_shapes=[pltpu.VMEM((width // LANES, tm, LANES), F32)],
        compiler_params=_params(("parallel",)),
        name="proj_c",
    )(x, gn.reshape(1, D_MODEL), w, jnp.tile(gq * (hd ** -0.5 * LOG2E), C_HEADS).reshape(1, width),
      jnp.tile(gk, C_HEADS).reshape(1, width), ones)
    return [outs[3 * g:3 * g + 3] for g in range(len(C_PATTERNS))]


def _band_masked_dist(row0, lo, hi, sub, w):
    nk = sub + 2 * w
    r = lax.broadcasted_iota(jnp.int32, (sub, nk), 0)
    c = lax.broadcasted_iota(jnp.int32, (sub, nk), 1)
    dist = jnp.abs(r + w - c)
    jpos = row0 - w + lax.broadcasted_iota(jnp.int32, (1, nk), 1)
    outside = jnp.where(jpos < lo, w + 1, 0) + jnp.where(jpos >= hi, w + 1, 0)
    return jnp.where(dist + outside <= w, dist.astype(F32), MASK_DIST)


def _two_stage(n_items, first, second, ahead=2):
    pending = {i: first(i) for i in range(min(ahead, n_items))}
    results = []
    for i in range(n_items):
        if i + ahead < n_items:
            pending[i + ahead] = first(i + ahead)
        results.append(second(i, pending.pop(i)))
    return results


def _band_softmax_pv(s, vv, sink):
    m = jnp.max(s, axis=-1, keepdims=True)
    if sink is not None:
        m = jnp.maximum(m, sink)
    p = jnp.exp2(s - m)
    den = jnp.sum(p, axis=-1, keepdims=True)
    if sink is not None:
        den = den + jnp.exp2(sink - m)
    return _dot(p.astype(BF16), vv) / den, m + jnp.log2(den)


def _attn_a_kernel(slope_ref, sink_ref, q_ref, kp_ref, kc_ref, kn_ref, vp_ref, vc_ref, vn_ref,
                   o_ref, *, tq, p_len, s_len):
    w, sub = A_HALF_WINDOW, BAND_SUB
    group = A_HEADS // A_KV_HEADS
    kvh = pl.program_id(0)
    row0 = pl.program_id(1) * tq
    lo, hi = _seq_bounds(row0, p_len, s_len)
    kk = jnp.concatenate([kp_ref[0], kc_ref[0], kn_ref[0]], axis=0)
    vv = jnp.concatenate([vp_ref[0], vc_ref[0], vn_ref[0]], axis=0)
    sink = jnp.concatenate(
        [jnp.full((sub, 1), sink_ref[kvh * group + g], F32) for g in range(group)], axis=0)

    def keys(r):
        return slice(r * sub, (r + 1) * sub + 2 * w)

    def logits(r):
        md = _band_masked_dist(row0 + r * sub, lo, hi, sub, w)
        bias = jnp.concatenate([md * slope_ref[kvh * group + g] for g in range(group)], axis=0)
        q = q_ref[:, r * sub:(r + 1) * sub, :].reshape(group * sub, HEAD_DIM)
        return _dot_nt(q, kk[keys(r)]) + bias

    tiles = _two_stage(tq // sub, logits, lambda r, s: _band_softmax_pv(s, vv[keys(r)], sink)[0])
    outs = [[o[g * sub:(g + 1) * sub] for o in tiles] for g in range(group)]
    o_ref[...] = jnp.concatenate(
        [jnp.concatenate(rows, axis=0) for rows in outs], axis=-1).astype(BF16)


def _attn_a_call(q, k, v, sink, geo):
    p_len, s_len, t_rows = geo
    tq, w, hd = 512, A_HALF_WINDOW, HEAD_DIM
    group = A_HEADS // A_KV_HEADS
    per = tq // w
    last = t_rows // w - 1
    prev = pl.BlockSpec((1, w, hd), lambda h, i: (h, jnp.maximum(i * per - 1, 0), 0))
    cur = pl.BlockSpec((1, tq, hd), lambda h, i: (h, i, 0))
    nxt = pl.BlockSpec((1, w, hd), lambda h, i: (h, jnp.minimum((i + 1) * per, last), 0))
    smem = pl.BlockSpec(memory_space=pltpu.SMEM)
    slopes = jnp.asarray(-LOG2E * _alibi_slopes(A_HEADS), F32)
    return pl.pallas_call(
        functools.partial(_attn_a_kernel, tq=tq, p_len=p_len, s_len=s_len),
        grid=(A_KV_HEADS, t_rows // tq),
        in_specs=[smem, smem, pl.BlockSpec((group, tq, hd), lambda h, i: (h, i, 0)),
                  prev, cur, nxt, prev, cur, nxt],
        out_specs=pl.BlockSpec((tq, group * hd), lambda h, i: (i, h)),
        out_shape=jax.ShapeDtypeStruct((t_rows, A_HEADS * hd), BF16),
        compiler_params=_params(("parallel", "parallel")),
        name="attn_a",
    )(slopes, sink.astype(F32) * LOG2E, q, k, k, k, v, v, v)


def _attn_c_kernel(q_ref, kp_ref, kc_ref, kn_ref, vp_ref, vc_ref, vn_ref, o_ref, lse_ref,
                   *, tq, dil, p_len, s_len):
    w, hd, sub = C_STEPS, HEAD_DIM, min(BAND_SUB, tq)
    row0 = pl.program_id(0) * tq
    lo, hi = _seq_bounds(row0, p_len // dil, s_len // dil)
    slopes = -LOG2E * dil * _alibi_slopes(C_HEADS)
    mds = [_band_masked_dist(row0 + r * sub, lo, hi, sub, w) for r in range(tq // sub)]
    n_sub = tq // sub

    def window(prev_ref, cur_ref, next_ref, i):
        n, r = divmod(i, n_sub)
        hs = slice(n * hd, (n + 1) * hd)
        rows = jnp.concatenate([prev_ref[:, hs], cur_ref[:, hs], next_ref[:, hs]], axis=0)
        return rows[r * sub:(r + 1) * sub + 2 * w]

    def logits(i):
        n, r = divmod(i, n_sub)
        q = q_ref[r * sub:(r + 1) * sub, n * hd:(n + 1) * hd]
        return _dot_nt(q, window(kp_ref, kc_ref, kn_ref, i)) + mds[r] * float(slopes[n])

    def attend(i, s):
        o, lse2 = _band_softmax_pv(s, window(vp_ref, vc_ref, vn_ref, i), None)
        return o, jnp.broadcast_to(lse2 * (1.0 / LOG2E), (sub, hd))

    tiles = _two_stage(C_HEADS * n_sub, logits, attend)
    outs = [jnp.concatenate([t[0] for t in tiles[n * n_sub:(n + 1) * n_sub]], axis=0)
            for n in range(C_HEADS)]
    lses = [jnp.concatenate([t[1] for t in tiles[n * n_sub:(n + 1) * n_sub]], axis=0)
            for n in range(C_HEADS)]
    o_ref[...] = jnp.concatenate(outs, axis=-1)
    lse_ref[...] = jnp.concatenate(lses, axis=-1)


def _attn_c_call(q, k, v, dil, geo):
    p_len, s_len, t_rows = geo
    w, width = C_STEPS, C_HEADS * HEAD_DIM
    rows = t_rows // dil
    tq = min(512, s_len // dil)
    per = tq // w
    last = rows // w - 1
    prev = pl.BlockSpec((w, width), lambda i, r: (jnp.maximum(i * per - 1, 0), r))
    cur = pl.BlockSpec((tq, width), lambda i, r: (i, r))
    nxt = pl.BlockSpec((w, width), lambda i, r: (jnp.minimum((i + 1) * per, last), r))
    shape = jax.ShapeDtypeStruct((rows, dil * width), F32)
    return pl.pallas_call(
        functools.partial(_attn_c_kernel, tq=tq, dil=dil, p_len=p_len, s_len=s_len),
        grid=(rows // tq, dil),
        in_specs=[cur, prev, cur, nxt, prev, cur, nxt],
        out_specs=[cur, cur],
        out_shape=[shape, shape],
        compiler_params=_params(("parallel", "parallel")),
        name="attn_c_d%d" % dil,
    )(q, k, k, k, v, v, v)


def _attn_dense_kernel(qt_ref, k_ref, vt_ref, o_ref, sa_ref, sb_ref, sc_ref, sd_ref,
                       *, nq, tq, tk, p_len, s_len):
    dv = vt_ref.shape[-2] - BF16_SUBLANES
    width = nq * tq
    shift = int(math.log2(tk))

    def first_chunk(j):
        row0 = (pl.program_id(1) * Q_TILES_PER_STEP + j) * tq
        lo, hi = _seq_bounds(row0, p_len, s_len)
        return lax.shift_right_logical(lo, shift), lax.shift_right_logical(hi - lo, shift)

    def scores(j, c, st_ref):
        start = pl.multiple_of(c * tk, tk)
        qt = jnp.concatenate([qt_ref[g, j] for g in range(nq)], axis=-1)
        st = _dot(k_ref[0, pl.ds(start, tk), :], qt).astype(BF16)
        st_ref[...] = st
        return jnp.max(st, axis=0, keepdims=True).astype(F32)

    def update(c, st_ref, mc, m, acc):
        m_new = jnp.maximum(m, mc)
        alpha = jnp.exp2(m - m_new)
        pt = jnp.exp2(st_ref[...] - m_new.astype(BF16))
        return m_new, alpha * acc + _dot(vt_ref[0, c], pt)

    def q_tile(j, mcs, start_next):
        first, n_chunks = first_chunk(j)

        def quad(i, carry):
            m, acc, mc_a, mc_b = carry
            c = first + 4 * i
            mc_c = scores(j, c + 2, sc_ref)
            m, acc = update(c, sa_ref, mc_a, m, acc)
            mc_d = scores(j, c + 3, sd_ref)
            m, acc = update(c + 1, sb_ref, mc_b, m, acc)
            mc_a = scores(j, c + 4, sa_ref)
            m, acc = update(c + 2, sc_ref, mc_c, m, acc)
            mc_b = scores(j, c + 5, sb_ref)
            m, acc = update(c + 3, sd_ref, mc_d, m, acc)
            return m, acc, mc_a, mc_b

        carry = (jnp.full((1, width), NEG_INF, F32), jnp.zeros((vt_ref.shape[-2], width), F32),
                 *mcs)
        n_body = lax.shift_right_logical(n_chunks, 2) - 1
        carry = lax.fori_loop(0, lax.shift_right_logical(n_body, 1),
                              lambda i, cr: quad(2 * i + 1, quad(2 * i, cr)), carry)
        carry = lax.fori_loop(0, n_body & 1, lambda _, cr: quad(n_body - 1, cr), carry)
        m, acc, mc_a, mc_b = carry
        c = first + n_chunks - 4
        mc_c = scores(j, c + 2, sc_ref)
        m, acc = update(c, sa_ref, mc_a, m, acc)
        mc_d = scores(j, c + 3, sd_ref)
        m, acc = update(c + 1, sb_ref, mc_b, m, acc)
        if start_next:
            first_next, _ = first_chunk(j + 1)
            mc_a = scores(j + 1, first_next, sa_ref)
        m, acc = update(c + 2, sc_ref, mc_c, m, acc)
        if start_next:
            mc_b = scores(j + 1, first_next + 1, sb_ref)
        m, acc = update(c + 3, sd_ref, mc_d, m, acc)
        o = acc[:dv] / acc[dv:dv + 1]
        rows = pl.ds(pl.multiple_of(j * tq, tq), tq)
        heads = [o[:, g * tq:(g + 1) * tq].T.astype(BF16) for g in range(nq)]
        if nq == 1:
            o_ref[0, rows, :] = heads[0]
        else:
            o_ref[rows, :] = jnp.concatenate(heads, axis=-1)
        return mc_a, mc_b

    first, _ = first_chunk(0)
    mcs = (scores(0, first, sa_ref), scores(0, first + 1, sb_ref))
    mcs = lax.fori_loop(0, Q_TILES_PER_STEP - 1, lambda j, mcs: q_tile(j, mcs, True), mcs)
    q_tile(Q_TILES_PER_STEP - 1, mcs, False)


def _attn_dense_call(qt, k, vt, geo):
    p_len, s_len, t_rows = geo
    hq, n_tiles, dq, tq = qt.shape
    hk, n_chunks, rows_v, tk = vt.shape
    dv = rows_v - BF16_SUBLANES
    nq, per = hq // hk, Q_TILES_PER_STEP
    assert s_len % (4 * tk) == 0 and p_len % (4 * tk) == 0 and n_tiles % per == 0
    return pl.pallas_call(
        functools.partial(_attn_dense_kernel, nq=nq, tq=tq, tk=tk, p_len=p_len, s_len=s_len),
        grid=(hk, n_tiles // per),
        in_specs=[pl.BlockSpec((nq, per, dq, tq), lambda h, i: (h, i, 0, 0)),
                  pl.BlockSpec((1, t_rows, dq), lambda h, i: (h, 0, 0), pipeline_mode=pl.Buffered(1)),
                  pl.BlockSpec((1, n_chunks, dv + BF16_SUBLANES, tk), lambda h, i: (h, 0, 0, 0),
                               pipeline_mode=pl.Buffered(1))],
        out_specs=(pl.BlockSpec((1, per * tq, dv), lambda h, i: (h, i, 0)) if nq == 1
                   else pl.BlockSpec((per * tq, nq * dv), lambda h, i: (i, h))),
        out_shape=jax.ShapeDtypeStruct((hq, t_rows, dv) if nq == 1 else (t_rows, hq * dv), BF16),
        scratch_shapes=[pltpu.VMEM((tk, nq * tq), BF16)] * 4,
        compiler_params=_params(("parallel", "parallel")),
        name="attn_dense_%d" % dq,
    )(qt, k, vt)


def _trunk(x, p, geo, pos):
    p_tiles = geo[0] // TOKEN_TILE
    all_tiles = geo[2] // TOKEN_TILE
    for i in range(DEPTH):
        mixer, j = i % 4, i // 4
        x = _ffn_call(x, p["norm_ffn1"][i], p["ffn1_wg"][i], p["ffn1_wu"][i], p["ffn1_wd"][i])
        gn = p["norm_mix"][i]
        if mixer == 0:
            q, k, v = _proj_a_call(x, gn, p["a_wqkv"][j], p["a_gq"][j], p["a_gk"][j])
            pre = (_attn_a_call(q, k, v, p["a_sink"][j], geo), p["a_wo"][j])
        elif mixer == 1:
            qt, k, vt = _proj_b_call(x, gn, p["b_wdq"][j], p["b_gcq"][j], p["b_wuq"][j], p["b_wdkv"][j],
                                     p["b_gckv"][j], p["b_wukv"][j], p["b_gq"][j], p["b_gk"][j], pos, geo)
            pre = (_attn_dense_call(qt, k, vt, geo), p["b_wo"][j])
        elif mixer == 2:
            qkv = _proj_c_call(x, gn, p["c_wqkv"][j], p["c_gq"][j], p["c_gk"][j])
            res = [_attn_c_call(*qkv[g], dil, geo) for g, (_, dil) in enumerate(C_PATTERNS)]
            pre = ([r[0] for r in res], [r[1] for r in res], p["c_wo"][j])
        else:
            qt, k, vt = _proj_d_call(x, gn, p["d_wqkv"][j], p["d_gq"][j], p["d_gk"][j], pos, geo, tq=128)
            pre = (_attn_dense_call(qt, k, vt, geo), p["d_wo"][j])
        ffn2 = (p["norm_ffn2"][i], p["ffn2_wg"][i], p["ffn2_wu"][i], p["ffn2_wd"][i])
        if i < DEPTH - 1:
            x = _ffn_call(x, *ffn2, pre=pre)
        else:
            x = (_ffn_call(x, *ffn2, pre=pre, tiles=(0, p_tiles)),
                 _ffn_call(x, *ffn2, pre=pre, tiles=(p_tiles, all_tiles - p_tiles)))
    return x


def kernel(x_prompt, x_sample, norm_ffn1, ffn1_wg, ffn1_wu, ffn1_wd, norm_mix, norm_ffn2, ffn2_wg, ffn2_wu, ffn2_wd, a_wqkv, a_gq, a_gk, a_sink, a_wo, b_wdq, b_gcq, b_wuq, b_wdkv, b_gckv, b_wukv, b_gq, b_gk, b_wo, c_wqkv, c_gq, c_gk, c_wo, d_wqkv, d_gq, d_gk, d_wo):
    p = dict(norm_ffn1=norm_ffn1, ffn1_wg=ffn1_wg, ffn1_wu=ffn1_wu, ffn1_wd=ffn1_wd,
             norm_mix=norm_mix, norm_ffn2=norm_ffn2, ffn2_wg=ffn2_wg, ffn2_wu=ffn2_wu,
             ffn2_wd=ffn2_wd, a_wqkv=a_wqkv, a_gq=a_gq, a_gk=a_gk, a_sink=a_sink, a_wo=a_wo,
             b_wdq=b_wdq, b_gcq=b_gcq, b_wuq=b_wuq, b_wdkv=b_wdkv, b_gckv=b_gckv,
             b_wukv=b_wukv, b_gq=b_gq, b_gk=b_gk, b_wo=b_wo, c_wqkv=c_wqkv, c_gq=c_gq,
             c_gk=c_gk, c_wo=c_wo, d_wqkv=d_wqkv, d_gq=d_gq, d_gk=d_gk, d_wo=d_wo)
    n_p, p_seq, _ = x_prompt.shape
    n_s, s_len, _ = x_sample.shape
    assert n_p == 1 and s_len & (s_len - 1) == 0 and p_seq % s_len == 0
    p_len = n_p * p_seq
    t_rows = p_len + n_s * s_len
    geo = (p_len, s_len, t_rows)
    pos = jnp.arange(p_len, dtype=jnp.int32)
    x = (x_prompt.reshape(p_len, D_MODEL), x_sample.reshape(n_s * s_len, D_MODEL))
    y_p, y_s = _trunk(x, p, geo, pos)
    return y_p.reshape(x_prompt.shape), y_s.reshape(x_sample.shape)
```

```python
import functools
import math

import numpy as np
import jax
import jax.numpy as jnp
from jax import lax
from jax.experimental import pallas as pl
from jax.experimental.pallas import tpu as pltpu

F32 = jnp.float32
BF16 = jnp.bfloat16

D_MODEL = 1024
DEPTH = 4
HEAD_DIM = 64
D_FF = 2816
NORM_EPS = 1e-6
NEG_INF = -1e30
GRID_W = 64
ROPE_THETA = 10000.0
LOG2E = math.log2(math.e)
LANES = 128
MXU_TILE = 256
BF16_SUBLANES = 16
BAND_SUB = 128
Q_TILES_PER_STEP = 8
MASK_DIST = 1e33

A_HEADS, A_KV_HEADS, A_HALF_WINDOW = 16, 4, 128
B_HEADS, B_Q_LORA, B_KV_LORA, B_NOPE, B_ROPE, B_V = 16, 512, 256, 64, 32, 64
B_QK = B_NOPE + B_ROPE
B_PAD = 128
C_HEADS = 8
C_PATTERNS = ((128, 1), (512, 4), (2048, 16))
C_STEPS = 64
D_HEADS, D_KV_HEADS = 16, 4

FF_CHUNK = 256
N_FF_CHUNKS = D_FF // FF_CHUNK
TOKEN_TILE = 512
VMEM_LIMIT = 56 * 1024 * 1024


def _alibi_slopes(n):
    return 2.0 ** (-8.0 * np.arange(1, n + 1) / n)


def _params(sem, vmem=VMEM_LIMIT):
    return pltpu.CompilerParams(dimension_semantics=sem, vmem_limit_bytes=vmem)


def _const_spec(shape):
    nd = len(shape)
    return pl.BlockSpec(tuple(shape), lambda *_: (0,) * nd, pipeline_mode=pl.Buffered(1))


def _seq_bounds(row0, p_len, s_len):
    in_prompt = row0 < p_len
    b = lax.shift_right_logical(jnp.maximum(row0 - p_len, 0), int(math.log2(s_len)))
    lo = jnp.where(in_prompt, 0, p_len + b * s_len)
    hi = jnp.where(in_prompt, p_len, lo + s_len)
    return lo, hi


def _rms(x, g):
    return x * lax.rsqrt(jnp.mean(x * x, axis=-1, keepdims=True) + NORM_EPS) * g


def _dot(a, b):
    return jnp.dot(a, b, preferred_element_type=F32)


def _dot_nt(a, b):
    return lax.dot_general(a, b, (((1,), (1,)), ((), ())), preferred_element_type=F32)


def _ffn_kernel(*refs, mode, first_tiles=0):
    if mode == "plain":
        x_ref, gn_ref, wg_ref, wu_ref, wd_ref, out_ref, acc_ref, act_a, act_b = refs
        x = x_ref[...]
    elif mode == "plain2":
        xa_ref, xb_ref, gn_ref, wg_ref, wu_ref, wd_ref, out_ref, acc_ref, act_a, act_b = refs
        x = jnp.where(pl.program_id(0) < first_tiles, xa_ref[...], xb_ref[...])
    elif mode == "proj":
        (x_ref, o_ref, wo_ref, gn_ref, wg_ref, wu_ref, wd_ref, out_ref, acc_ref, act_a,
         act_b) = refs
        if len(o_ref.shape) == 3:
            o = jnp.concatenate([o_ref[n] for n in range(o_ref.shape[0])], axis=-1)
        else:
            o = o_ref[...]
        x = x_ref[...] + _dot(o, wo_ref[...])
    else:
        (x_ref, o0, o1, o2, l0, l1, l2, wo_ref, gn_ref, wg_ref, wu_ref, wd_ref,
         out_ref, acc_ref, act_a, act_b, *scr) = refs
        tm, width = x_ref.shape[0], wo_ref.shape[0]

        def token_major(ref, dil, scr_ref):
            slabs = width // LANES
            for r in range(dil):
                for s in range(slabs):
                    col = r * width + s * LANES
                    scr_ref[s, pl.ds(r, tm // dil, stride=dil), :] = ref[:, col:col + LANES]
            return jnp.concatenate([scr_ref[s] for s in range(slabs)], axis=-1)

        dils = [d for _, d in C_PATTERNS]
        assert dils[0] == 1
        oa, la = o0[...], l0[...]
        ob, lb = token_major(o1, dils[1], scr[0]), token_major(l1, dils[1], scr[1])
        oc, lc = token_major(o2, dils[2], scr[2]), token_major(l2, dils[2], scr[3])
        m = jnp.maximum(jnp.maximum(la, lb), lc)
        ea, eb, ec = jnp.exp(la - m), jnp.exp(lb - m), jnp.exp(lc - m)
        o = (ea * oa + eb * ob + ec * oc) / (ea + eb + ec)
        x = x_ref[...] + _dot(o.astype(BF16), wo_ref[...])

    h = _rms(x, gn_ref[...]).astype(BF16)

    def gate_up(c, a_ref):
        cols = pl.ds(pl.multiple_of(c * FF_CHUNK, FF_CHUNK), FF_CHUNK)
        g = _dot(h, wg_ref[:, cols])
        u = _dot(h, wu_ref[:, cols])
        a_ref[...] = (g * jax.nn.sigmoid(g) * u).astype(BF16)

    def down(c, a_ref):
        acc_ref[...] += _dot(a_ref[...], wd_ref[c])

    assert N_FF_CHUNKS % 2 == 1
    acc_ref[...] = jnp.zeros_like(acc_ref)
    gate_up(0, act_a)

    def pair(i, carry):
        c = 2 * i
        gate_up(c + 1, act_b)
        down(c, act_a)
        gate_up(c + 2, act_a)
        down(c + 1, act_b)
        return carry

    lax.fori_loop(0, N_FF_CHUNKS // 2, pair, 0)
    down(N_FF_CHUNKS - 1, act_a)
    out_ref[...] = x + 0.5 * acc_ref[...]


def _ffn_weights(wg, wu, wd):
    wg3 = wg.astype(BF16)
    wu3 = wu.astype(BF16)
    wd3 = wd.astype(BF16).reshape(N_FF_CHUNKS, FF_CHUNK, D_MODEL)
    return wg3, wu3, wd3


def _ffn_call(x, gn, wg, wu, wd, pre=None, tiles=None):
    tm = TOKEN_TILE
    xs = x if isinstance(x, tuple) else (x,)
    off, n_tiles = tiles if tiles is not None else (0, sum(a.shape[0] for a in xs) // tm)
    wg3, wu3, wd3 = _ffn_weights(wg, wu, wd)
    row = lambda w: pl.BlockSpec((tm, w), lambda i: (i + off, 0))
    ffn_args = [gn.reshape(1, D_MODEL), wg3, wu3, wd3]
    ffn_specs = [_const_spec((1, D_MODEL)), _const_spec(wg3.shape), _const_spec(wu3.shape),
                 _const_spec(wd3.shape)]
    scratch = []
    first_tiles = 0
    if len(xs) == 2:
        assert pre is None and tiles is None
        first_tiles = xs[0].shape[0] // tm
        last = xs[1].shape[0] // tm - 1
        mode, args = "plain2", list(xs)
        specs = [pl.BlockSpec((tm, D_MODEL), lambda i: (jnp.minimum(i, first_tiles - 1), 0)),
                 pl.BlockSpec((tm, D_MODEL), lambda i: (jnp.clip(i - first_tiles, 0, last), 0))]
    elif pre is None:
        mode, args, specs = "plain", [x], [row(D_MODEL)]
    elif len(pre) == 2:
        o, wo = pre
        wo = wo.astype(BF16)
        mode = "proj"
        args = [x, o, wo]
        o_spec = row(o.shape[1]) if o.ndim == 2 else pl.BlockSpec(
            (o.shape[0], tm, o.shape[2]), lambda i: (0, i + off, 0))
        specs = [row(D_MODEL), o_spec, _const_spec(wo.shape)]
    else:
        outs, lses, wo = pre
        wo = wo.astype(BF16)
        mode = "merge"
        args = [x, *outs, *lses, wo]
        width = wo.shape[0]
        dilated = [pl.BlockSpec((tm // d, d * width), lambda i: (i + off, 0)) for _, d in C_PATTERNS]
        specs = [row(D_MODEL)] + dilated * 2 + [_const_spec(wo.shape)]
        scratch = [pltpu.VMEM((width // LANES, tm, LANES), F32)] * 4
    return pl.pallas_call(
        functools.partial(_ffn_kernel, mode=mode, first_tiles=first_tiles),
        grid=(n_tiles,),
        in_specs=specs + ffn_specs,
        out_specs=pl.BlockSpec((tm, D_MODEL), lambda i: (i, 0)),
        out_shape=jax.ShapeDtypeStruct((n_tiles * tm, D_MODEL), F32),
        scratch_shapes=[pltpu.VMEM((tm, D_MODEL), F32)] + [pltpu.VMEM((tm, FF_CHUNK), BF16)] * 2
        + scratch,
        compiler_params=_params(("parallel",)),
        name="ffn_" + mode,
    )(*args, *ffn_args)


def _head_norm(t, width):
    return lax.rsqrt(jnp.sum(t * t, axis=-1, keepdims=True) * (1.0 / width) + NORM_EPS)


def _segment_ones(seg):
    n = MXU_TILE // seg
    return jnp.kron(jnp.eye(n, dtype=F32), jnp.ones((seg, seg), F32)).astype(BF16)


def _segment_rsqrt(t, e_ref, width):
    sq = t * t
    hi = sq.astype(BF16)
    lo = (sq - hi.astype(F32)).astype(BF16)
    e = e_ref[...]
    sums = [_dot(hi[:, b:b + MXU_TILE], e) + _dot(lo[:, b:b + MXU_TILE], e)
            for b in range(0, t.shape[-1], MXU_TILE)]
    ssq = sums[0] if len(sums) == 1 else jnp.concatenate(sums, axis=-1)
    return lax.rsqrt(ssq * (1.0 / width) + NORM_EPS)


def _store_vt(vt_ref, n, v_t):
    dv, rows = v_t.shape
    vt_ref[n, 0, :dv, :] = v_t.astype(BF16)
    first = lax.broadcasted_iota(jnp.int32, (BF16_SUBLANES, rows), 0) == 0
    vt_ref[n, 0, dv:, :] = jnp.where(first, 1.0, 0.0).astype(BF16)


def _proj_a_kernel(x_ref, gn_ref, w_ref, gain_ref, e_ref, q_ref, k_ref, v_ref):
    hd = HEAD_DIM
    nqk = (A_HEADS + A_KV_HEADS) * hd
    h = _rms(x_ref[...], gn_ref[...]).astype(BF16)
    y = _dot(h, w_ref[...])
    qk = y[:, :nqk]
    qk = (qk * _segment_rsqrt(qk, e_ref, hd) * gain_ref[...]).astype(BF16)
    for n in range(A_HEADS):
        q_ref[n] = qk[:, n * hd:(n + 1) * hd]
    for n in range(A_KV_HEADS):
        k_ref[n] = qk[:, (A_HEADS + n) * hd:(A_HEADS + n + 1) * hd]
        v_ref[n] = y[:, nqk + n * hd: nqk + (n + 1) * hd].astype(BF16)


def _proj_a_call(x, gn, wqkv, gq, gk):
    t_rows = x.shape[0]
    tm, hd = TOKEN_TILE, HEAD_DIM
    w = wqkv.astype(BF16)
    gain = jnp.concatenate([jnp.tile(gq * (hd ** -0.5 * LOG2E), A_HEADS), jnp.tile(gk, A_KV_HEADS)])
    ones = _segment_ones(hd)
    head_major = lambda n: pl.BlockSpec((n, tm, hd), lambda i: (0, i, 0))
    return pl.pallas_call(
        _proj_a_kernel,
        grid=(t_rows // tm,),
        in_specs=[pl.BlockSpec((tm, D_MODEL), lambda i: (i, 0)), _const_spec((1, D_MODEL)),
                  _const_spec(w.shape), _const_spec((1, gain.shape[0])), _const_spec(ones.shape)],
        out_specs=[head_major(A_HEADS), head_major(A_KV_HEADS), head_major(A_KV_HEADS)],
        out_shape=[jax.ShapeDtypeStruct((A_HEADS, t_rows, hd), BF16),
                   jax.ShapeDtypeStruct((A_KV_HEADS, t_rows, hd), BF16),
                   jax.ShapeDtypeStruct((A_KV_HEADS, t_rows, hd), BF16)],
        compiler_params=_params(("parallel",)),
        name="proj_a",
    )(x, gn.reshape(1, D_MODEL), w, gain.reshape(1, -1), ones)


def _proj_d_kernel(x_ref, gn_ref, w_ref, e_ref, cq_ref, sq_ref, ck_ref, sk_ref, qt_ref, k_ref, vt_ref):
    hd = HEAD_DIM
    nq, nk = D_HEADS * hd, D_KV_HEADS * hd
    tq = qt_ref.shape[-1]
    h = _rms(x_ref[...], gn_ref[...]).astype(BF16)
    y = _dot(h, w_ref[...])
    swapped = nq + 2 * nk
    t, ts = y[:, :nq + nk], y[:, swapped:swapped + nq + nk]
    r = _segment_rsqrt(t, e_ref, hd)
    for b in range((nq + nk) // LANES):
        is_q = b < nq // LANES
        c_ref, s_ref = (cq_ref, sq_ref) if is_q else (ck_ref, sk_ref)
        sl = slice(b * LANES, (b + 1) * LANES)
        blk = (t[:, sl] * c_ref[...] + ts[:, sl] * s_ref[...]) * r[:, sl]
        blk = blk.T.astype(BF16) if is_q else blk.astype(BF16)
        for half in range(LANES // hd):
            n = b * (LANES // hd) + half
            if is_q:
                for j in range(qt_ref.shape[1]):
                    qt_ref[n, j] = blk[half * hd:(half + 1) * hd, j * tq:(j + 1) * tq]
            else:
                k_ref[n - D_HEADS] = blk[:, half * hd:(half + 1) * hd]
    v_t = y[:, nq + nk: nq + 2 * nk].T
    for n in range(D_KV_HEADS):
        _store_vt(vt_ref, n, v_t[n * hd:(n + 1) * hd])


def _swap_halves_cols(w, group):
    k, n = w.shape
    return w.reshape(k, n // group, 2, group // 2)[:, :, ::-1, :].reshape(k, n)


def _position_spec(geo, tm, width):
    p_len, s_len, _ = geo
    p_tiles, s_tiles = p_len // tm, s_len // tm
    return pl.BlockSpec(
        (tm, width), lambda i: (jnp.where(i < p_tiles, i, (i - p_tiles) % s_tiles), 0))


def _rope_tables(pos, gain, scale):
    half = gain.shape[-1] // 2
    inv = ROPE_THETA ** (-jnp.arange(half, dtype=F32) / half)
    ang = pos.astype(F32)[:, None] * inv[None, :]
    cos, sin = lax.optimization_barrier((jnp.cos(ang), jnp.sin(ang)))
    g_sw = jnp.concatenate([gain[half:], gain[:half]])
    c = jnp.concatenate([cos, cos], axis=-1) * gain[None, :] * scale
    s = jnp.concatenate([-sin, sin], axis=-1) * g_sw[None, :] * scale
    return c, s


def _proj_d_call(x, gn, wqkv, gq, gk, pos, geo, tq):
    t_rows = x.shape[0]
    tm, hd = TOKEN_TILE, HEAD_DIM
    nq, nk = D_HEADS * hd, D_KV_HEADS * hd
    half = hd // 2
    w = jnp.concatenate([wqkv, _swap_halves_cols(wqkv[:, :nq + nk], half)], axis=1).astype(BF16)
    rows, cols = pos // GRID_W, pos % GRID_W

    def tables(g, scale):
        cr, sr = _rope_tables(rows, g[:half], scale)
        cc, sc = _rope_tables(cols, g[half:], scale)
        reps = LANES // hd
        return jnp.concatenate([cr, cc] * reps, axis=-1), jnp.concatenate([sr, sc] * reps, axis=-1)

    cq, sq = tables(gq, hd ** -0.5 * LOG2E)
    ck, sk = tables(gk, 1.0)
    ones = _segment_ones(hd)
    head_major = lambda n: pl.BlockSpec((n, tm, hd), lambda i: (0, i, 0))
    tab = _position_spec(geo, tm, LANES)
    return pl.pallas_call(
        _proj_d_kernel,
        grid=(t_rows // tm,),
        in_specs=[pl.BlockSpec((tm, D_MODEL), lambda i: (i, 0)), _const_spec((1, D_MODEL)),
                  _const_spec(w.shape), _const_spec(ones.shape), tab, tab, tab, tab],
        out_specs=[pl.BlockSpec((D_HEADS, tm // tq, hd, tq), lambda i: (0, i, 0, 0)),
                   head_major(D_KV_HEADS),
                   pl.BlockSpec((D_KV_HEADS, 1, hd + BF16_SUBLANES, tm), lambda i: (0, i, 0, 0))],
        out_shape=[jax.ShapeDtypeStruct((D_HEADS, t_rows // tq, hd, tq), BF16),
                   jax.ShapeDtypeStruct((D_KV_HEADS, t_rows, hd), BF16),
                   jax.ShapeDtypeStruct((D_KV_HEADS, t_rows // tm, hd + BF16_SUBLANES, tm), BF16)],
        compiler_params=_params(("parallel",)),
        name="proj_d",
    )(x, gn.reshape(1, D_MODEL), w, ones, cq, sq, ck, sk)


def _proj_b_kernel(x_ref, gn_ref, w1_ref, gcq_ref, gckv_ref, wq_ref, wqs_ref, wk_ref, wv_ref,
                   cq_ref, sq_ref, ck_ref, sk_ref, qt_ref, k_ref, vt_ref):
    h = _rms(x_ref[...], gn_ref[...]).astype(BF16)
    y1 = _dot(h, w1_ref[...])
    c_q = _rms(y1[:, :B_Q_LORA], gcq_ref[...]).astype(BF16)
    c_kv = _rms(y1[:, B_Q_LORA:B_Q_LORA + B_KV_LORA], gckv_ref[...]).astype(BF16)
    off = B_Q_LORA + B_KV_LORA
    kr = y1[:, off:off + B_PAD]
    krs = y1[:, off + B_PAD:off + 2 * B_PAD]
    q = _dot(c_q, wq_ref[...])
    qs = _dot(c_q, wqs_ref[...])
    kn = _dot(c_kv, wk_ref[...])
    v = _dot(c_kv, wv_ref[...])
    cq, sq, ck, sk = cq_ref[...], sq_ref[...], ck_ref[...], sk_ref[...]
    krs_s = krs * sk
    for n in range(B_HEADS):
        t = q[:, n * B_PAD:(n + 1) * B_PAD]
        ts = qs[:, n * B_PAD:(n + 1) * B_PAD]
        qt_ref[n, 0] = ((t * cq + ts * sq) * _head_norm(t, B_QK)).T.astype(BF16)
        t = kn[:, n * B_PAD:(n + 1) * B_PAD] + kr
        k_ref[n] = ((t * ck + krs_s) * _head_norm(t, B_QK)).astype(BF16)
    v_t = v.T
    for n in range(B_HEADS):
        _store_vt(vt_ref, n, v_t[n * B_V:(n + 1) * B_V])


def _pad_heads(w, lo, width):
    k, nh, _ = w.shape
    out = jnp.zeros((k, nh, B_PAD), w.dtype).at[:, :, lo:lo + width].set(w)
    return out.reshape(k, nh * B_PAD)


def _proj_b_call(x, gn, wdq, gcq, wuq, wdkv, gckv, wukv, gq, gk, pos, geo):
    t_rows = x.shape[0]
    tm = TOKEN_TILE
    wuq3 = wuq.reshape(B_Q_LORA, B_HEADS, B_QK)
    wq = _pad_heads(wuq3, 0, B_QK).astype(BF16)
    wuq_rope_sw = _swap_halves_cols(wuq3[:, :, B_NOPE:].reshape(B_Q_LORA, -1), B_ROPE)
    wqs = _pad_heads(wuq_rope_sw.reshape(B_Q_LORA, B_HEADS, B_ROPE), B_NOPE, B_ROPE).astype(BF16)
    wukv3 = wukv.reshape(B_KV_LORA, B_HEADS, B_NOPE + B_V)
    wk = _pad_heads(wukv3[:, :, :B_NOPE], 0, B_NOPE).astype(BF16)
    wv = wukv3[:, :, B_NOPE:].reshape(B_KV_LORA, B_HEADS * B_V).astype(BF16)
    w_kr = wdkv[:, B_KV_LORA:]
    kr_pad = _pad_heads(w_kr[:, None, :], B_NOPE, B_ROPE)
    krs_pad = _pad_heads(_swap_halves_cols(w_kr, B_ROPE)[:, None, :], B_NOPE, B_ROPE)
    w1 = jnp.concatenate([wdq, wdkv[:, :B_KV_LORA], kr_pad, krs_pad], axis=1).astype(BF16)

    def tables(g, scale):
        c, s = _rope_tables(pos, g[B_NOPE:], scale)
        n_pos = pos.shape[0]
        zeros = jnp.zeros((n_pos, B_PAD - B_QK), F32)
        c_full = jnp.concatenate(
            [jnp.broadcast_to(g[None, :B_NOPE] * scale, (n_pos, B_NOPE)), c, zeros], axis=-1)
        s_full = jnp.concatenate([jnp.zeros((n_pos, B_NOPE), F32), s, zeros], axis=-1)
        return c_full, s_full

    cq, sq = tables(gq, B_QK ** -0.5 * LOG2E)
    ck, sk = tables(gk, 1.0)
    tab = _position_spec(geo, tm, B_PAD)
    head_major = lambda w_: pl.BlockSpec((B_HEADS, tm, w_), lambda i: (0, i, 0))
    return pl.pallas_call(
        _proj_b_kernel,
        grid=(t_rows // tm,),
        in_specs=[pl.BlockSpec((tm, D_MODEL), lambda i: (i, 0)), _const_spec((1, D_MODEL)),
                  _const_spec(w1.shape), _const_spec((1, B_Q_LORA)), _const_spec((1, B_KV_LORA)),
                  _const_spec(wq.shape), _const_spec(wqs.shape), _const_spec(wk.shape),
                  _const_spec(wv.shape), tab, tab, tab, tab],
        out_specs=[pl.BlockSpec((B_HEADS, 1, B_PAD, tm), lambda i: (0, i, 0, 0)),
                   head_major(B_PAD),
                   pl.BlockSpec((B_HEADS, 1, B_V + BF16_SUBLANES, tm), lambda i: (0, i, 0, 0))],
        out_shape=[jax.ShapeDtypeStruct((B_HEADS, t_rows // tm, B_PAD, tm), BF16),
                   jax.ShapeDtypeStruct((B_HEADS, t_rows, B_PAD), BF16),
                   jax.ShapeDtypeStruct((B_HEADS, t_rows // tm, B_V + BF16_SUBLANES, tm), BF16)],
        compiler_params=_params(("parallel",)),
        name="proj_b",
    )(x, gn.reshape(1, D_MODEL), w1, gcq.reshape(1, -1), gckv.reshape(1, -1), wq, wqs, wk, wv,
      cq, sq, ck, sk)


def _proj_c_kernel(x_ref, gn_ref, w_ref, gq_ref, gk_ref, e_ref, *refs):
    hd = HEAD_DIM
    width = C_HEADS * hd
    outs, scr_ref = refs[:-1], refs[-1]
    tm = x_ref.shape[0]
    h = _rms(x_ref[...], gn_ref[...]).astype(BF16)
    y = _dot(h, w_ref[...])
    gains = (gq_ref[...], gk_ref[...])
    for g, (_, dil) in enumerate(C_PATTERNS):
        for j in range(3):
            base = (g * 3 + j) * width
            val = y[:, base: base + width]
            if j < 2:
                val = val * _segment_rsqrt(val, e_ref, hd) * gains[j]
            out_ref = outs[g * 3 + j]
            if dil == 1:
                out_ref[...] = val.astype(BF16)
            else:
                for s in range(width // LANES):
                    scr_ref[s] = val[:, s * LANES:(s + 1) * LANES]
                for r in range(dil):
                    for s in range(width // LANES):
                        rows = scr_ref[s, pl.ds(r, tm // dil, stride=dil), :]
                        col = r * width + s * LANES
                        out_ref[:, col:col + LANES] = rows.astype(BF16)


def _proj_c_call(x, gn, wqkv, gq, gk):
    t_rows = x.shape[0]
    tm, hd = TOKEN_TILE // 2, HEAD_DIM
    width = C_HEADS * hd
    w = wqkv.astype(BF16)
    ones = _segment_ones(hd)
    specs, shapes = [], []
    for _, dil in C_PATTERNS:
        specs += [pl.BlockSpec((tm // dil, dil * width), lambda i: (i, 0))] * 3
        shapes += [jax.ShapeDtypeStruct((t_rows // dil, dil * width), BF16)] * 3
    outs = pl.pallas_call(
        _proj_c_kernel,
        grid=(t_rows // tm,),
        in_specs=[pl.BlockSpec((tm, D_MODEL), lambda i: (i, 0)), _const_spec((1, D_MODEL)),
                  _const_spec(w.shape), _const_spec((1, width)), _const_spec((1, width)),
                  _const_spec(ones.shape)],
        out_specs=specs,
        out_shape=shapes,
        scratch_shapes=[pltpu.VMEM((width // LANES, tm, LANES), F32)],
        compiler_params=_params(("parallel",)),
        name="proj_c",
    )(x, gn.reshape(1, D_MODEL), w, jnp.tile(gq * (hd ** -0.5 * LOG2E), C_HEADS).reshape(1, width),
      jnp.tile(gk, C_HEADS).reshape(1, width), ones)
    return [outs[3 * g:3 * g + 3] for g in range(len(C_PATTERNS))]


def _band_masked_dist(row0, lo, hi, sub, w):
    nk = sub + 2 * w
    r = lax.broadcasted_iota(jnp.int32, (sub, nk), 0)
    c = lax.broadcasted_iota(jnp.int32, (sub, nk), 1)
    dist = jnp.abs(r + w - c)
    jpos = row0 - w + lax.broadcasted_iota(jnp.int32, (1, nk), 1)
    outside = jnp.where(jpos < lo, w + 1, 0) + jnp.where(jpos >= hi, w + 1, 0)
    return jnp.where(dist + outside <= w, dist.astype(F32), MASK_DIST)


def _two_stage(n_items, first, second, ahead=2):
    pending = {i: first(i) for i in range(min(ahead, n_items))}
    results = []
    for i in range(n_items):
        if i + ahead < n_items:
            pending[i + ahead] = first(i + ahead)
        results.append(second(i, pending.pop(i)))
    return results


def _with_ones(vv):
    return jnp.concatenate([vv, jnp.ones((vv.shape[0], BF16_SUBLANES), BF16)], axis=-1)


def _band_softmax_pv(s, vv_ones, sink):
    m = jnp.max(s, axis=-1, keepdims=True)
    if sink is not None:
        m = jnp.maximum(m, sink)
    hd = vv_ones.shape[-1] - BF16_SUBLANES
    pv = _dot(jnp.exp2(s - m).astype(BF16), vv_ones)
    den = pv[:, hd:hd + 1]
    if sink is not None:
        den = den + jnp.exp2(sink - m)
    return pv[:, :hd] / den, m + jnp.log2(den)


def _attn_a_kernel(slope_ref, sink_ref, q_ref, kp_ref, kc_ref, kn_ref, vp_ref, vc_ref, vn_ref,
                   o_ref, *, tq, p_len, s_len):
    w, sub = A_HALF_WINDOW, BAND_SUB
    group = A_HEADS // A_KV_HEADS
    kvh = pl.program_id(0)
    row0 = pl.program_id(1) * tq
    lo, hi = _seq_bounds(row0, p_len, s_len)
    kk = jnp.concatenate([kp_ref[0], kc_ref[0], kn_ref[0]], axis=0)
    vv = _with_ones(jnp.concatenate([vp_ref[0], vc_ref[0], vn_ref[0]], axis=0))
    sink = jnp.concatenate(
        [jnp.full((sub, 1), sink_ref[kvh * group + g], F32) for g in range(group)], axis=0)

    def keys(r):
        return slice(r * sub, (r + 1) * sub + 2 * w)

    def logits(r):
        md = _band_masked_dist(row0 + r * sub, lo, hi, sub, w)
        bias = jnp.concatenate([md * slope_ref[kvh * group + g] for g in range(group)], axis=0)
        q = q_ref[:, r * sub:(r + 1) * sub, :].reshape(group * sub, HEAD_DIM)
        return _dot_nt(q, kk[keys(r)]) + bias

    tiles = _two_stage(tq // sub, logits, lambda r, s: _band_softmax_pv(s, vv[keys(r)], sink)[0])
    outs = [[o[g * sub:(g + 1) * sub] for o in tiles] for g in range(group)]
    o_ref[...] = jnp.concatenate(
        [jnp.concatenate(rows, axis=0) for rows in outs], axis=-1).astype(BF16)


def _attn_a_call(q, k, v, sink, geo):
    p_len, s_len, t_rows = geo
    tq, w, hd = 512, A_HALF_WINDOW, HEAD_DIM
    group = A_HEADS // A_KV_HEADS
    per = tq // w
    last = t_rows // w - 1
    prev = pl.BlockSpec((1, w, hd), lambda h, i: (h, jnp.maximum(i * per - 1, 0), 0))
    cur = pl.BlockSpec((1, tq, hd), lambda h, i: (h, i, 0))
    nxt = pl.BlockSpec((1, w, hd), lambda h, i: (h, jnp.minimum((i + 1) * per, last), 0))
    smem = pl.BlockSpec(memory_space=pltpu.SMEM)
    slopes = jnp.asarray(-LOG2E * _alibi_slopes(A_HEADS), F32)
    return pl.pallas_call(
        functools.partial(_attn_a_kernel, tq=tq, p_len=p_len, s_len=s_len),
        grid=(A_KV_HEADS, t_rows // tq),
        in_specs=[smem, smem, pl.BlockSpec((group, tq, hd), lambda h, i: (h, i, 0)),
                  prev, cur, nxt, prev, cur, nxt],
        out_specs=pl.BlockSpec((tq, group * hd), lambda h, i: (i, h)),
        out_shape=jax.ShapeDtypeStruct((t_rows, A_HEADS * hd), BF16),
        compiler_params=_params(("parallel", "parallel")),
        name="attn_a",
    )(slopes, sink.astype(F32) * LOG2E, q, k, k, k, v, v, v)


def _attn_c_kernel(q_ref, kp_ref, kc_ref, kn_ref, vp_ref, vc_ref, vn_ref, o_ref, lse_ref,
                   *, tq, dil, p_len, s_len):
    w, hd, sub = C_STEPS, HEAD_DIM, min(BAND_SUB, tq)
    row0 = pl.program_id(0) * tq
    lo, hi = _seq_bounds(row0, p_len // dil, s_len // dil)
    slopes = -LOG2E * dil * _alibi_slopes(C_HEADS)
    mds = [_band_masked_dist(row0 + r * sub, lo, hi, sub, w) for r in range(tq // sub)]
    n_sub = tq // sub

    heads = {}

    def head_rows(n):
        if n not in heads:
            hs = slice(n * hd, (n + 1) * hd)
            kk = jnp.concatenate([kp_ref[:, hs], kc_ref[:, hs], kn_ref[:, hs]], axis=0)
            vv = jnp.concatenate([vp_ref[:, hs], vc_ref[:, hs], vn_ref[:, hs]], axis=0)
            heads[n] = (q_ref[:, hs], kk, _with_ones(vv))
        return heads[n]

    def logits(i):
        n, r = divmod(i, n_sub)
        q, kk, _ = head_rows(n)
        keys = kk[r * sub:(r + 1) * sub + 2 * w]
        return _dot_nt(q[r * sub:(r + 1) * sub], keys) + mds[r] * float(slopes[n])

    def attend(i, s):
        n, r = divmod(i, n_sub)
        o, lse2 = _band_softmax_pv(s, head_rows(n)[2][r * sub:(r + 1) * sub + 2 * w], None)
        return o, jnp.broadcast_to(lse2 * (1.0 / LOG2E), (sub, hd))

    tiles = _two_stage(C_HEADS * n_sub, logits, attend)
    outs = [jnp.concatenate([t[0] for t in tiles[n * n_sub:(n + 1) * n_sub]], axis=0)
            for n in range(C_HEADS)]
    lses = [jnp.concatenate([t[1] for t in tiles[n * n_sub:(n + 1) * n_sub]], axis=0)
            for n in range(C_HEADS)]
    o_ref[...] = jnp.concatenate(outs, axis=-1)
    lse_ref[...] = jnp.concatenate(lses, axis=-1)


def _attn_c_call(q, k, v, dil, geo):
    p_len, s_len, t_rows = geo
    w, width = C_STEPS, C_HEADS * HEAD_DIM
    rows = t_rows // dil
    tq = min(512, s_len // dil)
    per = tq // w
    last = rows // w - 1
    prev = pl.BlockSpec((w, width), lambda i, r: (jnp.maximum(i * per - 1, 0), r))
    cur = pl.BlockSpec((tq, width), lambda i, r: (i, r))
    nxt = pl.BlockSpec((w, width), lambda i, r: (jnp.minimum((i + 1) * per, last), r))
    shape = jax.ShapeDtypeStruct((rows, dil * width), F32)
    return pl.pallas_call(
        functools.partial(_attn_c_kernel, tq=tq, dil=dil, p_len=p_len, s_len=s_len),
        grid=(rows // tq, dil),
        in_specs=[cur, prev, cur, nxt, prev, cur, nxt],
        out_specs=[cur, cur],
        out_shape=[shape, shape],
        compiler_params=_params(("parallel", "parallel")),
        name="attn_c_d%d" % dil,
    )(q, k, k, k, v, v, v)


def _attn_dense_kernel(qt_ref, k_ref, vt_ref, o_ref, sa_ref, sb_ref, sc_ref, sd_ref,
                       *, nq, tq, tk, p_len, s_len):
    dv = vt_ref.shape[-2] - BF16_SUBLANES
    width = nq * tq
    shift = int(math.log2(tk))

    def first_chunk(j):
        row0 = (pl.program_id(1) * Q_TILES_PER_STEP + j) * tq
        lo, hi = _seq_bounds(row0, p_len, s_len)
        return lax.shift_right_logical(lo, shift), lax.shift_right_logical(hi - lo, shift)

    def scores(j, c, st_ref):
        start = pl.multiple_of(c * tk, tk)
        qt = jnp.concatenate([qt_ref[g, j] for g in range(nq)], axis=-1)
        st = _dot(k_ref[0, pl.ds(start, tk), :], qt).astype(BF16)
        st_ref[...] = st
        return jnp.max(st, axis=0, keepdims=True).astype(F32)

    def update(c, st_ref, mc, m, acc):
        m_new = jnp.maximum(m, mc)
        alpha = jnp.exp2(m - m_new)
        pt = jnp.exp2(st_ref[...] - m_new.astype(BF16))
        return m_new, alpha * acc + _dot(vt_ref[0, c], pt)

    def q_tile(j, mcs, start_next):
        first, n_chunks = first_chunk(j)

        def quad(i, carry):
            m, acc, mc_a, mc_b = carry
            c = first + 4 * i
            mc_c = scores(j, c + 2, sc_ref)
            m, acc = update(c, sa_ref, mc_a, m, acc)
            mc_d = scores(j, c + 3, sd_ref)
            m, acc = update(c + 1, sb_ref, mc_b, m, acc)
            mc_a = scores(j, c + 4, sa_ref)
            m, acc = update(c + 2, sc_ref, mc_c, m, acc)
            mc_b = scores(j, c + 5, sb_ref)
            m, acc = update(c + 3, sd_ref, mc_d, m, acc)
            return m, acc, mc_a, mc_b

        carry = (jnp.full((1, width), NEG_INF, F32), jnp.zeros((vt_ref.shape[-2], width), F32),
                 *mcs)
        n_body = lax.shift_right_logical(n_chunks, 2) - 1
        carry = lax.fori_loop(0, lax.shift_right_logical(n_body, 1),
                              lambda i, cr: quad(2 * i + 1, quad(2 * i, cr)), carry)
        carry = lax.fori_loop(0, n_body & 1, lambda _, cr: quad(n_body - 1, cr), carry)
        m, acc, mc_a, mc_b = carry
        c = first + n_chunks - 4
        mc_c = scores(j, c + 2, sc_ref)
        m, acc = update(c, sa_ref, mc_a, m, acc)
        mc_d = scores(j, c + 3, sd_ref)
        m, acc = update(c + 1, sb_ref, mc_b, m, acc)
        if start_next:
            first_next, _ = first_chunk(j + 1)
            mc_a = scores(j + 1, first_next, sa_ref)
        m, acc = update(c + 2, sc_ref, mc_c, m, acc)
        if start_next:
            mc_b = scores(j + 1, first_next + 1, sb_ref)
        m, acc = update(c + 3, sd_ref, mc_d, m, acc)
        o = acc[:dv] / acc[dv:dv + 1]
        rows = pl.ds(pl.multiple_of(j * tq, tq), tq)
        heads = [o[:, g * tq:(g + 1) * tq].T.astype(BF16) for g in range(nq)]
        if nq == 1:
            o_ref[0, rows, :] = heads[0]
        else:
            o_ref[rows, :] = jnp.concatenate(heads, axis=-1)
        return mc_a, mc_b

    first, _ = first_chunk(0)
    mcs = (scores(0, first, sa_ref), scores(0, first + 1, sb_ref))
    mcs = lax.fori_loop(0, Q_TILES_PER_STEP - 1, lambda j, mcs: q_tile(j, mcs, True), mcs)
    q_tile(Q_TILES_PER_STEP - 1, mcs, False)


def _attn_dense_call(qt, k, vt, geo):
    p_len, s_len, t_rows = geo
    hq, n_tiles, dq, tq = qt.shape
    hk, n_chunks, rows_v, tk = vt.shape
    dv = rows_v - BF16_SUBLANES
    nq, per = hq // hk, Q_TILES_PER_STEP
    assert s_len % (4 * tk) == 0 and p_len % (4 * tk) == 0 and n_tiles % per == 0
    return pl.pallas_call(
        functools.partial(_attn_dense_kernel, nq=nq, tq=tq, tk=tk, p_len=p_len, s_len=s_len),
        grid=(hk, n_tiles // per),
        in_specs=[pl.BlockSpec((nq, per, dq, tq), lambda h, i: (h, i, 0, 0)),
                  pl.BlockSpec((1, t_rows, dq), lambda h, i: (h, 0, 0)),
                  pl.BlockSpec((1, n_chunks, dv + BF16_SUBLANES, tk), lambda h, i: (h, 0, 0, 0))],
        out_specs=(pl.BlockSpec((1, per * tq, dv), lambda h, i: (h, i, 0)) if nq == 1
                   else pl.BlockSpec((per * tq, nq * dv), lambda h, i: (i, h))),
        out_shape=jax.ShapeDtypeStruct((hq, t_rows, dv) if nq == 1 else (t_rows, hq * dv), BF16),
        scratch_shapes=[pltpu.VMEM((tk, nq * tq), BF16)] * 4,
        compiler_params=_params(("parallel", "parallel")),
        name="attn_dense_%d" % dq,
    )(qt, k, vt)


def _trunk(x, p, geo, pos):
    p_tiles = geo[0] // TOKEN_TILE
    all_tiles = geo[2] // TOKEN_TILE
    for i in range(DEPTH):
        mixer, j = i % 4, i // 4
        x = _ffn_call(x, p["norm_ffn1"][i], p["ffn1_wg"][i], p["ffn1_wu"][i], p["ffn1_wd"][i])
        gn = p["norm_mix"][i]
        if mixer == 0:
            q, k, v = _proj_a_call(x, gn, p["a_wqkv"][j], p["a_gq"][j], p["a_gk"][j])
            pre = (_attn_a_call(q, k, v, p["a_sink"][j], geo), p["a_wo"][j])
        elif mixer == 1:
            qt, k, vt = _proj_b_call(x, gn, p["b_wdq"][j], p["b_gcq"][j], p["b_wuq"][j], p["b_wdkv"][j],
                                     p["b_gckv"][j], p["b_wukv"][j], p["b_gq"][j], p["b_gk"][j], pos, geo)
            pre = (_attn_dense_call(qt, k, vt, geo), p["b_wo"][j])
        elif mixer == 2:
            qkv = _proj_c_call(x, gn, p["c_wqkv"][j], p["c_gq"][j], p["c_gk"][j])
            res = [_attn_c_call(*qkv[g], dil, geo) for g, (_, dil) in enumerate(C_PATTERNS)]
            pre = ([r[0] for r in res], [r[1] for r in res], p["c_wo"][j])
        else:
            qt, k, vt = _proj_d_call(x, gn, p["d_wqkv"][j], p["d_gq"][j], p["d_gk"][j], pos, geo, tq=128)
            pre = (_attn_dense_call(qt, k, vt, geo), p["d_wo"][j])
        ffn2 = (p["norm_ffn2"][i], p["ffn2_wg"][i], p["ffn2_wu"][i], p["ffn2_wd"][i])
        if i < DEPTH - 1:
            x = _ffn_call(x, *ffn2, pre=pre)
        else:
            x = (_ffn_call(x, *ffn2, pre=pre, tiles=(0, p_tiles)),
                 _ffn_call(x, *ffn2, pre=pre, tiles=(p_tiles, all_tiles - p_tiles)))
    return x


def kernel(x_prompt, x_sample, norm_ffn1, ffn1_wg, ffn1_wu, ffn1_wd, norm_mix, norm_ffn2, ffn2_wg, ffn2_wu, ffn2_wd, a_wqkv, a_gq, a_gk, a_sink, a_wo, b_wdq, b_gcq, b_wuq, b_wdkv, b_gckv, b_wukv, b_gq, b_gk, b_wo, c_wqkv, c_gq, c_gk, c_wo, d_wqkv, d_gq, d_gk, d_wo):
    p = dict(norm_ffn1=norm_ffn1, ffn1_wg=ffn1_wg, ffn1_wu=ffn1_wu, ffn1_wd=ffn1_wd,
             norm_mix=norm_mix, norm_ffn2=norm_ffn2, ffn2_wg=ffn2_wg, ffn2_wu=ffn2_wu,
             ffn2_wd=ffn2_wd, a_wqkv=a_wqkv, a_gq=a_gq, a_gk=a_gk, a_sink=a_sink, a_wo=a_wo,
             b_wdq=b_wdq, b_gcq=b_gcq, b_wuq=b_wuq, b_wdkv=b_wdkv, b_gckv=b_gckv,
             b_wukv=b_wukv, b_gq=b_gq, b_gk=b_gk, b_wo=b_wo, c_wqkv=c_wqkv, c_gq=c_gq,
             c_gk=c_gk, c_wo=c_wo, d_wqkv=d_wqkv, d_gq=d_gq, d_gk=d_gk, d_wo=d_wo)
    n_p, p_seq, _ = x_prompt.shape
    n_s, s_len, _ = x_sample.shape
    assert n_p == 1 and s_len & (s_len - 1) == 0 and p_seq % s_len == 0
    p_len = n_p * p_seq
    t_rows = p_len + n_s * s_len
    geo = (p_len, s_len, t_rows)
    pos = jnp.arange(p_len, dtype=jnp.int32)
    x = (x_prompt.reshape(p_len, D_MODEL), x_sample.reshape(n_s * s_len, D_MODEL))
    y_p, y_s = _trunk(x, p, geo, pos)
    return y_p.reshape(x_prompt.shape), y_s.reshape(x_sample.shape)
```

```python
import functools
import math

import numpy as np
import jax
import jax.numpy as jnp
from jax import lax
from jax.experimental import pallas as pl
from jax.experimental.pallas import tpu as pltpu

F32 = jnp.float32
BF16 = jnp.bfloat16

D_MODEL = 1024
DEPTH = 4
HEAD_DIM = 64
D_FF = 2816
NORM_EPS = 1e-6
NEG_INF = -1e30
GRID_W = 64
ROPE_THETA = 10000.0
LOG2E = math.log2(math.e)
LANES = 128
MXU_TILE = 256
BF16_SUBLANES = 16
BAND_SUB = 128
Q_TILES_PER_STEP = 8
MASK_DIST = 1e33

A_HEADS, A_KV_HEADS, A_HALF_WINDOW = 16, 4, 128
B_HEADS, B_Q_LORA, B_KV_LORA, B_NOPE, B_ROPE, B_V = 16, 512, 256, 64, 32, 64
B_QK = B_NOPE + B_ROPE
B_PAD = 128
C_HEADS = 8
C_PATTERNS = ((128, 1), (512, 4), (2048, 16))
C_STEPS = 64
D_HEADS, D_KV_HEADS = 16, 4

FF_CHUNK = 256
N_FF_CHUNKS = D_FF // FF_CHUNK
TOKEN_TILE = 512
VMEM_LIMIT = 56 * 1024 * 1024


def _alibi_slopes(n):
    return 2.0 ** (-8.0 * np.arange(1, n + 1) / n)


def _params(sem, vmem=VMEM_LIMIT):
    return pltpu.CompilerParams(dimension_semantics=sem, vmem_limit_bytes=vmem)


def _const_spec(shape):
    nd = len(shape)
    return pl.BlockSpec(tuple(shape), lambda *_: (0,) * nd, pipeline_mode=pl.Buffered(1))


def _seq_bounds(row0, p_len, s_len):
    in_prompt = row0 < p_len
    b = lax.shift_right_logical(jnp.maximum(row0 - p_len, 0), int(math.log2(s_len)))
    lo = jnp.where(in_prompt, 0, p_len + b * s_len)
    hi = jnp.where(in_prompt, p_len, lo + s_len)
    return lo, hi


def _rms(x, g):
    return x * lax.rsqrt(jnp.mean(x * x, axis=-1, keepdims=True) + NORM_EPS) * g


def _dot(a, b):
    return jnp.dot(a, b, preferred_element_type=F32)


def _dot_nt(a, b):
    return lax.dot_general(a, b, (((1,), (1,)), ((), ())), preferred_element_type=F32)


def _ffn_kernel(*refs, mode, first_tiles=0):
    if mode == "plain":
        x_ref, gn_ref, wg_ref, wu_ref, wd_ref, out_ref, acc_ref, act_a, act_b = refs
        x = x_ref[...]
    elif mode == "plain2":
        xa_ref, xb_ref, gn_ref, wg_ref, wu_ref, wd_ref, out_ref, acc_ref, act_a, act_b = refs
        x = jnp.where(pl.program_id(0) < first_tiles, xa_ref[...], xb_ref[...])
    elif mode == "proj":
        (x_ref, o_ref, wo_ref, gn_ref, wg_ref, wu_ref, wd_ref, out_ref, acc_ref, act_a,
         act_b) = refs
        if len(o_ref.shape) == 3:
            o = jnp.concatenate([o_ref[n] for n in range(o_ref.shape[0])], axis=-1)
        else:
            o = o_ref[...]
        x = x_ref[...] + _dot(o, wo_ref[...])
    else:
        (x_ref, o0, o1, o2, l0, l1, l2, wo_ref, gn_ref, wg_ref, wu_ref, wd_ref,
         out_ref, acc_ref, act_a, act_b, *scr) = refs
        tm, width = x_ref.shape[0], wo_ref.shape[0]

        def token_major(ref, dil, scr_ref):
            slabs = width // LANES
            for r in range(dil):
                for s in range(slabs):
                    col = r * width + s * LANES
                    scr_ref[s, pl.ds(r, tm // dil, stride=dil), :] = ref[:, col:col + LANES]
            return jnp.concatenate([scr_ref[s] for s in range(slabs)], axis=-1)

        dils = [d for _, d in C_PATTERNS]
        assert dils[0] == 1
        oa, la = o0[...], l0[...]
        ob, lb = token_major(o1, dils[1], scr[0]), token_major(l1, dils[1], scr[1])
        oc, lc = token_major(o2, dils[2], scr[2]), token_major(l2, dils[2], scr[3])
        m = jnp.maximum(jnp.maximum(la, lb), lc)
        ea, eb, ec = jnp.exp(la - m), jnp.exp(lb - m), jnp.exp(lc - m)
        o = (ea * oa + eb * ob + ec * oc) / (ea + eb + ec)
        x = x_ref[...] + _dot(o.astype(BF16), wo_ref[...])

    h = _rms(x, gn_ref[...]).astype(BF16)

    def gate_up(c, a_ref):
        cols = pl.ds(pl.multiple_of(c * FF_CHUNK, FF_CHUNK), FF_CHUNK)
        g = _dot(h, wg_ref[:, cols])
        u = _dot(h, wu_ref[:, cols])
        a_ref[...] = (g * jax.nn.sigmoid(g) * u).astype(BF16)

    def down(c, a_ref):
        acc_ref[...] += _dot(a_ref[...], wd_ref[c])

    assert N_FF_CHUNKS % 2 == 1
    acc_ref[...] = jnp.zeros_like(acc_ref)
    gate_up(0, act_a)

    def pair(i, carry):
        c = 2 * i
        gate_up(c + 1, act_b)
        down(c, act_a)
        gate_up(c + 2, act_a)
        down(c + 1, act_b)
        return carry

    lax.fori_loop(0, N_FF_CHUNKS // 2, pair, 0)
    down(N_FF_CHUNKS - 1, act_a)
    out_ref[...] = x + 0.5 * acc_ref[...]


def _ffn_weights(wg, wu, wd):
    wg3 = wg.astype(BF16)
    wu3 = wu.astype(BF16)
    wd3 = wd.astype(BF16).reshape(N_FF_CHUNKS, FF_CHUNK, D_MODEL)
    return wg3, wu3, wd3


def _ffn_call(x, gn, wg, wu, wd, pre=None, tiles=None):
    tm = TOKEN_TILE
    xs = x if isinstance(x, tuple) else (x,)
    off, n_tiles = tiles if tiles is not None else (0, sum(a.shape[0] for a in xs) // tm)
    wg3, wu3, wd3 = _ffn_weights(wg, wu, wd)
    row = lambda w: pl.BlockSpec((tm, w), lambda i: (i + off, 0))
    ffn_args = [gn.reshape(1, D_MODEL), wg3, wu3, wd3]
    ffn_specs = [_const_spec((1, D_MODEL)), _const_spec(wg3.shape), _const_spec(wu3.shape),
                 _const_spec(wd3.shape)]
    scratch = []
    first_tiles = 0
    if len(xs) == 2:
        assert pre is None and tiles is None
        first_tiles = xs[0].shape[0] // tm
        last = xs[1].shape[0] // tm - 1
        mode, args = "plain2", list(xs)
        specs = [pl.BlockSpec((tm, D_MODEL), lambda i: (jnp.minimum(i, first_tiles - 1), 0)),
                 pl.BlockSpec((tm, D_MODEL), lambda i: (jnp.clip(i - first_tiles, 0, last), 0))]
    elif pre is None:
        mode, args, specs = "plain", [x], [row(D_MODEL)]
    elif len(pre) == 2:
        o, wo = pre
        wo = wo.astype(BF16)
        mode = "proj"
        args = [x, o, wo]
        o_spec = row(o.shape[1]) if o.ndim == 2 else pl.BlockSpec(
            (o.shape[0], tm, o.shape[2]), lambda i: (0, i + off, 0))
        specs = [row(D_MODEL), o_spec, _const_spec(wo.shape)]
    else:
        outs, lses, wo = pre
        wo = wo.astype(BF16)
        mode = "merge"
        args = [x, *outs, *lses, wo]
        width = wo.shape[0]
        dilated = [pl.BlockSpec((tm // d, d * width), lambda i: (i + off, 0)) for _, d in C_PATTERNS]
        specs = [row(D_MODEL)] + dilated * 2 + [_const_spec(wo.shape)]
        scratch = [pltpu.VMEM((width // LANES, tm, LANES), F32)] * 4
    return pl.pallas_call(
        functools.partial(_ffn_kernel, mode=mode, first_tiles=first_tiles),
        grid=(n_tiles,),
        in_specs=specs + ffn_specs,
        out_specs=pl.BlockSpec((tm, D_MODEL), lambda i: (i, 0)),
        out_shape=jax.ShapeDtypeStruct((n_tiles * tm, D_MODEL), F32),
        scratch_shapes=[pltpu.VMEM((tm, D_MODEL), F32)] + [pltpu.VMEM((tm, FF_CHUNK), BF16)] * 2
        + scratch,
        compiler_params=_params(("parallel",)),
        name="ffn_" + mode,
    )(*args, *ffn_args)


def _head_norm(t, width):
    return lax.rsqrt(jnp.sum(t * t, axis=-1, keepdims=True) * (1.0 / width) + NORM_EPS)


def _segment_ones(seg):
    n = MXU_TILE // seg
    return jnp.kron(jnp.eye(n, dtype=F32), jnp.ones((seg, seg), F32)).astype(BF16)


def _segment_rsqrt(t, e_ref, width):
    sq = t * t
    hi = sq.astype(BF16)
    lo = (sq - hi.astype(F32)).astype(BF16)
    e = e_ref[...]
    sums = [_dot(hi[:, b:b + MXU_TILE], e) + _dot(lo[:, b:b + MXU_TILE], e)
            for b in range(0, t.shape[-1], MXU_TILE)]
    ssq = sums[0] if len(sums) == 1 else jnp.concatenate(sums, axis=-1)
    return lax.rsqrt(ssq * (1.0 / width) + NORM_EPS)


def _store_vt(vt_ref, n, v_t):
    dv, rows = v_t.shape
    vt_ref[n, 0, :dv, :] = v_t.astype(BF16)
    first = lax.broadcasted_iota(jnp.int32, (BF16_SUBLANES, rows), 0) == 0
    vt_ref[n, 0, dv:, :] = jnp.where(first, 1.0, 0.0).astype(BF16)


def _proj_a_kernel(x_ref, gn_ref, w_ref, gain_ref, e_ref, q_ref, k_ref, v_ref):
    hd = HEAD_DIM
    nqk = (A_HEADS + A_KV_HEADS) * hd
    h = _rms(x_ref[...], gn_ref[...]).astype(BF16)
    y = _dot(h, w_ref[...])
    qk = y[:, :nqk]
    qk = (qk * _segment_rsqrt(qk, e_ref, hd) * gain_ref[...]).astype(BF16)
    for n in range(A_HEADS):
        q_ref[n] = qk[:, n * hd:(n + 1) * hd]
    for n in range(A_KV_HEADS):
        k_ref[n] = qk[:, (A_HEADS + n) * hd:(A_HEADS + n + 1) * hd]
        v_ref[n] = y[:, nqk + n * hd: nqk + (n + 1) * hd].astype(BF16)


def _proj_a_call(x, gn, wqkv, gq, gk):
    t_rows = x.shape[0]
    tm, hd = TOKEN_TILE, HEAD_DIM
    w = wqkv.astype(BF16)
    gain = jnp.concatenate([jnp.tile(gq * (hd ** -0.5 * LOG2E), A_HEADS), jnp.tile(gk, A_KV_HEADS)])
    ones = _segment_ones(hd)
    head_major = lambda n: pl.BlockSpec((n, tm, hd), lambda i: (0, i, 0))
    return pl.pallas_call(
        _proj_a_kernel,
        grid=(t_rows // tm,),
        in_specs=[pl.BlockSpec((tm, D_MODEL), lambda i: (i, 0)), _const_spec((1, D_MODEL)),
                  _const_spec(w.shape), _const_spec((1, gain.shape[0])), _const_spec(ones.shape)],
        out_specs=[head_major(A_HEADS), head_major(A_KV_HEADS), head_major(A_KV_HEADS)],
        out_shape=[jax.ShapeDtypeStruct((A_HEADS, t_rows, hd), BF16),
                   jax.ShapeDtypeStruct((A_KV_HEADS, t_rows, hd), BF16),
                   jax.ShapeDtypeStruct((A_KV_HEADS, t_rows, hd), BF16)],
        compiler_params=_params(("parallel",)),
        name="proj_a",
    )(x, gn.reshape(1, D_MODEL), w, gain.reshape(1, -1), ones)


def _proj_d_kernel(x_ref, gn_ref, w_ref, e_ref, cq_ref, sq_ref, ck_ref, sk_ref, qt_ref, k_ref, vt_ref):
    hd = HEAD_DIM
    nq, nk = D_HEADS * hd, D_KV_HEADS * hd
    tq = qt_ref.shape[-1]
    h = _rms(x_ref[...], gn_ref[...]).astype(BF16)
    y = _dot(h, w_ref[...])
    swapped = nq + 2 * nk
    t, ts = y[:, :nq + nk], y[:, swapped:swapped + nq + nk]
    r = _segment_rsqrt(t, e_ref, hd)
    for b in range((nq + nk) // LANES):
        is_q = b < nq // LANES
        c_ref, s_ref = (cq_ref, sq_ref) if is_q else (ck_ref, sk_ref)
        sl = slice(b * LANES, (b + 1) * LANES)
        blk = (t[:, sl] * c_ref[...] + ts[:, sl] * s_ref[...]) * r[:, sl]
        blk = blk.T.astype(BF16) if is_q else blk.astype(BF16)
        for half in range(LANES // hd):
            n = b * (LANES // hd) + half
            if is_q:
                for j in range(qt_ref.shape[1]):
                    qt_ref[n, j] = blk[half * hd:(half + 1) * hd, j * tq:(j + 1) * tq]
            else:
                k_ref[n - D_HEADS] = blk[:, half * hd:(half + 1) * hd]
    v_t = y[:, nq + nk: nq + 2 * nk].T
    for n in range(D_KV_HEADS):
        _store_vt(vt_ref, n, v_t[n * hd:(n + 1) * hd])


def _swap_halves_cols(w, group):
    k, n = w.shape
    return w.reshape(k, n // group, 2, group // 2)[:, :, ::-1, :].reshape(k, n)


def _position_spec(geo, tm, width):
    p_len, s_len, _ = geo
    p_tiles, s_tiles = p_len // tm, s_len // tm
    return pl.BlockSpec(
        (tm, width), lambda i: (jnp.where(i < p_tiles, i, (i - p_tiles) % s_tiles), 0))


def _rope_tables(pos, gain, scale):
    half = gain.shape[-1] // 2
    inv = ROPE_THETA ** (-jnp.arange(half, dtype=F32) / half)
    ang = pos.astype(F32)[:, None] * inv[None, :]
    cos, sin = lax.optimization_barrier((jnp.cos(ang), jnp.sin(ang)))
    g_sw = jnp.concatenate([gain[half:], gain[:half]])
    c = jnp.concatenate([cos, cos], axis=-1) * gain[None, :] * scale
    s = jnp.concatenate([-sin, sin], axis=-1) * g_sw[None, :] * scale
    return c, s


def _proj_d_call(x, gn, wqkv, gq, gk, pos, geo, tq):
    t_rows = x.shape[0]
    tm, hd = TOKEN_TILE, HEAD_DIM
    nq, nk = D_HEADS * hd, D_KV_HEADS * hd
    half = hd // 2
    w = jnp.concatenate([wqkv, _swap_halves_cols(wqkv[:, :nq + nk], half)], axis=1).astype(BF16)
    rows, cols = pos // GRID_W, pos % GRID_W

    def tables(g, scale):
        cr, sr = _rope_tables(rows, g[:half], scale)
        cc, sc = _rope_tables(cols, g[half:], scale)
        reps = LANES // hd
        return jnp.concatenate([cr, cc] * reps, axis=-1), jnp.concatenate([sr, sc] * reps, axis=-1)

    cq, sq = tables(gq, hd ** -0.5 * LOG2E)
    ck, sk = tables(gk, 1.0)
    ones = _segment_ones(hd)
    head_major = lambda n: pl.BlockSpec((n, tm, hd), lambda i: (0, i, 0))
    tab = _position_spec(geo, tm, LANES)
    return pl.pallas_call(
        _proj_d_kernel,
        grid=(t_rows // tm,),
        in_specs=[pl.BlockSpec((tm, D_MODEL), lambda i: (i, 0)), _const_spec((1, D_MODEL)),
                  _const_spec(w.shape), _const_spec(ones.shape), tab, tab, tab, tab],
        out_specs=[pl.BlockSpec((D_HEADS, tm // tq, hd, tq), lambda i: (0, i, 0, 0)),
                   head_major(D_KV_HEADS),
                   pl.BlockSpec((D_KV_HEADS, 1, hd + BF16_SUBLANES, tm), lambda i: (0, i, 0, 0))],
        out_shape=[jax.ShapeDtypeStruct((D_HEADS, t_rows // tq, hd, tq), BF16),
                   jax.ShapeDtypeStruct((D_KV_HEADS, t_rows, hd), BF16),
                   jax.ShapeDtypeStruct((D_KV_HEADS, t_rows // tm, hd + BF16_SUBLANES, tm), BF16)],
        compiler_params=_params(("parallel",)),
        name="proj_d",
    )(x, gn.reshape(1, D_MODEL), w, ones, cq, sq, ck, sk)


def _proj_b_kernel(x_ref, gn_ref, w1_ref, gcq_ref, gckv_ref, wq_ref, wqs_ref, wk_ref, wv_ref,
                   cq_ref, sq_ref, ck_ref, sk_ref, qt_ref, k_ref, vt_ref):
    h = _rms(x_ref[...], gn_ref[...]).astype(BF16)
    y1 = _dot(h, w1_ref[...])
    c_q = _rms(y1[:, :B_Q_LORA], gcq_ref[...]).astype(BF16)
    c_kv = _rms(y1[:, B_Q_LORA:B_Q_LORA + B_KV_LORA], gckv_ref[...]).astype(BF16)
    off = B_Q_LORA + B_KV_LORA
    kr = y1[:, off:off + B_PAD]
    krs = y1[:, off + B_PAD:off + 2 * B_PAD]
    q = _dot(c_q, wq_ref[...])
    qs = _dot(c_q, wqs_ref[...])
    kn = _dot(c_kv, wk_ref[...])
    v = _dot(c_kv, wv_ref[...])
    cq, sq, ck, sk = cq_ref[...], sq_ref[...], ck_ref[...], sk_ref[...]
    krs_s = krs * sk
    for n in range(B_HEADS):
        t = q[:, n * B_PAD:(n + 1) * B_PAD]
        ts = qs[:, n * B_PAD:(n + 1) * B_PAD]
        qt_ref[n, 0] = ((t * cq + ts * sq) * _head_norm(t, B_QK)).T.astype(BF16)
        t = kn[:, n * B_PAD:(n + 1) * B_PAD] + kr
        k_ref[n] = ((t * ck + krs_s) * _head_norm(t, B_QK)).astype(BF16)
    v_t = v.T
    for n in range(B_HEADS):
        _store_vt(vt_ref, n, v_t[n * B_V:(n + 1) * B_V])


def _pad_heads(w, lo, width):
    k, nh, _ = w.shape
    out = jnp.zeros((k, nh, B_PAD), w.dtype).at[:, :, lo:lo + width].set(w)
    return out.reshape(k, nh * B_PAD)


def _proj_b_call(x, gn, wdq, gcq, wuq, wdkv, gckv, wukv, gq, gk, pos, geo):
    t_rows = x.shape[0]
    tm = TOKEN_TILE
    wuq3 = wuq.reshape(B_Q_LORA, B_HEADS, B_QK)
    wq = _pad_heads(wuq3, 0, B_QK).astype(BF16)
    wuq_rope_sw = _swap_halves_cols(wuq3[:, :, B_NOPE:].reshape(B_Q_LORA, -1), B_ROPE)
    wqs = _pad_heads(wuq_rope_sw.reshape(B_Q_LORA, B_HEADS, B_ROPE), B_NOPE, B_ROPE).astype(BF16)
    wukv3 = wukv.reshape(B_KV_LORA, B_HEADS, B_NOPE + B_V)
    wk = _pad_heads(wukv3[:, :, :B_NOPE], 0, B_NOPE).astype(BF16)
    wv = wukv3[:, :, B_NOPE:].reshape(B_KV_LORA, B_HEADS * B_V).astype(BF16)
    w_kr = wdkv[:, B_KV_LORA:]
    kr_pad = _pad_heads(w_kr[:, None, :], B_NOPE, B_ROPE)
    krs_pad = _pad_heads(_swap_halves_cols(w_kr, B_ROPE)[:, None, :], B_NOPE, B_ROPE)
    w1 = jnp.concatenate([wdq, wdkv[:, :B_KV_LORA], kr_pad, krs_pad], axis=1).astype(BF16)

    def tables(g, scale):
        c, s = _rope_tables(pos, g[B_NOPE:], scale)
        n_pos = pos.shape[0]
        zeros = jnp.zeros((n_pos, B_PAD - B_QK), F32)
        c_full = jnp.concatenate(
            [jnp.broadcast_to(g[None, :B_NOPE] * scale, (n_pos, B_NOPE)), c, zeros], axis=-1)
        s_full = jnp.concatenate([jnp.zeros((n_pos, B_NOPE), F32), s, zeros], axis=-1)
        return c_full, s_full

    cq, sq = tables(gq, B_QK ** -0.5 * LOG2E)
    ck, sk = tables(gk, 1.0)
    tab = _position_spec(geo, tm, B_PAD)
    head_major = lambda w_: pl.BlockSpec((B_HEADS, tm, w_), lambda i: (0, i, 0))
    return pl.pallas_call(
        _proj_b_kernel,
        grid=(t_rows // tm,),
        in_specs=[pl.BlockSpec((tm, D_MODEL), lambda i: (i, 0)), _const_spec((1, D_MODEL)),
                  _const_spec(w1.shape), _const_spec((1, B_Q_LORA)), _const_spec((1, B_KV_LORA)),
                  _const_spec(wq.shape), _const_spec(wqs.shape), _const_spec(wk.shape),
                  _const_spec(wv.shape), tab, tab, tab, tab],
        out_specs=[pl.BlockSpec((B_HEADS, 1, B_PAD, tm), lambda i: (0, i, 0, 0)),
                   head_major(B_PAD),
                   pl.BlockSpec((B_HEADS, 1, B_V + BF16_SUBLANES, tm), lambda i: (0, i, 0, 0))],
        out_shape=[jax.ShapeDtypeStruct((B_HEADS, t_rows // tm, B_PAD, tm), BF16),
                   jax.ShapeDtypeStruct((B_HEADS, t_rows, B_PAD), BF16),
                   jax.ShapeDtypeStruct((B_HEADS, t_rows // tm, B_V + BF16_SUBLANES, tm), BF16)],
        compiler_params=_params(("parallel",)),
        name="proj_b",
    )(x, gn.reshape(1, D_MODEL), w1, gcq.reshape(1, -1), gckv.reshape(1, -1), wq, wqs, wk, wv,
      cq, sq, ck, sk)


def _proj_c_kernel(x_ref, gn_ref, w_ref, gq_ref, gk_ref, e_ref, *refs):
    hd = HEAD_DIM
    width = C_HEADS * hd
    outs, scr_ref = refs[:-1], refs[-1]
    tm = x_ref.shape[0]
    h = _rms(x_ref[...], gn_ref[...]).astype(BF16)
    y = _dot(h, w_ref[...])
    gains = (gq_ref[...], gk_ref[...])
    for g, (_, dil) in enumerate(C_PATTERNS):
        for j in range(3):
            base = (g * 3 + j) * width
            val = y[:, base: base + width]
            if j < 2:
                val = val * _segment_rsqrt(val, e_ref, hd) * gains[j]
            out_ref = outs[g * 3 + j]
            if dil == 1:
                out_ref[...] = val.astype(BF16)
            else:
                for s in range(width // LANES):
                    scr_ref[s] = val[:, s * LANES:(s + 1) * LANES]
                for r in range(dil):
                    for s in range(width // LANES):
                        rows = scr_ref[s, pl.ds(r, tm // dil, stride=dil), :]
                        col = r * width + s * LANES
                        out_ref[:, col:col + LANES] = rows.astype(BF16)


def _proj_c_call(x, gn, wqkv, gq, gk):
    t_rows = x.shape[0]
    tm, hd = TOKEN_TILE // 2, HEAD_DIM
    width = C_HEADS * hd
    w = wqkv.astype(BF16)
    ones = _segment_ones(hd)
    specs, shapes = [], []
    for _, dil in C_PATTERNS:
        specs += [pl.BlockSpec((tm // dil, dil * width), lambda i: (i, 0))] * 3
        shapes += [jax.ShapeDtypeStruct((t_rows // dil, dil * width), BF16)] * 3
    outs = pl.pallas_call(
        _proj_c_kernel,
        grid=(t_rows // tm,),
        in_specs=[pl.BlockSpec((tm, D_MODEL), lambda i: (i, 0)), _const_spec((1, D_MODEL)),
                  _const_spec(w.shape), _const_spec((1, width)), _const_spec((1, width)),
                  _const_spec(ones.shape)],
        out_specs=specs,
        out_shape=shapes,
        scratch_shapes=[pltpu.VMEM((width // LANES, tm, LANES), F32)],
        compiler_params=_params(("parallel",)),
        name="proj_c",
    )(x, gn.reshape(1, D_MODEL), w, jnp.tile(gq * (hd ** -0.5 * LOG2E), C_HEADS).reshape(1, width),
      jnp.tile(gk, C_HEADS).reshape(1, width), ones)
    return [outs[3 * g:3 * g + 3] for g in range(len(C_PATTERNS))]


def _band_masked_dist(row0, lo, hi, sub, w):
    nk = sub + 2 * w
    r = lax.broadcasted_iota(jnp.int32, (sub, nk), 0)
    c = lax.broadcasted_iota(jnp.int32, (sub, nk), 1)
    dist = jnp.abs(r + w - c)
    jpos = row0 - w + lax.broadcasted_iota(jnp.int32, (1, nk), 1)
    outside = jnp.where(jpos < lo, w + 1, 0) + jnp.where(jpos >= hi, w + 1, 0)
    return jnp.where(dist + outside <= w, dist.astype(F32), MASK_DIST)


def _two_stage(n_items, first, second, ahead=2):
    pending = {i: first(i) for i in range(min(ahead, n_items))}
    results = []
    for i in range(n_items):
        if i + ahead < n_items:
            pending[i + ahead] = first(i + ahead)
        results.append(second(i, pending.pop(i)))
    return results


def _with_ones(vv):
    return jnp.concatenate([vv, jnp.ones((vv.shape[0], BF16_SUBLANES), BF16)], axis=-1)


def _band_softmax_pv(s, vv_ones, sink):
    m = jnp.max(s, axis=-1, keepdims=True)
    if sink is not None:
        m = jnp.maximum(m, sink)
    hd = vv_ones.shape[-1] - BF16_SUBLANES
    pv = _dot(jnp.exp2(s - m).astype(BF16), vv_ones)
    den = pv[:, hd:hd + 1]
    if sink is not None:
        den = den + jnp.exp2(sink - m)
    return pv[:, :hd] / den, m + jnp.log2(den)


def _attn_a_kernel(slope_ref, sink_ref, q_ref, kp_ref, kc_ref, kn_ref, vp_ref, vc_ref, vn_ref,
                   o_ref, *, tq, p_len, s_len):
    w, sub = A_HALF_WINDOW, BAND_SUB
    group = A_HEADS // A_KV_HEADS
    kvh = pl.program_id(0)
    row0 = pl.program_id(1) * tq
    lo, hi = _seq_bounds(row0, p_len, s_len)
    kk = jnp.concatenate([kp_ref[0], kc_ref[0], kn_ref[0]], axis=0)
    vv = _with_ones(jnp.concatenate([vp_ref[0], vc_ref[0], vn_ref[0]], axis=0))
    sink = jnp.concatenate(
        [jnp.full((sub, 1), sink_ref[kvh * group + g], F32) for g in range(group)], axis=0)

    def keys(r):
        return slice(r * sub, (r + 1) * sub + 2 * w)

    def logits(r):
        md = _band_masked_dist(row0 + r * sub, lo, hi, sub, w)
        bias = jnp.concatenate([md * slope_ref[kvh * group + g] for g in range(group)], axis=0)
        q = q_ref[:, r * sub:(r + 1) * sub, :].reshape(group * sub, HEAD_DIM)
        return _dot_nt(q, kk[keys(r)]) + bias

    tiles = _two_stage(tq // sub, logits, lambda r, s: _band_softmax_pv(s, vv[keys(r)], sink)[0])
    outs = [[o[g * sub:(g + 1) * sub] for o in tiles] for g in range(group)]
    o_ref[...] = jnp.concatenate(
        [jnp.concatenate(rows, axis=0) for rows in outs], axis=-1).astype(BF16)


def _attn_a_call(q, k, v, sink, geo):
    p_len, s_len, t_rows = geo
    tq, w, hd = 512, A_HALF_WINDOW, HEAD_DIM
    group = A_HEADS // A_KV_HEADS
    per = tq // w
    last = t_rows // w - 1
    prev = pl.BlockSpec((1, w, hd), lambda h, i: (h, jnp.maximum(i * per - 1, 0), 0))
    cur = pl.BlockSpec((1, tq, hd), lambda h, i: (h, i, 0))
    nxt = pl.BlockSpec((1, w, hd), lambda h, i: (h, jnp.minimum((i + 1) * per, last), 0))
    smem = pl.BlockSpec(memory_space=pltpu.SMEM)
    slopes = jnp.asarray(-LOG2E * _alibi_slopes(A_HEADS), F32)
    return pl.pallas_call(
        functools.partial(_attn_a_kernel, tq=tq, p_len=p_len, s_len=s_len),
        grid=(A_KV_HEADS, t_rows // tq),
        in_specs=[smem, smem, pl.BlockSpec((group, tq, hd), lambda h, i: (h, i, 0)),
                  prev, cur, nxt, prev, cur, nxt],
        out_specs=pl.BlockSpec((tq, group * hd), lambda h, i: (i, h)),
        out_shape=jax.ShapeDtypeStruct((t_rows, A_HEADS * hd), BF16),
        compiler_params=_params(("parallel", "parallel")),
        name="attn_a",
    )(slopes, sink.astype(F32) * LOG2E, q, k, k, k, v, v, v)


def _attn_c_kernel(q_ref, kp_ref, kc_ref, kn_ref, vp_ref, vc_ref, vn_ref, o_ref, lse_ref,
                   *, tq, dil, p_len, s_len):
    w, hd, sub = C_STEPS, HEAD_DIM, min(BAND_SUB, tq)
    row0 = pl.program_id(0) * tq
    lo, hi = _seq_bounds(row0, p_len // dil, s_len // dil)
    slopes = -LOG2E * dil * _alibi_slopes(C_HEADS)
    mds = [_band_masked_dist(row0 + r * sub, lo, hi, sub, w) for r in range(tq // sub)]
    n_sub = tq // sub

    heads = {}

    def head_rows(n):
        if n not in heads:
            hs = slice(n * hd, (n + 1) * hd)
            kk = jnp.concatenate([kp_ref[:, hs], kc_ref[:, hs], kn_ref[:, hs]], axis=0)
            vv = jnp.concatenate([vp_ref[:, hs], vc_ref[:, hs], vn_ref[:, hs]], axis=0)
            heads[n] = (q_ref[:, hs], kk, _with_ones(vv))
        return heads[n]

    def logits(i):
        n, r = divmod(i, n_sub)
        q, kk, _ = head_rows(n)
        keys = kk[r * sub:(r + 1) * sub + 2 * w]
        return _dot_nt(q[r * sub:(r + 1) * sub], keys) + mds[r] * float(slopes[n])

    def attend(i, s):
        n, r = divmod(i, n_sub)
        o, lse2 = _band_softmax_pv(s, head_rows(n)[2][r * sub:(r + 1) * sub + 2 * w], None)
        return o, jnp.broadcast_to(lse2 * (1.0 / LOG2E), (sub, hd))

    tiles = _two_stage(C_HEADS * n_sub, logits, attend)
    outs = [jnp.concatenate([t[0] for t in tiles[n * n_sub:(n + 1) * n_sub]], axis=0)
            for n in range(C_HEADS)]
    lses = [jnp.concatenate([t[1] for t in tiles[n * n_sub:(n + 1) * n_sub]], axis=0)
            for n in range(C_HEADS)]
    o_ref[...] = jnp.concatenate(outs, axis=-1)
    lse_ref[...] = jnp.concatenate(lses, axis=-1)


def _attn_c_call(q, k, v, dil, geo):
    p_len, s_len, t_rows = geo
    w, width = C_STEPS, C_HEADS * HEAD_DIM
    rows = t_rows // dil
    tq = min(512, s_len // dil)
    per = tq // w
    last = rows // w - 1
    prev = pl.BlockSpec((w, width), lambda i, r: (jnp.maximum(i * per - 1, 0), r))
    cur = pl.BlockSpec((tq, width), lambda i, r: (i, r))
    nxt = pl.BlockSpec((w, width), lambda i, r: (jnp.minimum((i + 1) * per, last), r))
    shape = jax.ShapeDtypeStruct((rows, dil * width), F32)
    return pl.pallas_call(
        functools.partial(_attn_c_kernel, tq=tq, dil=dil, p_len=p_len, s_len=s_len),
        grid=(rows // tq, dil),
        in_specs=[cur, prev, cur, nxt, prev, cur, nxt],
        out_specs=[cur, cur],
        out_shape=[shape, shape],
        compiler_params=_params(("parallel", "parallel")),
        name="attn_c_d%d" % dil,
    )(q, k, k, k, v, v, v)


def _attn_dense_kernel(qt_ref, k_ref, vt_ref, o_ref, sa_ref, sb_ref, sc_ref, sd_ref, acc_ref,
                       *, nq, tq, tk, p_len, s_len):
    dv = vt_ref.shape[-2] - BF16_SUBLANES
    width = nq * tq
    shift = int(math.log2(tk))

    def first_chunk(j):
        row0 = (pl.program_id(1) * Q_TILES_PER_STEP + j) * tq
        lo, hi = _seq_bounds(row0, p_len, s_len)
        return lax.shift_right_logical(lo, shift), lax.shift_right_logical(hi - lo, shift)

    def scores(j, c, st_ref):
        start = pl.multiple_of(c * tk, tk)
        qt = jnp.concatenate([qt_ref[g, j] for g in range(nq)], axis=-1)
        st = _dot(k_ref[0, pl.ds(start, tk), :], qt).astype(BF16)
        st_ref[...] = st
        return jnp.max(st, axis=0, keepdims=True).astype(F32)

    def update(c, st_ref, mc, m, acc):
        m_new = jnp.maximum(m, mc)
        alpha = jnp.exp2(m - m_new)
        pt = jnp.exp2(st_ref[...] - m_new.astype(BF16))
        acc_ref[...] = alpha * acc_ref[...] + _dot(vt_ref[0, c], pt)
        return m_new, acc

    def q_tile(j, mcs, start_next):
        first, n_chunks = first_chunk(j)

        def quad(i, carry):
            m, acc, mc_a, mc_b = carry
            c = first + 4 * i
            mc_c = scores(j, c + 2, sc_ref)
            m, acc = update(c, sa_ref, mc_a, m, acc)
            mc_d = scores(j, c + 3, sd_ref)
            m, acc = update(c + 1, sb_ref, mc_b, m, acc)
            mc_a = scores(j, c + 4, sa_ref)
            m, acc = update(c + 2, sc_ref, mc_c, m, acc)
            mc_b = scores(j, c + 5, sb_ref)
            m, acc = update(c + 3, sd_ref, mc_d, m, acc)
            return m, acc, mc_a, mc_b

        acc_ref[...] = jnp.zeros_like(acc_ref)
        carry = (jnp.full((1, width), NEG_INF, F32), 0, *mcs)
        n_body = lax.shift_right_logical(n_chunks, 2) - 1
        carry = lax.fori_loop(0, lax.shift_right_logical(n_body, 1),
                              lambda i, cr: quad(2 * i + 1, quad(2 * i, cr)), carry)
        carry = lax.fori_loop(0, n_body & 1, lambda _, cr: quad(n_body - 1, cr), carry)
        m, acc, mc_a, mc_b = carry
        c = first + n_chunks - 4
        mc_c = scores(j, c + 2, sc_ref)
        m, acc = update(c, sa_ref, mc_a, m, acc)
        mc_d = scores(j, c + 3, sd_ref)
        m, acc = update(c + 1, sb_ref, mc_b, m, acc)
        if start_next:
            first_next, _ = first_chunk(j + 1)
            mc_a = scores(j + 1, first_next, sa_ref)
        m, acc = update(c + 2, sc_ref, mc_c, m, acc)
        if start_next:
            mc_b = scores(j + 1, first_next + 1, sb_ref)
        m, acc = update(c + 3, sd_ref, mc_d, m, acc)
        o = acc_ref[:dv, :] / acc_ref[dv:dv + 1, :]
        rows = pl.ds(pl.multiple_of(j * tq, tq), tq)
        heads = [o[:, g * tq:(g + 1) * tq].T.astype(BF16) for g in range(nq)]
        if nq == 1:
            o_ref[0, rows, :] = heads[0]
        else:
            o_ref[rows, :] = jnp.concatenate(heads, axis=-1)
        return mc_a, mc_b

    first, _ = first_chunk(0)
    mcs = (scores(0, first, sa_ref), scores(0, first + 1, sb_ref))
    mcs = lax.fori_loop(0, Q_TILES_PER_STEP - 1, lambda j, mcs: q_tile(j, mcs, True), mcs)
    q_tile(Q_TILES_PER_STEP - 1, mcs, False)


def _attn_dense_call(qt, k, vt, geo):
    p_len, s_len, t_rows = geo
    hq, n_tiles, dq, tq = qt.shape
    hk, n_chunks, rows_v, tk = vt.shape
    dv = rows_v - BF16_SUBLANES
    nq, per = hq // hk, Q_TILES_PER_STEP
    assert s_len % (4 * tk) == 0 and p_len % (4 * tk) == 0 and n_tiles % per == 0
    return pl.pallas_call(
        functools.partial(_attn_dense_kernel, nq=nq, tq=tq, tk=tk, p_len=p_len, s_len=s_len),
        grid=(hk, n_tiles // per),
        in_specs=[pl.BlockSpec((nq, per, dq, tq), lambda h, i: (h, i, 0, 0)),
                  pl.BlockSpec((1, t_rows, dq), lambda h, i: (h, 0, 0)),
                  pl.BlockSpec((1, n_chunks, dv + BF16_SUBLANES, tk), lambda h, i: (h, 0, 0, 0))],
        out_specs=(pl.BlockSpec((1, per * tq, dv), lambda h, i: (h, i, 0)) if nq == 1
                   else pl.BlockSpec((per * tq, nq * dv), lambda h, i: (i, h))),
        out_shape=jax.ShapeDtypeStruct((hq, t_rows, dv) if nq == 1 else (t_rows, hq * dv), BF16),
        scratch_shapes=[pltpu.VMEM((tk, nq * tq), BF16)] * 4 + [pltpu.VMEM((rows_v, nq * tq), F32)],
        compiler_params=_params(("parallel", "parallel")),
        name="attn_dense_%d" % dq,
    )(qt, k, vt)


def _trunk(x, p, geo, pos):
    p_tiles = geo[0] // TOKEN_TILE
    all_tiles = geo[2] // TOKEN_TILE
    for i in range(DEPTH):
        mixer, j = i % 4, i // 4
        x = _ffn_call(x, p["norm_ffn1"][i], p["ffn1_wg"][i], p["ffn1_wu"][i], p["ffn1_wd"][i])
        gn = p["norm_mix"][i]
        if mixer == 0:
            q, k, v = _proj_a_call(x, gn, p["a_wqkv"][j], p["a_gq"][j], p["a_gk"][j])
            pre = (_attn_a_call(q, k, v, p["a_sink"][j], geo), p["a_wo"][j])
        elif mixer == 1:
            qt, k, vt = _proj_b_call(x, gn, p["b_wdq"][j], p["b_gcq"][j], p["b_wuq"][j], p["b_wdkv"][j],
                                     p["b_gckv"][j], p["b_wukv"][j], p["b_gq"][j], p["b_gk"][j], pos, geo)
            pre = (_attn_dense_call(qt, k, vt, geo), p["b_wo"][j])
        elif mixer == 2:
            qkv = _proj_c_call(x, gn, p["c_wqkv"][j], p["c_gq"][j], p["c_gk"][j])
            res = [_attn_c_call(*qkv[g], dil, geo) for g, (_, dil) in enumerate(C_PATTERNS)]
            pre = ([r[0] for r in res], [r[1] for r in res], p["c_wo"][j])
        else:
            qt, k, vt = _proj_d_call(x, gn, p["d_wqkv"][j], p["d_gq"][j], p["d_gk"][j], pos, geo, tq=128)
            pre = (_attn_dense_call(qt, k, vt, geo), p["d_wo"][j])
        ffn2 = (p["norm_ffn2"][i], p["ffn2_wg"][i], p["ffn2_wu"][i], p["ffn2_wd"][i])
        if i < DEPTH - 1:
            x = _ffn_call(x, *ffn2, pre=pre)
        else:
            x = (_ffn_call(x, *ffn2, pre=pre, tiles=(0, p_tiles)),
                 _ffn_call(x, *ffn2, pre=pre, tiles=(p_tiles, all_tiles - p_tiles)))
    return x


def kernel(x_prompt, x_sample, norm_ffn1, ffn1_wg, ffn1_wu, ffn1_wd, norm_mix, norm_ffn2, ffn2_wg, ffn2_wu, ffn2_wd, a_wqkv, a_gq, a_gk, a_sink, a_wo, b_wdq, b_gcq, b_wuq, b_wdkv, b_gckv, b_wukv, b_gq, b_gk, b_wo, c_wqkv, c_gq, c_gk, c_wo, d_wqkv, d_gq, d_gk, d_wo):
    p = dict(norm_ffn1=norm_ffn1, ffn1_wg=ffn1_wg, ffn1_wu=ffn1_wu, ffn1_wd=ffn1_wd,
             norm_mix=norm_mix, norm_ffn2=norm_ffn2, ffn2_wg=ffn2_wg, ffn2_wu=ffn2_wu,
             ffn2_wd=ffn2_wd, a_wqkv=a_wqkv, a_gq=a_gq, a_gk=a_gk, a_sink=a_sink, a_wo=a_wo,
             b_wdq=b_wdq, b_gcq=b_gcq, b_wuq=b_wuq, b_wdkv=b_wdkv, b_gckv=b_gckv,
             b_wukv=b_wukv, b_gq=b_gq, b_gk=b_gk, b_wo=b_wo, c_wqkv=c_wqkv, c_gq=c_gq,
             c_gk=c_gk, c_wo=c_wo, d_wqkv=d_wqkv, d_gq=d_gq, d_gk=d_gk, d_wo=d_wo)
    n_p, p_seq, _ = x_prompt.shape
    n_s, s_len, _ = x_sample.shape
    assert n_p == 1 and s_len & (s_len - 1) == 0 and p_seq % s_len == 0
    p_len = n_p * p_seq
    t_rows = p_len + n_s * s_len
    geo = (p_len, s_len, t_rows)
    pos = jnp.arange(p_len, dtype=jnp.int32)
    x = (x_prompt.reshape(p_len, D_MODEL), x_sample.reshape(n_s * s_len, D_MODEL))
    y_p, y_s = _trunk(x, p, geo, pos)
    return y_p.reshape(x_prompt.shape), y_s.reshape(x_sample.shape)
```

```python
import functools
import math

import numpy as np
import jax
import jax.numpy as jnp
from jax import lax
from jax.experimental import pallas as pl
from jax.experimental.pallas import tpu as pltpu

F32 = jnp.float32
BF16 = jnp.bfloat16

D_MODEL = 1024
DEPTH = 4
HEAD_DIM = 64
D_FF = 2816
NORM_EPS = 1e-6
NEG_INF = -1e30
GRID_W = 64
ROPE_THETA = 10000.0
LOG2E = math.log2(math.e)
LANES = 128
MXU_TILE = 256
BF16_SUBLANES = 16
BAND_SUB = 128
Q_TILES_PER_STEP = 8
MASK_DIST = 1e33

A_HEADS, A_KV_HEADS, A_HALF_WINDOW = 16, 4, 128
B_HEADS, B_Q_LORA, B_KV_LORA, B_NOPE, B_ROPE, B_V = 16, 512, 256, 64, 32, 64
B_QK = B_NOPE + B_ROPE
B_PAD = 128
C_HEADS = 8
C_PATTERNS = ((128, 1), (512, 4), (2048, 16))
C_STEPS = 64
D_HEADS, D_KV_HEADS = 16, 4

FF_CHUNK = 256
N_FF_CHUNKS = D_FF // FF_CHUNK
TOKEN_TILE = 512
VMEM_LIMIT = 56 * 1024 * 1024


def _alibi_slopes(n):
    return 2.0 ** (-8.0 * np.arange(1, n + 1) / n)


def _params(sem, vmem=VMEM_LIMIT):
    return pltpu.CompilerParams(dimension_semantics=sem, vmem_limit_bytes=vmem)


def _const_spec(shape):
    nd = len(shape)
    return pl.BlockSpec(tuple(shape), lambda *_: (0,) * nd, pipeline_mode=pl.Buffered(1))


def _seq_bounds(row0, p_len, s_len):
    in_prompt = row0 < p_len
    b = lax.shift_right_logical(jnp.maximum(row0 - p_len, 0), int(math.log2(s_len)))
    lo = jnp.where(in_prompt, 0, p_len + b * s_len)
    hi = jnp.where(in_prompt, p_len, lo + s_len)
    return lo, hi


def _rms(x, g):
    return x * lax.rsqrt(jnp.mean(x * x, axis=-1, keepdims=True) + NORM_EPS) * g


def _dot(a, b):
    return jnp.dot(a, b, preferred_element_type=F32)


def _dot_nt(a, b):
    return lax.dot_general(a, b, (((1,), (1,)), ((), ())), preferred_element_type=F32)


def _ffn_kernel(*refs, mode, first_tiles=0):
    if mode == "plain":
        x_ref, gn_ref, wg_ref, wu_ref, wd_ref, out_ref, acc_ref, act_a, act_b = refs
        x = x_ref[...]
    elif mode == "plain2":
        xa_ref, xb_ref, gn_ref, wg_ref, wu_ref, wd_ref, out_ref, acc_ref, act_a, act_b = refs
        x = jnp.where(pl.program_id(0) < first_tiles, xa_ref[...], xb_ref[...])
    elif mode == "proj":
        (x_ref, o_ref, wo_ref, gn_ref, wg_ref, wu_ref, wd_ref, out_ref, acc_ref, act_a,
         act_b) = refs
        if len(o_ref.shape) == 3:
            o = jnp.concatenate([o_ref[n] for n in range(o_ref.shape[0])], axis=-1)
        else:
            o = o_ref[...]
        x = x_ref[...] + _dot(o, wo_ref[...])
    else:
        (x_ref, o0, o1, o2, l0, l1, l2, wo_ref, gn_ref, wg_ref, wu_ref, wd_ref,
         out_ref, acc_ref, act_a, act_b, *scr) = refs
        tm, width = x_ref.shape[0], wo_ref.shape[0]

        def token_major(ref, dil, scr_ref):
            slabs = width // LANES
            for r in range(dil):
                for s in range(slabs):
                    col = r * width + s * LANES
                    scr_ref[s, pl.ds(r, tm // dil, stride=dil), :] = ref[:, col:col + LANES]
            return jnp.concatenate([scr_ref[s] for s in range(slabs)], axis=-1)

        dils = [d for _, d in C_PATTERNS]
        assert dils[0] == 1
        oa, la = o0[...], l0[...]
        ob, lb = token_major(o1, dils[1], scr[0]), token_major(l1, dils[1], scr[1])
        oc, lc = token_major(o2, dils[2], scr[2]), token_major(l2, dils[2], scr[3])
        m = jnp.maximum(jnp.maximum(la, lb), lc)
        ea, eb, ec = jnp.exp(la - m), jnp.exp(lb - m), jnp.exp(lc - m)
        o = (ea * oa + eb * ob + ec * oc) / (ea + eb + ec)
        x = x_ref[...] + _dot(o.astype(BF16), wo_ref[...])

    h = _rms(x, gn_ref[...]).astype(BF16)

    def gate_up(c, a_ref):
        cols = pl.ds(pl.multiple_of(c * FF_CHUNK, FF_CHUNK), FF_CHUNK)
        g = _dot(h, wg_ref[:, cols])
        u = _dot(h, wu_ref[:, cols])
        a_ref[...] = (g * jax.nn.sigmoid(g) * u).astype(BF16)

    def down(c, a_ref):
        acc_ref[...] += _dot(a_ref[...], wd_ref[c])

    assert N_FF_CHUNKS % 2 == 1
    gate_up(0, act_a)
    gate_up(1, act_b)
    acc_ref[...] = _dot(act_a[...], wd_ref[0])

    def pair(i, carry):
        c = 2 * i + 1
        gate_up(c + 1, act_a)
        down(c, act_b)
        gate_up(c + 2, act_b)
        down(c + 1, act_a)
        return carry

    lax.fori_loop(0, N_FF_CHUNKS // 2 - 1, pair, 0)
    gate_up(N_FF_CHUNKS - 1, act_a)
    down(N_FF_CHUNKS - 2, act_b)
    down(N_FF_CHUNKS - 1, act_a)
    out_ref[...] = x + 0.5 * acc_ref[...]


def _ffn_weights(wg, wu, wd):
    wg3 = wg.astype(BF16)
    wu3 = wu.astype(BF16)
    wd3 = wd.astype(BF16).reshape(N_FF_CHUNKS, FF_CHUNK, D_MODEL)
    return wg3, wu3, wd3


def _ffn_call(x, gn, wg, wu, wd, pre=None, tiles=None):
    tm = TOKEN_TILE
    xs = x if isinstance(x, tuple) else (x,)
    off, n_tiles = tiles if tiles is not None else (0, sum(a.shape[0] for a in xs) // tm)
    wg3, wu3, wd3 = _ffn_weights(wg, wu, wd)
    row = lambda w: pl.BlockSpec((tm, w), lambda i: (i + off, 0))
    ffn_args = [gn.reshape(1, D_MODEL), wg3, wu3, wd3]
    ffn_specs = [_const_spec((1, D_MODEL)), _const_spec(wg3.shape), _const_spec(wu3.shape),
                 _const_spec(wd3.shape)]
    scratch = []
    first_tiles = 0
    if len(xs) == 2:
        assert pre is None and tiles is None
        first_tiles = xs[0].shape[0] // tm
        last = xs[1].shape[0] // tm - 1
        mode, args = "plain2", list(xs)
        specs = [pl.BlockSpec((tm, D_MODEL), lambda i: (jnp.minimum(i, first_tiles - 1), 0)),
                 pl.BlockSpec((tm, D_MODEL), lambda i: (jnp.clip(i - first_tiles, 0, last), 0))]
    elif pre is None:
        mode, args, specs = "plain", [x], [row(D_MODEL)]
    elif len(pre) == 2:
        o, wo = pre
        wo = wo.astype(BF16)
        mode = "proj"
        args = [x, o, wo]
        o_spec = row(o.shape[1]) if o.ndim == 2 else pl.BlockSpec(
            (o.shape[0], tm, o.shape[2]), lambda i: (0, i + off, 0))
        specs = [row(D_MODEL), o_spec, _const_spec(wo.shape)]
    else:
        outs, lses, wo = pre
        wo = wo.astype(BF16)
        mode = "merge"
        args = [x, *outs, *lses, wo]
        width = wo.shape[0]
        dilated = [pl.BlockSpec((tm // d, d * width), lambda i: (i + off, 0)) for _, d in C_PATTERNS]
        specs = [row(D_MODEL)] + dilated * 2 + [_const_spec(wo.shape)]
        scratch = [pltpu.VMEM((width // LANES, tm, LANES), F32)] * 4
    return pl.pallas_call(
        functools.partial(_ffn_kernel, mode=mode, first_tiles=first_tiles),
        grid=(n_tiles,),
        in_specs=specs + ffn_specs,
        out_specs=pl.BlockSpec((tm, D_MODEL), lambda i: (i, 0)),
        out_shape=jax.ShapeDtypeStruct((n_tiles * tm, D_MODEL), F32),
        scratch_shapes=[pltpu.VMEM((tm, D_MODEL), F32)] + [pltpu.VMEM((tm, FF_CHUNK), BF16)] * 2
        + scratch,
        compiler_params=_params(("parallel",)),
        name="ffn_" + mode,
    )(*args, *ffn_args)


def _head_norm(t, width):
    return lax.rsqrt(jnp.sum(t * t, axis=-1, keepdims=True) * (1.0 / width) + NORM_EPS)


def _segment_ones(seg):
    n = MXU_TILE // seg
    return jnp.kron(jnp.eye(n, dtype=F32), jnp.ones((seg, seg), F32)).astype(BF16)


def _segment_rsqrt(t, e_ref, width):
    sq = t * t
    hi = sq.astype(BF16)
    lo = (sq - hi.astype(F32)).astype(BF16)
    e = e_ref[...]
    sums = [_dot(hi[:, b:b + MXU_TILE], e) + _dot(lo[:, b:b + MXU_TILE], e)
            for b in range(0, t.shape[-1], MXU_TILE)]
    ssq = sums[0] if len(sums) == 1 else jnp.concatenate(sums, axis=-1)
    return lax.rsqrt(ssq * (1.0 / width) + NORM_EPS)


def _store_vt(vt_ref, n, v_t):
    dv, rows = v_t.shape
    vt_ref[n, 0, :dv, :] = v_t.astype(BF16)
    first = lax.broadcasted_iota(jnp.int32, (BF16_SUBLANES, rows), 0) == 0
    vt_ref[n, 0, dv:, :] = jnp.where(first, 1.0, 0.0).astype(BF16)


def _proj_a_kernel(x_ref, gn_ref, w_ref, gain_ref, e_ref, q_ref, k_ref, v_ref):
    hd = HEAD_DIM
    nqk = (A_HEADS + A_KV_HEADS) * hd
    h = _rms(x_ref[...], gn_ref[...]).astype(BF16)
    y = _dot(h, w_ref[...])
    qk = y[:, :nqk]
    qk = (qk * _segment_rsqrt(qk, e_ref, hd) * gain_ref[...]).astype(BF16)
    for n in range(A_HEADS):
        q_ref[n] = qk[:, n * hd:(n + 1) * hd]
    for n in range(A_KV_HEADS):
        k_ref[n] = qk[:, (A_HEADS + n) * hd:(A_HEADS + n + 1) * hd]
        v_ref[n] = y[:, nqk + n * hd: nqk + (n + 1) * hd].astype(BF16)


def _proj_a_call(x, gn, wqkv, gq, gk):
    t_rows = x.shape[0]
    tm, hd = TOKEN_TILE, HEAD_DIM
    w = wqkv.astype(BF16)
    gain = jnp.concatenate([jnp.tile(gq * (hd ** -0.5 * LOG2E), A_HEADS), jnp.tile(gk, A_KV_HEADS)])
    ones = _segment_ones(hd)
    head_major = lambda n: pl.BlockSpec((n, tm, hd), lambda i: (0, i, 0))
    return pl.pallas_call(
        _proj_a_kernel,
        grid=(t_rows // tm,),
        in_specs=[pl.BlockSpec((tm, D_MODEL), lambda i: (i, 0)), _const_spec((1, D_MODEL)),
                  _const_spec(w.shape), _const_spec((1, gain.shape[0])), _const_spec(ones.shape)],
        out_specs=[head_major(A_HEADS), head_major(A_KV_HEADS), head_major(A_KV_HEADS)],
        out_shape=[jax.ShapeDtypeStruct((A_HEADS, t_rows, hd), BF16),
                   jax.ShapeDtypeStruct((A_KV_HEADS, t_rows, hd), BF16),
                   jax.ShapeDtypeStruct((A_KV_HEADS, t_rows, hd), BF16)],
        compiler_params=_params(("parallel",)),
        name="proj_a",
    )(x, gn.reshape(1, D_MODEL), w, gain.reshape(1, -1), ones)


def _proj_d_kernel(x_ref, gn_ref, w_ref, e_ref, cq_ref, sq_ref, ck_ref, sk_ref, qt_ref, k_ref, vt_ref):
    hd = HEAD_DIM
    nq, nk = D_HEADS * hd, D_KV_HEADS * hd
    tq = qt_ref.shape[-1]
    h = _rms(x_ref[...], gn_ref[...]).astype(BF16)
    y = _dot(h, w_ref[...])
    swapped = nq + 2 * nk
    t, ts = y[:, :nq + nk], y[:, swapped:swapped + nq + nk]
    r = _segment_rsqrt(t, e_ref, hd)
    for b in range((nq + nk) // LANES):
        is_q = b < nq // LANES
        c_ref, s_ref = (cq_ref, sq_ref) if is_q else (ck_ref, sk_ref)
        sl = slice(b * LANES, (b + 1) * LANES)
        blk = (t[:, sl] * c_ref[...] + ts[:, sl] * s_ref[...]) * r[:, sl]
        blk = blk.T.astype(BF16) if is_q else blk.astype(BF16)
        for half in range(LANES // hd):
            n = b * (LANES // hd) + half
            if is_q:
                for j in range(qt_ref.shape[1]):
                    qt_ref[n, j] = blk[half * hd:(half + 1) * hd, j * tq:(j + 1) * tq]
            else:
                k_ref[n - D_HEADS] = blk[:, half * hd:(half + 1) * hd]
    v_t = y[:, nq + nk: nq + 2 * nk].T
    for n in range(D_KV_HEADS):
        _store_vt(vt_ref, n, v_t[n * hd:(n + 1) * hd])


def _swap_halves_cols(w, group):
    k, n = w.shape
    return w.reshape(k, n // group, 2, group // 2)[:, :, ::-1, :].reshape(k, n)


def _position_spec(geo, tm, width):
    p_len, s_len, _ = geo
    p_tiles, s_tiles = p_len // tm, s_len // tm
    return pl.BlockSpec(
        (tm, width), lambda i: (jnp.where(i < p_tiles, i, (i - p_tiles) % s_tiles), 0))


def _rope_tables(pos, gain, scale):
    half = gain.shape[-1] // 2
    inv = ROPE_THETA ** (-jnp.arange(half, dtype=F32) / half)
    ang = pos.astype(F32)[:, None] * inv[None, :]
    cos, sin = lax.optimization_barrier((jnp.cos(ang), jnp.sin(ang)))
    g_sw = jnp.concatenate([gain[half:], gain[:half]])
    c = jnp.concatenate([cos, cos], axis=-1) * gain[None, :] * scale
    s = jnp.concatenate([-sin, sin], axis=-1) * g_sw[None, :] * scale
    return c, s


def _proj_d_call(x, gn, wqkv, gq, gk, pos, geo, tq):
    t_rows = x.shape[0]
    tm, hd = TOKEN_TILE, HEAD_DIM
    nq, nk = D_HEADS * hd, D_KV_HEADS * hd
    half = hd // 2
    w = jnp.concatenate([wqkv, _swap_halves_cols(wqkv[:, :nq + nk], half)], axis=1).astype(BF16)
    rows, cols = pos // GRID_W, pos % GRID_W

    def tables(g, scale):
        cr, sr = _rope_tables(rows, g[:half], scale)
        cc, sc = _rope_tables(cols, g[half:], scale)
        reps = LANES // hd
        return jnp.concatenate([cr, cc] * reps, axis=-1), jnp.concatenate([sr, sc] * reps, axis=-1)

    cq, sq = tables(gq, hd ** -0.5 * LOG2E)
    ck, sk = tables(gk, 1.0)
    ones = _segment_ones(hd)
    head_major = lambda n: pl.BlockSpec((n, tm, hd), lambda i: (0, i, 0))
    tab = _position_spec(geo, tm, LANES)
    return pl.pallas_call(
        _proj_d_kernel,
        grid=(t_rows // tm,),
        in_specs=[pl.BlockSpec((tm, D_MODEL), lambda i: (i, 0)), _const_spec((1, D_MODEL)),
                  _const_spec(w.shape), _const_spec(ones.shape), tab, tab, tab, tab],
        out_specs=[pl.BlockSpec((D_HEADS, tm // tq, hd, tq), lambda i: (0, i, 0, 0)),
                   head_major(D_KV_HEADS),
                   pl.BlockSpec((D_KV_HEADS, 1, hd + BF16_SUBLANES, tm), lambda i: (0, i, 0, 0))],
        out_shape=[jax.ShapeDtypeStruct((D_HEADS, t_rows // tq, hd, tq), BF16),
                   jax.ShapeDtypeStruct((D_KV_HEADS, t_rows, hd), BF16),
                   jax.ShapeDtypeStruct((D_KV_HEADS, t_rows // tm, hd + BF16_SUBLANES, tm), BF16)],
        compiler_params=_params(("parallel",)),
        name="proj_d",
    )(x, gn.reshape(1, D_MODEL), w, ones, cq, sq, ck, sk)


def _proj_b_kernel(x_ref, gn_ref, w1_ref, gcq_ref, gckv_ref, wq_ref, wqs_ref, wk_ref, wv_ref,
                   cq_ref, sq_ref, ck_ref, sk_ref, qt_ref, k_ref, vt_ref):
    h = _rms(x_ref[...], gn_ref[...]).astype(BF16)
    y1 = _dot(h, w1_ref[...])
    c_q = _rms(y1[:, :B_Q_LORA], gcq_ref[...]).astype(BF16)
    c_kv = _rms(y1[:, B_Q_LORA:B_Q_LORA + B_KV_LORA], gckv_ref[...]).astype(BF16)
    off = B_Q_LORA + B_KV_LORA
    kr = y1[:, off:off + B_PAD]
    krs = y1[:, off + B_PAD:off + 2 * B_PAD]
    q = _dot(c_q, wq_ref[...])
    qs = _dot(c_q, wqs_ref[...])
    kn = _dot(c_kv, wk_ref[...])
    v = _dot(c_kv, wv_ref[...])
    cq, sq, ck, sk = cq_ref[...], sq_ref[...], ck_ref[...], sk_ref[...]
    krs_s = krs * sk
    for n in range(B_HEADS):
        t = q[:, n * B_PAD:(n + 1) * B_PAD]
        ts = qs[:, n * B_PAD:(n + 1) * B_PAD]
        qt_ref[n, 0] = ((t * cq + ts * sq) * _head_norm(t, B_QK)).T.astype(BF16)
        t = kn[:, n * B_PAD:(n + 1) * B_PAD] + kr
        k_ref[n] = ((t * ck + krs_s) * _head_norm(t, B_QK)).astype(BF16)
    v_t = v.T
    for n in range(B_HEADS):
        _store_vt(vt_ref, n, v_t[n * B_V:(n + 1) * B_V])


def _pad_heads(w, lo, width):
    k, nh, _ = w.shape
    out = jnp.zeros((k, nh, B_PAD), w.dtype).at[:, :, lo:lo + width].set(w)
    return out.reshape(k, nh * B_PAD)


def _proj_b_call(x, gn, wdq, gcq, wuq, wdkv, gckv, wukv, gq, gk, pos, geo):
    t_rows = x.shape[0]
    tm = TOKEN_TILE
    wuq3 = wuq.reshape(B_Q_LORA, B_HEADS, B_QK)
    wq = _pad_heads(wuq3, 0, B_QK).astype(BF16)
    wuq_rope_sw = _swap_halves_cols(wuq3[:, :, B_NOPE:].reshape(B_Q_LORA, -1), B_ROPE)
    wqs = _pad_heads(wuq_rope_sw.reshape(B_Q_LORA, B_HEADS, B_ROPE), B_NOPE, B_ROPE).astype(BF16)
    wukv3 = wukv.reshape(B_KV_LORA, B_HEADS, B_NOPE + B_V)
    wk = _pad_heads(wukv3[:, :, :B_NOPE], 0, B_NOPE).astype(BF16)
    wv = wukv3[:, :, B_NOPE:].reshape(B_KV_LORA, B_HEADS * B_V).astype(BF16)
    w_kr = wdkv[:, B_KV_LORA:]
    kr_pad = _pad_heads(w_kr[:, None, :], B_NOPE, B_ROPE)
    krs_pad = _pad_heads(_swap_halves_cols(w_kr, B_ROPE)[:, None, :], B_NOPE, B_ROPE)
    w1 = jnp.concatenate([wdq, wdkv[:, :B_KV_LORA], kr_pad, krs_pad], axis=1).astype(BF16)

    def tables(g, scale):
        c, s = _rope_tables(pos, g[B_NOPE:], scale)
        n_pos = pos.shape[0]
        zeros = jnp.zeros((n_pos, B_PAD - B_QK), F32)
        c_full = jnp.concatenate(
            [jnp.broadcast_to(g[None, :B_NOPE] * scale, (n_pos, B_NOPE)), c, zeros], axis=-1)
        s_full = jnp.concatenate([jnp.zeros((n_pos, B_NOPE), F32), s, zeros], axis=-1)
        return c_full, s_full

    cq, sq = tables(gq, B_QK ** -0.5 * LOG2E)
    ck, sk = tables(gk, 1.0)
    tab = _position_spec(geo, tm, B_PAD)
    head_major = lambda w_: pl.BlockSpec((B_HEADS, tm, w_), lambda i: (0, i, 0))
    return pl.pallas_call(
        _proj_b_kernel,
        grid=(t_rows // tm,),
        in_specs=[pl.BlockSpec((tm, D_MODEL), lambda i: (i, 0)), _const_spec((1, D_MODEL)),
                  _const_spec(w1.shape), _const_spec((1, B_Q_LORA)), _const_spec((1, B_KV_LORA)),
                  _const_spec(wq.shape), _const_spec(wqs.shape), _const_spec(wk.shape),
                  _const_spec(wv.shape), tab, tab, tab, tab],
        out_specs=[pl.BlockSpec((B_HEADS, 1, B_PAD, tm), lambda i: (0, i, 0, 0)),
                   head_major(B_PAD),
                   pl.BlockSpec((B_HEADS, 1, B_V + BF16_SUBLANES, tm), lambda i: (0, i, 0, 0))],
        out_shape=[jax.ShapeDtypeStruct((B_HEADS, t_rows // tm, B_PAD, tm), BF16),
                   jax.ShapeDtypeStruct((B_HEADS, t_rows, B_PAD), BF16),
                   jax.ShapeDtypeStruct((B_HEADS, t_rows // tm, B_V + BF16_SUBLANES, tm), BF16)],
        compiler_params=_params(("parallel",)),
        name="proj_b",
    )(x, gn.reshape(1, D_MODEL), w1, gcq.reshape(1, -1), gckv.reshape(1, -1), wq, wqs, wk, wv,
      cq, sq, ck, sk)


def _proj_c_kernel(x_ref, gn_ref, w_ref, gq_ref, gk_ref, e_ref, *refs):
    hd = HEAD_DIM
    width = C_HEADS * hd
    outs, scr_ref = refs[:-1], refs[-1]
    tm = x_ref.shape[0]
    h = _rms(x_ref[...], gn_ref[...]).astype(BF16)
    y = _dot(h, w_ref[...])
    gains = (gq_ref[...], gk_ref[...])
    for g, (_, dil) in enumerate(C_PATTERNS):
        for j in range(3):
            base = (g * 3 + j) * width
            val = y[:, base: base + width]
            if j < 2:
                val = val * _segment_rsqrt(val, e_ref, hd) * gains[j]
            out_ref = outs[g * 3 + j]
            if dil == 1:
                out_ref[...] = val.astype(BF16)
            else:
                for s in range(width // LANES):
                    scr_ref[s] = val[:, s * LANES:(s + 1) * LANES]
                for r in range(dil):
                    for s in range(width // LANES):
                        rows = scr_ref[s, pl.ds(r, tm // dil, stride=dil), :]
                        col = r * width + s * LANES
                        out_ref[:, col:col + LANES] = rows.astype(BF16)


def _proj_c_call(x, gn, wqkv, gq, gk):
    t_rows = x.shape[0]
    tm, hd = TOKEN_TILE, HEAD_DIM
    width = C_HEADS * hd
    w = wqkv.astype(BF16)
    ones = _segment_ones(hd)
    specs, shapes = [], []
    for _, dil in C_PATTERNS:
        specs += [pl.BlockSpec((tm // dil, dil * width), lambda i: (i, 0))] * 3
        shapes += [jax.ShapeDtypeStruct((t_rows // dil, dil * width), BF16)] * 3
    outs = pl.pallas_call(
        _proj_c_kernel,
        grid=(t_rows // tm,),
        in_specs=[pl.BlockSpec((tm, D_MODEL), lambda i: (i, 0)), _const_spec((1, D_MODEL)),
                  _const_spec(w.shape), _const_spec((1, width)), _const_spec((1, width)),
                  _const_spec(ones.shape)],
        out_specs=specs,
        out_shape=shapes,
        scratch_shapes=[pltpu.VMEM((width // LANES, tm, LANES), F32)],
        compiler_params=_params(("parallel",)),
        name="proj_c",
    )(x, gn.reshape(1, D_MODEL), w, jnp.tile(gq * (hd ** -0.5 * LOG2E), C_HEADS).reshape(1, width),
      jnp.tile(gk, C_HEADS).reshape(1, width), ones)
    return [outs[3 * g:3 * g + 3] for g in range(len(C_PATTERNS))]


def _band_masked_dist(row0, lo, hi, sub, w):
    nk = sub + 2 * w
    r = lax.broadcasted_iota(jnp.int32, (sub, nk), 0)
    c = lax.broadcasted_iota(jnp.int32, (sub, nk), 1)
    dist = jnp.abs(r + w - c)
    jpos = row0 - w + lax.broadcasted_iota(jnp.int32, (1, nk), 1)
    outside = jnp.where(jpos < lo, w + 1, 0) + jnp.where(jpos >= hi, w + 1, 0)
    return jnp.where(dist + outside <= w, dist.astype(F32), MASK_DIST)


def _two_stage(n_items, first, second, ahead=2):
    pending = {i: first(i) for i in range(min(ahead, n_items))}
    results = []
    for i in range(n_items):
        if i + ahead < n_items:
            pending[i + ahead] = first(i + ahead)
        results.append(second(i, pending.pop(i)))
    return results


def _with_ones(vv):
    return jnp.concatenate([vv, jnp.ones((vv.shape[0], BF16_SUBLANES), BF16)], axis=-1)


def _band_softmax_pv(s, vv_ones, sink):
    m = jnp.max(s, axis=-1, keepdims=True)
    if sink is not None:
        m = jnp.maximum(m, sink)
    hd = vv_ones.shape[-1] - BF16_SUBLANES
    pv = _dot(jnp.exp2(s - m).astype(BF16), vv_ones)
    den = pv[:, hd:hd + 1]
    if sink is not None:
        den = den + jnp.exp2(sink - m)
    return pv[:, :hd] / den, m + jnp.log2(den)


def _attn_a_kernel(slope_ref, sink_ref, q_ref, kp_ref, kc_ref, kn_ref, vp_ref, vc_ref, vn_ref,
                   o_ref, *, tq, p_len, s_len):
    w, sub = A_HALF_WINDOW, BAND_SUB
    group = A_HEADS // A_KV_HEADS
    kvh = pl.program_id(0)
    row0 = pl.program_id(1) * tq
    lo, hi = _seq_bounds(row0, p_len, s_len)
    kk = jnp.concatenate([kp_ref[0], kc_ref[0], kn_ref[0]], axis=0)
    vv = _with_ones(jnp.concatenate([vp_ref[0], vc_ref[0], vn_ref[0]], axis=0))
    sink = jnp.concatenate(
        [jnp.full((sub, 1), sink_ref[kvh * group + g], F32) for g in range(group)], axis=0)

    def keys(r):
        return slice(r * sub, (r + 1) * sub + 2 * w)

    def logits(r):
        md = _band_masked_dist(row0 + r * sub, lo, hi, sub, w)
        bias = jnp.concatenate([md * slope_ref[kvh * group + g] for g in range(group)], axis=0)
        q = q_ref[:, r * sub:(r + 1) * sub, :].reshape(group * sub, HEAD_DIM)
        return _dot_nt(q, kk[keys(r)]) + bias

    tiles = _two_stage(tq // sub, logits, lambda r, s: _band_softmax_pv(s, vv[keys(r)], sink)[0])
    outs = [[o[g * sub:(g + 1) * sub] for o in tiles] for g in range(group)]
    o_ref[...] = jnp.concatenate(
        [jnp.concatenate(rows, axis=0) for rows in outs], axis=-1).astype(BF16)


def _attn_a_call(q, k, v, sink, geo):
    p_len, s_len, t_rows = geo
    tq, w, hd = 512, A_HALF_WINDOW, HEAD_DIM
    group = A_HEADS // A_KV_HEADS
    per = tq // w
    last = t_rows // w - 1
    prev = pl.BlockSpec((1, w, hd), lambda h, i: (h, jnp.maximum(i * per - 1, 0), 0))
    cur = pl.BlockSpec((1, tq, hd), lambda h, i: (h, i, 0))
    nxt = pl.BlockSpec((1, w, hd), lambda h, i: (h, jnp.minimum((i + 1) * per, last), 0))
    smem = pl.BlockSpec(memory_space=pltpu.SMEM)
    slopes = jnp.asarray(-LOG2E * _alibi_slopes(A_HEADS), F32)
    return pl.pallas_call(
        functools.partial(_attn_a_kernel, tq=tq, p_len=p_len, s_len=s_len),
        grid=(A_KV_HEADS, t_rows // tq),
        in_specs=[smem, smem, pl.BlockSpec((group, tq, hd), lambda h, i: (h, i, 0)),
                  prev, cur, nxt, prev, cur, nxt],
        out_specs=pl.BlockSpec((tq, group * hd), lambda h, i: (i, h)),
        out_shape=jax.ShapeDtypeStruct((t_rows, A_HEADS * hd), BF16),
        compiler_params=_params(("parallel", "parallel")),
        name="attn_a",
    )(slopes, sink.astype(F32) * LOG2E, q, k, k, k, v, v, v)


def _attn_c_kernel(q_ref, kp_ref, kc_ref, kn_ref, vp_ref, vc_ref, vn_ref, o_ref, lse_ref,
                   *, tq, dil, p_len, s_len):
    w, hd, sub = C_STEPS, HEAD_DIM, min(BAND_SUB, tq)
    row0 = pl.program_id(0) * tq
    lo, hi = _seq_bounds(row0, p_len // dil, s_len // dil)
    slopes = -LOG2E * dil * _alibi_slopes(C_HEADS)
    mds = [_band_masked_dist(row0 + r * sub, lo, hi, sub, w) for r in range(tq // sub)]
    n_sub = tq // sub

    heads = {}

    def head_rows(n):
        if n not in heads:
            hs = slice(n * hd, (n + 1) * hd)
            kk = jnp.concatenate([kp_ref[:, hs], kc_ref[:, hs], kn_ref[:, hs]], axis=0)
            vv = jnp.concatenate([vp_ref[:, hs], vc_ref[:, hs], vn_ref[:, hs]], axis=0)
            heads[n] = (q_ref[:, hs], kk, _with_ones(vv))
        return heads[n]

    def logits(i):
        n, r = divmod(i, n_sub)
        q, kk, _ = head_rows(n)
        keys = kk[r * sub:(r + 1) * sub + 2 * w]
        return _dot_nt(q[r * sub:(r + 1) * sub], keys) + mds[r] * float(slopes[n])

    def attend(i, s):
        n, r = divmod(i, n_sub)
        o, lse2 = _band_softmax_pv(s, head_rows(n)[2][r * sub:(r + 1) * sub + 2 * w], None)
        return o, jnp.broadcast_to(lse2 * (1.0 / LOG2E), (sub, hd))

    tiles = _two_stage(C_HEADS * n_sub, logits, attend)
    outs = [jnp.concatenate([t[0] for t in tiles[n * n_sub:(n + 1) * n_sub]], axis=0)
            for n in range(C_HEADS)]
    lses = [jnp.concatenate([t[1] for t in tiles[n * n_sub:(n + 1) * n_sub]], axis=0)
            for n in range(C_HEADS)]
    o_ref[...] = jnp.concatenate(outs, axis=-1)
    lse_ref[...] = jnp.concatenate(lses, axis=-1)


def _attn_c_call(q, k, v, dil, geo):
    p_len, s_len, t_rows = geo
    w, width = C_STEPS, C_HEADS * HEAD_DIM
    rows = t_rows // dil
    tq = min(512, s_len // dil)
    per = tq // w
    last = rows // w - 1
    prev = pl.BlockSpec((w, width), lambda i, r: (jnp.maximum(i * per - 1, 0), r))
    cur = pl.BlockSpec((tq, width), lambda i, r: (i, r))
    nxt = pl.BlockSpec((w, width), lambda i, r: (jnp.minimum((i + 1) * per, last), r))
    shape = jax.ShapeDtypeStruct((rows, dil * width), F32)
    return pl.pallas_call(
        functools.partial(_attn_c_kernel, tq=tq, dil=dil, p_len=p_len, s_len=s_len),
        grid=(rows // tq, dil),
        in_specs=[cur, prev, cur, nxt, prev, cur, nxt],
        out_specs=[cur, cur],
        out_shape=[shape, shape],
        compiler_params=_params(("parallel", "parallel")),
        name="attn_c_d%d" % dil,
    )(q, k, k, k, v, v, v)


def _attn_dense_kernel(qt_ref, k_ref, vt_ref, o_ref, sa_ref, sb_ref, sc_ref, sd_ref, acc_ref,
                       *, nq, tq, tk, p_len, s_len):
    dv = vt_ref.shape[-2] - BF16_SUBLANES
    width = nq * tq
    shift = int(math.log2(tk))

    def first_chunk(j):
        row0 = (pl.program_id(1) * Q_TILES_PER_STEP + j) * tq
        lo, hi = _seq_bounds(row0, p_len, s_len)
        return lax.shift_right_logical(lo, shift), lax.shift_right_logical(hi - lo, shift)

    def scores(j, c, st_ref):
        start = pl.multiple_of(c * tk, tk)
        qt = jnp.concatenate([qt_ref[g, j] for g in range(nq)], axis=-1)
        st = _dot(k_ref[0, pl.ds(start, tk), :], qt).astype(BF16)
        st_ref[...] = st
        return jnp.max(st, axis=0, keepdims=True).astype(F32)

    def update(c, st_ref, mc, m, acc):
        m_new = jnp.maximum(m, mc)
        alpha = jnp.exp2(m - m_new)
        pt = jnp.exp2(st_ref[...] - m_new.astype(BF16))
        acc_ref[...] = alpha * acc_ref[...] + _dot(vt_ref[0, c], pt)
        return m_new, acc

    def q_tile(j, mcs, start_next):
        first, n_chunks = first_chunk(j)

        def quad(i, carry):
            m, acc, mc_a, mc_b = carry
            c = first + 4 * i
            mc_c = scores(j, c + 2, sc_ref)
            m, acc = update(c, sa_ref, mc_a, m, acc)
            mc_d = scores(j, c + 3, sd_ref)
            m, acc = update(c + 1, sb_ref, mc_b, m, acc)
            mc_a = scores(j, c + 4, sa_ref)
            m, acc = update(c + 2, sc_ref, mc_c, m, acc)
            mc_b = scores(j, c + 5, sb_ref)
            m, acc = update(c + 3, sd_ref, mc_d, m, acc)
            return m, acc, mc_a, mc_b

        acc_ref[...] = jnp.zeros_like(acc_ref)
        carry = (jnp.full((1, width), NEG_INF, F32), 0, *mcs)
        n_body = lax.shift_right_logical(n_chunks, 2) - 1
        carry = lax.fori_loop(0, lax.shift_right_logical(n_body, 1),
                              lambda i, cr: quad(2 * i + 1, quad(2 * i, cr)), carry)
        carry = lax.fori_loop(0, n_body & 1, lambda _, cr: quad(n_body - 1, cr), carry)
        m, acc, mc_a, mc_b = carry
        c = first + n_chunks - 4
        mc_c = scores(j, c + 2, sc_ref)
        m, acc = update(c, sa_ref, mc_a, m, acc)
        mc_d = scores(j, c + 3, sd_ref)
        m, acc = update(c + 1, sb_ref, mc_b, m, acc)
        if start_next:
            first_next, _ = first_chunk(j + 1)
            mc_a = scores(j + 1, first_next, sa_ref)
        m, acc = update(c + 2, sc_ref, mc_c, m, acc)
        if start_next:
            mc_b = scores(j + 1, first_next + 1, sb_ref)
        m, acc = update(c + 3, sd_ref, mc_d, m, acc)
        o = acc_ref[:dv, :] / acc_ref[dv:dv + 1, :]
        rows = pl.ds(pl.multiple_of(j * tq, tq), tq)
        heads = [o[:, g * tq:(g + 1) * tq].T.astype(BF16) for g in range(nq)]
        if nq == 1:
            o_ref[0, rows, :] = heads[0]
        else:
            o_ref[rows, :] = jnp.concatenate(heads, axis=-1)
        return mc_a, mc_b

    first, _ = first_chunk(0)
    mcs = (scores(0, first, sa_ref), scores(0, first + 1, sb_ref))
    mcs = lax.fori_loop(0, Q_TILES_PER_STEP - 1, lambda j, mcs: q_tile(j, mcs, True), mcs)
    q_tile(Q_TILES_PER_STEP - 1, mcs, False)


def _attn_dense_call(qt, k, vt, geo):
    p_len, s_len, t_rows = geo
    hq, n_tiles, dq, tq = qt.shape
    hk, n_chunks, rows_v, tk = vt.shape
    dv = rows_v - BF16_SUBLANES
    nq, per = hq // hk, Q_TILES_PER_STEP
    assert s_len % (4 * tk) == 0 and p_len % (4 * tk) == 0 and n_tiles % per == 0
    return pl.pallas_call(
        functools.partial(_attn_dense_kernel, nq=nq, tq=tq, tk=tk, p_len=p_len, s_len=s_len),
        grid=(hk, n_tiles // per),
        in_specs=[pl.BlockSpec((nq, per, dq, tq), lambda h, i: (h, i, 0, 0)),
                  pl.BlockSpec((1, t_rows, dq), lambda h, i: (h, 0, 0)),
                  pl.BlockSpec((1, n_chunks, dv + BF16_SUBLANES, tk), lambda h, i: (h, 0, 0, 0))],
        out_specs=(pl.BlockSpec((1, per * tq, dv), lambda h, i: (h, i, 0)) if nq == 1
                   else pl.BlockSpec((per * tq, nq * dv), lambda h, i: (i, h))),
        out_shape=jax.ShapeDtypeStruct((hq, t_rows, dv) if nq == 1 else (t_rows, hq * dv), BF16),
        scratch_shapes=[pltpu.VMEM((tk, nq * tq), BF16)] * 4 + [pltpu.VMEM((rows_v, nq * tq), F32)],
        compiler_params=_params(("parallel", "parallel")),
        name="attn_dense_%d" % dq,
    )(qt, k, vt)


def _trunk(x, p, geo, pos):
    p_tiles = geo[0] // TOKEN_TILE
    all_tiles = geo[2] // TOKEN_TILE
    for i in range(DEPTH):
        mixer, j = i % 4, i // 4
        x = _ffn_call(x, p["norm_ffn1"][i], p["ffn1_wg"][i], p["ffn1_wu"][i], p["ffn1_wd"][i])
        gn = p["norm_mix"][i]
        if mixer == 0:
            q, k, v = _proj_a_call(x, gn, p["a_wqkv"][j], p["a_gq"][j], p["a_gk"][j])
            pre = (_attn_a_call(q, k, v, p["a_sink"][j], geo), p["a_wo"][j])
        elif mixer == 1:
            qt, k, vt = _proj_b_call(x, gn, p["b_wdq"][j], p["b_gcq"][j], p["b_wuq"][j], p["b_wdkv"][j],
                                     p["b_gckv"][j], p["b_wukv"][j], p["b_gq"][j], p["b_gk"][j], pos, geo)
            pre = (_attn_dense_call(qt, k, vt, geo), p["b_wo"][j])
        elif mixer == 2:
            qkv = _proj_c_call(x, gn, p["c_wqkv"][j], p["c_gq"][j], p["c_gk"][j])
            res = [_attn_c_call(*qkv[g], dil, geo) for g, (_, dil) in enumerate(C_PATTERNS)]
            pre = ([r[0] for r in res], [r[1] for r in res], p["c_wo"][j])
        else:
            qt, k, vt = _proj_d_call(x, gn, p["d_wqkv"][j], p["d_gq"][j], p["d_gk"][j], pos, geo, tq=128)
            pre = (_attn_dense_call(qt, k, vt, geo), p["d_wo"][j])
        ffn2 = (p["norm_ffn2"][i], p["ffn2_wg"][i], p["ffn2_wu"][i], p["ffn2_wd"][i])
        if i < DEPTH - 1:
            x = _ffn_call(x, *ffn2, pre=pre)
        else:
            x = (_ffn_call(x, *ffn2, pre=pre, tiles=(0, p_tiles)),
                 _ffn_call(x, *ffn2, pre=pre, tiles=(p_tiles, all_tiles - p_tiles)))
    return x


def kernel(x_prompt, x_sample, norm_ffn1, ffn1_wg, ffn1_wu, ffn1_wd, norm_mix, norm_ffn2, ffn2_wg, ffn2_wu, ffn2_wd, a_wqkv, a_gq, a_gk, a_sink, a_wo, b_wdq, b_gcq, b_wuq, b_wdkv, b_gckv, b_wukv, b_gq, b_gk, b_wo, c_wqkv, c_gq, c_gk, c_wo, d_wqkv, d_gq, d_gk, d_wo):
    p = dict(norm_ffn1=norm_ffn1, ffn1_wg=ffn1_wg, ffn1_wu=ffn1_wu, ffn1_wd=ffn1_wd,
             norm_mix=norm_mix, norm_ffn2=norm_ffn2, ffn2_wg=ffn2_wg, ffn2_wu=ffn2_wu,
             ffn2_wd=ffn2_wd, a_wqkv=a_wqkv, a_gq=a_gq, a_gk=a_gk, a_sink=a_sink, a_wo=a_wo,
             b_wdq=b_wdq, b_gcq=b_gcq, b_wuq=b_wuq, b_wdkv=b_wdkv, b_gckv=b_gckv,
             b_wukv=b_wukv, b_gq=b_gq, b_gk=b_gk, b_wo=b_wo, c_wqkv=c_wqkv, c_gq=c_gq,
             c_gk=c_gk, c_wo=c_wo, d_wqkv=d_wqkv, d_gq=d_gq, d_gk=d_gk, d_wo=d_wo)
    n_p, p_seq, _ = x_prompt.shape
    n_s, s_len, _ = x_sample.shape
    assert n_p == 1 and s_len & (s_len - 1) == 0 and p_seq % s_len == 0
    p_len = n_p * p_seq
    t_rows = p_len + n_s * s_len
    geo = (p_len, s_len, t_rows)
    pos = jnp.arange(p_len, dtype=jnp.int32)
    x = (x_prompt.reshape(p_len, D_MODEL), x_sample.reshape(n_s * s_len, D_MODEL))
    y_p, y_s = _trunk(x, p, geo, pos)
    return y_p.reshape(x_prompt.shape), y_s.reshape(x_sample.shape)
```

```python
import functools
import math

import numpy as np
import jax
import jax.numpy as jnp
from jax import lax
from jax.experimental import pallas as pl
from jax.experimental.pallas import tpu as pltpu

F32 = jnp.float32
BF16 = jnp.bfloat16

D_MODEL = 1024
DEPTH = 4
HEAD_DIM = 64
D_FF = 2816
NORM_EPS = 1e-6
NEG_INF = -1e30
GRID_W = 64
ROPE_THETA = 10000.0
LOG2E = math.log2(math.e)
LANES = 128
MXU_TILE = 256
BF16_SUBLANES = 16
BAND_SUB = 128
Q_TILES_PER_STEP = 8
MASK_DIST = 1e33

A_HEADS, A_KV_HEADS, A_HALF_WINDOW = 16, 4, 128
B_HEADS, B_Q_LORA, B_KV_LORA, B_NOPE, B_ROPE, B_V = 16, 512, 256, 64, 32, 64
B_QK = B_NOPE + B_ROPE
B_PAD = 128
C_HEADS = 8
C_PATTERNS = ((128, 1), (512, 4), (2048, 16))
C_STEPS = 64
D_HEADS, D_KV_HEADS = 16, 4

FF_CHUNK = 256
N_FF_CHUNKS = D_FF // FF_CHUNK
TOKEN_TILE = 512
VMEM_LIMIT = 56 * 1024 * 1024


def _alibi_slopes(n):
    return 2.0 ** (-8.0 * np.arange(1, n + 1) / n)


def _params(sem, vmem=VMEM_LIMIT):
    return pltpu.CompilerParams(dimension_semantics=sem, vmem_limit_bytes=vmem)


def _const_spec(shape):
    nd = len(shape)
    return pl.BlockSpec(tuple(shape), lambda *_: (0,) * nd, pipeline_mode=pl.Buffered(1))


def _seq_bounds(row0, p_len, s_len):
    in_prompt = row0 < p_len
    b = lax.shift_right_logical(jnp.maximum(row0 - p_len, 0), int(math.log2(s_len)))
    lo = jnp.where(in_prompt, 0, p_len + b * s_len)
    hi = jnp.where(in_prompt, p_len, lo + s_len)
    return lo, hi


def _rms(x, g):
    return x * lax.rsqrt(jnp.mean(x * x, axis=-1, keepdims=True) + NORM_EPS) * g


def _dot(a, b):
    return jnp.dot(a, b, preferred_element_type=F32)


def _dot_nt(a, b):
    return lax.dot_general(a, b, (((1,), (1,)), ((), ())), preferred_element_type=F32)


def _ffn_kernel(*refs, mode, first_tiles=0):
    if mode == "plain":
        x_ref, gn_ref, wg_ref, wu_ref, wd_ref, out_ref, acc_ref, act_a, act_b = refs
        x = x_ref[...]
    elif mode == "plain2":
        xa_ref, xb_ref, gn_ref, wg_ref, wu_ref, wd_ref, out_ref, acc_ref, act_a, act_b = refs
        x = jnp.where(pl.program_id(0) < first_tiles, xa_ref[...], xb_ref[...])
    elif mode == "proj":
        (x_ref, o_ref, wo_ref, gn_ref, wg_ref, wu_ref, wd_ref, out_ref, acc_ref, act_a,
         act_b) = refs
        if len(o_ref.shape) == 3:
            o = jnp.concatenate([o_ref[n] for n in range(o_ref.shape[0])], axis=-1)
        else:
            o = o_ref[...]
        x = x_ref[...] + _dot(o, wo_ref[...])
    else:
        (x_ref, o0, o1, o2, l0, l1, l2, wo_ref, gn_ref, wg_ref, wu_ref, wd_ref,
         out_ref, acc_ref, act_a, act_b, *scr) = refs
        tm, width = x_ref.shape[0], wo_ref.shape[0]

        def token_major(ref, dil, scr_ref):
            slabs = width // LANES
            for r in range(dil):
                for s in range(slabs):
                    col = r * width + s * LANES
                    scr_ref[s, pl.ds(r, tm // dil, stride=dil), :] = ref[:, col:col + LANES]
            return jnp.concatenate([scr_ref[s] for s in range(slabs)], axis=-1)

        dils = [d for _, d in C_PATTERNS]
        assert dils[0] == 1
        oa, la = o0[...], l0[...]
        ob, lb = token_major(o1, dils[1], scr[0]), token_major(l1, dils[1], scr[1])
        oc, lc = token_major(o2, dils[2], scr[2]), token_major(l2, dils[2], scr[3])
        m = jnp.maximum(jnp.maximum(la, lb), lc)
        ea, eb, ec = jnp.exp(la - m), jnp.exp(lb - m), jnp.exp(lc - m)
        o = (ea * oa + eb * ob + ec * oc) / (ea + eb + ec)
        x = x_ref[...] + _dot(o.astype(BF16), wo_ref[...])

    h = _rms(x, gn_ref[...]).astype(BF16)

    def gate_up(c, a_ref):
        cols = pl.ds(pl.multiple_of(c * FF_CHUNK, FF_CHUNK), FF_CHUNK)
        g = _dot(h, wg_ref[:, cols])
        u = _dot(h, wu_ref[:, cols])
        a_ref[...] = (g * jax.nn.sigmoid(g) * u).astype(BF16)

    def down(c, a_ref):
        acc_ref[...] += _dot(a_ref[...], wd_ref[c])

    assert N_FF_CHUNKS % 2 == 1
    gate_up(0, act_a)
    gate_up(1, act_b)
    acc_ref[...] = _dot(act_a[...], wd_ref[0])

    def pair(i, carry):
        c = 2 * i + 1
        gate_up(c + 1, act_a)
        down(c, act_b)
        gate_up(c + 2, act_b)
        down(c + 1, act_a)
        return carry

    lax.fori_loop(0, N_FF_CHUNKS // 2 - 1, pair, 0)
    gate_up(N_FF_CHUNKS - 1, act_a)
    down(N_FF_CHUNKS - 2, act_b)
    last = _dot(act_a[...], wd_ref[N_FF_CHUNKS - 1])
    out_ref[...] = x + 0.5 * (acc_ref[...] + last)


def _ffn_weights(wg, wu, wd):
    wg3 = wg.astype(BF16)
    wu3 = wu.astype(BF16)
    wd3 = wd.astype(BF16).reshape(N_FF_CHUNKS, FF_CHUNK, D_MODEL)
    return wg3, wu3, wd3


def _ffn_call(x, gn, wg, wu, wd, pre=None, tiles=None):
    tm = TOKEN_TILE
    xs = x if isinstance(x, tuple) else (x,)
    off, n_tiles = tiles if tiles is not None else (0, sum(a.shape[0] for a in xs) // tm)
    wg3, wu3, wd3 = _ffn_weights(wg, wu, wd)
    row = lambda w: pl.BlockSpec((tm, w), lambda i: (i + off, 0))
    ffn_args = [gn.reshape(1, D_MODEL), wg3, wu3, wd3]
    ffn_specs = [_const_spec((1, D_MODEL)), _const_spec(wg3.shape), _const_spec(wu3.shape),
                 _const_spec(wd3.shape)]
    scratch = []
    first_tiles = 0
    if len(xs) == 2:
        assert pre is None and tiles is None
        first_tiles = xs[0].shape[0] // tm
        last = xs[1].shape[0] // tm - 1
        mode, args = "plain2", list(xs)
        specs = [pl.BlockSpec((tm, D_MODEL), lambda i: (jnp.minimum(i, first_tiles - 1), 0)),
                 pl.BlockSpec((tm, D_MODEL), lambda i: (jnp.clip(i - first_tiles, 0, last), 0))]
    elif pre is None:
        mode, args, specs = "plain", [x], [row(D_MODEL)]
    elif len(pre) == 2:
        o, wo = pre
        wo = wo.astype(BF16)
        mode = "proj"
        args = [x, o, wo]
        o_spec = row(o.shape[1]) if o.ndim == 2 else pl.BlockSpec(
            (o.shape[0], tm, o.shape[2]), lambda i: (0, i + off, 0))
        specs = [row(D_MODEL), o_spec, _const_spec(wo.shape)]
    else:
        outs, lses, wo = pre
        wo = wo.astype(BF16)
        mode = "merge"
        args = [x, *outs, *lses, wo]
        width = wo.shape[0]
        dilated = [pl.BlockSpec((tm // d, d * width), lambda i: (i + off, 0)) for _, d in C_PATTERNS]
        specs = [row(D_MODEL)] + dilated * 2 + [_const_spec(wo.shape)]
        scratch = [pltpu.VMEM((width // LANES, tm, LANES), F32)] * 4
    return pl.pallas_call(
        functools.partial(_ffn_kernel, mode=mode, first_tiles=first_tiles),
        grid=(n_tiles,),
        in_specs=specs + ffn_specs,
        out_specs=pl.BlockSpec((tm, D_MODEL), lambda i: (i, 0)),
        out_shape=jax.ShapeDtypeStruct((n_tiles * tm, D_MODEL), F32),
        scratch_shapes=[pltpu.VMEM((tm, D_MODEL), F32)] + [pltpu.VMEM((tm, FF_CHUNK), BF16)] * 2
        + scratch,
        compiler_params=_params(("parallel",)),
        name="ffn_" + mode,
    )(*args, *ffn_args)


def _head_norm(t, width):
    return lax.rsqrt(jnp.sum(t * t, axis=-1, keepdims=True) * (1.0 / width) + NORM_EPS)


def _segment_ones(seg):
    n = MXU_TILE // seg
    return jnp.kron(jnp.eye(n, dtype=F32), jnp.ones((seg, seg), F32)).astype(BF16)


def _segment_rsqrt(t, e_ref, width):
    sq = t * t
    hi = sq.astype(BF16)
    lo = (sq - hi.astype(F32)).astype(BF16)
    e = e_ref[...]
    sums = [_dot(hi[:, b:b + MXU_TILE], e) + _dot(lo[:, b:b + MXU_TILE], e)
            for b in range(0, t.shape[-1], MXU_TILE)]
    ssq = sums[0] if len(sums) == 1 else jnp.concatenate(sums, axis=-1)
    return lax.rsqrt(ssq * (1.0 / width) + NORM_EPS)


def _store_vt(vt_ref, n, v_t):
    dv, rows = v_t.shape
    vt_ref[n, 0, :dv, :] = v_t.astype(BF16)
    first = lax.broadcasted_iota(jnp.int32, (BF16_SUBLANES, rows), 0) == 0
    vt_ref[n, 0, dv:, :] = jnp.where(first, 1.0, 0.0).astype(BF16)


def _proj_a_kernel(x_ref, gn_ref, w_ref, gain_ref, e_ref, q_ref, k_ref, v_ref):
    hd = HEAD_DIM
    nqk = (A_HEADS + A_KV_HEADS) * hd
    h = _rms(x_ref[...], gn_ref[...]).astype(BF16)
    y = _dot(h, w_ref[...])
    qk = y[:, :nqk]
    qk = (qk * _segment_rsqrt(qk, e_ref, hd) * gain_ref[...]).astype(BF16)
    for n in range(A_HEADS):
        q_ref[n] = qk[:, n * hd:(n + 1) * hd]
    for n in range(A_KV_HEADS):
        k_ref[n] = qk[:, (A_HEADS + n) * hd:(A_HEADS + n + 1) * hd]
        v_ref[n] = y[:, nqk + n * hd: nqk + (n + 1) * hd].astype(BF16)


def _proj_a_call(x, gn, wqkv, gq, gk):
    t_rows = x.shape[0]
    tm, hd = TOKEN_TILE, HEAD_DIM
    w = wqkv.astype(BF16)
    gain = jnp.concatenate([jnp.tile(gq * (hd ** -0.5 * LOG2E), A_HEADS), jnp.tile(gk, A_KV_HEADS)])
    ones = _segment_ones(hd)
    head_major = lambda n: pl.BlockSpec((n, tm, hd), lambda i: (0, i, 0))
    return pl.pallas_call(
        _proj_a_kernel,
        grid=(t_rows // tm,),
        in_specs=[pl.BlockSpec((tm, D_MODEL), lambda i: (i, 0)), _const_spec((1, D_MODEL)),
                  _const_spec(w.shape), _const_spec((1, gain.shape[0])), _const_spec(ones.shape)],
        out_specs=[head_major(A_HEADS), head_major(A_KV_HEADS), head_major(A_KV_HEADS)],
        out_shape=[jax.ShapeDtypeStruct((A_HEADS, t_rows, hd), BF16),
                   jax.ShapeDtypeStruct((A_KV_HEADS, t_rows, hd), BF16),
                   jax.ShapeDtypeStruct((A_KV_HEADS, t_rows, hd), BF16)],
        compiler_params=_params(("parallel",)),
        name="proj_a",
    )(x, gn.reshape(1, D_MODEL), w, gain.reshape(1, -1), ones)


def _proj_d_kernel(x_ref, gn_ref, w_ref, e_ref, cq_ref, sq_ref, ck_ref, sk_ref, qt_ref, k_ref, vt_ref):
    hd = HEAD_DIM
    nq, nk = D_HEADS * hd, D_KV_HEADS * hd
    tq = qt_ref.shape[-1]
    h = _rms(x_ref[...], gn_ref[...]).astype(BF16)
    y = _dot(h, w_ref[...])
    swapped = nq + 2 * nk
    t, ts = y[:, :nq + nk], y[:, swapped:swapped + nq + nk]
    r = _segment_rsqrt(t, e_ref, hd)
    for b in range((nq + nk) // LANES):
        is_q = b < nq // LANES
        c_ref, s_ref = (cq_ref, sq_ref) if is_q else (ck_ref, sk_ref)
        sl = slice(b * LANES, (b + 1) * LANES)
        blk = (t[:, sl] * c_ref[...] + ts[:, sl] * s_ref[...]) * r[:, sl]
        blk = blk.T.astype(BF16) if is_q else blk.astype(BF16)
        for half in range(LANES // hd):
            n = b * (LANES // hd) + half
            if is_q:
                for j in range(qt_ref.shape[1]):
                    qt_ref[n, j] = blk[half * hd:(half + 1) * hd, j * tq:(j + 1) * tq]
            else:
                k_ref[n - D_HEADS] = blk[:, half * hd:(half + 1) * hd]
    v_t = y[:, nq + nk: nq + 2 * nk].T
    for n in range(D_KV_HEADS):
        _store_vt(vt_ref, n, v_t[n * hd:(n + 1) * hd])


def _swap_halves_cols(w, group):
    k, n = w.shape
    return w.reshape(k, n // group, 2, group // 2)[:, :, ::-1, :].reshape(k, n)


def _position_spec(geo, tm, width):
    p_len, s_len, _ = geo
    p_tiles, s_tiles = p_len // tm, s_len // tm
    return pl.BlockSpec(
        (tm, width), lambda i: (jnp.where(i < p_tiles, i, (i - p_tiles) % s_tiles), 0))


def _rope_tables(pos, gain, scale):
    half = gain.shape[-1] // 2
    inv = ROPE_THETA ** (-jnp.arange(half, dtype=F32) / half)
    ang = pos.astype(F32)[:, None] * inv[None, :]
    cos, sin = lax.optimization_barrier((jnp.cos(ang), jnp.sin(ang)))
    g_sw = jnp.concatenate([gain[half:], gain[:half]])
    c = jnp.concatenate([cos, cos], axis=-1) * gain[None, :] * scale
    s = jnp.concatenate([-sin, sin], axis=-1) * g_sw[None, :] * scale
    return c, s


def _proj_d_call(x, gn, wqkv, gq, gk, pos, geo, tq):
    t_rows = x.shape[0]
    tm, hd = TOKEN_TILE, HEAD_DIM
    nq, nk = D_HEADS * hd, D_KV_HEADS * hd
    half = hd // 2
    w = jnp.concatenate([wqkv, _swap_halves_cols(wqkv[:, :nq + nk], half)], axis=1).astype(BF16)
    rows, cols = pos // GRID_W, pos % GRID_W

    def tables(g, scale):
        cr, sr = _rope_tables(rows, g[:half], scale)
        cc, sc = _rope_tables(cols, g[half:], scale)
        reps = LANES // hd
        return jnp.concatenate([cr, cc] * reps, axis=-1), jnp.concatenate([sr, sc] * reps, axis=-1)

    cq, sq = tables(gq, hd ** -0.5 * LOG2E)
    ck, sk = tables(gk, 1.0)
    ones = _segment_ones(hd)
    head_major = lambda n: pl.BlockSpec((n, tm, hd), lambda i: (0, i, 0))
    tab = _position_spec(geo, tm, LANES)
    return pl.pallas_call(
        _proj_d_kernel,
        grid=(t_rows // tm,),
        in_specs=[pl.BlockSpec((tm, D_MODEL), lambda i: (i, 0)), _const_spec((1, D_MODEL)),
                  _const_spec(w.shape), _const_spec(ones.shape), tab, tab, tab, tab],
        out_specs=[pl.BlockSpec((D_HEADS, tm // tq, hd, tq), lambda i: (0, i, 0, 0)),
                   head_major(D_KV_HEADS),
                   pl.BlockSpec((D_KV_HEADS, 1, hd + BF16_SUBLANES, tm), lambda i: (0, i, 0, 0))],
        out_shape=[jax.ShapeDtypeStruct((D_HEADS, t_rows // tq, hd, tq), BF16),
                   jax.ShapeDtypeStruct((D_KV_HEADS, t_rows, hd), BF16),
                   jax.ShapeDtypeStruct((D_KV_HEADS, t_rows // tm, hd + BF16_SUBLANES, tm), BF16)],
        compiler_params=_params(("parallel",)),
        name="proj_d",
    )(x, gn.reshape(1, D_MODEL), w, ones, cq, sq, ck, sk)


def _proj_b_kernel(x_ref, gn_ref, w1_ref, gcq_ref, gckv_ref, wq_ref, wqs_ref, wk_ref, wv_ref,
                   cq_ref, sq_ref, ck_ref, sk_ref, qt_ref, k_ref, vt_ref):
    h = _rms(x_ref[...], gn_ref[...]).astype(BF16)
    y1 = _dot(h, w1_ref[...])
    c_q = _rms(y1[:, :B_Q_LORA], gcq_ref[...]).astype(BF16)
    c_kv = _rms(y1[:, B_Q_LORA:B_Q_LORA + B_KV_LORA], gckv_ref[...]).astype(BF16)
    off = B_Q_LORA + B_KV_LORA
    kr = y1[:, off:off + B_PAD]
    krs = y1[:, off + B_PAD:off + 2 * B_PAD]
    q = _dot(c_q, wq_ref[...])
    qs = _dot(c_q, wqs_ref[...])
    kn = _dot(c_kv, wk_ref[...])
    v = _dot(c_kv, wv_ref[...])
    cq, sq, ck, sk = cq_ref[...], sq_ref[...], ck_ref[...], sk_ref[...]
    krs_s = krs * sk
    for n in range(B_HEADS):
        t = q[:, n * B_PAD:(n + 1) * B_PAD]
        ts = qs[:, n * B_PAD:(n + 1) * B_PAD]
        qt_ref[n, 0] = ((t * cq + ts * sq) * _head_norm(t, B_QK)).T.astype(BF16)
        t = kn[:, n * B_PAD:(n + 1) * B_PAD] + kr
        k_ref[n] = ((t * ck + krs_s) * _head_norm(t, B_QK)).astype(BF16)
    v_t = v.T
    for n in range(B_HEADS):
        _store_vt(vt_ref, n, v_t[n * B_V:(n + 1) * B_V])


def _pad_heads(w, lo, width):
    k, nh, _ = w.shape
    out = jnp.zeros((k, nh, B_PAD), w.dtype).at[:, :, lo:lo + width].set(w)
    return out.reshape(k, nh * B_PAD)


def _proj_b_call(x, gn, wdq, gcq, wuq, wdkv, gckv, wukv, gq, gk, pos, geo):
    t_rows = x.shape[0]
    tm = TOKEN_TILE
    wuq3 = wuq.reshape(B_Q_LORA, B_HEADS, B_QK)
    wq = _pad_heads(wuq3, 0, B_QK).astype(BF16)
    wuq_rope_sw = _swap_halves_cols(wuq3[:, :, B_NOPE:].reshape(B_Q_LORA, -1), B_ROPE)
    wqs = _pad_heads(wuq_rope_sw.reshape(B_Q_LORA, B_HEADS, B_ROPE), B_NOPE, B_ROPE).astype(BF16)
    wukv3 = wukv.reshape(B_KV_LORA, B_HEADS, B_NOPE + B_V)
    wk = _pad_heads(wukv3[:, :, :B_NOPE], 0, B_NOPE).astype(BF16)
    wv = wukv3[:, :, B_NOPE:].reshape(B_KV_LORA, B_HEADS * B_V).astype(BF16)
    w_kr = wdkv[:, B_KV_LORA:]
    kr_pad = _pad_heads(w_kr[:, None, :], B_NOPE, B_ROPE)
    krs_pad = _pad_heads(_swap_halves_cols(w_kr, B_ROPE)[:, None, :], B_NOPE, B_ROPE)
    w1 = jnp.concatenate([wdq, wdkv[:, :B_KV_LORA], kr_pad, krs_pad], axis=1).astype(BF16)

    def tables(g, scale):
        c, s = _rope_tables(pos, g[B_NOPE:], scale)
        n_pos = pos.shape[0]
        zeros = jnp.zeros((n_pos, B_PAD - B_QK), F32)
        c_full = jnp.concatenate(
            [jnp.broadcast_to(g[None, :B_NOPE] * scale, (n_pos, B_NOPE)), c, zeros], axis=-1)
        s_full = jnp.concatenate([jnp.zeros((n_pos, B_NOPE), F32), s, zeros], axis=-1)
        return c_full, s_full

    cq, sq = tables(gq, B_QK ** -0.5 * LOG2E)
    ck, sk = tables(gk, 1.0)
    tab = _position_spec(geo, tm, B_PAD)
    head_major = lambda w_: pl.BlockSpec((B_HEADS, tm, w_), lambda i: (0, i, 0))
    return pl.pallas_call(
        _proj_b_kernel,
        grid=(t_rows // tm,),
        in_specs=[pl.BlockSpec((tm, D_MODEL), lambda i: (i, 0)), _const_spec((1, D_MODEL)),
                  _const_spec(w1.shape), _const_spec((1, B_Q_LORA)), _const_spec((1, B_KV_LORA)),
                  _const_spec(wq.shape), _const_spec(wqs.shape), _const_spec(wk.shape),
                  _const_spec(wv.shape), tab, tab, tab, tab],
        out_specs=[pl.BlockSpec((B_HEADS, 1, B_PAD, tm), lambda i: (0, i, 0, 0)),
                   head_major(B_PAD),
                   pl.BlockSpec((B_HEADS, 1, B_V + BF16_SUBLANES, tm), lambda i: (0, i, 0, 0))],
        out_shape=[jax.ShapeDtypeStruct((B_HEADS, t_rows // tm, B_PAD, tm), BF16),
                   jax.ShapeDtypeStruct((B_HEADS, t_rows, B_PAD), BF16),
                   jax.ShapeDtypeStruct((B_HEADS, t_rows // tm, B_V + BF16_SUBLANES, tm), BF16)],
        compiler_params=_params(("parallel",)),
        name="proj_b",
    )(x, gn.reshape(1, D_MODEL), w1, gcq.reshape(1, -1), gckv.reshape(1, -1), wq, wqs, wk, wv,
      cq, sq, ck, sk)


def _proj_c_kernel(x_ref, gn_ref, w_ref, gq_ref, gk_ref, e_ref, *refs):
    hd = HEAD_DIM
    width = C_HEADS * hd
    outs, scr_ref = refs[:-1], refs[-1]
    tm = x_ref.shape[0]
    h = _rms(x_ref[...], gn_ref[...]).astype(BF16)
    y = _dot(h, w_ref[...])
    gains = (gq_ref[...], gk_ref[...])
    for g, (_, dil) in enumerate(C_PATTERNS):
        for j in range(3):
            base = (g * 3 + j) * width
            val = y[:, base: base + width]
            if j < 2:
                val = val * _segment_rsqrt(val, e_ref, hd) * gains[j]
            out_ref = outs[g * 3 + j]
            if dil == 1:
                out_ref[...] = val.astype(BF16)
            else:
                for s in range(width // LANES):
                    scr_ref[s] = val[:, s * LANES:(s + 1) * LANES]
                for r in range(dil):
                    for s in range(width // LANES):
                        rows = scr_ref[s, pl.ds(r, tm // dil, stride=dil), :]
                        col = r * width + s * LANES
                        out_ref[:, col:col + LANES] = rows.astype(BF16)


def _proj_c_call(x, gn, wqkv, gq, gk):
    t_rows = x.shape[0]
    tm, hd = TOKEN_TILE, HEAD_DIM
    width = C_HEADS * hd
    w = wqkv.astype(BF16)
    ones = _segment_ones(hd)
    specs, shapes = [], []
    for _, dil in C_PATTERNS:
        specs += [pl.BlockSpec((tm // dil, dil * width), lambda i: (i, 0))] * 3
        shapes += [jax.ShapeDtypeStruct((t_rows // dil, dil * width), BF16)] * 3
    outs = pl.pallas_call(
        _proj_c_kernel,
        grid=(t_rows // tm,),
        in_specs=[pl.BlockSpec((tm, D_MODEL), lambda i: (i, 0)), _const_spec((1, D_MODEL)),
                  _const_spec(w.shape), _const_spec((1, width)), _const_spec((1, width)),
                  _const_spec(ones.shape)],
        out_specs=specs,
        out_shape=shapes,
        scratch_shapes=[pltpu.VMEM((width // LANES, tm, LANES), F32)],
        compiler_params=_params(("parallel",)),
        name="proj_c",
    )(x, gn.reshape(1, D_MODEL), w, jnp.tile(gq * (hd ** -0.5 * LOG2E), C_HEADS).reshape(1, width),
      jnp.tile(gk, C_HEADS).reshape(1, width), ones)
    return [outs[3 * g:3 * g + 3] for g in range(len(C_PATTERNS))]


def _band_masked_dist(row0, lo, hi, sub, w):
    nk = sub + 2 * w
    r = lax.broadcasted_iota(jnp.int32, (sub, nk), 0)
    c = lax.broadcasted_iota(jnp.int32, (sub, nk), 1)
    dist = jnp.abs(r + w - c)
    jpos = row0 - w + lax.broadcasted_iota(jnp.int32, (1, nk), 1)
    outside = jnp.where(jpos < lo, w + 1, 0) + jnp.where(jpos >= hi, w + 1, 0)
    return jnp.where(dist + outside <= w, dist.astype(F32), MASK_DIST)


def _two_stage(n_items, first, second, ahead=2):
    pending = {i: first(i) for i in range(min(ahead, n_items))}
    results = []
    for i in range(n_items):
        if i + ahead < n_items:
            pending[i + ahead] = first(i + ahead)
        results.append(second(i, pending.pop(i)))
    return results


def _with_ones(vv):
    return jnp.concatenate([vv, jnp.ones((vv.shape[0], BF16_SUBLANES), BF16)], axis=-1)


def _band_softmax_pv(s, vv_ones, sink):
    m = jnp.max(s, axis=-1, keepdims=True)
    if sink is not None:
        m = jnp.maximum(m, sink)
    hd = vv_ones.shape[-1] - BF16_SUBLANES
    pv = _dot(jnp.exp2(s - m).astype(BF16), vv_ones)
    den = pv[:, hd:hd + 1]
    if sink is not None:
        den = den + jnp.exp2(sink - m)
    return pv[:, :hd] / den, m + jnp.log2(den)


def _attn_a_kernel(slope_ref, sink_ref, q_ref, kp_ref, kc_ref, kn_ref, vp_ref, vc_ref, vn_ref,
                   o_ref, *, tq, p_len, s_len):
    w, sub = A_HALF_WINDOW, BAND_SUB
    group = A_HEADS // A_KV_HEADS
    kvh = pl.program_id(0)
    row0 = pl.program_id(1) * tq
    lo, hi = _seq_bounds(row0, p_len, s_len)
    kk = jnp.concatenate([kp_ref[0], kc_ref[0], kn_ref[0]], axis=0)
    vv = _with_ones(jnp.concatenate([vp_ref[0], vc_ref[0], vn_ref[0]], axis=0))
    sink = jnp.concatenate(
        [jnp.full((sub, 1), sink_ref[kvh * group + g], F32) for g in range(group)], axis=0)

    def keys(r):
        return slice(r * sub, (r + 1) * sub + 2 * w)

    def logits(r):
        md = _band_masked_dist(row0 + r * sub, lo, hi, sub, w)
        bias = jnp.concatenate([md * slope_ref[kvh * group + g] for g in range(group)], axis=0)
        q = q_ref[:, r * sub:(r + 1) * sub, :].reshape(group * sub, HEAD_DIM)
        return _dot_nt(q, kk[keys(r)]) + bias

    tiles = _two_stage(tq // sub, logits, lambda r, s: _band_softmax_pv(s, vv[keys(r)], sink)[0])
    outs = [[o[g * sub:(g + 1) * sub] for o in tiles] for g in range(group)]
    o_ref[...] = jnp.concatenate(
        [jnp.concatenate(rows, axis=0) for rows in outs], axis=-1).astype(BF16)


def _attn_a_call(q, k, v, sink, geo):
    p_len, s_len, t_rows = geo
    tq, w, hd = 1024, A_HALF_WINDOW, HEAD_DIM
    group = A_HEADS // A_KV_HEADS
    per = tq // w
    last = t_rows // w - 1
    prev = pl.BlockSpec((1, w, hd), lambda h, i: (h, jnp.maximum(i * per - 1, 0), 0))
    cur = pl.BlockSpec((1, tq, hd), lambda h, i: (h, i, 0))
    nxt = pl.BlockSpec((1, w, hd), lambda h, i: (h, jnp.minimum((i + 1) * per, last), 0))
    smem = pl.BlockSpec(memory_space=pltpu.SMEM)
    slopes = jnp.asarray(-LOG2E * _alibi_slopes(A_HEADS), F32)
    return pl.pallas_call(
        functools.partial(_attn_a_kernel, tq=tq, p_len=p_len, s_len=s_len),
        grid=(A_KV_HEADS, t_rows // tq),
        in_specs=[smem, smem, pl.BlockSpec((group, tq, hd), lambda h, i: (h, i, 0)),
                  prev, cur, nxt, prev, cur, nxt],
        out_specs=pl.BlockSpec((tq, group * hd), lambda h, i: (i, h)),
        out_shape=jax.ShapeDtypeStruct((t_rows, A_HEADS * hd), BF16),
        compiler_params=_params(("parallel", "parallel")),
        name="attn_a",
    )(slopes, sink.astype(F32) * LOG2E, q, k, k, k, v, v, v)


def _attn_c_kernel(q_ref, kp_ref, kc_ref, kn_ref, vp_ref, vc_ref, vn_ref, o_ref, lse_ref,
                   *, tq, dil, p_len, s_len):
    w, hd, sub = C_STEPS, HEAD_DIM, min(BAND_SUB, tq)
    row0 = pl.program_id(0) * tq
    lo, hi = _seq_bounds(row0, p_len // dil, s_len // dil)
    slopes = -LOG2E * dil * _alibi_slopes(C_HEADS)
    mds = [_band_masked_dist(row0 + r * sub, lo, hi, sub, w) for r in range(tq // sub)]
    n_sub = tq // sub

    heads = {}

    def head_rows(n):
        if n not in heads:
            hs = slice(n * hd, (n + 1) * hd)
            kk = jnp.concatenate([kp_ref[:, hs], kc_ref[:, hs], kn_ref[:, hs]], axis=0)
            vv = jnp.concatenate([vp_ref[:, hs], vc_ref[:, hs], vn_ref[:, hs]], axis=0)
            heads[n] = (q_ref[:, hs], kk, _with_ones(vv))
        return heads[n]

    def logits(i):
        n, r = divmod(i, n_sub)
        q, kk, _ = head_rows(n)
        keys = kk[r * sub:(r + 1) * sub + 2 * w]
        return _dot_nt(q[r * sub:(r + 1) * sub], keys) + mds[r] * float(slopes[n])

    def attend(i, s):
        n, r = divmod(i, n_sub)
        o, lse2 = _band_softmax_pv(s, head_rows(n)[2][r * sub:(r + 1) * sub + 2 * w], None)
        return o, jnp.broadcast_to(lse2 * (1.0 / LOG2E), (sub, hd))

    tiles = _two_stage(C_HEADS * n_sub, logits, attend, ahead=3 if n_sub > 1 else 2)
    outs = [jnp.concatenate([t[0] for t in tiles[n * n_sub:(n + 1) * n_sub]], axis=0)
            for n in range(C_HEADS)]
    lses = [jnp.concatenate([t[1] for t in tiles[n * n_sub:(n + 1) * n_sub]], axis=0)
            for n in range(C_HEADS)]
    o_ref[...] = jnp.concatenate(outs, axis=-1)
    lse_ref[...] = jnp.concatenate(lses, axis=-1)


def _attn_c_call(q, k, v, dil, geo):
    p_len, s_len, t_rows = geo
    w, width = C_STEPS, C_HEADS * HEAD_DIM
    rows = t_rows // dil
    tq = min(512, s_len // dil)
    per = tq // w
    last = rows // w - 1
    prev = pl.BlockSpec((w, width), lambda i, r: (jnp.maximum(i * per - 1, 0), r))
    cur = pl.BlockSpec((tq, width), lambda i, r: (i, r))
    nxt = pl.BlockSpec((w, width), lambda i, r: (jnp.minimum((i + 1) * per, last), r))
    shape = jax.ShapeDtypeStruct((rows, dil * width), F32)
    return pl.pallas_call(
        functools.partial(_attn_c_kernel, tq=tq, dil=dil, p_len=p_len, s_len=s_len),
        grid=(rows // tq, dil),
        in_specs=[cur, prev, cur, nxt, prev, cur, nxt],
        out_specs=[cur, cur],
        out_shape=[shape, shape],
        compiler_params=_params(("parallel", "parallel")),
        name="attn_c_d%d" % dil,
    )(q, k, k, k, v, v, v)


def _attn_dense_kernel(qt_ref, k_ref, vt_ref, o_ref, sa_ref, sb_ref, sc_ref, sd_ref, acc_ref,
                       *, nq, tq, tk, p_len, s_len):
    dv = vt_ref.shape[-2] - BF16_SUBLANES
    width = nq * tq
    shift = int(math.log2(tk))

    def first_chunk(j):
        row0 = (pl.program_id(1) * Q_TILES_PER_STEP + j) * tq
        lo, hi = _seq_bounds(row0, p_len, s_len)
        return lax.shift_right_logical(lo, shift), lax.shift_right_logical(hi - lo, shift)

    def scores(j, c, st_ref):
        start = pl.multiple_of(c * tk, tk)
        qt = jnp.concatenate([qt_ref[g, j] for g in range(nq)], axis=-1)
        st = _dot(k_ref[0, pl.ds(start, tk), :], qt).astype(BF16)
        st_ref[...] = st
        return jnp.max(st, axis=0, keepdims=True).astype(F32)

    def update(c, st_ref, mc, m, acc):
        m_new = jnp.maximum(m, mc)
        alpha = jnp.exp2(m - m_new)
        pt = jnp.exp2(st_ref[...] - m_new.astype(BF16))
        acc_ref[...] = alpha * acc_ref[...] + _dot(vt_ref[0, c], pt)
        return m_new, acc

    def q_tile(j, mcs, start_next):
        first, n_chunks = first_chunk(j)

        def quad(i, carry):
            m, acc, mc_a, mc_b, mc_c = carry
            c = first + 4 * i
            mc_d = scores(j, c + 3, sd_ref)
            m, acc = update(c, sa_ref, mc_a, m, acc)
            mc_a = scores(j, c + 4, sa_ref)
            m, acc = update(c + 1, sb_ref, mc_b, m, acc)
            mc_b = scores(j, c + 5, sb_ref)
            m, acc = update(c + 2, sc_ref, mc_c, m, acc)
            mc_c = scores(j, c + 6, sc_ref)
            m, acc = update(c + 3, sd_ref, mc_d, m, acc)
            return m, acc, mc_a, mc_b, mc_c

        acc_ref[...] = jnp.zeros_like(acc_ref)
        carry = (jnp.full((1, width), NEG_INF, F32), 0, *mcs)
        n_body = lax.shift_right_logical(n_chunks, 2) - 1
        carry = lax.fori_loop(0, lax.shift_right_logical(n_body, 1),
                              lambda i, cr: quad(2 * i + 1, quad(2 * i, cr)), carry)
        carry = lax.fori_loop(0, n_body & 1, lambda _, cr: quad(n_body - 1, cr), carry)
        m, acc, mc_a, mc_b, mc_c = carry
        c = first + n_chunks - 4
        mc_d = scores(j, c + 3, sd_ref)
        m, acc = update(c, sa_ref, mc_a, m, acc)
        if start_next:
            first_next, _ = first_chunk(j + 1)
            mc_a = scores(j + 1, first_next, sa_ref)
        m, acc = update(c + 1, sb_ref, mc_b, m, acc)
        if start_next:
            mc_b = scores(j + 1, first_next + 1, sb_ref)
        m, acc = update(c + 2, sc_ref, mc_c, m, acc)
        if start_next:
            mc_c = scores(j + 1, first_next + 2, sc_ref)
        m, acc = update(c + 3, sd_ref, mc_d, m, acc)
        o = acc_ref[:dv, :] / acc_ref[dv:dv + 1, :]
        rows = pl.ds(pl.multiple_of(j * tq, tq), tq)
        heads = [o[:, g * tq:(g + 1) * tq].T.astype(BF16) for g in range(nq)]
        if nq == 1:
            o_ref[0, rows, :] = heads[0]
        else:
            o_ref[rows, :] = jnp.concatenate(heads, axis=-1)
        return mc_a, mc_b, mc_c

    first, _ = first_chunk(0)
    mcs = (scores(0, first, sa_ref), scores(0, first + 1, sb_ref), scores(0, first + 2, sc_ref))
    mcs = lax.fori_loop(0, Q_TILES_PER_STEP - 1, lambda j, mcs: q_tile(j, mcs, True), mcs)
    q_tile(Q_TILES_PER_STEP - 1, mcs, False)


def _attn_dense_call(qt, k, vt, geo):
    p_len, s_len, t_rows = geo
    hq, n_tiles, dq, tq = qt.shape
    hk, n_chunks, rows_v, tk = vt.shape
    dv = rows_v - BF16_SUBLANES
    nq, per = hq // hk, Q_TILES_PER_STEP
    assert s_len % (4 * tk) == 0 and p_len % (4 * tk) == 0 and n_tiles % per == 0
    return pl.pallas_call(
        functools.partial(_attn_dense_kernel, nq=nq, tq=tq, tk=tk, p_len=p_len, s_len=s_len),
        grid=(hk, n_tiles // per),
        in_specs=[pl.BlockSpec((nq, per, dq, tq), lambda h, i: (h, i, 0, 0)),
                  pl.BlockSpec((1, t_rows, dq), lambda h, i: (h, 0, 0)),
                  pl.BlockSpec((1, n_chunks, dv + BF16_SUBLANES, tk), lambda h, i: (h, 0, 0, 0))],
        out_specs=(pl.BlockSpec((1, per * tq, dv), lambda h, i: (h, i, 0)) if nq == 1
                   else pl.BlockSpec((per * tq, nq * dv), lambda h, i: (i, h))),
        out_shape=jax.ShapeDtypeStruct((hq, t_rows, dv) if nq == 1 else (t_rows, hq * dv), BF16),
        scratch_shapes=[pltpu.VMEM((tk, nq * tq), BF16)] * 4 + [pltpu.VMEM((rows_v, nq * tq), F32)],
        compiler_params=_params(("parallel", "parallel")),
        name="attn_dense_%d" % dq,
    )(qt, k, vt)


def _trunk(x, p, geo, pos):
    p_tiles = geo[0] // TOKEN_TILE
    all_tiles = geo[2] // TOKEN_TILE
    for i in range(DEPTH):
        mixer, j = i % 4, i // 4
        x = _ffn_call(x, p["norm_ffn1"][i], p["ffn1_wg"][i], p["ffn1_wu"][i], p["ffn1_wd"][i])
        gn = p["norm_mix"][i]
        if mixer == 0:
            q, k, v = _proj_a_call(x, gn, p["a_wqkv"][j], p["a_gq"][j], p["a_gk"][j])
            pre = (_attn_a_call(q, k, v, p["a_sink"][j], geo), p["a_wo"][j])
        elif mixer == 1:
            qt, k, vt = _proj_b_call(x, gn, p["b_wdq"][j], p["b_gcq"][j], p["b_wuq"][j], p["b_wdkv"][j],
                                     p["b_gckv"][j], p["b_wukv"][j], p["b_gq"][j], p["b_gk"][j], pos, geo)
            pre = (_attn_dense_call(qt, k, vt, geo), p["b_wo"][j])
        elif mixer == 2:
            qkv = _proj_c_call(x, gn, p["c_wqkv"][j], p["c_gq"][j], p["c_gk"][j])
            res = [_attn_c_call(*qkv[g], dil, geo) for g, (_, dil) in enumerate(C_PATTERNS)]
            pre = ([r[0] for r in res], [r[1] for r in res], p["c_wo"][j])
        else:
            qt, k, vt = _proj_d_call(x, gn, p["d_wqkv"][j], p["d_gq"][j], p["d_gk"][j], pos, geo, tq=128)
            pre = (_attn_dense_call(qt, k, vt, geo), p["d_wo"][j])
        ffn2 = (p["norm_ffn2"][i], p["ffn2_wg"][i], p["ffn2_wu"][i], p["ffn2_wd"][i])
        if i < DEPTH - 1:
            x = _ffn_call(x, *ffn2, pre=pre)
        else:
            x = (_ffn_call(x, *ffn2, pre=pre, tiles=(0, p_tiles)),
                 _ffn_call(x, *ffn2, pre=pre, tiles=(p_tiles, all_tiles - p_tiles)))
    return x


def kernel(x_prompt, x_sample, norm_ffn1, ffn1_wg, ffn1_wu, ffn1_wd, norm_mix, norm_ffn2, ffn2_wg, ffn2_wu, ffn2_wd, a_wqkv, a_gq, a_gk, a_sink, a_wo, b_wdq, b_gcq, b_wuq, b_wdkv, b_gckv, b_wukv, b_gq, b_gk, b_wo, c_wqkv, c_gq, c_gk, c_wo, d_wqkv, d_gq, d_gk, d_wo):
    p = dict(norm_ffn1=norm_ffn1, ffn1_wg=ffn1_wg, ffn1_wu=ffn1_wu, ffn1_wd=ffn1_wd,
             norm_mix=norm_mix, norm_ffn2=norm_ffn2, ffn2_wg=ffn2_wg, ffn2_wu=ffn2_wu,
             ffn2_wd=ffn2_wd, a_wqkv=a_wqkv, a_gq=a_gq, a_gk=a_gk, a_sink=a_sink, a_wo=a_wo,
             b_wdq=b_wdq, b_gcq=b_gcq, b_wuq=b_wuq, b_wdkv=b_wdkv, b_gckv=b_gckv,
             b_wukv=b_wukv, b_gq=b_gq, b_gk=b_gk, b_wo=b_wo, c_wqkv=c_wqkv, c_gq=c_gq,
             c_gk=c_gk, c_wo=c_wo, d_wqkv=d_wqkv, d_gq=d_gq, d_gk=d_gk, d_wo=d_wo)
    n_p, p_seq, _ = x_prompt.shape
    n_s, s_len, _ = x_sample.shape
    assert n_p == 1 and s_len & (s_len - 1) == 0 and p_seq % s_len == 0
    p_len = n_p * p_seq
    t_rows = p_len + n_s * s_len
    geo = (p_len, s_len, t_rows)
    pos = jnp.arange(p_len, dtype=jnp.int32)
    x = (x_prompt.reshape(p_len, D_MODEL), x_sample.reshape(n_s * s_len, D_MODEL))
    y_p, y_s = _trunk(x, p, geo, pos)
    return y_p.reshape(x_prompt.shape), y_s.reshape(x_sample.shape)
```

```python
import functools
import math

import numpy as np
import jax
import jax.numpy as jnp
from jax import lax
from jax.experimental import pallas as pl
from jax.experimental.pallas import tpu as pltpu

F32 = jnp.float32
BF16 = jnp.bfloat16

D_MODEL = 1024
DEPTH = 4
HEAD_DIM = 64
D_FF = 2816
NORM_EPS = 1e-6
NEG_INF = -1e30
GRID_W = 64
ROPE_THETA = 10000.0
LOG2E = math.log2(math.e)
LANES = 128
MXU_TILE = 256
BF16_SUBLANES = 16
BAND_SUB = 128
Q_TILES_PER_STEP = 8
MASK_DIST = 1e33

A_HEADS, A_KV_HEADS, A_HALF_WINDOW = 16, 4, 128
B_HEADS, B_Q_LORA, B_KV_LORA, B_NOPE, B_ROPE, B_V = 16, 512, 256, 64, 32, 64
B_QK = B_NOPE + B_ROPE
B_PAD = 128
C_HEADS = 8
C_PATTERNS = ((128, 1), (512, 4), (2048, 16))
C_STEPS = 64
D_HEADS, D_KV_HEADS = 16, 4

FF_CHUNK = 256
N_FF_CHUNKS = D_FF // FF_CHUNK
TOKEN_TILE = 512
VMEM_LIMIT = 56 * 1024 * 1024


def _alibi_slopes(n):
    return 2.0 ** (-8.0 * np.arange(1, n + 1) / n)


def _params(sem, vmem=VMEM_LIMIT):
    return pltpu.CompilerParams(dimension_semantics=sem, vmem_limit_bytes=vmem)


def _const_spec(shape):
    nd = len(shape)
    return pl.BlockSpec(tuple(shape), lambda *_: (0,) * nd, pipeline_mode=pl.Buffered(1))


def _seq_bounds(row0, p_len, s_len):
    in_prompt = row0 < p_len
    b = lax.shift_right_logical(jnp.maximum(row0 - p_len, 0), int(math.log2(s_len)))
    lo = jnp.where(in_prompt, 0, p_len + b * s_len)
    hi = jnp.where(in_prompt, p_len, lo + s_len)
    return lo, hi


def _rms(x, g):
    return x * lax.rsqrt(jnp.mean(x * x, axis=-1, keepdims=True) + NORM_EPS) * g


def _dot(a, b):
    return jnp.dot(a, b, preferred_element_type=F32)


def _dot_nt(a, b):
    return lax.dot_general(a, b, (((1,), (1,)), ((), ())), preferred_element_type=F32)


def _ffn_kernel(*refs, mode, first_tiles=0):
    if mode == "plain":
        x_ref, gn_ref, wg_ref, wu_ref, wd_ref, out_ref, acc_ref, act_a, act_b = refs
        x = x_ref[...]
    elif mode == "plain2":
        xa_ref, xb_ref, gn_ref, wg_ref, wu_ref, wd_ref, out_ref, acc_ref, act_a, act_b = refs
        x = jnp.where(pl.program_id(0) < first_tiles, xa_ref[...], xb_ref[...])
    elif mode == "proj":
        (x_ref, o_ref, wo_ref, gn_ref, wg_ref, wu_ref, wd_ref, out_ref, acc_ref, act_a,
         act_b) = refs
        if len(o_ref.shape) == 3:
            o = jnp.concatenate([o_ref[n] for n in range(o_ref.shape[0])], axis=-1)
        else:
            o = o_ref[...]
        x = x_ref[...] + _dot(o, wo_ref[...])
    else:
        (x_ref, o0, o1, o2, l0, l1, l2, wo_ref, gn_ref, wg_ref, wu_ref, wd_ref,
         out_ref, acc_ref, act_a, act_b, *scr) = refs
        tm, width = x_ref.shape[0], wo_ref.shape[0]

        def token_major(ref, dil, scr_ref):
            slabs = width // LANES
            for r in range(dil):
                for s in range(slabs):
                    col = r * width + s * LANES
                    scr_ref[s, pl.ds(r, tm // dil, stride=dil), :] = ref[:, col:col + LANES]
            return jnp.concatenate([scr_ref[s] for s in range(slabs)], axis=-1)

        dils = [d for _, d in C_PATTERNS]
        assert dils[0] == 1
        oa, la = o0[...], l0[...]
        ob, lb = token_major(o1, dils[1], scr[0]), token_major(l1, dils[1], scr[1])
        oc, lc = token_major(o2, dils[2], scr[2]), token_major(l2, dils[2], scr[3])
        m = jnp.maximum(jnp.maximum(la, lb), lc)
        ea, eb, ec = jnp.exp(la - m), jnp.exp(lb - m), jnp.exp(lc - m)
        o = (ea * oa + eb * ob + ec * oc) / (ea + eb + ec)
        x = x_ref[...] + _dot(o.astype(BF16), wo_ref[...])

    h = _rms(x, gn_ref[...]).astype(BF16)

    def gate_up(c, a_ref):
        cols = pl.ds(pl.multiple_of(c * FF_CHUNK, FF_CHUNK), FF_CHUNK)
        g = _dot(h, wg_ref[:, cols])
        u = _dot(h, wu_ref[:, cols])
        a_ref[...] = (g * jax.nn.sigmoid(g) * u).astype(BF16)

    def down(c, a_ref):
        acc_ref[...] += _dot(a_ref[...], wd_ref[c])

    assert N_FF_CHUNKS % 2 == 1
    gate_up(0, act_a)
    gate_up(1, act_b)
    acc_ref[...] = _dot(act_a[...], wd_ref[0])

    def pair(i, carry):
        c = 2 * i + 1
        gate_up(c + 1, act_a)
        down(c, act_b)
        gate_up(c + 2, act_b)
        down(c + 1, act_a)
        return carry

    lax.fori_loop(0, N_FF_CHUNKS // 2 - 1, pair, 0)
    gate_up(N_FF_CHUNKS - 1, act_a)
    down(N_FF_CHUNKS - 2, act_b)
    down(N_FF_CHUNKS - 1, act_a)
    out_ref[...] = x + 0.5 * acc_ref[...]


def _ffn_weights(wg, wu, wd):
    wg3 = wg.astype(BF16)
    wu3 = wu.astype(BF16)
    wd3 = wd.astype(BF16).reshape(N_FF_CHUNKS, FF_CHUNK, D_MODEL)
    return wg3, wu3, wd3


def _ffn_call(x, gn, wg, wu, wd, pre=None, tiles=None):
    tm = TOKEN_TILE
    xs = x if isinstance(x, tuple) else (x,)
    off, n_tiles = tiles if tiles is not None else (0, sum(a.shape[0] for a in xs) // tm)
    wg3, wu3, wd3 = _ffn_weights(wg, wu, wd)
    row = lambda w: pl.BlockSpec((tm, w), lambda i: (i + off, 0))
    ffn_args = [gn.reshape(1, D_MODEL), wg3, wu3, wd3]
    ffn_specs = [_const_spec((1, D_MODEL)), _const_spec(wg3.shape), _const_spec(wu3.shape),
                 _const_spec(wd3.shape)]
    scratch = []
    first_tiles = 0
    if len(xs) == 2:
        assert pre is None and tiles is None
        first_tiles = xs[0].shape[0] // tm
        last = xs[1].shape[0] // tm - 1
        mode, args = "plain2", list(xs)
        specs = [pl.BlockSpec((tm, D_MODEL), lambda i: (jnp.minimum(i, first_tiles - 1), 0)),
                 pl.BlockSpec((tm, D_MODEL), lambda i: (jnp.clip(i - first_tiles, 0, last), 0))]
    elif pre is None:
        mode, args, specs = "plain", [x], [row(D_MODEL)]
    elif len(pre) == 2:
        o, wo = pre
        wo = wo.astype(BF16)
        mode = "proj"
        args = [x, o, wo]
        o_spec = row(o.shape[1]) if o.ndim == 2 else pl.BlockSpec(
            (o.shape[0], tm, o.shape[2]), lambda i: (0, i + off, 0))
        specs = [row(D_MODEL), o_spec, _const_spec(wo.shape)]
    else:
        outs, lses, wo = pre
        wo = wo.astype(BF16)
        mode = "merge"
        args = [x, *outs, *lses, wo]
        width = wo.shape[0]
        dilated = [pl.BlockSpec((tm // d, d * width), lambda i: (i + off, 0)) for _, d in C_PATTERNS]
        specs = [row(D_MODEL)] + dilated * 2 + [_const_spec(wo.shape)]
        scratch = [pltpu.VMEM((width // LANES, tm, LANES), F32)] * 4
    return pl.pallas_call(
        functools.partial(_ffn_kernel, mode=mode, first_tiles=first_tiles),
        grid=(n_tiles,),
        in_specs=specs + ffn_specs,
        out_specs=pl.BlockSpec((tm, D_MODEL), lambda i: (i, 0)),
        out_shape=jax.ShapeDtypeStruct((n_tiles * tm, D_MODEL), F32),
        scratch_shapes=[pltpu.VMEM((tm, D_MODEL), F32)] + [pltpu.VMEM((tm, FF_CHUNK), BF16)] * 2
        + scratch,
        compiler_params=_params(("parallel",)),
        name="ffn_" + mode,
    )(*args, *ffn_args)


def _head_norm(t, width):
    return lax.rsqrt(jnp.sum(t * t, axis=-1, keepdims=True) * (1.0 / width) + NORM_EPS)


def _segment_ones(seg):
    n = MXU_TILE // seg
    return jnp.kron(jnp.eye(n, dtype=F32), jnp.ones((seg, seg), F32)).astype(BF16)


def _segment_rsqrt(t, e_ref, width):
    sq = t * t
    hi = sq.astype(BF16)
    lo = (sq - hi.astype(F32)).astype(BF16)
    e = e_ref[...]
    sums = [_dot(hi[:, b:b + MXU_TILE], e) + _dot(lo[:, b:b + MXU_TILE], e)
            for b in range(0, t.shape[-1], MXU_TILE)]
    ssq = sums[0] if len(sums) == 1 else jnp.concatenate(sums, axis=-1)
    return lax.rsqrt(ssq * (1.0 / width) + NORM_EPS)


def _store_vt(vt_ref, n, v_t):
    dv, rows = v_t.shape
    vt_ref[n, 0, :dv, :] = v_t.astype(BF16)
    first = lax.broadcasted_iota(jnp.int32, (BF16_SUBLANES, rows), 0) == 0
    vt_ref[n, 0, dv:, :] = jnp.where(first, 1.0, 0.0).astype(BF16)


def _proj_a_kernel(x_ref, gn_ref, w_ref, gain_ref, e_ref, q_ref, k_ref, v_ref):
    hd = HEAD_DIM
    nqk = (A_HEADS + A_KV_HEADS) * hd
    h = _rms(x_ref[...], gn_ref[...]).astype(BF16)
    y = _dot(h, w_ref[...])
    qk = y[:, :nqk]
    qk = (qk * _segment_rsqrt(qk, e_ref, hd) * gain_ref[...]).astype(BF16)
    for n in range(A_HEADS):
        q_ref[n] = qk[:, n * hd:(n + 1) * hd]
    for n in range(A_KV_HEADS):
        k_ref[n] = qk[:, (A_HEADS + n) * hd:(A_HEADS + n + 1) * hd]
        v_ref[n] = y[:, nqk + n * hd: nqk + (n + 1) * hd].astype(BF16)


def _proj_a_call(x, gn, wqkv, gq, gk):
    t_rows = x.shape[0]
    tm, hd = TOKEN_TILE, HEAD_DIM
    w = wqkv.astype(BF16)
    gain = jnp.concatenate([jnp.tile(gq * (hd ** -0.5 * LOG2E), A_HEADS), jnp.tile(gk, A_KV_HEADS)])
    ones = _segment_ones(hd)
    head_major = lambda n: pl.BlockSpec((n, tm, hd), lambda i: (0, i, 0))
    return pl.pallas_call(
        _proj_a_kernel,
        grid=(t_rows // tm,),
        in_specs=[pl.BlockSpec((tm, D_MODEL), lambda i: (i, 0)), _const_spec((1, D_MODEL)),
                  _const_spec(w.shape), _const_spec((1, gain.shape[0])), _const_spec(ones.shape)],
        out_specs=[head_major(A_HEADS), head_major(A_KV_HEADS), head_major(A_KV_HEADS)],
        out_shape=[jax.ShapeDtypeStruct((A_HEADS, t_rows, hd), BF16),
                   jax.ShapeDtypeStruct((A_KV_HEADS, t_rows, hd), BF16),
                   jax.ShapeDtypeStruct((A_KV_HEADS, t_rows, hd), BF16)],
        compiler_params=_params(("parallel",)),
        name="proj_a",
    )(x, gn.reshape(1, D_MODEL), w, gain.reshape(1, -1), ones)


def _proj_d_kernel(x_ref, gn_ref, w_ref, e_ref, cq_ref, sq_ref, ck_ref, sk_ref, qt_ref, k_ref, vt_ref):
    hd = HEAD_DIM
    nq, nk = D_HEADS * hd, D_KV_HEADS * hd
    tq = qt_ref.shape[-1]
    h = _rms(x_ref[...], gn_ref[...]).astype(BF16)
    y = _dot(h, w_ref[...])
    swapped = nq + 2 * nk
    t, ts = y[:, :nq + nk], y[:, swapped:swapped + nq + nk]
    r = _segment_rsqrt(t, e_ref, hd)
    for b in range((nq + nk) // LANES):
        is_q = b < nq // LANES
        c_ref, s_ref = (cq_ref, sq_ref) if is_q else (ck_ref, sk_ref)
        sl = slice(b * LANES, (b + 1) * LANES)
        blk = (t[:, sl] * c_ref[...] + ts[:, sl] * s_ref[...]) * r[:, sl]
        blk = blk.T.astype(BF16) if is_q else blk.astype(BF16)
        for half in range(LANES // hd):
            n = b * (LANES // hd) + half
            if is_q:
                for j in range(qt_ref.shape[1]):
                    qt_ref[n, j] = blk[half * hd:(half + 1) * hd, j * tq:(j + 1) * tq]
            else:
                k_ref[n - D_HEADS] = blk[:, half * hd:(half + 1) * hd]
    v_t = y[:, nq + nk: nq + 2 * nk].T
    for n in range(D_KV_HEADS):
        _store_vt(vt_ref, n, v_t[n * hd:(n + 1) * hd])


def _swap_halves_cols(w, group):
    k, n = w.shape
    return w.reshape(k, n // group, 2, group // 2)[:, :, ::-1, :].reshape(k, n)


def _position_spec(geo, tm, width):
    p_len, s_len, _ = geo
    p_tiles, s_tiles = p_len // tm, s_len // tm
    return pl.BlockSpec(
        (tm, width), lambda i: (jnp.where(i < p_tiles, i, (i - p_tiles) % s_tiles), 0))


def _rope_tables(pos, gain, scale):
    half = gain.shape[-1] // 2
    inv = ROPE_THETA ** (-jnp.arange(half, dtype=F32) / half)
    ang = pos.astype(F32)[:, None] * inv[None, :]
    cos, sin = lax.optimization_barrier((jnp.cos(ang), jnp.sin(ang)))
    g_sw = jnp.concatenate([gain[half:], gain[:half]])
    c = jnp.concatenate([cos, cos], axis=-1) * gain[None, :] * scale
    s = jnp.concatenate([-sin, sin], axis=-1) * g_sw[None, :] * scale
    return c, s


def _proj_d_call(x, gn, wqkv, gq, gk, pos, geo, tq):
    t_rows = x.shape[0]
    tm, hd = TOKEN_TILE, HEAD_DIM
    nq, nk = D_HEADS * hd, D_KV_HEADS * hd
    half = hd // 2
    w = jnp.concatenate([wqkv, _swap_halves_cols(wqkv[:, :nq + nk], half)], axis=1).astype(BF16)
    rows, cols = pos // GRID_W, pos % GRID_W

    def tables(g, scale):
        cr, sr = _rope_tables(rows, g[:half], scale)
        cc, sc = _rope_tables(cols, g[half:], scale)
        reps = LANES // hd
        return jnp.concatenate([cr, cc] * reps, axis=-1), jnp.concatenate([sr, sc] * reps, axis=-1)

    cq, sq = tables(gq, hd ** -0.5 * LOG2E)
    ck, sk = tables(gk, 1.0)
    ones = _segment_ones(hd)
    head_major = lambda n: pl.BlockSpec((n, tm, hd), lambda i: (0, i, 0))
    tab = _position_spec(geo, tm, LANES)
    return pl.pallas_call(
        _proj_d_kernel,
        grid=(t_rows // tm,),
        in_specs=[pl.BlockSpec((tm, D_MODEL), lambda i: (i, 0)), _const_spec((1, D_MODEL)),
                  _const_spec(w.shape), _const_spec(ones.shape), tab, tab, tab, tab],
        out_specs=[pl.BlockSpec((D_HEADS, tm // tq, hd, tq), lambda i: (0, i, 0, 0)),
                   head_major(D_KV_HEADS),
                   pl.BlockSpec((D_KV_HEADS, 1, hd + BF16_SUBLANES, tm), lambda i: (0, i, 0, 0))],
        out_shape=[jax.ShapeDtypeStruct((D_HEADS, t_rows // tq, hd, tq), BF16),
                   jax.ShapeDtypeStruct((D_KV_HEADS, t_rows, hd), BF16),
                   jax.ShapeDtypeStruct((D_KV_HEADS, t_rows // tm, hd + BF16_SUBLANES, tm), BF16)],
        compiler_params=_params(("parallel",)),
        name="proj_d",
    )(x, gn.reshape(1, D_MODEL), w, ones, cq, sq, ck, sk)


def _proj_b_kernel(x_ref, gn_ref, w1_ref, gcq_ref, gckv_ref, wq_ref, wqs_ref, wk_ref, wv_ref,
                   cq_ref, sq_ref, ck_ref, sk_ref, qt_ref, k_ref, vt_ref):
    h = _rms(x_ref[...], gn_ref[...]).astype(BF16)
    y1 = _dot(h, w1_ref[...])
    c_q = _rms(y1[:, :B_Q_LORA], gcq_ref[...]).astype(BF16)
    c_kv = _rms(y1[:, B_Q_LORA:B_Q_LORA + B_KV_LORA], gckv_ref[...]).astype(BF16)
    off = B_Q_LORA + B_KV_LORA
    kr = y1[:, off:off + B_PAD]
    krs = y1[:, off + B_PAD:off + 2 * B_PAD]
    q = _dot(c_q, wq_ref[...])
    qs = _dot(c_q, wqs_ref[...])
    kn = _dot(c_kv, wk_ref[...])
    v = _dot(c_kv, wv_ref[...])
    cq, sq, ck, sk = cq_ref[...], sq_ref[...], ck_ref[...], sk_ref[...]
    krs_s = krs * sk
    for n in range(B_HEADS):
        t = q[:, n * B_PAD:(n + 1) * B_PAD]
        ts = qs[:, n * B_PAD:(n + 1) * B_PAD]
        qt_ref[n, 0] = ((t * cq + ts * sq) * _head_norm(t, B_QK)).T.astype(BF16)
        t = kn[:, n * B_PAD:(n + 1) * B_PAD] + kr
        k_ref[n] = ((t * ck + krs_s) * _head_norm(t, B_QK)).astype(BF16)
    v_t = v.T
    for n in range(B_HEADS):
        _store_vt(vt_ref, n, v_t[n * B_V:(n + 1) * B_V])


def _pad_heads(w, lo, width):
    k, nh, _ = w.shape
    out = jnp.zeros((k, nh, B_PAD), w.dtype).at[:, :, lo:lo + width].set(w)
    return out.reshape(k, nh * B_PAD)


def _proj_b_call(x, gn, wdq, gcq, wuq, wdkv, gckv, wukv, gq, gk, pos, geo):
    t_rows = x.shape[0]
    tm = TOKEN_TILE
    wuq3 = wuq.reshape(B_Q_LORA, B_HEADS, B_QK)
    wq = _pad_heads(wuq3, 0, B_QK).astype(BF16)
    wuq_rope_sw = _swap_halves_cols(wuq3[:, :, B_NOPE:].reshape(B_Q_LORA, -1), B_ROPE)
    wqs = _pad_heads(wuq_rope_sw.reshape(B_Q_LORA, B_HEADS, B_ROPE), B_NOPE, B_ROPE).astype(BF16)
    wukv3 = wukv.reshape(B_KV_LORA, B_HEADS, B_NOPE + B_V)
    wk = _pad_heads(wukv3[:, :, :B_NOPE], 0, B_NOPE).astype(BF16)
    wv = wukv3[:, :, B_NOPE:].reshape(B_KV_LORA, B_HEADS * B_V).astype(BF16)
    w_kr = wdkv[:, B_KV_LORA:]
    kr_pad = _pad_heads(w_kr[:, None, :], B_NOPE, B_ROPE)
    krs_pad = _pad_heads(_swap_halves_cols(w_kr, B_ROPE)[:, None, :], B_NOPE, B_ROPE)
    w1 = jnp.concatenate([wdq, wdkv[:, :B_KV_LORA], kr_pad, krs_pad], axis=1).astype(BF16)

    def tables(g, scale):
        c, s = _rope_tables(pos, g[B_NOPE:], scale)
        n_pos = pos.shape[0]
        zeros = jnp.zeros((n_pos, B_PAD - B_QK), F32)
        c_full = jnp.concatenate(
            [jnp.broadcast_to(g[None, :B_NOPE] * scale, (n_pos, B_NOPE)), c, zeros], axis=-1)
        s_full = jnp.concatenate([jnp.zeros((n_pos, B_NOPE), F32), s, zeros], axis=-1)
        return c_full, s_full

    cq, sq = tables(gq, B_QK ** -0.5 * LOG2E)
    ck, sk = tables(gk, 1.0)
    tab = _position_spec(geo, tm, B_PAD)
    head_major = lambda w_: pl.BlockSpec((B_HEADS, tm, w_), lambda i: (0, i, 0))
    return pl.pallas_call(
        _proj_b_kernel,
        grid=(t_rows // tm,),
        in_specs=[pl.BlockSpec((tm, D_MODEL), lambda i: (i, 0)), _const_spec((1, D_MODEL)),
                  _const_spec(w1.shape), _const_spec((1, B_Q_LORA)), _const_spec((1, B_KV_LORA)),
                  _const_spec(wq.shape), _const_spec(wqs.shape), _const_spec(wk.shape),
                  _const_spec(wv.shape), tab, tab, tab, tab],
        out_specs=[pl.BlockSpec((B_HEADS, 1, B_PAD, tm), lambda i: (0, i, 0, 0)),
                   head_major(B_PAD),
                   pl.BlockSpec((B_HEADS, 1, B_V + BF16_SUBLANES, tm), lambda i: (0, i, 0, 0))],
        out_shape=[jax.ShapeDtypeStruct((B_HEADS, t_rows // tm, B_PAD, tm), BF16),
                   jax.ShapeDtypeStruct((B_HEADS, t_rows, B_PAD), BF16),
                   jax.ShapeDtypeStruct((B_HEADS, t_rows // tm, B_V + BF16_SUBLANES, tm), BF16)],
        compiler_params=_params(("parallel",)),
        name="proj_b",
    )(x, gn.reshape(1, D_MODEL), w1, gcq.reshape(1, -1), gckv.reshape(1, -1), wq, wqs, wk, wv,
      cq, sq, ck, sk)


def _proj_c_kernel(x_ref, gn_ref, w_ref, gq_ref, gk_ref, e_ref, *refs):
    hd = HEAD_DIM
    width = C_HEADS * hd
    outs, scr_ref = refs[:-1], refs[-1]
    tm = x_ref.shape[0]
    h = _rms(x_ref[...], gn_ref[...]).astype(BF16)
    y = _dot(h, w_ref[...])
    gains = (gq_ref[...], gk_ref[...])
    for g, (_, dil) in enumerate(C_PATTERNS):
        for j in range(3):
            base = (g * 3 + j) * width
            val = y[:, base: base + width]
            if j < 2:
                val = val * _segment_rsqrt(val, e_ref, hd) * gains[j]
            out_ref = outs[g * 3 + j]
            if dil == 1:
                out_ref[...] = val.astype(BF16)
            else:
                for s in range(width // LANES):
                    scr_ref[s] = val[:, s * LANES:(s + 1) * LANES]
                for r in range(dil):
                    for s in range(width // LANES):
                        rows = scr_ref[s, pl.ds(r, tm // dil, stride=dil), :]
                        col = r * width + s * LANES
                        out_ref[:, col:col + LANES] = rows.astype(BF16)


def _proj_c_call(x, gn, wqkv, gq, gk):
    t_rows = x.shape[0]
    tm, hd = TOKEN_TILE, HEAD_DIM
    width = C_HEADS * hd
    w = wqkv.astype(BF16)
    ones = _segment_ones(hd)
    specs, shapes = [], []
    for _, dil in C_PATTERNS:
        specs += [pl.BlockSpec((tm // dil, dil * width), lambda i: (i, 0))] * 3
        shapes += [jax.ShapeDtypeStruct((t_rows // dil, dil * width), BF16)] * 3
    outs = pl.pallas_call(
        _proj_c_kernel,
        grid=(t_rows // tm,),
        in_specs=[pl.BlockSpec((tm, D_MODEL), lambda i: (i, 0)), _const_spec((1, D_MODEL)),
                  _const_spec(w.shape), _const_spec((1, width)), _const_spec((1, width)),
                  _const_spec(ones.shape)],
        out_specs=specs,
        out_shape=shapes,
        scratch_shapes=[pltpu.VMEM((width // LANES, tm, LANES), F32)],
        compiler_params=_params(("parallel",)),
        name="proj_c",
    )(x, gn.reshape(1, D_MODEL), w, jnp.tile(gq * (hd ** -0.5 * LOG2E), C_HEADS).reshape(1, width),
      jnp.tile(gk, C_HEADS).reshape(1, width), ones)
    return [outs[3 * g:3 * g + 3] for g in range(len(C_PATTERNS))]


def _band_masked_dist(row0, lo, hi, sub, w):
    nk = sub + 2 * w
    r = lax.broadcasted_iota(jnp.int32, (sub, nk), 0)
    c = lax.broadcasted_iota(jnp.int32, (sub, nk), 1)
    dist = jnp.abs(r + w - c)
    jpos = row0 - w + lax.broadcasted_iota(jnp.int32, (1, nk), 1)
    outside = jnp.where(jpos < lo, w + 1, 0) + jnp.where(jpos >= hi, w + 1, 0)
    return jnp.where(dist + outside <= w, dist.astype(F32), MASK_DIST)


def _two_stage(n_items, first, second, ahead=2):
    pending = {i: first(i) for i in range(min(ahead, n_items))}
    results = []
    for i in range(n_items):
        if i + ahead < n_items:
            pending[i + ahead] = first(i + ahead)
        results.append(second(i, pending.pop(i)))
    return results


def _with_ones(vv):
    return jnp.concatenate([vv, jnp.ones((vv.shape[0], BF16_SUBLANES), BF16)], axis=-1)


def _band_softmax_pv(s, vv_ones, sink):
    m = jnp.max(s, axis=-1, keepdims=True)
    if sink is not None:
        m = jnp.maximum(m, sink)
    hd = vv_ones.shape[-1] - BF16_SUBLANES
    pv = _dot(jnp.exp2(s - m).astype(BF16), vv_ones)
    den = pv[:, hd:hd + 1]
    if sink is not None:
        den = den + jnp.exp2(sink - m)
    return pv[:, :hd] / den, m + jnp.log2(den)


def _attn_a_kernel(slope_ref, sink_ref, q_ref, kp_ref, kc_ref, kn_ref, vp_ref, vc_ref, vn_ref,
                   o_ref, *, tq, p_len, s_len):
    w, sub = A_HALF_WINDOW, BAND_SUB
    group = A_HEADS // A_KV_HEADS
    kvh = pl.program_id(0)
    row0 = pl.program_id(1) * tq
    lo, hi = _seq_bounds(row0, p_len, s_len)
    kk = jnp.concatenate([kp_ref[0], kc_ref[0], kn_ref[0]], axis=0)
    vv = _with_ones(jnp.concatenate([vp_ref[0], vc_ref[0], vn_ref[0]], axis=0))
    sink = jnp.concatenate(
        [jnp.full((sub, 1), sink_ref[kvh * group + g], F32) for g in range(group)], axis=0)

    def keys(r):
        return slice(r * sub, (r + 1) * sub + 2 * w)

    def logits(r):
        md = _band_masked_dist(row0 + r * sub, lo, hi, sub, w)
        bias = jnp.concatenate([md * slope_ref[kvh * group + g] for g in range(group)], axis=0)
        q = q_ref[:, r * sub:(r + 1) * sub, :].reshape(group * sub, HEAD_DIM)
        return _dot_nt(q, kk[keys(r)]) + bias

    tiles = _two_stage(tq // sub, logits, lambda r, s: _band_softmax_pv(s, vv[keys(r)], sink)[0])
    outs = [[o[g * sub:(g + 1) * sub] for o in tiles] for g in range(group)]
    o_ref[...] = jnp.concatenate(
        [jnp.concatenate(rows, axis=0) for rows in outs], axis=-1).astype(BF16)


def _attn_a_call(q, k, v, sink, geo):
    p_len, s_len, t_rows = geo
    tq, w, hd = 1024, A_HALF_WINDOW, HEAD_DIM
    group = A_HEADS // A_KV_HEADS
    per = tq // w
    last = t_rows // w - 1
    prev = pl.BlockSpec((1, w, hd), lambda h, i: (h, jnp.maximum(i * per - 1, 0), 0))
    cur = pl.BlockSpec((1, tq, hd), lambda h, i: (h, i, 0))
    nxt = pl.BlockSpec((1, w, hd), lambda h, i: (h, jnp.minimum((i + 1) * per, last), 0))
    smem = pl.BlockSpec(memory_space=pltpu.SMEM)
    slopes = jnp.asarray(-LOG2E * _alibi_slopes(A_HEADS), F32)
    return pl.pallas_call(
        functools.partial(_attn_a_kernel, tq=tq, p_len=p_len, s_len=s_len),
        grid=(A_KV_HEADS, t_rows // tq),
        in_specs=[smem, smem, pl.BlockSpec((group, tq, hd), lambda h, i: (h, i, 0)),
                  prev, cur, nxt, prev, cur, nxt],
        out_specs=pl.BlockSpec((tq, group * hd), lambda h, i: (i, h)),
        out_shape=jax.ShapeDtypeStruct((t_rows, A_HEADS * hd), BF16),
        compiler_params=_params(("parallel", "parallel")),
        name="attn_a",
    )(slopes, sink.astype(F32) * LOG2E, q, k, k, k, v, v, v)


def _attn_c_kernel(q_ref, kp_ref, kc_ref, kn_ref, vp_ref, vc_ref, vn_ref, o_ref, lse_ref,
                   *, tq, dil, p_len, s_len):
    w, hd, sub = C_STEPS, HEAD_DIM, min(BAND_SUB, tq)
    row0 = pl.program_id(0) * tq
    lo, hi = _seq_bounds(row0, p_len // dil, s_len // dil)
    slopes = -LOG2E * dil * _alibi_slopes(C_HEADS)
    mds = [_band_masked_dist(row0 + r * sub, lo, hi, sub, w) for r in range(tq // sub)]
    n_sub = tq // sub

    heads = {}

    def head_rows(n):
        if n not in heads:
            hs = slice(n * hd, (n + 1) * hd)
            kk = jnp.concatenate([kp_ref[:, hs], kc_ref[:, hs], kn_ref[:, hs]], axis=0)
            vv = jnp.concatenate([vp_ref[:, hs], vc_ref[:, hs], vn_ref[:, hs]], axis=0)
            heads[n] = (q_ref[:, hs], kk, _with_ones(vv))
        return heads[n]

    def logits(i):
        n, r = divmod(i, n_sub)
        q, kk, _ = head_rows(n)
        keys = kk[r * sub:(r + 1) * sub + 2 * w]
        return _dot_nt(q[r * sub:(r + 1) * sub], keys) + mds[r] * float(slopes[n])

    def attend(i, s):
        n, r = divmod(i, n_sub)
        o, lse2 = _band_softmax_pv(s, head_rows(n)[2][r * sub:(r + 1) * sub + 2 * w], None)
        return o, jnp.broadcast_to(lse2 * (1.0 / LOG2E), (sub, hd))

    tiles = _two_stage(C_HEADS * n_sub, logits, attend, ahead=3 if n_sub > 1 else 2)
    outs = [jnp.concatenate([t[0] for t in tiles[n * n_sub:(n + 1) * n_sub]], axis=0)
            for n in range(C_HEADS)]
    lses = [jnp.concatenate([t[1] for t in tiles[n * n_sub:(n + 1) * n_sub]], axis=0)
            for n in range(C_HEADS)]
    o_ref[...] = jnp.concatenate(outs, axis=-1)
    lse_ref[...] = jnp.concatenate(lses, axis=-1)


def _attn_c_call(q, k, v, dil, geo):
    p_len, s_len, t_rows = geo
    w, width = C_STEPS, C_HEADS * HEAD_DIM
    rows = t_rows // dil
    tq = min(512, s_len // dil)
    per = tq // w
    last = rows // w - 1
    prev = pl.BlockSpec((w, width), lambda i, r: (jnp.maximum(i * per - 1, 0), r))
    cur = pl.BlockSpec((tq, width), lambda i, r: (i, r))
    nxt = pl.BlockSpec((w, width), lambda i, r: (jnp.minimum((i + 1) * per, last), r))
    shape = jax.ShapeDtypeStruct((rows, dil * width), F32)
    return pl.pallas_call(
        functools.partial(_attn_c_kernel, tq=tq, dil=dil, p_len=p_len, s_len=s_len),
        grid=(rows // tq, dil),
        in_specs=[cur, prev, cur, nxt, prev, cur, nxt],
        out_specs=[cur, cur],
        out_shape=[shape, shape],
        compiler_params=_params(("parallel", "parallel")),
        name="attn_c_d%d" % dil,
    )(q, k, k, k, v, v, v)


def _attn_dense_kernel(qt_ref, k_ref, vt_ref, o_ref, sa_ref, sb_ref, sc_ref, sd_ref, acc_ref,
                       *, nq, tq, tk, p_len, s_len):
    dv = vt_ref.shape[-2] - BF16_SUBLANES
    width = nq * tq
    shift = int(math.log2(tk))

    def first_chunk(j):
        row0 = (pl.program_id(1) * Q_TILES_PER_STEP + j) * tq
        lo, hi = _seq_bounds(row0, p_len, s_len)
        return lax.shift_right_logical(lo, shift), lax.shift_right_logical(hi - lo, shift)

    def scores(j, c, st_ref):
        start = pl.multiple_of(c * tk, tk)
        qt = jnp.concatenate([qt_ref[g, j] for g in range(nq)], axis=-1)
        st = _dot(k_ref[0, pl.ds(start, tk), :], qt).astype(BF16)
        st_ref[...] = st
        return jnp.max(st, axis=0, keepdims=True).astype(F32)

    def update(c, st_ref, mc, m, acc):
        m_new = jnp.maximum(m, mc)
        alpha = jnp.exp2(m - m_new)
        pt = jnp.exp2(st_ref[...] - m_new.astype(BF16))
        acc_ref[...] = alpha * acc_ref[...] + _dot(vt_ref[0, c], pt)
        return m_new, acc

    def q_tile(j, mcs, start_next):
        first, n_chunks = first_chunk(j)

        def quad(i, carry):
            m, acc, mc_a, mc_b = carry
            c = first + 4 * i
            mc_c = scores(j, c + 2, sc_ref)
            m, acc = update(c, sa_ref, mc_a, m, acc)
            mc_d = scores(j, c + 3, sd_ref)
            m, acc = update(c + 1, sb_ref, mc_b, m, acc)
            mc_a = scores(j, c + 4, sa_ref)
            m, acc = update(c + 2, sc_ref, mc_c, m, acc)
            mc_b = scores(j, c + 5, sb_ref)
            m, acc = update(c + 3, sd_ref, mc_d, m, acc)
            return m, acc, mc_a, mc_b

        acc_ref[...] = jnp.zeros_like(acc_ref)
        carry = (jnp.full((1, width), NEG_INF, F32), 0, *mcs)
        n_body = lax.shift_right_logical(n_chunks, 2) - 1
        carry = lax.fori_loop(0, lax.shift_right_logical(n_body, 1),
                              lambda i, cr: quad(2 * i + 1, quad(2 * i, cr)), carry)
        carry = lax.fori_loop(0, n_body & 1, lambda _, cr: quad(n_body - 1, cr), carry)
        m, acc, mc_a, mc_b = carry
        c = first + n_chunks - 4
        mc_c = scores(j, c + 2, sc_ref)
        m, acc = update(c, sa_ref, mc_a, m, acc)
        mc_d = scores(j, c + 3, sd_ref)
        m, acc = update(c + 1, sb_ref, mc_b, m, acc)
        if start_next:
            first_next, _ = first_chunk(j + 1)
            mc_a = scores(j + 1, first_next, sa_ref)
        m, acc = update(c + 2, sc_ref, mc_c, m, acc)
        if start_next:
            mc_b = scores(j + 1, first_next + 1, sb_ref)
        m, acc = update(c + 3, sd_ref, mc_d, m, acc)
        o = acc_ref[:dv, :] / acc_ref[dv:dv + 1, :]
        rows = pl.ds(pl.multiple_of(j * tq, tq), tq)
        heads = [o[:, g * tq:(g + 1) * tq].T.astype(BF16) for g in range(nq)]
        if nq == 1:
            o_ref[0, rows, :] = heads[0]
        else:
            o_ref[rows, :] = jnp.concatenate(heads, axis=-1)
        return mc_a, mc_b

    first, _ = first_chunk(0)
    mcs = (scores(0, first, sa_ref), scores(0, first + 1, sb_ref))
    mcs = lax.fori_loop(0, Q_TILES_PER_STEP - 1, lambda j, mcs: q_tile(j, mcs, True), mcs)
    q_tile(Q_TILES_PER_STEP - 1, mcs, False)


def _attn_dense_call(qt, k, vt, geo):
    p_len, s_len, t_rows = geo
    hq, n_tiles, dq, tq = qt.shape
    hk, n_chunks, rows_v, tk = vt.shape
    dv = rows_v - BF16_SUBLANES
    nq, per = hq // hk, Q_TILES_PER_STEP
    assert s_len % (4 * tk) == 0 and p_len % (4 * tk) == 0 and n_tiles % per == 0
    return pl.pallas_call(
        functools.partial(_attn_dense_kernel, nq=nq, tq=tq, tk=tk, p_len=p_len, s_len=s_len),
        grid=(hk, n_tiles // per),
        in_specs=[pl.BlockSpec((nq, per, dq, tq), lambda h, i: (h, i, 0, 0)),
                  pl.BlockSpec((1, t_rows, dq), lambda h, i: (h, 0, 0)),
                  pl.BlockSpec((1, n_chunks, dv + BF16_SUBLANES, tk), lambda h, i: (h, 0, 0, 0))],
        out_specs=(pl.BlockSpec((1, per * tq, dv), lambda h, i: (h, i, 0)) if nq == 1
                   else pl.BlockSpec((per * tq, nq * dv), lambda h, i: (i, h))),
        out_shape=jax.ShapeDtypeStruct((hq, t_rows, dv) if nq == 1 else (t_rows, hq * dv), BF16),
        scratch_shapes=[pltpu.VMEM((tk, nq * tq), BF16)] * 4 + [pltpu.VMEM((rows_v, nq * tq), F32)],
        compiler_params=_params(("parallel", "parallel")),
        name="attn_dense_%d" % dq,
    )(qt, k, vt)


def _trunk(x, p, geo, pos):
    p_tiles = geo[0] // TOKEN_TILE
    all_tiles = geo[2] // TOKEN_TILE
    for i in range(DEPTH):
        mixer, j = i % 4, i // 4
        x = _ffn_call(x, p["norm_ffn1"][i], p["ffn1_wg"][i], p["ffn1_wu"][i], p["ffn1_wd"][i])
        gn = p["norm_mix"][i]
        if mixer == 0:
            q, k, v = _proj_a_call(x, gn, p["a_wqkv"][j], p["a_gq"][j], p["a_gk"][j])
            pre = (_attn_a_call(q, k, v, p["a_sink"][j], geo), p["a_wo"][j])
        elif mixer == 1:
            qt, k, vt = _proj_b_call(x, gn, p["b_wdq"][j], p["b_gcq"][j], p["b_wuq"][j], p["b_wdkv"][j],
                                     p["b_gckv"][j], p["b_wukv"][j], p["b_gq"][j], p["b_gk"][j], pos, geo)
            pre = (_attn_dense_call(qt, k, vt, geo), p["b_wo"][j])
        elif mixer == 2:
            qkv = _proj_c_call(x, gn, p["c_wqkv"][j], p["c_gq"][j], p["c_gk"][j])
            res = [_attn_c_call(*qkv[g], dil, geo) for g, (_, dil) in enumerate(C_PATTERNS)]
            pre = ([r[0] for r in res], [r[1] for r in res], p["c_wo"][j])
        else:
            qt, k, vt = _proj_d_call(x, gn, p["d_wqkv"][j], p["d_gq"][j], p["d_gk"][j], pos, geo, tq=128)
            pre = (_attn_dense_call(qt, k, vt, geo), p["d_wo"][j])
        ffn2 = (p["norm_ffn2"][i], p["ffn2_wg"][i], p["ffn2_wu"][i], p["ffn2_wd"][i])
        if i < DEPTH - 1:
            x = _ffn_call(x, *ffn2, pre=pre)
        else:
            x = (_ffn_call(x, *ffn2, pre=pre, tiles=(0, p_tiles)),
                 _ffn_call(x, *ffn2, pre=pre, tiles=(p_tiles, all_tiles - p_tiles)))
    return x


def kernel(x_prompt, x_sample, norm_ffn1, ffn1_wg, ffn1_wu, ffn1_wd, norm_mix, norm_ffn2, ffn2_wg, ffn2_wu, ffn2_wd, a_wqkv, a_gq, a_gk, a_sink, a_wo, b_wdq, b_gcq, b_wuq, b_wdkv, b_gckv, b_wukv, b_gq, b_gk, b_wo, c_wqkv, c_gq, c_gk, c_wo, d_wqkv, d_gq, d_gk, d_wo):
    p = dict(norm_ffn1=norm_ffn1, ffn1_wg=ffn1_wg, ffn1_wu=ffn1_wu, ffn1_wd=ffn1_wd,
             norm_mix=norm_mix, norm_ffn2=norm_ffn2, ffn2_wg=ffn2_wg, ffn2_wu=ffn2_wu,
             ffn2_wd=ffn2_wd, a_wqkv=a_wqkv, a_gq=a_gq, a_gk=a_gk, a_sink=a_sink, a_wo=a_wo,
             b_wdq=b_wdq, b_gcq=b_gcq, b_wuq=b_wuq, b_wdkv=b_wdkv, b_gckv=b_gckv,
             b_wukv=b_wukv, b_gq=b_gq, b_gk=b_gk, b_wo=b_wo, c_wqkv=c_wqkv, c_gq=c_gq,
             c_gk=c_gk, c_wo=c_wo, d_wqkv=d_wqkv, d_gq=d_gq, d_gk=d_gk, d_wo=d_wo)
    n_p, p_seq, _ = x_prompt.shape
    n_s, s_len, _ = x_sample.shape
    assert n_p == 1 and s_len & (s_len - 1) == 0 and p_seq % s_len == 0
    p_len = n_p * p_seq
    t_rows = p_len + n_s * s_len
    geo = (p_len, s_len, t_rows)
    pos = jnp.arange(p_len, dtype=jnp.int32)
    x = (x_prompt.reshape(p_len, D_MODEL), x_sample.reshape(n_s * s_len, D_MODEL))
    y_p, y_s = _trunk(x, p, geo, pos)
    return y_p.reshape(x_prompt.shape), y_s.reshape(x_sample.shape)
```

```python
import functools
import math

import numpy as np
import jax
import jax.numpy as jnp
from jax import lax
from jax.experimental import pallas as pl
from jax.experimental.pallas import tpu as pltpu

F32 = jnp.float32
BF16 = jnp.bfloat16

D_MODEL = 1024
DEPTH = 4
HEAD_DIM = 64
D_FF = 2816
NORM_EPS = 1e-6
NEG_INF = -1e30
GRID_W = 64
ROPE_THETA = 10000.0
LOG2E = math.log2(math.e)
LANES = 128
MXU_TILE = 256
BF16_SUBLANES = 16
BAND_SUB = 128
Q_TILES_PER_STEP = 8
MASK_DIST = 1e33

A_HEADS, A_KV_HEADS, A_HALF_WINDOW = 16, 4, 128
B_HEADS, B_Q_LORA, B_KV_LORA, B_NOPE, B_ROPE, B_V = 16, 512, 256, 64, 32, 64
B_QK = B_NOPE + B_ROPE
B_PAD = 128
C_HEADS = 8
C_PATTERNS = ((128, 1), (512, 4), (2048, 16))
C_STEPS = 64
D_HEADS, D_KV_HEADS = 16, 4

FF_CHUNK = 256
N_FF_CHUNKS = D_FF // FF_CHUNK
TOKEN_TILE = 512
FFN_TILE = 1024
VMEM_LIMIT = 56 * 1024 * 1024


def _alibi_slopes(n):
    return 2.0 ** (-8.0 * np.arange(1, n + 1) / n)


def _params(sem, vmem=VMEM_LIMIT):
    return pltpu.CompilerParams(dimension_semantics=sem, vmem_limit_bytes=vmem)


def _const_spec(shape):
    nd = len(shape)
    return pl.BlockSpec(tuple(shape), lambda *_: (0,) * nd, pipeline_mode=pl.Buffered(1))


def _seq_bounds(row0, p_len, s_len):
    in_prompt = row0 < p_len
    b = lax.shift_right_logical(jnp.maximum(row0 - p_len, 0), int(math.log2(s_len)))
    lo = jnp.where(in_prompt, 0, p_len + b * s_len)
    hi = jnp.where(in_prompt, p_len, lo + s_len)
    return lo, hi


def _rms(x, g):
    return x * lax.rsqrt(jnp.mean(x * x, axis=-1, keepdims=True) + NORM_EPS) * g


def _dot(a, b):
    return jnp.dot(a, b, preferred_element_type=F32)


def _dot_nt(a, b):
    return lax.dot_general(a, b, (((1,), (1,)), ((), ())), preferred_element_type=F32)


def _ffn_kernel(*refs, mode, first_tiles=0):
    if mode == "plain":
        x_ref, gn_ref, wg_ref, wu_ref, wd_ref, out_ref, acc_ref, act_a, act_b = refs
        x = x_ref[...]
    elif mode == "plain2":
        xa_ref, xb_ref, gn_ref, wg_ref, wu_ref, wd_ref, out_ref, acc_ref, act_a, act_b = refs
        x = jnp.where(pl.program_id(0) < first_tiles, xa_ref[...], xb_ref[...])
    elif mode == "proj":
        (x_ref, o_ref, wo_ref, gn_ref, wg_ref, wu_ref, wd_ref, out_ref, acc_ref, act_a,
         act_b) = refs
        if len(o_ref.shape) == 3:
            o = jnp.concatenate([o_ref[n] for n in range(o_ref.shape[0])], axis=-1)
        else:
            o = o_ref[...]
        x = x_ref[...] + _dot(o, wo_ref[...])
    else:
        (x_ref, o0, o1, o2, l0, l1, l2, wo_ref, gn_ref, wg_ref, wu_ref, wd_ref,
         out_ref, acc_ref, act_a, act_b, *scr) = refs
        tm, width = x_ref.shape[0], wo_ref.shape[0]

        def token_major(ref, dil, scr_ref):
            slabs = width // LANES
            for r in range(dil):
                for s in range(slabs):
                    col = r * width + s * LANES
                    scr_ref[s, pl.ds(r, tm // dil, stride=dil), :] = ref[:, col:col + LANES]
            return jnp.concatenate([scr_ref[s] for s in range(slabs)], axis=-1)

        dils = [d for _, d in C_PATTERNS]
        assert dils[0] == 1
        oa, la = o0[...], l0[...]
        ob, lb = token_major(o1, dils[1], scr[0]), token_major(l1, dils[1], scr[1])
        oc, lc = token_major(o2, dils[2], scr[2]), token_major(l2, dils[2], scr[3])
        m = jnp.maximum(jnp.maximum(la, lb), lc)
        ea, eb, ec = jnp.exp(la - m), jnp.exp(lb - m), jnp.exp(lc - m)
        o = (ea * oa + eb * ob + ec * oc) / (ea + eb + ec)
        x = x_ref[...] + _dot(o.astype(BF16), wo_ref[...])

    h = _rms(x, gn_ref[...]).astype(BF16)

    def gate_up(c, a_ref):
        cols = pl.ds(pl.multiple_of(c * FF_CHUNK, FF_CHUNK), FF_CHUNK)
        g = _dot(h, wg_ref[:, cols])
        u = _dot(h, wu_ref[:, cols])
        a_ref[...] = (g * jax.nn.sigmoid(g) * u).astype(BF16)

    def down(c, a_ref):
        acc_ref[...] += _dot(a_ref[...], wd_ref[c])

    assert N_FF_CHUNKS % 2 == 1
    gate_up(0, act_a)
    gate_up(1, act_b)
    acc_ref[...] = _dot(act_a[...], wd_ref[0])

    def pair(i, carry):
        c = 2 * i + 1
        gate_up(c + 1, act_a)
        down(c, act_b)
        gate_up(c + 2, act_b)
        down(c + 1, act_a)
        return carry

    lax.fori_loop(0, N_FF_CHUNKS // 2 - 1, pair, 0)
    gate_up(N_FF_CHUNKS - 1, act_a)
    down(N_FF_CHUNKS - 2, act_b)
    down(N_FF_CHUNKS - 1, act_a)
    out_ref[...] = x + 0.5 * acc_ref[...]


def _ffn_weights(wg, wu, wd):
    wg3 = wg.astype(BF16)
    wu3 = wu.astype(BF16)
    wd3 = wd.astype(BF16).reshape(N_FF_CHUNKS, FF_CHUNK, D_MODEL)
    return wg3, wu3, wd3


def _ffn_call(x, gn, wg, wu, wd, pre=None, rows=None):
    tm = TOKEN_TILE if pre is not None and len(pre) == 3 else FFN_TILE
    xs = x if isinstance(x, tuple) else (x,)
    off, n_tiles = (rows[0] // tm, rows[1] // tm) if rows is not None else (
        0, sum(a.shape[0] for a in xs) // tm)
    wg3, wu3, wd3 = _ffn_weights(wg, wu, wd)
    row = lambda w: pl.BlockSpec((tm, w), lambda i: (i + off, 0))
    ffn_args = [gn.reshape(1, D_MODEL), wg3, wu3, wd3]
    ffn_specs = [_const_spec((1, D_MODEL)), _const_spec(wg3.shape), _const_spec(wu3.shape),
                 _const_spec(wd3.shape)]
    scratch = []
    first_tiles = 0
    if len(xs) == 2:
        assert pre is None and rows is None
        first_tiles = xs[0].shape[0] // tm
        last = xs[1].shape[0] // tm - 1
        mode, args = "plain2", list(xs)
        specs = [pl.BlockSpec((tm, D_MODEL), lambda i: (jnp.minimum(i, first_tiles - 1), 0)),
                 pl.BlockSpec((tm, D_MODEL), lambda i: (jnp.clip(i - first_tiles, 0, last), 0))]
    elif pre is None:
        mode, args, specs = "plain", [x], [row(D_MODEL)]
    elif len(pre) == 2:
        o, wo = pre
        wo = wo.astype(BF16)
        mode = "proj"
        args = [x, o, wo]
        o_spec = row(o.shape[1]) if o.ndim == 2 else pl.BlockSpec(
            (o.shape[0], tm, o.shape[2]), lambda i: (0, i + off, 0))
        specs = [row(D_MODEL), o_spec, _const_spec(wo.shape)]
    else:
        outs, lses, wo = pre
        wo = wo.astype(BF16)
        mode = "merge"
        args = [x, *outs, *lses, wo]
        width = wo.shape[0]
        dilated = [pl.BlockSpec((tm // d, d * width), lambda i: (i + off, 0)) for _, d in C_PATTERNS]
        specs = [row(D_MODEL)] + dilated * 2 + [_const_spec(wo.shape)]
        scratch = [pltpu.VMEM((width // LANES, tm, LANES), F32)] * 4
    return pl.pallas_call(
        functools.partial(_ffn_kernel, mode=mode, first_tiles=first_tiles),
        grid=(n_tiles,),
        in_specs=specs + ffn_specs,
        out_specs=pl.BlockSpec((tm, D_MODEL), lambda i: (i, 0)),
        out_shape=jax.ShapeDtypeStruct((n_tiles * tm, D_MODEL), F32),
        scratch_shapes=[pltpu.VMEM((tm, D_MODEL), F32)] + [pltpu.VMEM((tm, FF_CHUNK), BF16)] * 2
        + scratch,
        compiler_params=_params(("parallel",)),
        name="ffn_" + mode,
    )(*args, *ffn_args)


def _head_norm(t, width):
    return lax.rsqrt(jnp.sum(t * t, axis=-1, keepdims=True) * (1.0 / width) + NORM_EPS)


def _segment_ones(seg):
    n = MXU_TILE // seg
    return jnp.kron(jnp.eye(n, dtype=F32), jnp.ones((seg, seg), F32)).astype(BF16)


def _segment_rsqrt(t, e_ref, width):
    sq = t * t
    hi = sq.astype(BF16)
    lo = (sq - hi.astype(F32)).astype(BF16)
    e = e_ref[...]
    sums = [_dot(hi[:, b:b + MXU_TILE], e) + _dot(lo[:, b:b + MXU_TILE], e)
            for b in range(0, t.shape[-1], MXU_TILE)]
    ssq = sums[0] if len(sums) == 1 else jnp.concatenate(sums, axis=-1)
    return lax.rsqrt(ssq * (1.0 / width) + NORM_EPS)


def _store_vt(vt_ref, n, v_t):
    dv, rows = v_t.shape
    vt_ref[n, 0, :dv, :] = v_t.astype(BF16)
    first = lax.broadcasted_iota(jnp.int32, (BF16_SUBLANES, rows), 0) == 0
    vt_ref[n, 0, dv:, :] = jnp.where(first, 1.0, 0.0).astype(BF16)


def _proj_a_kernel(x_ref, gn_ref, w_ref, gain_ref, e_ref, q_ref, k_ref, v_ref):
    hd = HEAD_DIM
    nqk = (A_HEADS + A_KV_HEADS) * hd
    h = _rms(x_ref[...], gn_ref[...]).astype(BF16)
    y = _dot(h, w_ref[...])
    qk = y[:, :nqk]
    qk = (qk * _segment_rsqrt(qk, e_ref, hd) * gain_ref[...]).astype(BF16)
    for n in range(A_HEADS):
        q_ref[n] = qk[:, n * hd:(n + 1) * hd]
    for n in range(A_KV_HEADS):
        k_ref[n] = qk[:, (A_HEADS + n) * hd:(A_HEADS + n + 1) * hd]
        v_ref[n] = y[:, nqk + n * hd: nqk + (n + 1) * hd].astype(BF16)


def _proj_a_call(x, gn, wqkv, gq, gk):
    t_rows = x.shape[0]
    tm, hd = TOKEN_TILE, HEAD_DIM
    w = wqkv.astype(BF16)
    gain = jnp.concatenate([jnp.tile(gq * (hd ** -0.5 * LOG2E), A_HEADS), jnp.tile(gk, A_KV_HEADS)])
    ones = _segment_ones(hd)
    head_major = lambda n: pl.BlockSpec((n, tm, hd), lambda i: (0, i, 0))
    return pl.pallas_call(
        _proj_a_kernel,
        grid=(t_rows // tm,),
        in_specs=[pl.BlockSpec((tm, D_MODEL), lambda i: (i, 0)), _const_spec((1, D_MODEL)),
                  _const_spec(w.shape), _const_spec((1, gain.shape[0])), _const_spec(ones.shape)],
        out_specs=[head_major(A_HEADS), head_major(A_KV_HEADS), head_major(A_KV_HEADS)],
        out_shape=[jax.ShapeDtypeStruct((A_HEADS, t_rows, hd), BF16),
                   jax.ShapeDtypeStruct((A_KV_HEADS, t_rows, hd), BF16),
                   jax.ShapeDtypeStruct((A_KV_HEADS, t_rows, hd), BF16)],
        compiler_params=_params(("parallel",)),
        name="proj_a",
    )(x, gn.reshape(1, D_MODEL), w, gain.reshape(1, -1), ones)


def _proj_d_kernel(x_ref, gn_ref, w_ref, e_ref, cq_ref, sq_ref, ck_ref, sk_ref, qt_ref, k_ref, vt_ref):
    hd = HEAD_DIM
    nq, nk = D_HEADS * hd, D_KV_HEADS * hd
    tq = qt_ref.shape[-1]
    h = _rms(x_ref[...], gn_ref[...]).astype(BF16)
    y = _dot(h, w_ref[...])
    swapped = nq + 2 * nk
    t, ts = y[:, :nq + nk], y[:, swapped:swapped + nq + nk]
    r = _segment_rsqrt(t, e_ref, hd)
    for b in range((nq + nk) // LANES):
        is_q = b < nq // LANES
        c_ref, s_ref = (cq_ref, sq_ref) if is_q else (ck_ref, sk_ref)
        sl = slice(b * LANES, (b + 1) * LANES)
        blk = (t[:, sl] * c_ref[...] + ts[:, sl] * s_ref[...]) * r[:, sl]
        blk = blk.T.astype(BF16) if is_q else blk.astype(BF16)
        for half in range(LANES // hd):
            n = b * (LANES // hd) + half
            if is_q:
                for j in range(qt_ref.shape[1]):
                    qt_ref[n, j] = blk[half * hd:(half + 1) * hd, j * tq:(j + 1) * tq]
            else:
                k_ref[n - D_HEADS] = blk[:, half * hd:(half + 1) * hd]
    v_t = y[:, nq + nk: nq + 2 * nk].T
    for n in range(D_KV_HEADS):
        _store_vt(vt_ref, n, v_t[n * hd:(n + 1) * hd])


def _swap_halves_cols(w, group):
    k, n = w.shape
    return w.reshape(k, n // group, 2, group // 2)[:, :, ::-1, :].reshape(k, n)


def _position_spec(geo, tm, width):
    p_len, s_len, _ = geo
    p_tiles, s_tiles = p_len // tm, s_len // tm
    return pl.BlockSpec(
        (tm, width), lambda i: (jnp.where(i < p_tiles, i, (i - p_tiles) % s_tiles), 0))


def _rope_tables(pos, gain, scale):
    half = gain.shape[-1] // 2
    inv = ROPE_THETA ** (-jnp.arange(half, dtype=F32) / half)
    ang = pos.astype(F32)[:, None] * inv[None, :]
    cos, sin = lax.optimization_barrier((jnp.cos(ang), jnp.sin(ang)))
    g_sw = jnp.concatenate([gain[half:], gain[:half]])
    c = jnp.concatenate([cos, cos], axis=-1) * gain[None, :] * scale
    s = jnp.concatenate([-sin, sin], axis=-1) * g_sw[None, :] * scale
    return c, s


def _proj_d_call(x, gn, wqkv, gq, gk, pos, geo, tq):
    t_rows = x.shape[0]
    tm, hd = TOKEN_TILE, HEAD_DIM
    nq, nk = D_HEADS * hd, D_KV_HEADS * hd
    half = hd // 2
    w = jnp.concatenate([wqkv, _swap_halves_cols(wqkv[:, :nq + nk], half)], axis=1).astype(BF16)
    rows, cols = pos // GRID_W, pos % GRID_W

    def tables(g, scale):
        cr, sr = _rope_tables(rows, g[:half], scale)
        cc, sc = _rope_tables(cols, g[half:], scale)
        reps = LANES // hd
        return jnp.concatenate([cr, cc] * reps, axis=-1), jnp.concatenate([sr, sc] * reps, axis=-1)

    cq, sq = tables(gq, hd ** -0.5 * LOG2E)
    ck, sk = tables(gk, 1.0)
    ones = _segment_ones(hd)
    head_major = lambda n: pl.BlockSpec((n, tm, hd), lambda i: (0, i, 0))
    tab = _position_spec(geo, tm, LANES)
    return pl.pallas_call(
        _proj_d_kernel,
        grid=(t_rows // tm,),
        in_specs=[pl.BlockSpec((tm, D_MODEL), lambda i: (i, 0)), _const_spec((1, D_MODEL)),
                  _const_spec(w.shape), _const_spec(ones.shape), tab, tab, tab, tab],
        out_specs=[pl.BlockSpec((D_HEADS, tm // tq, hd, tq), lambda i: (0, i, 0, 0)),
                   head_major(D_KV_HEADS),
                   pl.BlockSpec((D_KV_HEADS, 1, hd + BF16_SUBLANES, tm), lambda i: (0, i, 0, 0))],
        out_shape=[jax.ShapeDtypeStruct((D_HEADS, t_rows // tq, hd, tq), BF16),
                   jax.ShapeDtypeStruct((D_KV_HEADS, t_rows, hd), BF16),
                   jax.ShapeDtypeStruct((D_KV_HEADS, t_rows // tm, hd + BF16_SUBLANES, tm), BF16)],
        compiler_params=_params(("parallel",)),
        name="proj_d",
    )(x, gn.reshape(1, D_MODEL), w, ones, cq, sq, ck, sk)


def _proj_b_kernel(x_ref, gn_ref, w1_ref, gcq_ref, gckv_ref, wq_ref, wqs_ref, wk_ref, wv_ref,
                   cq_ref, sq_ref, ck_ref, sk_ref, qt_ref, k_ref, vt_ref):
    h = _rms(x_ref[...], gn_ref[...]).astype(BF16)
    y1 = _dot(h, w1_ref[...])
    c_q = _rms(y1[:, :B_Q_LORA], gcq_ref[...]).astype(BF16)
    c_kv = _rms(y1[:, B_Q_LORA:B_Q_LORA + B_KV_LORA], gckv_ref[...]).astype(BF16)
    off = B_Q_LORA + B_KV_LORA
    kr = y1[:, off:off + B_PAD]
    krs = y1[:, off + B_PAD:off + 2 * B_PAD]
    q = _dot(c_q, wq_ref[...])
    qs = _dot(c_q, wqs_ref[...])
    kn = _dot(c_kv, wk_ref[...])
    v = _dot(c_kv, wv_ref[...])
    cq, sq, ck, sk = cq_ref[...], sq_ref[...], ck_ref[...], sk_ref[...]
    krs_s = krs * sk
    for n in range(B_HEADS):
        t = q[:, n * B_PAD:(n + 1) * B_PAD]
        ts = qs[:, n * B_PAD:(n + 1) * B_PAD]
        qt_ref[n, 0] = ((t * cq + ts * sq) * _head_norm(t, B_QK)).T.astype(BF16)
        t = kn[:, n * B_PAD:(n + 1) * B_PAD] + kr
        k_ref[n] = ((t * ck + krs_s) * _head_norm(t, B_QK)).astype(BF16)
    v_t = v.T
    for n in range(B_HEADS):
        _store_vt(vt_ref, n, v_t[n * B_V:(n + 1) * B_V])


def _pad_heads(w, lo, width):
    k, nh, _ = w.shape
    out = jnp.zeros((k, nh, B_PAD), w.dtype).at[:, :, lo:lo + width].set(w)
    return out.reshape(k, nh * B_PAD)


def _proj_b_call(x, gn, wdq, gcq, wuq, wdkv, gckv, wukv, gq, gk, pos, geo):
    t_rows = x.shape[0]
    tm = TOKEN_TILE
    wuq3 = wuq.reshape(B_Q_LORA, B_HEADS, B_QK)
    wq = _pad_heads(wuq3, 0, B_QK).astype(BF16)
    wuq_rope_sw = _swap_halves_cols(wuq3[:, :, B_NOPE:].reshape(B_Q_LORA, -1), B_ROPE)
    wqs = _pad_heads(wuq_rope_sw.reshape(B_Q_LORA, B_HEADS, B_ROPE), B_NOPE, B_ROPE).astype(BF16)
    wukv3 = wukv.reshape(B_KV_LORA, B_HEADS, B_NOPE + B_V)
    wk = _pad_heads(wukv3[:, :, :B_NOPE], 0, B_NOPE).astype(BF16)
    wv = wukv3[:, :, B_NOPE:].reshape(B_KV_LORA, B_HEADS * B_V).astype(BF16)
    w_kr = wdkv[:, B_KV_LORA:]
    kr_pad = _pad_heads(w_kr[:, None, :], B_NOPE, B_ROPE)
    krs_pad = _pad_heads(_swap_halves_cols(w_kr, B_ROPE)[:, None, :], B_NOPE, B_ROPE)
    w1 = jnp.concatenate([wdq, wdkv[:, :B_KV_LORA], kr_pad, krs_pad], axis=1).astype(BF16)

    def tables(g, scale):
        c, s = _rope_tables(pos, g[B_NOPE:], scale)
        n_pos = pos.shape[0]
        zeros = jnp.zeros((n_pos, B_PAD - B_QK), F32)
        c_full = jnp.concatenate(
            [jnp.broadcast_to(g[None, :B_NOPE] * scale, (n_pos, B_NOPE)), c, zeros], axis=-1)
        s_full = jnp.concatenate([jnp.zeros((n_pos, B_NOPE), F32), s, zeros], axis=-1)
        return c_full, s_full

    cq, sq = tables(gq, B_QK ** -0.5 * LOG2E)
    ck, sk = tables(gk, 1.0)
    tab = _position_spec(geo, tm, B_PAD)
    head_major = lambda w_: pl.BlockSpec((B_HEADS, tm, w_), lambda i: (0, i, 0))
    return pl.pallas_call(
        _proj_b_kernel,
        grid=(t_rows // tm,),
        in_specs=[pl.BlockSpec((tm, D_MODEL), lambda i: (i, 0)), _const_spec((1, D_MODEL)),
                  _const_spec(w1.shape), _const_spec((1, B_Q_LORA)), _const_spec((1, B_KV_LORA)),
                  _const_spec(wq.shape), _const_spec(wqs.shape), _const_spec(wk.shape),
                  _const_spec(wv.shape), tab, tab, tab, tab],
        out_specs=[pl.BlockSpec((B_HEADS, 1, B_PAD, tm), lambda i: (0, i, 0, 0)),
                   head_major(B_PAD),
                   pl.BlockSpec((B_HEADS, 1, B_V + BF16_SUBLANES, tm), lambda i: (0, i, 0, 0))],
        out_shape=[jax.ShapeDtypeStruct((B_HEADS, t_rows // tm, B_PAD, tm), BF16),
                   jax.ShapeDtypeStruct((B_HEADS, t_rows, B_PAD), BF16),
                   jax.ShapeDtypeStruct((B_HEADS, t_rows // tm, B_V + BF16_SUBLANES, tm), BF16)],
        compiler_params=_params(("parallel",)),
        name="proj_b",
    )(x, gn.reshape(1, D_MODEL), w1, gcq.reshape(1, -1), gckv.reshape(1, -1), wq, wqs, wk, wv,
      cq, sq, ck, sk)


def _proj_c_kernel(x_ref, gn_ref, w_ref, gq_ref, gk_ref, e_ref, *refs):
    hd = HEAD_DIM
    width = C_HEADS * hd
    outs, scr_ref = refs[:-1], refs[-1]
    tm = x_ref.shape[0]
    h = _rms(x_ref[...], gn_ref[...]).astype(BF16)
    y = _dot(h, w_ref[...])
    gains = (gq_ref[...], gk_ref[...])
    for g, (_, dil) in enumerate(C_PATTERNS):
        for j in range(3):
            base = (g * 3 + j) * width
            val = y[:, base: base + width]
            if j < 2:
                val = val * _segment_rsqrt(val, e_ref, hd) * gains[j]
            out_ref = outs[g * 3 + j]
            if dil == 1:
                out_ref[...] = val.astype(BF16)
            else:
                for s in range(width // LANES):
                    scr_ref[s] = val[:, s * LANES:(s + 1) * LANES]
                for r in range(dil):
                    for s in range(width // LANES):
                        rows = scr_ref[s, pl.ds(r, tm // dil, stride=dil), :]
                        col = r * width + s * LANES
                        out_ref[:, col:col + LANES] = rows.astype(BF16)


def _proj_c_call(x, gn, wqkv, gq, gk):
    t_rows = x.shape[0]
    tm, hd = TOKEN_TILE, HEAD_DIM
    width = C_HEADS * hd
    w = wqkv.astype(BF16)
    ones = _segment_ones(hd)
    specs, shapes = [], []
    for _, dil in C_PATTERNS:
        specs += [pl.BlockSpec((tm // dil, dil * width), lambda i: (i, 0))] * 3
        shapes += [jax.ShapeDtypeStruct((t_rows // dil, dil * width), BF16)] * 3
    outs = pl.pallas_call(
        _proj_c_kernel,
        grid=(t_rows // tm,),
        in_specs=[pl.BlockSpec((tm, D_MODEL), lambda i: (i, 0)), _const_spec((1, D_MODEL)),
                  _const_spec(w.shape), _const_spec((1, width)), _const_spec((1, width)),
                  _const_spec(ones.shape)],
        out_specs=specs,
        out_shape=shapes,
        scratch_shapes=[pltpu.VMEM((width // LANES, tm, LANES), F32)],
        compiler_params=_params(("parallel",)),
        name="proj_c",
    )(x, gn.reshape(1, D_MODEL), w, jnp.tile(gq * (hd ** -0.5 * LOG2E), C_HEADS).reshape(1, width),
      jnp.tile(gk, C_HEADS).reshape(1, width), ones)
    return [outs[3 * g:3 * g + 3] for g in range(len(C_PATTERNS))]


def _band_masked_dist(row0, lo, hi, sub, w):
    nk = sub + 2 * w
    r = lax.broadcasted_iota(jnp.int32, (sub, nk), 0)
    c = lax.broadcasted_iota(jnp.int32, (sub, nk), 1)
    dist = jnp.abs(r + w - c)
    jpos = row0 - w + lax.broadcasted_iota(jnp.int32, (1, nk), 1)
    outside = jnp.where(jpos < lo, w + 1, 0) + jnp.where(jpos >= hi, w + 1, 0)
    return jnp.where(dist + outside <= w, dist.astype(F32), MASK_DIST)


def _two_stage(n_items, first, second, ahead=2):
    pending = {i: first(i) for i in range(min(ahead, n_items))}
    results = []
    for i in range(n_items):
        if i + ahead < n_items:
            pending[i + ahead] = first(i + ahead)
        results.append(second(i, pending.pop(i)))
    return results


def _with_ones(vv):
    return jnp.concatenate([vv, jnp.ones((vv.shape[0], BF16_SUBLANES), BF16)], axis=-1)


def _band_softmax_pv(s, vv_ones, sink):
    m = jnp.max(s, axis=-1, keepdims=True)
    if sink is not None:
        m = jnp.maximum(m, sink)
    hd = vv_ones.shape[-1] - BF16_SUBLANES
    pv = _dot(jnp.exp2(s - m).astype(BF16), vv_ones)
    den = pv[:, hd:hd + 1]
    if sink is not None:
        den = den + jnp.exp2(sink - m)
    return pv[:, :hd] / den, m + jnp.log2(den)


def _attn_a_kernel(slope_ref, sink_ref, q_ref, kp_ref, kc_ref, kn_ref, vp_ref, vc_ref, vn_ref,
                   o_ref, *, tq, p_len, s_len):
    w, sub = A_HALF_WINDOW, BAND_SUB
    group = A_HEADS // A_KV_HEADS
    kvh = pl.program_id(0)
    row0 = pl.program_id(1) * tq
    lo, hi = _seq_bounds(row0, p_len, s_len)
    kk = jnp.concatenate([kp_ref[0], kc_ref[0], kn_ref[0]], axis=0)
    vv = _with_ones(jnp.concatenate([vp_ref[0], vc_ref[0], vn_ref[0]], axis=0))
    sink = jnp.concatenate(
        [jnp.full((sub, 1), sink_ref[kvh * group + g], F32) for g in range(group)], axis=0)

    def keys(r):
        return slice(r * sub, (r + 1) * sub + 2 * w)

    def logits(r):
        md = _band_masked_dist(row0 + r * sub, lo, hi, sub, w)
        bias = jnp.concatenate([md * slope_ref[kvh * group + g] for g in range(group)], axis=0)
        q = q_ref[:, r * sub:(r + 1) * sub, :].reshape(group * sub, HEAD_DIM)
        return _dot_nt(q, kk[keys(r)]) + bias

    tiles = _two_stage(tq // sub, logits, lambda r, s: _band_softmax_pv(s, vv[keys(r)], sink)[0])
    outs = [[o[g * sub:(g + 1) * sub] for o in tiles] for g in range(group)]
    o_ref[...] = jnp.concatenate(
        [jnp.concatenate(rows, axis=0) for rows in outs], axis=-1).astype(BF16)


def _attn_a_call(q, k, v, sink, geo):
    p_len, s_len, t_rows = geo
    tq, w, hd = 1024, A_HALF_WINDOW, HEAD_DIM
    group = A_HEADS // A_KV_HEADS
    per = tq // w
    last = t_rows // w - 1
    prev = pl.BlockSpec((1, w, hd), lambda h, i: (h, jnp.maximum(i * per - 1, 0), 0))
    cur = pl.BlockSpec((1, tq, hd), lambda h, i: (h, i, 0))
    nxt = pl.BlockSpec((1, w, hd), lambda h, i: (h, jnp.minimum((i + 1) * per, last), 0))
    smem = pl.BlockSpec(memory_space=pltpu.SMEM)
    slopes = jnp.asarray(-LOG2E * _alibi_slopes(A_HEADS), F32)
    return pl.pallas_call(
        functools.partial(_attn_a_kernel, tq=tq, p_len=p_len, s_len=s_len),
        grid=(A_KV_HEADS, t_rows // tq),
        in_specs=[smem, smem, pl.BlockSpec((group, tq, hd), lambda h, i: (h, i, 0)),
                  prev, cur, nxt, prev, cur, nxt],
        out_specs=pl.BlockSpec((tq, group * hd), lambda h, i: (i, h)),
        out_shape=jax.ShapeDtypeStruct((t_rows, A_HEADS * hd), BF16),
        compiler_params=_params(("parallel", "parallel")),
        name="attn_a",
    )(slopes, sink.astype(F32) * LOG2E, q, k, k, k, v, v, v)


def _attn_c_kernel(q_ref, kp_ref, kc_ref, kn_ref, vp_ref, vc_ref, vn_ref, o_ref, lse_ref,
                   *, tq, dil, p_len, s_len):
    w, hd, sub = C_STEPS, HEAD_DIM, min(BAND_SUB, tq)
    row0 = pl.program_id(0) * tq
    lo, hi = _seq_bounds(row0, p_len // dil, s_len // dil)
    slopes = -LOG2E * dil * _alibi_slopes(C_HEADS)
    mds = [_band_masked_dist(row0 + r * sub, lo, hi, sub, w) for r in range(tq // sub)]
    n_sub = tq // sub

    heads = {}

    def head_rows(n):
        if n not in heads:
            hs = slice(n * hd, (n + 1) * hd)
            kk = jnp.concatenate([kp_ref[:, hs], kc_ref[:, hs], kn_ref[:, hs]], axis=0)
            vv = jnp.concatenate([vp_ref[:, hs], vc_ref[:, hs], vn_ref[:, hs]], axis=0)
            heads[n] = (q_ref[:, hs], kk, _with_ones(vv))
        return heads[n]

    def logits(i):
        n, r = divmod(i, n_sub)
        q, kk, _ = head_rows(n)
        keys = kk[r * sub:(r + 1) * sub + 2 * w]
        return _dot_nt(q[r * sub:(r + 1) * sub], keys) + mds[r] * float(slopes[n])

    def attend(i, s):
        n, r = divmod(i, n_sub)
        o, lse2 = _band_softmax_pv(s, head_rows(n)[2][r * sub:(r + 1) * sub + 2 * w], None)
        return o, jnp.broadcast_to(lse2 * (1.0 / LOG2E), (sub, hd))

    tiles = _two_stage(C_HEADS * n_sub, logits, attend, ahead=3 if n_sub > 1 else 2)
    outs = [jnp.concatenate([t[0] for t in tiles[n * n_sub:(n + 1) * n_sub]], axis=0)
            for n in range(C_HEADS)]
    lses = [jnp.concatenate([t[1] for t in tiles[n * n_sub:(n + 1) * n_sub]], axis=0)
            for n in range(C_HEADS)]
    o_ref[...] = jnp.concatenate(outs, axis=-1)
    lse_ref[...] = jnp.concatenate(lses, axis=-1)


def _attn_c_call(q, k, v, dil, geo):
    p_len, s_len, t_rows = geo
    w, width = C_STEPS, C_HEADS * HEAD_DIM
    rows = t_rows // dil
    tq = min(512, s_len // dil)
    per = tq // w
    last = rows // w - 1
    prev = pl.BlockSpec((w, width), lambda i, r: (jnp.maximum(i * per - 1, 0), r))
    cur = pl.BlockSpec((tq, width), lambda i, r: (i, r))
    nxt = pl.BlockSpec((w, width), lambda i, r: (jnp.minimum((i + 1) * per, last), r))
    shape = jax.ShapeDtypeStruct((rows, dil * width), F32)
    return pl.pallas_call(
        functools.partial(_attn_c_kernel, tq=tq, dil=dil, p_len=p_len, s_len=s_len),
        grid=(rows // tq, dil),
        in_specs=[cur, prev, cur, nxt, prev, cur, nxt],
        out_specs=[cur, cur],
        out_shape=[shape, shape],
        compiler_params=_params(("parallel", "parallel")),
        name="attn_c_d%d" % dil,
    )(q, k, k, k, v, v, v)


def _attn_dense_kernel(qt_ref, k_ref, vt_ref, o_ref, sa_ref, sb_ref, sc_ref, sd_ref, acc_ref,
                       *, nq, tq, tk, p_len, s_len):
    dv = vt_ref.shape[-2] - BF16_SUBLANES
    width = nq * tq
    shift = int(math.log2(tk))

    def first_chunk(j):
        row0 = (pl.program_id(1) * Q_TILES_PER_STEP + j) * tq
        lo, hi = _seq_bounds(row0, p_len, s_len)
        return lax.shift_right_logical(lo, shift), lax.shift_right_logical(hi - lo, shift)

    def scores(j, c, st_ref):
        start = pl.multiple_of(c * tk, tk)
        qt = jnp.concatenate([qt_ref[g, j] for g in range(nq)], axis=-1)
        st = _dot(k_ref[0, pl.ds(start, tk), :], qt).astype(BF16)
        st_ref[...] = st
        return jnp.max(st, axis=0, keepdims=True).astype(F32)

    def update(c, st_ref, mc, m, acc):
        m_new = jnp.maximum(m, mc)
        alpha = jnp.exp2(m - m_new)
        pt = jnp.exp2(st_ref[...] - m_new.astype(BF16))
        acc_ref[...] = alpha * acc_ref[...] + _dot(vt_ref[0, c], pt)
        return m_new, acc

    def q_tile(j, mcs, start_next):
        first, n_chunks = first_chunk(j)

        def quad(i, carry):
            m, acc, mc_a, mc_b = carry
            c = first + 4 * i
            mc_c = scores(j, c + 2, sc_ref)
            m, acc = update(c, sa_ref, mc_a, m, acc)
            mc_d = scores(j, c + 3, sd_ref)
            m, acc = update(c + 1, sb_ref, mc_b, m, acc)
            mc_a = scores(j, c + 4, sa_ref)
            m, acc = update(c + 2, sc_ref, mc_c, m, acc)
            mc_b = scores(j, c + 5, sb_ref)
            m, acc = update(c + 3, sd_ref, mc_d, m, acc)
            return m, acc, mc_a, mc_b

        acc_ref[...] = jnp.zeros_like(acc_ref)
        carry = (jnp.full((1, width), NEG_INF, F32), 0, *mcs)
        n_body = lax.shift_right_logical(n_chunks, 2) - 1
        carry = lax.fori_loop(0, lax.shift_right_logical(n_body, 1),
                              lambda i, cr: quad(2 * i + 1, quad(2 * i, cr)), carry)
        carry = lax.fori_loop(0, n_body & 1, lambda _, cr: quad(n_body - 1, cr), carry)
        m, acc, mc_a, mc_b = carry
        c = first + n_chunks - 4
        mc_c = scores(j, c + 2, sc_ref)
        m, acc = update(c, sa_ref, mc_a, m, acc)
        mc_d = scores(j, c + 3, sd_ref)
        m, acc = update(c + 1, sb_ref, mc_b, m, acc)
        if start_next:
            first_next, _ = first_chunk(j + 1)
            mc_a = scores(j + 1, first_next, sa_ref)
        m, acc = update(c + 2, sc_ref, mc_c, m, acc)
        if start_next:
            mc_b = scores(j + 1, first_next + 1, sb_ref)
        m, acc = update(c + 3, sd_ref, mc_d, m, acc)
        o = acc_ref[:dv, :] / acc_ref[dv:dv + 1, :]
        rows = pl.ds(pl.multiple_of(j * tq, tq), tq)
        heads = [o[:, g * tq:(g + 1) * tq].T.astype(BF16) for g in range(nq)]
        if nq == 1:
            o_ref[0, rows, :] = heads[0]
        else:
            o_ref[rows, :] = jnp.concatenate(heads, axis=-1)
        return mc_a, mc_b

    first, _ = first_chunk(0)
    mcs = (scores(0, first, sa_ref), scores(0, first + 1, sb_ref))
    mcs = lax.fori_loop(0, Q_TILES_PER_STEP - 1, lambda j, mcs: q_tile(j, mcs, True), mcs)
    q_tile(Q_TILES_PER_STEP - 1, mcs, False)


def _attn_dense_call(qt, k, vt, geo):
    p_len, s_len, t_rows = geo
    hq, n_tiles, dq, tq = qt.shape
    hk, n_chunks, rows_v, tk = vt.shape
    dv = rows_v - BF16_SUBLANES
    nq, per = hq // hk, Q_TILES_PER_STEP
    assert s_len % (4 * tk) == 0 and p_len % (4 * tk) == 0 and n_tiles % per == 0
    return pl.pallas_call(
        functools.partial(_attn_dense_kernel, nq=nq, tq=tq, tk=tk, p_len=p_len, s_len=s_len),
        grid=(hk, n_tiles // per),
        in_specs=[pl.BlockSpec((nq, per, dq, tq), lambda h, i: (h, i, 0, 0)),
                  pl.BlockSpec((1, t_rows, dq), lambda h, i: (h, 0, 0)),
                  pl.BlockSpec((1, n_chunks, dv + BF16_SUBLANES, tk), lambda h, i: (h, 0, 0, 0))],
        out_specs=(pl.BlockSpec((1, per * tq, dv), lambda h, i: (h, i, 0)) if nq == 1
                   else pl.BlockSpec((per * tq, nq * dv), lambda h, i: (i, h))),
        out_shape=jax.ShapeDtypeStruct((hq, t_rows, dv) if nq == 1 else (t_rows, hq * dv), BF16),
        scratch_shapes=[pltpu.VMEM((tk, nq * tq), BF16)] * 4 + [pltpu.VMEM((rows_v, nq * tq), F32)],
        compiler_params=_params(("parallel", "parallel")),
        name="attn_dense_%d" % dq,
    )(qt, k, vt)


def _trunk(x, p, geo, pos):
    for i in range(DEPTH):
        mixer, j = i % 4, i // 4
        x = _ffn_call(x, p["norm_ffn1"][i], p["ffn1_wg"][i], p["ffn1_wu"][i], p["ffn1_wd"][i])
        gn = p["norm_mix"][i]
        if mixer == 0:
            q, k, v = _proj_a_call(x, gn, p["a_wqkv"][j], p["a_gq"][j], p["a_gk"][j])
            pre = (_attn_a_call(q, k, v, p["a_sink"][j], geo), p["a_wo"][j])
        elif mixer == 1:
            qt, k, vt = _proj_b_call(x, gn, p["b_wdq"][j], p["b_gcq"][j], p["b_wuq"][j], p["b_wdkv"][j],
                                     p["b_gckv"][j], p["b_wukv"][j], p["b_gq"][j], p["b_gk"][j], pos, geo)
            pre = (_attn_dense_call(qt, k, vt, geo), p["b_wo"][j])
        elif mixer == 2:
            qkv = _proj_c_call(x, gn, p["c_wqkv"][j], p["c_gq"][j], p["c_gk"][j])
            res = [_attn_c_call(*qkv[g], dil, geo) for g, (_, dil) in enumerate(C_PATTERNS)]
            pre = ([r[0] for r in res], [r[1] for r in res], p["c_wo"][j])
        else:
            qt, k, vt = _proj_d_call(x, gn, p["d_wqkv"][j], p["d_gq"][j], p["d_gk"][j], pos, geo, tq=128)
            pre = (_attn_dense_call(qt, k, vt, geo), p["d_wo"][j])
        ffn2 = (p["norm_ffn2"][i], p["ffn2_wg"][i], p["ffn2_wu"][i], p["ffn2_wd"][i])
        if i < DEPTH - 1:
            x = _ffn_call(x, *ffn2, pre=pre)
        else:
            x = (_ffn_call(x, *ffn2, pre=pre, rows=(0, geo[0])),
                 _ffn_call(x, *ffn2, pre=pre, rows=(geo[0], geo[2] - geo[0])))
    return x


def kernel(x_prompt, x_sample, norm_ffn1, ffn1_wg, ffn1_wu, ffn1_wd, norm_mix, norm_ffn2, ffn2_wg, ffn2_wu, ffn2_wd, a_wqkv, a_gq, a_gk, a_sink, a_wo, b_wdq, b_gcq, b_wuq, b_wdkv, b_gckv, b_wukv, b_gq, b_gk, b_wo, c_wqkv, c_gq, c_gk, c_wo, d_wqkv, d_gq, d_gk, d_wo):
    p = dict(norm_ffn1=norm_ffn1, ffn1_wg=ffn1_wg, ffn1_wu=ffn1_wu, ffn1_wd=ffn1_wd,
             norm_mix=norm_mix, norm_ffn2=norm_ffn2, ffn2_wg=ffn2_wg, ffn2_wu=ffn2_wu,
             ffn2_wd=ffn2_wd, a_wqkv=a_wqkv, a_gq=a_gq, a_gk=a_gk, a_sink=a_sink, a_wo=a_wo,
             b_wdq=b_wdq, b_gcq=b_gcq, b_wuq=b_wuq, b_wdkv=b_wdkv, b_gckv=b_gckv,
             b_wukv=b_wukv, b_gq=b_gq, b_gk=b_gk, b_wo=b_wo, c_wqkv=c_wqkv, c_gq=c_gq,
             c_gk=c_gk, c_wo=c_wo, d_wqkv=d_wqkv, d_gq=d_gq, d_gk=d_gk, d_wo=d_wo)
    n_p, p_seq, _ = x_prompt.shape
    n_s, s_len, _ = x_sample.shape
    assert n_p == 1 and s_len & (s_len - 1) == 0 and p_seq % s_len == 0
    p_len = n_p * p_seq
    t_rows = p_len + n_s * s_len
    geo = (p_len, s_len, t_rows)
    pos = jnp.arange(p_len, dtype=jnp.int32)
    x = (x_prompt.reshape(p_len, D_MODEL), x_sample.reshape(n_s * s_len, D_MODEL))
    y_p, y_s = _trunk(x, p, geo, pos)
    return y_p.reshape(x_prompt.shape), y_s.reshape(x_sample.shape)
```

```python
import functools
import math

import numpy as np
import jax
import jax.numpy as jnp
from jax import lax
from jax.experimental import pallas as pl
from jax.experimental.pallas import tpu as pltpu

F32 = jnp.float32
BF16 = jnp.bfloat16

D_MODEL = 1024
DEPTH = 4
HEAD_DIM = 64
D_FF = 2816
NORM_EPS = 1e-6
NEG_INF = -1e30
GRID_W = 64
ROPE_THETA = 10000.0
LOG2E = math.log2(math.e)
LANES = 128
MXU_TILE = 256
BF16_SUBLANES = 16
BAND_SUB = 128
Q_TILES_PER_STEP = 8
MASK_DIST = 1e33

A_HEADS, A_KV_HEADS, A_HALF_WINDOW = 16, 4, 128
B_HEADS, B_Q_LORA, B_KV_LORA, B_NOPE, B_ROPE, B_V = 16, 512, 256, 64, 32, 64
B_QK = B_NOPE + B_ROPE
B_PAD = 128
C_HEADS = 8
C_PATTERNS = ((128, 1), (512, 4), (2048, 16))
C_STEPS = 64
D_HEADS, D_KV_HEADS = 16, 4

FF_CHUNK = 256
N_FF_CHUNKS = D_FF // FF_CHUNK
TOKEN_TILE = 512
FFN_TILE = 1024
VMEM_LIMIT = 56 * 1024 * 1024


def _alibi_slopes(n):
    return 2.0 ** (-8.0 * np.arange(1, n + 1) / n)


def _params(sem, vmem=VMEM_LIMIT):
    return pltpu.CompilerParams(dimension_semantics=sem, vmem_limit_bytes=vmem)


def _const_spec(shape):
    nd = len(shape)
    return pl.BlockSpec(tuple(shape), lambda *_: (0,) * nd, pipeline_mode=pl.Buffered(1))


def _seq_bounds(row0, p_len, s_len):
    in_prompt = row0 < p_len
    b = lax.shift_right_logical(jnp.maximum(row0 - p_len, 0), int(math.log2(s_len)))
    lo = jnp.where(in_prompt, 0, p_len + b * s_len)
    hi = jnp.where(in_prompt, p_len, lo + s_len)
    return lo, hi


def _rms(x, g):
    return x * lax.rsqrt(jnp.mean(x * x, axis=-1, keepdims=True) + NORM_EPS) * g


def _dot(a, b):
    return jnp.dot(a, b, preferred_element_type=F32)


def _dot_nt(a, b):
    return lax.dot_general(a, b, (((1,), (1,)), ((), ())), preferred_element_type=F32)


def _ffn_kernel(*refs, mode, first_tiles=0):
    if mode == "plain":
        x_ref, gn_ref, wg_ref, wu_ref, wd_ref, out_ref, acc_ref, act_a, act_b = refs
        x = x_ref[...]
    elif mode == "plain2":
        xa_ref, xb_ref, gn_ref, wg_ref, wu_ref, wd_ref, out_ref, acc_ref, act_a, act_b = refs
        x = jnp.where(pl.program_id(0) < first_tiles, xa_ref[...], xb_ref[...])
    elif mode == "proj":
        (x_ref, o_ref, wo_ref, gn_ref, wg_ref, wu_ref, wd_ref, out_ref, acc_ref, act_a,
         act_b) = refs
        if len(o_ref.shape) == 3:
            o = jnp.concatenate([o_ref[n] for n in range(o_ref.shape[0])], axis=-1)
        else:
            o = o_ref[...]
        x = x_ref[...] + _dot(o, wo_ref[...])
    else:
        (x_ref, o0, o1, o2, l0, l1, l2, wo_ref, gn_ref, wg_ref, wu_ref, wd_ref,
         out_ref, acc_ref, act_a, act_b, *scr) = refs
        tm, width = x_ref.shape[0], wo_ref.shape[0]

        def token_major(ref, dil, scr_ref):
            slabs = width // LANES
            for r in range(dil):
                for s in range(slabs):
                    col = r * width + s * LANES
                    scr_ref[s, pl.ds(r, tm // dil, stride=dil), :] = ref[:, col:col + LANES]
            return jnp.concatenate([scr_ref[s] for s in range(slabs)], axis=-1)

        dils = [d for _, d in C_PATTERNS]
        assert dils[0] == 1
        oa, la = o0[...], l0[...]
        ob, lb = token_major(o1, dils[1], scr[0]), token_major(l1, dils[1], scr[1])
        oc, lc = token_major(o2, dils[2], scr[2]), token_major(l2, dils[2], scr[3])
        m = jnp.maximum(jnp.maximum(la, lb), lc)
        ea, eb, ec = jnp.exp(la - m), jnp.exp(lb - m), jnp.exp(lc - m)
        o = (ea * oa + eb * ob + ec * oc) / (ea + eb + ec)
        x = x_ref[...] + _dot(o.astype(BF16), wo_ref[...])

    h = _rms(x, gn_ref[...]).astype(BF16)

    def gate_up(c, a_ref):
        cols = pl.ds(pl.multiple_of(c * FF_CHUNK, FF_CHUNK), FF_CHUNK)
        g = _dot(h, wg_ref[:, cols])
        u = _dot(h, wu_ref[:, cols])
        a_ref[...] = (g * jax.nn.sigmoid(g) * u).astype(BF16)

    def down(c, a_ref):
        acc_ref[...] += _dot(a_ref[...], wd_ref[c])

    assert N_FF_CHUNKS % 2 == 1
    gate_up(0, act_a)
    gate_up(1, act_b)
    acc_ref[...] = _dot(act_a[...], wd_ref[0])

    def pair(i, carry):
        c = 2 * i + 1
        gate_up(c + 1, act_a)
        down(c, act_b)
        gate_up(c + 2, act_b)
        down(c + 1, act_a)
        return carry

    lax.fori_loop(0, N_FF_CHUNKS // 2 - 1, pair, 0)
    gate_up(N_FF_CHUNKS - 1, act_a)
    down(N_FF_CHUNKS - 2, act_b)
    down(N_FF_CHUNKS - 1, act_a)
    out_ref[...] = x + 0.5 * acc_ref[...]


def _ffn_weights(wg, wu, wd):
    wg3 = wg.astype(BF16)
    wu3 = wu.astype(BF16)
    wd3 = wd.astype(BF16).reshape(N_FF_CHUNKS, FF_CHUNK, D_MODEL)
    return wg3, wu3, wd3


def _ffn_call(x, gn, wg, wu, wd, pre=None, rows=None):
    tm = TOKEN_TILE if pre is not None and len(pre) == 3 else FFN_TILE
    xs = x if isinstance(x, tuple) else (x,)
    off, n_tiles = (rows[0] // tm, rows[1] // tm) if rows is not None else (
        0, sum(a.shape[0] for a in xs) // tm)
    wg3, wu3, wd3 = _ffn_weights(wg, wu, wd)
    row = lambda w: pl.BlockSpec((tm, w), lambda i: (i + off, 0))
    ffn_args = [gn.reshape(1, D_MODEL), wg3, wu3, wd3]
    ffn_specs = [_const_spec((1, D_MODEL)), _const_spec(wg3.shape), _const_spec(wu3.shape),
                 _const_spec(wd3.shape)]
    scratch = []
    first_tiles = 0
    if len(xs) == 2:
        assert pre is None and rows is None
        first_tiles = xs[0].shape[0] // tm
        last = xs[1].shape[0] // tm - 1
        mode, args = "plain2", list(xs)
        specs = [pl.BlockSpec((tm, D_MODEL), lambda i: (jnp.minimum(i, first_tiles - 1), 0)),
                 pl.BlockSpec((tm, D_MODEL), lambda i: (jnp.clip(i - first_tiles, 0, last), 0))]
    elif pre is None:
        mode, args, specs = "plain", [x], [row(D_MODEL)]
    elif len(pre) == 2:
        o, wo = pre
        wo = wo.astype(BF16)
        mode = "proj"
        args = [x, o, wo]
        o_spec = row(o.shape[1]) if o.ndim == 2 else pl.BlockSpec(
            (o.shape[0], tm, o.shape[2]), lambda i: (0, i + off, 0))
        specs = [row(D_MODEL), o_spec, _const_spec(wo.shape)]
    else:
        outs, lses, wo = pre
        wo = wo.astype(BF16)
        mode = "merge"
        args = [x, *outs, *lses, wo]
        width = wo.shape[0]
        dilated = [pl.BlockSpec((tm // d, d * width), lambda i: (i + off, 0)) for _, d in C_PATTERNS]
        specs = [row(D_MODEL)] + dilated * 2 + [_const_spec(wo.shape)]
        scratch = [pltpu.VMEM((width // LANES, tm, LANES), F32)] * 4
    return pl.pallas_call(
        functools.partial(_ffn_kernel, mode=mode, first_tiles=first_tiles),
        grid=(n_tiles,),
        in_specs=specs + ffn_specs,
        out_specs=pl.BlockSpec((tm, D_MODEL), lambda i: (i, 0)),
        out_shape=jax.ShapeDtypeStruct((n_tiles * tm, D_MODEL), F32),
        scratch_shapes=[pltpu.VMEM((tm, D_MODEL), F32)] + [pltpu.VMEM((tm, FF_CHUNK), BF16)] * 2
        + scratch,
        compiler_params=_params(("parallel",)),
        name="ffn_" + mode,
    )(*args, *ffn_args)


def _head_norm(t, width):
    return lax.rsqrt(jnp.sum(t * t, axis=-1, keepdims=True) * (1.0 / width) + NORM_EPS)


def _segment_ones(seg):
    n = MXU_TILE // seg
    return jnp.kron(jnp.eye(n, dtype=F32), jnp.ones((seg, seg), F32)).astype(BF16)


def _segment_rsqrt(t, e_ref, width):
    sq = t * t
    hi = sq.astype(BF16)
    lo = (sq - hi.astype(F32)).astype(BF16)
    e = e_ref[...]
    sums = [_dot(hi[:, b:b + MXU_TILE], e) + _dot(lo[:, b:b + MXU_TILE], e)
            for b in range(0, t.shape[-1], MXU_TILE)]
    ssq = sums[0] if len(sums) == 1 else jnp.concatenate(sums, axis=-1)
    return lax.rsqrt(ssq * (1.0 / width) + NORM_EPS)


def _store_vt(vt_ref, n, v_t):
    dv, rows = v_t.shape
    vt_ref[n, 0, :dv, :] = v_t.astype(BF16)
    first = lax.broadcasted_iota(jnp.int32, (BF16_SUBLANES, rows), 0) == 0
    vt_ref[n, 0, dv:, :] = jnp.where(first, 1.0, 0.0).astype(BF16)


def _proj_a_kernel(x_ref, gn_ref, w_ref, gain_ref, e_ref, q_ref, k_ref, v_ref):
    hd = HEAD_DIM
    nqk = (A_HEADS + A_KV_HEADS) * hd
    h = _rms(x_ref[...], gn_ref[...]).astype(BF16)
    y = _dot(h, w_ref[...])
    qk = y[:, :nqk]
    qk = (qk * _segment_rsqrt(qk, e_ref, hd) * gain_ref[...]).astype(BF16)
    for n in range(A_HEADS):
        q_ref[n] = qk[:, n * hd:(n + 1) * hd]
    for n in range(A_KV_HEADS):
        k_ref[n] = qk[:, (A_HEADS + n) * hd:(A_HEADS + n + 1) * hd]
        v_ref[n] = y[:, nqk + n * hd: nqk + (n + 1) * hd].astype(BF16)


def _proj_a_call(x, gn, wqkv, gq, gk):
    t_rows = x.shape[0]
    tm, hd = FFN_TILE, HEAD_DIM
    w = wqkv.astype(BF16)
    gain = jnp.concatenate([jnp.tile(gq * (hd ** -0.5 * LOG2E), A_HEADS), jnp.tile(gk, A_KV_HEADS)])
    ones = _segment_ones(hd)
    head_major = lambda n: pl.BlockSpec((n, tm, hd), lambda i: (0, i, 0))
    return pl.pallas_call(
        _proj_a_kernel,
        grid=(t_rows // tm,),
        in_specs=[pl.BlockSpec((tm, D_MODEL), lambda i: (i, 0)), _const_spec((1, D_MODEL)),
                  _const_spec(w.shape), _const_spec((1, gain.shape[0])), _const_spec(ones.shape)],
        out_specs=[head_major(A_HEADS), head_major(A_KV_HEADS), head_major(A_KV_HEADS)],
        out_shape=[jax.ShapeDtypeStruct((A_HEADS, t_rows, hd), BF16),
                   jax.ShapeDtypeStruct((A_KV_HEADS, t_rows, hd), BF16),
                   jax.ShapeDtypeStruct((A_KV_HEADS, t_rows, hd), BF16)],
        compiler_params=_params(("parallel",)),
        name="proj_a",
    )(x, gn.reshape(1, D_MODEL), w, gain.reshape(1, -1), ones)


def _proj_d_kernel(x_ref, gn_ref, w_ref, e_ref, cq_ref, sq_ref, ck_ref, sk_ref, qt_ref, k_ref, vt_ref):
    hd = HEAD_DIM
    nq, nk = D_HEADS * hd, D_KV_HEADS * hd
    tq = qt_ref.shape[-1]
    h = _rms(x_ref[...], gn_ref[...]).astype(BF16)
    y = _dot(h, w_ref[...])
    swapped = nq + 2 * nk
    t, ts = y[:, :nq + nk], y[:, swapped:swapped + nq + nk]
    r = _segment_rsqrt(t, e_ref, hd)
    for b in range((nq + nk) // LANES):
        is_q = b < nq // LANES
        c_ref, s_ref = (cq_ref, sq_ref) if is_q else (ck_ref, sk_ref)
        sl = slice(b * LANES, (b + 1) * LANES)
        blk = (t[:, sl] * c_ref[...] + ts[:, sl] * s_ref[...]) * r[:, sl]
        blk = blk.T.astype(BF16) if is_q else blk.astype(BF16)
        for half in range(LANES // hd):
            n = b * (LANES // hd) + half
            if is_q:
                for j in range(qt_ref.shape[1]):
                    qt_ref[n, j] = blk[half * hd:(half + 1) * hd, j * tq:(j + 1) * tq]
            else:
                k_ref[n - D_HEADS] = blk[:, half * hd:(half + 1) * hd]
    v_t = y[:, nq + nk: nq + 2 * nk].T
    for n in range(D_KV_HEADS):
        _store_vt(vt_ref, n, v_t[n * hd:(n + 1) * hd])


def _swap_halves_cols(w, group):
    k, n = w.shape
    return w.reshape(k, n // group, 2, group // 2)[:, :, ::-1, :].reshape(k, n)


def _position_spec(geo, tm, width):
    p_len, s_len, _ = geo
    p_tiles, s_tiles = p_len // tm, s_len // tm
    return pl.BlockSpec(
        (tm, width), lambda i: (jnp.where(i < p_tiles, i, (i - p_tiles) % s_tiles), 0))


def _rope_tables(pos, gain, scale):
    half = gain.shape[-1] // 2
    inv = ROPE_THETA ** (-jnp.arange(half, dtype=F32) / half)
    ang = pos.astype(F32)[:, None] * inv[None, :]
    cos, sin = lax.optimization_barrier((jnp.cos(ang), jnp.sin(ang)))
    g_sw = jnp.concatenate([gain[half:], gain[:half]])
    c = jnp.concatenate([cos, cos], axis=-1) * gain[None, :] * scale
    s = jnp.concatenate([-sin, sin], axis=-1) * g_sw[None, :] * scale
    return c, s


def _proj_d_call(x, gn, wqkv, gq, gk, pos, geo, tq):
    t_rows = x.shape[0]
    tm, hd = TOKEN_TILE, HEAD_DIM
    nq, nk = D_HEADS * hd, D_KV_HEADS * hd
    half = hd // 2
    w = jnp.concatenate([wqkv, _swap_halves_cols(wqkv[:, :nq + nk], half)], axis=1).astype(BF16)
    rows, cols = pos // GRID_W, pos % GRID_W

    def tables(g, scale):
        cr, sr = _rope_tables(rows, g[:half], scale)
        cc, sc = _rope_tables(cols, g[half:], scale)
        reps = LANES // hd
        return jnp.concatenate([cr, cc] * reps, axis=-1), jnp.concatenate([sr, sc] * reps, axis=-1)

    cq, sq = tables(gq, hd ** -0.5 * LOG2E)
    ck, sk = tables(gk, 1.0)
    ones = _segment_ones(hd)
    head_major = lambda n: pl.BlockSpec((n, tm, hd), lambda i: (0, i, 0))
    tab = _position_spec(geo, tm, LANES)
    return pl.pallas_call(
        _proj_d_kernel,
        grid=(t_rows // tm,),
        in_specs=[pl.BlockSpec((tm, D_MODEL), lambda i: (i, 0)), _const_spec((1, D_MODEL)),
                  _const_spec(w.shape), _const_spec(ones.shape), tab, tab, tab, tab],
        out_specs=[pl.BlockSpec((D_HEADS, tm // tq, hd, tq), lambda i: (0, i, 0, 0)),
                   head_major(D_KV_HEADS),
                   pl.BlockSpec((D_KV_HEADS, 1, hd + BF16_SUBLANES, tm), lambda i: (0, i, 0, 0))],
        out_shape=[jax.ShapeDtypeStruct((D_HEADS, t_rows // tq, hd, tq), BF16),
                   jax.ShapeDtypeStruct((D_KV_HEADS, t_rows, hd), BF16),
                   jax.ShapeDtypeStruct((D_KV_HEADS, t_rows // tm, hd + BF16_SUBLANES, tm), BF16)],
        compiler_params=_params(("parallel",)),
        name="proj_d",
    )(x, gn.reshape(1, D_MODEL), w, ones, cq, sq, ck, sk)


def _proj_b_kernel(x_ref, gn_ref, w1_ref, gcq_ref, gckv_ref, wq_ref, wqs_ref, wk_ref, wv_ref,
                   cq_ref, sq_ref, ck_ref, sk_ref, qt_ref, k_ref, vt_ref):
    h = _rms(x_ref[...], gn_ref[...]).astype(BF16)
    y1 = _dot(h, w1_ref[...])
    c_q = _rms(y1[:, :B_Q_LORA], gcq_ref[...]).astype(BF16)
    c_kv = _rms(y1[:, B_Q_LORA:B_Q_LORA + B_KV_LORA], gckv_ref[...]).astype(BF16)
    off = B_Q_LORA + B_KV_LORA
    kr = y1[:, off:off + B_PAD]
    krs = y1[:, off + B_PAD:off + 2 * B_PAD]
    q = _dot(c_q, wq_ref[...])
    qs = _dot(c_q, wqs_ref[...])
    kn = _dot(c_kv, wk_ref[...])
    v = _dot(c_kv, wv_ref[...])
    cq, sq, ck, sk = cq_ref[...], sq_ref[...], ck_ref[...], sk_ref[...]
    krs_s = krs * sk
    for n in range(B_HEADS):
        t = q[:, n * B_PAD:(n + 1) * B_PAD]
        ts = qs[:, n * B_PAD:(n + 1) * B_PAD]
        qt_ref[n, 0] = ((t * cq + ts * sq) * _head_norm(t, B_QK)).T.astype(BF16)
        t = kn[:, n * B_PAD:(n + 1) * B_PAD] + kr
        k_ref[n] = ((t * ck + krs_s) * _head_norm(t, B_QK)).astype(BF16)
    v_t = v.T
    for n in range(B_HEADS):
        _store_vt(vt_ref, n, v_t[n * B_V:(n + 1) * B_V])


def _pad_heads(w, lo, width):
    k, nh, _ = w.shape
    out = jnp.zeros((k, nh, B_PAD), w.dtype).at[:, :, lo:lo + width].set(w)
    return out.reshape(k, nh * B_PAD)


def _proj_b_call(x, gn, wdq, gcq, wuq, wdkv, gckv, wukv, gq, gk, pos, geo):
    t_rows = x.shape[0]
    tm = TOKEN_TILE
    wuq3 = wuq.reshape(B_Q_LORA, B_HEADS, B_QK)
    wq = _pad_heads(wuq3, 0, B_QK).astype(BF16)
    wuq_rope_sw = _swap_halves_cols(wuq3[:, :, B_NOPE:].reshape(B_Q_LORA, -1), B_ROPE)
    wqs = _pad_heads(wuq_rope_sw.reshape(B_Q_LORA, B_HEADS, B_ROPE), B_NOPE, B_ROPE).astype(BF16)
    wukv3 = wukv.reshape(B_KV_LORA, B_HEADS, B_NOPE + B_V)
    wk = _pad_heads(wukv3[:, :, :B_NOPE], 0, B_NOPE).astype(BF16)
    wv = wukv3[:, :, B_NOPE:].reshape(B_KV_LORA, B_HEADS * B_V).astype(BF16)
    w_kr = wdkv[:, B_KV_LORA:]
    kr_pad = _pad_heads(w_kr[:, None, :], B_NOPE, B_ROPE)
    krs_pad = _pad_heads(_swap_halves_cols(w_kr, B_ROPE)[:, None, :], B_NOPE, B_ROPE)
    w1 = jnp.concatenate([wdq, wdkv[:, :B_KV_LORA], kr_pad, krs_pad], axis=1).astype(BF16)

    def tables(g, scale):
        c, s = _rope_tables(pos, g[B_NOPE:], scale)
        n_pos = pos.shape[0]
        zeros = jnp.zeros((n_pos, B_PAD - B_QK), F32)
        c_full = jnp.concatenate(
            [jnp.broadcast_to(g[None, :B_NOPE] * scale, (n_pos, B_NOPE)), c, zeros], axis=-1)
        s_full = jnp.concatenate([jnp.zeros((n_pos, B_NOPE), F32), s, zeros], axis=-1)
        return c_full, s_full

    cq, sq = tables(gq, B_QK ** -0.5 * LOG2E)
    ck, sk = tables(gk, 1.0)
    tab = _position_spec(geo, tm, B_PAD)
    head_major = lambda w_: pl.BlockSpec((B_HEADS, tm, w_), lambda i: (0, i, 0))
    return pl.pallas_call(
        _proj_b_kernel,
        grid=(t_rows // tm,),
        in_specs=[pl.BlockSpec((tm, D_MODEL), lambda i: (i, 0)), _const_spec((1, D_MODEL)),
                  _const_spec(w1.shape), _const_spec((1, B_Q_LORA)), _const_spec((1, B_KV_LORA)),
                  _const_spec(wq.shape), _const_spec(wqs.shape), _const_spec(wk.shape),
                  _const_spec(wv.shape), tab, tab, tab, tab],
        out_specs=[pl.BlockSpec((B_HEADS, 1, B_PAD, tm), lambda i: (0, i, 0, 0)),
                   head_major(B_PAD),
                   pl.BlockSpec((B_HEADS, 1, B_V + BF16_SUBLANES, tm), lambda i: (0, i, 0, 0))],
        out_shape=[jax.ShapeDtypeStruct((B_HEADS, t_rows // tm, B_PAD, tm), BF16),
                   jax.ShapeDtypeStruct((B_HEADS, t_rows, B_PAD), BF16),
                   jax.ShapeDtypeStruct((B_HEADS, t_rows // tm, B_V + BF16_SUBLANES, tm), BF16)],
        compiler_params=_params(("parallel",)),
        name="proj_b",
    )(x, gn.reshape(1, D_MODEL), w1, gcq.reshape(1, -1), gckv.reshape(1, -1), wq, wqs, wk, wv,
      cq, sq, ck, sk)


def _proj_c_kernel(x_ref, gn_ref, w_ref, gq_ref, gk_ref, e_ref, *refs):
    hd = HEAD_DIM
    width = C_HEADS * hd
    outs, scr_ref = refs[:-1], refs[-1]
    tm = x_ref.shape[0]
    h = _rms(x_ref[...], gn_ref[...]).astype(BF16)
    y = _dot(h, w_ref[...])
    gains = (gq_ref[...], gk_ref[...])
    for g, (_, dil) in enumerate(C_PATTERNS):
        for j in range(3):
            base = (g * 3 + j) * width
            val = y[:, base: base + width]
            if j < 2:
                val = val * _segment_rsqrt(val, e_ref, hd) * gains[j]
            out_ref = outs[g * 3 + j]
            if dil == 1:
                out_ref[...] = val.astype(BF16)
            else:
                for s in range(width // LANES):
                    scr_ref[s] = val[:, s * LANES:(s + 1) * LANES]
                for r in range(dil):
                    for s in range(width // LANES):
                        rows = scr_ref[s, pl.ds(r, tm // dil, stride=dil), :]
                        col = r * width + s * LANES
                        out_ref[:, col:col + LANES] = rows.astype(BF16)


def _proj_c_call(x, gn, wqkv, gq, gk):
    t_rows = x.shape[0]
    tm, hd = TOKEN_TILE, HEAD_DIM
    width = C_HEADS * hd
    w = wqkv.astype(BF16)
    ones = _segment_ones(hd)
    specs, shapes = [], []
    for _, dil in C_PATTERNS:
        specs += [pl.BlockSpec((tm // dil, dil * width), lambda i: (i, 0))] * 3
        shapes += [jax.ShapeDtypeStruct((t_rows // dil, dil * width), BF16)] * 3
    outs = pl.pallas_call(
        _proj_c_kernel,
        grid=(t_rows // tm,),
        in_specs=[pl.BlockSpec((tm, D_MODEL), lambda i: (i, 0)), _const_spec((1, D_MODEL)),
                  _const_spec(w.shape), _const_spec((1, width)), _const_spec((1, width)),
                  _const_spec(ones.shape)],
        out_specs=specs,
        out_shape=shapes,
        scratch_shapes=[pltpu.VMEM((width // LANES, tm, LANES), F32)],
        compiler_params=_params(("parallel",)),
        name="proj_c",
    )(x, gn.reshape(1, D_MODEL), w, jnp.tile(gq * (hd ** -0.5 * LOG2E), C_HEADS).reshape(1, width),
      jnp.tile(gk, C_HEADS).reshape(1, width), ones)
    return [outs[3 * g:3 * g + 3] for g in range(len(C_PATTERNS))]


def _band_masked_dist(row0, lo, hi, sub, w):
    nk = sub + 2 * w
    r = lax.broadcasted_iota(jnp.int32, (sub, nk), 0)
    c = lax.broadcasted_iota(jnp.int32, (sub, nk), 1)
    dist = jnp.abs(r + w - c)
    jpos = row0 - w + lax.broadcasted_iota(jnp.int32, (1, nk), 1)
    outside = jnp.where(jpos < lo, w + 1, 0) + jnp.where(jpos >= hi, w + 1, 0)
    return jnp.where(dist + outside <= w, dist.astype(F32), MASK_DIST)


def _two_stage(n_items, first, second, ahead=2):
    pending = {i: first(i) for i in range(min(ahead, n_items))}
    results = []
    for i in range(n_items):
        if i + ahead < n_items:
            pending[i + ahead] = first(i + ahead)
        results.append(second(i, pending.pop(i)))
    return results


def _with_ones(vv):
    return jnp.concatenate([vv, jnp.ones((vv.shape[0], BF16_SUBLANES), BF16)], axis=-1)


def _band_softmax_pv(s, vv_ones, sink):
    m = jnp.max(s, axis=-1, keepdims=True)
    if sink is not None:
        m = jnp.maximum(m, sink)
    hd = vv_ones.shape[-1] - BF16_SUBLANES
    pv = _dot(jnp.exp2(s - m).astype(BF16), vv_ones)
    den = pv[:, hd:hd + 1]
    if sink is not None:
        den = den + jnp.exp2(sink - m)
    return pv[:, :hd] / den, m + jnp.log2(den)


def _attn_a_kernel(slope_ref, sink_ref, q_ref, kp_ref, kc_ref, kn_ref, vp_ref, vc_ref, vn_ref,
                   o_ref, *, tq, p_len, s_len):
    w, sub = A_HALF_WINDOW, BAND_SUB
    group = A_HEADS // A_KV_HEADS
    kvh = pl.program_id(0)
    row0 = pl.program_id(1) * tq
    lo, hi = _seq_bounds(row0, p_len, s_len)
    kk = jnp.concatenate([kp_ref[0], kc_ref[0], kn_ref[0]], axis=0)
    vv = _with_ones(jnp.concatenate([vp_ref[0], vc_ref[0], vn_ref[0]], axis=0))
    sink = jnp.concatenate(
        [jnp.full((sub, 1), sink_ref[kvh * group + g], F32) for g in range(group)], axis=0)

    def keys(r):
        return slice(r * sub, (r + 1) * sub + 2 * w)

    def logits(r):
        md = _band_masked_dist(row0 + r * sub, lo, hi, sub, w)
        bias = jnp.concatenate([md * slope_ref[kvh * group + g] for g in range(group)], axis=0)
        q = q_ref[:, r * sub:(r + 1) * sub, :].reshape(group * sub, HEAD_DIM)
        return _dot_nt(q, kk[keys(r)]) + bias

    tiles = _two_stage(tq // sub, logits, lambda r, s: _band_softmax_pv(s, vv[keys(r)], sink)[0])
    outs = [[o[g * sub:(g + 1) * sub] for o in tiles] for g in range(group)]
    o_ref[...] = jnp.concatenate(
        [jnp.concatenate(rows, axis=0) for rows in outs], axis=-1).astype(BF16)


def _attn_a_call(q, k, v, sink, geo):
    p_len, s_len, t_rows = geo
    tq, w, hd = 2048, A_HALF_WINDOW, HEAD_DIM
    group = A_HEADS // A_KV_HEADS
    per = tq // w
    last = t_rows // w - 1
    prev = pl.BlockSpec((1, w, hd), lambda h, i: (h, jnp.maximum(i * per - 1, 0), 0))
    cur = pl.BlockSpec((1, tq, hd), lambda h, i: (h, i, 0))
    nxt = pl.BlockSpec((1, w, hd), lambda h, i: (h, jnp.minimum((i + 1) * per, last), 0))
    smem = pl.BlockSpec(memory_space=pltpu.SMEM)
    slopes = jnp.asarray(-LOG2E * _alibi_slopes(A_HEADS), F32)
    return pl.pallas_call(
        functools.partial(_attn_a_kernel, tq=tq, p_len=p_len, s_len=s_len),
        grid=(A_KV_HEADS, t_rows // tq),
        in_specs=[smem, smem, pl.BlockSpec((group, tq, hd), lambda h, i: (h, i, 0)),
                  prev, cur, nxt, prev, cur, nxt],
        out_specs=pl.BlockSpec((tq, group * hd), lambda h, i: (i, h)),
        out_shape=jax.ShapeDtypeStruct((t_rows, A_HEADS * hd), BF16),
        compiler_params=_params(("parallel", "parallel")),
        name="attn_a",
    )(slopes, sink.astype(F32) * LOG2E, q, k, k, k, v, v, v)


def _attn_c_kernel(q_ref, kp_ref, kc_ref, kn_ref, vp_ref, vc_ref, vn_ref, o_ref, lse_ref,
                   *, tq, dil, p_len, s_len):
    w, hd, sub = C_STEPS, HEAD_DIM, min(BAND_SUB, tq)
    row0 = pl.program_id(0) * tq
    lo, hi = _seq_bounds(row0, p_len // dil, s_len // dil)
    slopes = -LOG2E * dil * _alibi_slopes(C_HEADS)
    mds = [_band_masked_dist(row0 + r * sub, lo, hi, sub, w) for r in range(tq // sub)]
    n_sub = tq // sub

    heads = {}

    def head_rows(n):
        if n not in heads:
            hs = slice(n * hd, (n + 1) * hd)
            kk = jnp.concatenate([kp_ref[:, hs], kc_ref[:, hs], kn_ref[:, hs]], axis=0)
            vv = jnp.concatenate([vp_ref[:, hs], vc_ref[:, hs], vn_ref[:, hs]], axis=0)
            heads[n] = (q_ref[:, hs], kk, _with_ones(vv))
        return heads[n]

    def logits(i):
        n, r = divmod(i, n_sub)
        q, kk, _ = head_rows(n)
        keys = kk[r * sub:(r + 1) * sub + 2 * w]
        return _dot_nt(q[r * sub:(r + 1) * sub], keys) + mds[r] * float(slopes[n])

    def attend(i, s):
        n, r = divmod(i, n_sub)
        o, lse2 = _band_softmax_pv(s, head_rows(n)[2][r * sub:(r + 1) * sub + 2 * w], None)
        return o, jnp.broadcast_to(lse2 * (1.0 / LOG2E), (sub, hd))

    tiles = _two_stage(C_HEADS * n_sub, logits, attend, ahead=3 if n_sub > 1 else 2)
    outs = [jnp.concatenate([t[0] for t in tiles[n * n_sub:(n + 1) * n_sub]], axis=0)
            for n in range(C_HEADS)]
    lses = [jnp.concatenate([t[1] for t in tiles[n * n_sub:(n + 1) * n_sub]], axis=0)
            for n in range(C_HEADS)]
    o_ref[...] = jnp.concatenate(outs, axis=-1)
    lse_ref[...] = jnp.concatenate(lses, axis=-1)


def _attn_c_call(q, k, v, dil, geo):
    p_len, s_len, t_rows = geo
    w, width = C_STEPS, C_HEADS * HEAD_DIM
    rows = t_rows // dil
    tq = min(1024, s_len // dil)
    per = tq // w
    last = rows // w - 1
    prev = pl.BlockSpec((w, width), lambda i, r: (jnp.maximum(i * per - 1, 0), r))
    cur = pl.BlockSpec((tq, width), lambda i, r: (i, r))
    nxt = pl.BlockSpec((w, width), lambda i, r: (jnp.minimum((i + 1) * per, last), r))
    shape = jax.ShapeDtypeStruct((rows, dil * width), F32)
    return pl.pallas_call(
        functools.partial(_attn_c_kernel, tq=tq, dil=dil, p_len=p_len, s_len=s_len),
        grid=(rows // tq, dil),
        in_specs=[cur, prev, cur, nxt, prev, cur, nxt],
        out_specs=[cur, cur],
        out_shape=[shape, shape],
        compiler_params=_params(("parallel", "parallel")),
        name="attn_c_d%d" % dil,
    )(q, k, k, k, v, v, v)


def _attn_dense_kernel(qt_ref, k_ref, vt_ref, o_ref, sa_ref, sb_ref, sc_ref, sd_ref, acc_ref,
                       *, nq, tq, tk, p_len, s_len):
    dv = vt_ref.shape[-2] - BF16_SUBLANES
    width = nq * tq
    shift = int(math.log2(tk))

    def first_chunk(j):
        row0 = (pl.program_id(1) * Q_TILES_PER_STEP + j) * tq
        lo, hi = _seq_bounds(row0, p_len, s_len)
        return lax.shift_right_logical(lo, shift), lax.shift_right_logical(hi - lo, shift)

    def scores(j, c, st_ref):
        start = pl.multiple_of(c * tk, tk)
        qt = jnp.concatenate([qt_ref[g, j] for g in range(nq)], axis=-1)
        st = _dot(k_ref[0, pl.ds(start, tk), :], qt).astype(BF16)
        st_ref[...] = st
        return jnp.max(st, axis=0, keepdims=True).astype(F32)

    def update(c, st_ref, mc, m, acc):
        m_new = jnp.maximum(m, mc)
        alpha = jnp.exp2(m - m_new)
        pt = jnp.exp2(st_ref[...] - m_new.astype(BF16))
        acc_ref[...] = alpha * acc_ref[...] + _dot(vt_ref[0, c], pt)
        return m_new, acc

    def q_tile(j, mcs, start_next):
        first, n_chunks = first_chunk(j)

        def quad(i, carry):
            m, acc, mc_a, mc_b = carry
            c = first + 4 * i
            mc_c = scores(j, c + 2, sc_ref)
            m, acc = update(c, sa_ref, mc_a, m, acc)
            mc_d = scores(j, c + 3, sd_ref)
            m, acc = update(c + 1, sb_ref, mc_b, m, acc)
            mc_a = scores(j, c + 4, sa_ref)
            m, acc = update(c + 2, sc_ref, mc_c, m, acc)
            mc_b = scores(j, c + 5, sb_ref)
            m, acc = update(c + 3, sd_ref, mc_d, m, acc)
            return m, acc, mc_a, mc_b

        acc_ref[...] = jnp.zeros_like(acc_ref)
        carry = (jnp.full((1, width), NEG_INF, F32), 0, *mcs)
        n_body = lax.shift_right_logical(n_chunks, 2) - 1
        carry = lax.fori_loop(0, lax.shift_right_logical(n_body, 1),
                              lambda i, cr: quad(2 * i + 1, quad(2 * i, cr)), carry)
        carry = lax.fori_loop(0, n_body & 1, lambda _, cr: quad(n_body - 1, cr), carry)
        m, acc, mc_a, mc_b = carry
        c = first + n_chunks - 4
        mc_c = scores(j, c + 2, sc_ref)
        m, acc = update(c, sa_ref, mc_a, m, acc)
        mc_d = scores(j, c + 3, sd_ref)
        m, acc = update(c + 1, sb_ref, mc_b, m, acc)
        if start_next:
            first_next, _ = first_chunk(j + 1)
            mc_a = scores(j + 1, first_next, sa_ref)
        m, acc = update(c + 2, sc_ref, mc_c, m, acc)
        if start_next:
            mc_b = scores(j + 1, first_next + 1, sb_ref)
        m, acc = update(c + 3, sd_ref, mc_d, m, acc)
        o = acc_ref[:dv, :] / acc_ref[dv:dv + 1, :]
        rows = pl.ds(pl.multiple_of(j * tq, tq), tq)
        heads = [o[:, g * tq:(g + 1) * tq].T.astype(BF16) for g in range(nq)]
        if nq == 1:
            o_ref[0, rows, :] = heads[0]
        else:
            o_ref[rows, :] = jnp.concatenate(heads, axis=-1)
        return mc_a, mc_b

    first, _ = first_chunk(0)
    mcs = (scores(0, first, sa_ref), scores(0, first + 1, sb_ref))
    mcs = lax.fori_loop(0, Q_TILES_PER_STEP - 1, lambda j, mcs: q_tile(j, mcs, True), mcs)
    q_tile(Q_TILES_PER_STEP - 1, mcs, False)


def _attn_dense_call(qt, k, vt, geo):
    p_len, s_len, t_rows = geo
    hq, n_tiles, dq, tq = qt.shape
    hk, n_chunks, rows_v, tk = vt.shape
    dv = rows_v - BF16_SUBLANES
    nq, per = hq // hk, Q_TILES_PER_STEP
    assert s_len % (4 * tk) == 0 and p_len % (4 * tk) == 0 and n_tiles % per == 0
    return pl.pallas_call(
        functools.partial(_attn_dense_kernel, nq=nq, tq=tq, tk=tk, p_len=p_len, s_len=s_len),
        grid=(hk, n_tiles // per),
        in_specs=[pl.BlockSpec((nq, per, dq, tq), lambda h, i: (h, i, 0, 0)),
                  pl.BlockSpec((1, t_rows, dq), lambda h, i: (h, 0, 0)),
                  pl.BlockSpec((1, n_chunks, dv + BF16_SUBLANES, tk), lambda h, i: (h, 0, 0, 0))],
        out_specs=(pl.BlockSpec((1, per * tq, dv), lambda h, i: (h, i, 0)) if nq == 1
                   else pl.BlockSpec((per * tq, nq * dv), lambda h, i: (i, h))),
        out_shape=jax.ShapeDtypeStruct((hq, t_rows, dv) if nq == 1 else (t_rows, hq * dv), BF16),
        scratch_shapes=[pltpu.VMEM((tk, nq * tq), BF16)] * 4 + [pltpu.VMEM((rows_v, nq * tq), F32)],
        compiler_params=_params(("parallel", "parallel")),
        name="attn_dense_%d" % dq,
    )(qt, k, vt)


def _trunk(x, p, geo, pos):
    for i in range(DEPTH):
        mixer, j = i % 4, i // 4
        x = _ffn_call(x, p["norm_ffn1"][i], p["ffn1_wg"][i], p["ffn1_wu"][i], p["ffn1_wd"][i])
        gn = p["norm_mix"][i]
        if mixer == 0:
            q, k, v = _proj_a_call(x, gn, p["a_wqkv"][j], p["a_gq"][j], p["a_gk"][j])
            pre = (_attn_a_call(q, k, v, p["a_sink"][j], geo), p["a_wo"][j])
        elif mixer == 1:
            qt, k, vt = _proj_b_call(x, gn, p["b_wdq"][j], p["b_gcq"][j], p["b_wuq"][j], p["b_wdkv"][j],
                                     p["b_gckv"][j], p["b_wukv"][j], p["b_gq"][j], p["b_gk"][j], pos, geo)
            pre = (_attn_dense_call(qt, k, vt, geo), p["b_wo"][j])
        elif mixer == 2:
            qkv = _proj_c_call(x, gn, p["c_wqkv"][j], p["c_gq"][j], p["c_gk"][j])
            res = [_attn_c_call(*qkv[g], dil, geo) for g, (_, dil) in enumerate(C_PATTERNS)]
            pre = ([r[0] for r in res], [r[1] for r in res], p["c_wo"][j])
        else:
            qt, k, vt = _proj_d_call(x, gn, p["d_wqkv"][j], p["d_gq"][j], p["d_gk"][j], pos, geo, tq=128)
            pre = (_attn_dense_call(qt, k, vt, geo), p["d_wo"][j])
        ffn2 = (p["norm_ffn2"][i], p["ffn2_wg"][i], p["ffn2_wu"][i], p["ffn2_wd"][i])
        if i < DEPTH - 1:
            x = _ffn_call(x, *ffn2, pre=pre)
        else:
            x = (_ffn_call(x, *ffn2, pre=pre, rows=(0, geo[0])),
                 _ffn_call(x, *ffn2, pre=pre, rows=(geo[0], geo[2] - geo[0])))
    return x


def kernel(x_prompt, x_sample, norm_ffn1, ffn1_wg, ffn1_wu, ffn1_wd, norm_mix, norm_ffn2, ffn2_wg, ffn2_wu, ffn2_wd, a_wqkv, a_gq, a_gk, a_sink, a_wo, b_wdq, b_gcq, b_wuq, b_wdkv, b_gckv, b_wukv, b_gq, b_gk, b_wo, c_wqkv, c_gq, c_gk, c_wo, d_wqkv, d_gq, d_gk, d_wo):
    p = dict(norm_ffn1=norm_ffn1, ffn1_wg=ffn1_wg, ffn1_wu=ffn1_wu, ffn1_wd=ffn1_wd,
             norm_mix=norm_mix, norm_ffn2=norm_ffn2, ffn2_wg=ffn2_wg, ffn2_wu=ffn2_wu,
             ffn2_wd=ffn2_wd, a_wqkv=a_wqkv, a_gq=a_gq, a_gk=a_gk, a_sink=a_sink, a_wo=a_wo,
             b_wdq=b_wdq, b_gcq=b_gcq, b_wuq=b_wuq, b_wdkv=b_wdkv, b_gckv=b_gckv,
             b_wukv=b_wukv, b_gq=b_gq, b_gk=b_gk, b_wo=b_wo, c_wqkv=c_wqkv, c_gq=c_gq,
             c_gk=c_gk, c_wo=c_wo, d_wqkv=d_wqkv, d_gq=d_gq, d_gk=d_gk, d_wo=d_wo)
    n_p, p_seq, _ = x_prompt.shape
    n_s, s_len, _ = x_sample.shape
    assert n_p == 1 and s_len & (s_len - 1) == 0 and p_seq % s_len == 0
    p_len = n_p * p_seq
    t_rows = p_len + n_s * s_len
    geo = (p_len, s_len, t_rows)
    pos = jnp.arange(p_len, dtype=jnp.int32)
    x = (x_prompt.reshape(p_len, D_MODEL), x_sample.reshape(n_s * s_len, D_MODEL))
    y_p, y_s = _trunk(x, p, geo, pos)
    return y_p.reshape(x_prompt.shape), y_s.reshape(x_sample.shape)
```

```python
import functools
import math

import numpy as np
import jax
import jax.numpy as jnp
from jax import lax
from jax.experimental import pallas as pl
from jax.experimental.pallas import tpu as pltpu

F32 = jnp.float32
BF16 = jnp.bfloat16

D_MODEL = 1024
DEPTH = 4
HEAD_DIM = 64
D_FF = 2816
NORM_EPS = 1e-6
NEG_INF = -1e30
GRID_W = 64
ROPE_THETA = 10000.0
LOG2E = math.log2(math.e)
LANES = 128
MXU_TILE = 256
BF16_SUBLANES = 16
BAND_SUB = 128
Q_TILES_PER_STEP = 8
MASK_DIST = 1e33

A_HEADS, A_KV_HEADS, A_HALF_WINDOW = 16, 4, 128
B_HEADS, B_Q_LORA, B_KV_LORA, B_NOPE, B_ROPE, B_V = 16, 512, 256, 64, 32, 64
B_QK = B_NOPE + B_ROPE
B_PAD = 128
C_HEADS = 8
C_PATTERNS = ((128, 1), (512, 4), (2048, 16))
C_STEPS = 64
D_HEADS, D_KV_HEADS = 16, 4

FF_CHUNK = 256
N_FF_CHUNKS = D_FF // FF_CHUNK
TOKEN_TILE = 512
FFN_TILE = 1024
VMEM_LIMIT = 56 * 1024 * 1024


def _alibi_slopes(n):
    return 2.0 ** (-8.0 * np.arange(1, n + 1) / n)


def _params(sem, vmem=VMEM_LIMIT):
    return pltpu.CompilerParams(dimension_semantics=sem, vmem_limit_bytes=vmem)


def _const_spec(shape):
    nd = len(shape)
    return pl.BlockSpec(tuple(shape), lambda *_: (0,) * nd, pipeline_mode=pl.Buffered(1))


def _seq_bounds(row0, p_len, s_len):
    in_prompt = row0 < p_len
    b = lax.shift_right_logical(jnp.maximum(row0 - p_len, 0), int(math.log2(s_len)))
    lo = jnp.where(in_prompt, 0, p_len + b * s_len)
    hi = jnp.where(in_prompt, p_len, lo + s_len)
    return lo, hi


def _rms(x, g):
    return x * lax.rsqrt(jnp.mean(x * x, axis=-1, keepdims=True) + NORM_EPS) * g


def _dot(a, b):
    return jnp.dot(a, b, preferred_element_type=F32)


def _dot_nt(a, b):
    return lax.dot_general(a, b, (((1,), (1,)), ((), ())), preferred_element_type=F32)


def _ffn_kernel(*refs, mode, first_tiles=0):
    if mode == "plain":
        x_ref, gn_ref, wg_ref, wu_ref, wd_ref, out_ref, acc_ref, act_a, act_b = refs
        x = x_ref[...]
    elif mode == "plain2":
        xa_ref, xb_ref, gn_ref, wg_ref, wu_ref, wd_ref, out_ref, acc_ref, act_a, act_b = refs
        x = jnp.where(pl.program_id(0) < first_tiles, xa_ref[...], xb_ref[...])
    elif mode == "proj":
        (x_ref, o_ref, wo_ref, gn_ref, wg_ref, wu_ref, wd_ref, out_ref, acc_ref, act_a,
         act_b) = refs
        if len(o_ref.shape) == 3:
            o = jnp.concatenate([o_ref[n] for n in range(o_ref.shape[0])], axis=-1)
        else:
            o = o_ref[...]
        x = x_ref[...] + _dot(o, wo_ref[...])
    else:
        (x_ref, o0, o1, o2, l0, l1, l2, wo_ref, gn_ref, wg_ref, wu_ref, wd_ref,
         out_ref, acc_ref, act_a, act_b, *scr) = refs
        tm, width = x_ref.shape[0], wo_ref.shape[0]

        def token_major(ref, dil, scr_ref):
            slabs = width // LANES
            for r in range(dil):
                for s in range(slabs):
                    col = r * width + s * LANES
                    scr_ref[s, pl.ds(r, tm // dil, stride=dil), :] = ref[:, col:col + LANES]
            return jnp.concatenate([scr_ref[s] for s in range(slabs)], axis=-1)

        dils = [d for _, d in C_PATTERNS]
        assert dils[0] == 1
        oa, la = o0[...], l0[...]
        ob, lb = token_major(o1, dils[1], scr[0]), token_major(l1, dils[1], scr[1])
        oc, lc = token_major(o2, dils[2], scr[2]), token_major(l2, dils[2], scr[3])
        m = jnp.maximum(jnp.maximum(la, lb), lc)
        ea, eb, ec = jnp.exp(la - m), jnp.exp(lb - m), jnp.exp(lc - m)
        o = (ea * oa + eb * ob + ec * oc) / (ea + eb + ec)
        x = x_ref[...] + _dot(o.astype(BF16), wo_ref[...])

    h = _rms(x, gn_ref[...]).astype(BF16)

    def gate_up(c, a_ref):
        cols = pl.ds(pl.multiple_of(c * FF_CHUNK, FF_CHUNK), FF_CHUNK)
        g = _dot(h, wg_ref[:, cols])
        u = _dot(h, wu_ref[:, cols])
        a_ref[...] = (g * jax.nn.sigmoid(g) * u).astype(BF16)

    def down(c, a_ref):
        acc_ref[...] += _dot(a_ref[...], wd_ref[c])

    assert N_FF_CHUNKS % 2 == 1
    gate_up(0, act_a)
    gate_up(1, act_b)
    acc_ref[...] = _dot(act_a[...], wd_ref[0])

    def pair(i, carry):
        c = 2 * i + 1
        gate_up(c + 1, act_a)
        down(c, act_b)
        gate_up(c + 2, act_b)
        down(c + 1, act_a)
        return carry

    lax.fori_loop(0, N_FF_CHUNKS // 2 - 1, pair, 0)
    gate_up(N_FF_CHUNKS - 1, act_a)
    down(N_FF_CHUNKS - 2, act_b)
    down(N_FF_CHUNKS - 1, act_a)
    out_ref[...] = x + 0.5 * acc_ref[...]


def _ffn_weights(wg, wu, wd):
    wg3 = wg.astype(BF16)
    wu3 = wu.astype(BF16)
    wd3 = wd.astype(BF16).reshape(N_FF_CHUNKS, FF_CHUNK, D_MODEL)
    return wg3, wu3, wd3


def _ffn_call(x, gn, wg, wu, wd, pre=None, rows=None):
    tm = TOKEN_TILE if pre is not None and len(pre) == 3 else FFN_TILE
    xs = x if isinstance(x, tuple) else (x,)
    off, n_tiles = (rows[0] // tm, rows[1] // tm) if rows is not None else (
        0, sum(a.shape[0] for a in xs) // tm)
    wg3, wu3, wd3 = _ffn_weights(wg, wu, wd)
    row = lambda w: pl.BlockSpec((tm, w), lambda i: (i + off, 0))
    ffn_args = [gn.reshape(1, D_MODEL), wg3, wu3, wd3]
    ffn_specs = [_const_spec((1, D_MODEL)), _const_spec(wg3.shape), _const_spec(wu3.shape),
                 _const_spec(wd3.shape)]
    scratch = []
    first_tiles = 0
    if len(xs) == 2:
        assert pre is None and rows is None
        first_tiles = xs[0].shape[0] // tm
        last = xs[1].shape[0] // tm - 1
        mode, args = "plain2", list(xs)
        specs = [pl.BlockSpec((tm, D_MODEL), lambda i: (jnp.minimum(i, first_tiles - 1), 0)),
                 pl.BlockSpec((tm, D_MODEL), lambda i: (jnp.clip(i - first_tiles, 0, last), 0))]
    elif pre is None:
        mode, args, specs = "plain", [x], [row(D_MODEL)]
    elif len(pre) == 2:
        o, wo = pre
        wo = wo.astype(BF16)
        mode = "proj"
        args = [x, o, wo]
        o_spec = row(o.shape[1]) if o.ndim == 2 else pl.BlockSpec(
            (o.shape[0], tm, o.shape[2]), lambda i: (0, i + off, 0))
        specs = [row(D_MODEL), o_spec, _const_spec(wo.shape)]
    else:
        outs, lses, wo = pre
        wo = wo.astype(BF16)
        mode = "merge"
        args = [x, *outs, *lses, wo]
        width = wo.shape[0]
        dilated = [pl.BlockSpec((tm // d, d * width), lambda i: (i + off, 0)) for _, d in C_PATTERNS]
        specs = [row(D_MODEL)] + dilated * 2 + [_const_spec(wo.shape)]
        scratch = [pltpu.VMEM((width // LANES, tm, LANES), F32)] * 4
    return pl.pallas_call(
        functools.partial(_ffn_kernel, mode=mode, first_tiles=first_tiles),
        grid=(n_tiles,),
        in_specs=specs + ffn_specs,
        out_specs=pl.BlockSpec((tm, D_MODEL), lambda i: (i, 0)),
        out_shape=jax.ShapeDtypeStruct((n_tiles * tm, D_MODEL), F32),
        scratch_shapes=[pltpu.VMEM((tm, D_MODEL), F32)] + [pltpu.VMEM((tm, FF_CHUNK), BF16)] * 2
        + scratch,
        compiler_params=_params(("parallel",)),
        name="ffn_" + mode,
    )(*args, *ffn_args)


def _head_norm(t, width):
    return lax.rsqrt(jnp.sum(t * t, axis=-1, keepdims=True) * (1.0 / width) + NORM_EPS)


def _segment_ones(seg):
    n = MXU_TILE // seg
    return jnp.kron(jnp.eye(n, dtype=F32), jnp.ones((seg, seg), F32)).astype(BF16)


def _segment_rsqrt(t, e_ref, width):
    sq = t * t
    hi = sq.astype(BF16)
    lo = (sq - hi.astype(F32)).astype(BF16)
    e = e_ref[...]
    sums = [_dot(hi[:, b:b + MXU_TILE], e) + _dot(lo[:, b:b + MXU_TILE], e)
            for b in range(0, t.shape[-1], MXU_TILE)]
    ssq = sums[0] if len(sums) == 1 else jnp.concatenate(sums, axis=-1)
    return lax.rsqrt(ssq * (1.0 / width) + NORM_EPS)


def _store_vt(vt_ref, n, v_t):
    dv, rows = v_t.shape
    vt_ref[n, 0, :dv, :] = v_t.astype(BF16)
    first = lax.broadcasted_iota(jnp.int32, (BF16_SUBLANES, rows), 0) == 0
    vt_ref[n, 0, dv:, :] = jnp.where(first, 1.0, 0.0).astype(BF16)


def _proj_a_kernel(x_ref, gn_ref, w_ref, gain_ref, e_ref, q_ref, k_ref, v_ref):
    hd = HEAD_DIM
    nqk = (A_HEADS + A_KV_HEADS) * hd
    h = _rms(x_ref[...], gn_ref[...]).astype(BF16)
    y = _dot(h, w_ref[...])
    qk = y[:, :nqk]
    qk = (qk * _segment_rsqrt(qk, e_ref, hd) * gain_ref[...]).astype(BF16)
    for n in range(A_HEADS):
        q_ref[n] = qk[:, n * hd:(n + 1) * hd]
    for n in range(A_KV_HEADS):
        k_ref[n] = qk[:, (A_HEADS + n) * hd:(A_HEADS + n + 1) * hd]
        v_ref[n] = y[:, nqk + n * hd: nqk + (n + 1) * hd].astype(BF16)


def _proj_a_call(x, gn, wqkv, gq, gk):
    t_rows = x.shape[0]
    tm, hd = FFN_TILE, HEAD_DIM
    w = wqkv.astype(BF16)
    gain = jnp.concatenate([jnp.tile(gq * (hd ** -0.5 * LOG2E), A_HEADS), jnp.tile(gk, A_KV_HEADS)])
    ones = _segment_ones(hd)
    head_major = lambda n: pl.BlockSpec((n, tm, hd), lambda i: (0, i, 0))
    return pl.pallas_call(
        _proj_a_kernel,
        grid=(t_rows // tm,),
        in_specs=[pl.BlockSpec((tm, D_MODEL), lambda i: (i, 0)), _const_spec((1, D_MODEL)),
                  _const_spec(w.shape), _const_spec((1, gain.shape[0])), _const_spec(ones.shape)],
        out_specs=[head_major(A_HEADS), head_major(A_KV_HEADS), head_major(A_KV_HEADS)],
        out_shape=[jax.ShapeDtypeStruct((A_HEADS, t_rows, hd), BF16),
                   jax.ShapeDtypeStruct((A_KV_HEADS, t_rows, hd), BF16),
                   jax.ShapeDtypeStruct((A_KV_HEADS, t_rows, hd), BF16)],
        compiler_params=_params(("parallel",)),
        name="proj_a",
    )(x, gn.reshape(1, D_MODEL), w, gain.reshape(1, -1), ones)


def _proj_d_kernel(x_ref, gn_ref, w_ref, e_ref, cq_ref, sq_ref, ck_ref, sk_ref, qt_ref, k_ref, vt_ref):
    hd = HEAD_DIM
    nq, nk = D_HEADS * hd, D_KV_HEADS * hd
    tq = qt_ref.shape[-1]
    h = _rms(x_ref[...], gn_ref[...]).astype(BF16)
    y = _dot(h, w_ref[...])
    swapped = nq + 2 * nk
    t, ts = y[:, :nq + nk], y[:, swapped:swapped + nq + nk]
    r = _segment_rsqrt(t, e_ref, hd)
    for b in range((nq + nk) // LANES):
        is_q = b < nq // LANES
        c_ref, s_ref = (cq_ref, sq_ref) if is_q else (ck_ref, sk_ref)
        sl = slice(b * LANES, (b + 1) * LANES)
        blk = (t[:, sl] * c_ref[...] + ts[:, sl] * s_ref[...]) * r[:, sl]
        blk = blk.T.astype(BF16) if is_q else blk.astype(BF16)
        for half in range(LANES // hd):
            n = b * (LANES // hd) + half
            if is_q:
                for j in range(qt_ref.shape[1]):
                    qt_ref[n, j] = blk[half * hd:(half + 1) * hd, j * tq:(j + 1) * tq]
            else:
                k_ref[n - D_HEADS] = blk[:, half * hd:(half + 1) * hd]
    v_t = y[:, nq + nk: nq + 2 * nk].T
    for n in range(D_KV_HEADS):
        _store_vt(vt_ref, n, v_t[n * hd:(n + 1) * hd])


def _swap_halves_cols(w, group):
    k, n = w.shape
    return w.reshape(k, n // group, 2, group // 2)[:, :, ::-1, :].reshape(k, n)


def _position_spec(geo, tm, width):
    p_len, s_len, _ = geo
    p_tiles, s_tiles = p_len // tm, s_len // tm
    return pl.BlockSpec(
        (tm, width), lambda i: (jnp.where(i < p_tiles, i, (i - p_tiles) % s_tiles), 0))


def _rope_tables(pos, gain, scale):
    half = gain.shape[-1] // 2
    inv = ROPE_THETA ** (-jnp.arange(half, dtype=F32) / half)
    ang = pos.astype(F32)[:, None] * inv[None, :]
    cos, sin = lax.optimization_barrier((jnp.cos(ang), jnp.sin(ang)))
    g_sw = jnp.concatenate([gain[half:], gain[:half]])
    c = jnp.concatenate([cos, cos], axis=-1) * gain[None, :] * scale
    s = jnp.concatenate([-sin, sin], axis=-1) * g_sw[None, :] * scale
    return c, s


def _proj_d_call(x, gn, wqkv, gq, gk, pos, geo, tq):
    t_rows = x.shape[0]
    tm, hd = TOKEN_TILE, HEAD_DIM
    nq, nk = D_HEADS * hd, D_KV_HEADS * hd
    half = hd // 2
    w = jnp.concatenate([wqkv, _swap_halves_cols(wqkv[:, :nq + nk], half)], axis=1).astype(BF16)
    rows, cols = pos // GRID_W, pos % GRID_W

    def tables(g, scale):
        cr, sr = _rope_tables(rows, g[:half], scale)
        cc, sc = _rope_tables(cols, g[half:], scale)
        reps = LANES // hd
        return jnp.concatenate([cr, cc] * reps, axis=-1), jnp.concatenate([sr, sc] * reps, axis=-1)

    cq, sq = tables(gq, hd ** -0.5 * LOG2E)
    ck, sk = tables(gk, 1.0)
    ones = _segment_ones(hd)
    head_major = lambda n: pl.BlockSpec((n, tm, hd), lambda i: (0, i, 0))
    tab = _position_spec(geo, tm, LANES)
    return pl.pallas_call(
        _proj_d_kernel,
        grid=(t_rows // tm,),
        in_specs=[pl.BlockSpec((tm, D_MODEL), lambda i: (i, 0)), _const_spec((1, D_MODEL)),
                  _const_spec(w.shape), _const_spec(ones.shape), tab, tab, tab, tab],
        out_specs=[pl.BlockSpec((D_HEADS, tm // tq, hd, tq), lambda i: (0, i, 0, 0)),
                   head_major(D_KV_HEADS),
                   pl.BlockSpec((D_KV_HEADS, 1, hd + BF16_SUBLANES, tm), lambda i: (0, i, 0, 0))],
        out_shape=[jax.ShapeDtypeStruct((D_HEADS, t_rows // tq, hd, tq), BF16),
                   jax.ShapeDtypeStruct((D_KV_HEADS, t_rows, hd), BF16),
                   jax.ShapeDtypeStruct((D_KV_HEADS, t_rows // tm, hd + BF16_SUBLANES, tm), BF16)],
        compiler_params=_params(("parallel",)),
        name="proj_d",
    )(x, gn.reshape(1, D_MODEL), w, ones, cq, sq, ck, sk)


def _proj_b_kernel(x_ref, gn_ref, w1_ref, gcq_ref, gckv_ref, wq_ref, wqs_ref, wk_ref, wv_ref,
                   cq_ref, sq_ref, ck_ref, sk_ref, qt_ref, k_ref, vt_ref):
    h = _rms(x_ref[...], gn_ref[...]).astype(BF16)
    y1 = _dot(h, w1_ref[...])
    c_q = _rms(y1[:, :B_Q_LORA], gcq_ref[...]).astype(BF16)
    c_kv = _rms(y1[:, B_Q_LORA:B_Q_LORA + B_KV_LORA], gckv_ref[...]).astype(BF16)
    off = B_Q_LORA + B_KV_LORA
    kr = y1[:, off:off + B_PAD]
    krs = y1[:, off + B_PAD:off + 2 * B_PAD]
    q = _dot(c_q, wq_ref[...])
    qs = _dot(c_q, wqs_ref[...])
    kn = _dot(c_kv, wk_ref[...])
    v = _dot(c_kv, wv_ref[...])
    cq, sq, ck, sk = cq_ref[...], sq_ref[...], ck_ref[...], sk_ref[...]
    krs_s = krs * sk
    for n in range(B_HEADS):
        t = q[:, n * B_PAD:(n + 1) * B_PAD]
        ts = qs[:, n * B_PAD:(n + 1) * B_PAD]
        qt_ref[n, 0] = ((t * cq + ts * sq) * _head_norm(t, B_QK)).T.astype(BF16)
        t = kn[:, n * B_PAD:(n + 1) * B_PAD] + kr
        k_ref[n] = ((t * ck + krs_s) * _head_norm(t, B_QK)).astype(BF16)
    v_t = v.T
    for n in range(B_HEADS):
        _store_vt(vt_ref, n, v_t[n * B_V:(n + 1) * B_V])


def _pad_heads(w, lo, width):
    k, nh, _ = w.shape
    out = jnp.zeros((k, nh, B_PAD), w.dtype).at[:, :, lo:lo + width].set(w)
    return out.reshape(k, nh * B_PAD)


def _proj_b_call(x, gn, wdq, gcq, wuq, wdkv, gckv, wukv, gq, gk, pos, geo):
    t_rows = x.shape[0]
    tm = TOKEN_TILE
    wuq3 = wuq.reshape(B_Q_LORA, B_HEADS, B_QK)
    wq = _pad_heads(wuq3, 0, B_QK).astype(BF16)
    wuq_rope_sw = _swap_halves_cols(wuq3[:, :, B_NOPE:].reshape(B_Q_LORA, -1), B_ROPE)
    wqs = _pad_heads(wuq_rope_sw.reshape(B_Q_LORA, B_HEADS, B_ROPE), B_NOPE, B_ROPE).astype(BF16)
    wukv3 = wukv.reshape(B_KV_LORA, B_HEADS, B_NOPE + B_V)
    wk = _pad_heads(wukv3[:, :, :B_NOPE], 0, B_NOPE).astype(BF16)
    wv = wukv3[:, :, B_NOPE:].reshape(B_KV_LORA, B_HEADS * B_V).astype(BF16)
    w_kr = wdkv[:, B_KV_LORA:]
    kr_pad = _pad_heads(w_kr[:, None, :], B_NOPE, B_ROPE)
    krs_pad = _pad_heads(_swap_halves_cols(w_kr, B_ROPE)[:, None, :], B_NOPE, B_ROPE)
    w1 = jnp.concatenate([wdq, wdkv[:, :B_KV_LORA], kr_pad, krs_pad], axis=1).astype(BF16)

    def tables(g, scale):
        c, s = _rope_tables(pos, g[B_NOPE:], scale)
        n_pos = pos.shape[0]
        zeros = jnp.zeros((n_pos, B_PAD - B_QK), F32)
        c_full = jnp.concatenate(
            [jnp.broadcast_to(g[None, :B_NOPE] * scale, (n_pos, B_NOPE)), c, zeros], axis=-1)
        s_full = jnp.concatenate([jnp.zeros((n_pos, B_NOPE), F32), s, zeros], axis=-1)
        return c_full, s_full

    cq, sq = tables(gq, B_QK ** -0.5 * LOG2E)
    ck, sk = tables(gk, 1.0)
    tab = _position_spec(geo, tm, B_PAD)
    head_major = lambda w_: pl.BlockSpec((B_HEADS, tm, w_), lambda i: (0, i, 0))
    return pl.pallas_call(
        _proj_b_kernel,
        grid=(t_rows // tm,),
        in_specs=[pl.BlockSpec((tm, D_MODEL), lambda i: (i, 0)), _const_spec((1, D_MODEL)),
                  _const_spec(w1.shape), _const_spec((1, B_Q_LORA)), _const_spec((1, B_KV_LORA)),
                  _const_spec(wq.shape), _const_spec(wqs.shape), _const_spec(wk.shape),
                  _const_spec(wv.shape), tab, tab, tab, tab],
        out_specs=[pl.BlockSpec((B_HEADS, 1, B_PAD, tm), lambda i: (0, i, 0, 0)),
                   head_major(B_PAD),
                   pl.BlockSpec((B_HEADS, 1, B_V + BF16_SUBLANES, tm), lambda i: (0, i, 0, 0))],
        out_shape=[jax.ShapeDtypeStruct((B_HEADS, t_rows // tm, B_PAD, tm), BF16),
                   jax.ShapeDtypeStruct((B_HEADS, t_rows, B_PAD), BF16),
                   jax.ShapeDtypeStruct((B_HEADS, t_rows // tm, B_V + BF16_SUBLANES, tm), BF16)],
        compiler_params=_params(("parallel",)),
        name="proj_b",
    )(x, gn.reshape(1, D_MODEL), w1, gcq.reshape(1, -1), gckv.reshape(1, -1), wq, wqs, wk, wv,
      cq, sq, ck, sk)


def _proj_c_kernel(x_ref, gn_ref, w_ref, gq_ref, gk_ref, e_ref, *refs):
    hd = HEAD_DIM
    width = C_HEADS * hd
    outs, scr_ref = refs[:-1], refs[-1]
    tm = x_ref.shape[0]
    h = _rms(x_ref[...], gn_ref[...]).astype(BF16)
    y = _dot(h, w_ref[...])
    gains = (gq_ref[...], gk_ref[...])
    for g, (_, dil) in enumerate(C_PATTERNS):
        for j in range(3):
            base = (g * 3 + j) * width
            val = y[:, base: base + width]
            if j < 2:
                val = val * _segment_rsqrt(val, e_ref, hd) * gains[j]
            out_ref = outs[g * 3 + j]
            if dil == 1:
                out_ref[...] = val.astype(BF16)
            else:
                for s in range(width // LANES):
                    scr_ref[s] = val[:, s * LANES:(s + 1) * LANES]
                for r in range(dil):
                    for s in range(width // LANES):
                        rows = scr_ref[s, pl.ds(r, tm // dil, stride=dil), :]
                        col = r * width + s * LANES
                        out_ref[:, col:col + LANES] = rows.astype(BF16)


def _proj_c_call(x, gn, wqkv, gq, gk):
    t_rows = x.shape[0]
    tm, hd = TOKEN_TILE, HEAD_DIM
    width = C_HEADS * hd
    w = wqkv.astype(BF16)
    ones = _segment_ones(hd)
    specs, shapes = [], []
    for _, dil in C_PATTERNS:
        specs += [pl.BlockSpec((tm // dil, dil * width), lambda i: (i, 0))] * 3
        shapes += [jax.ShapeDtypeStruct((t_rows // dil, dil * width), BF16)] * 3
    outs = pl.pallas_call(
        _proj_c_kernel,
        grid=(t_rows // tm,),
        in_specs=[pl.BlockSpec((tm, D_MODEL), lambda i: (i, 0)), _const_spec((1, D_MODEL)),
                  _const_spec(w.shape), _const_spec((1, width)), _const_spec((1, width)),
                  _const_spec(ones.shape)],
        out_specs=specs,
        out_shape=shapes,
        scratch_shapes=[pltpu.VMEM((width // LANES, tm, LANES), F32)],
        compiler_params=_params(("parallel",)),
        name="proj_c",
    )(x, gn.reshape(1, D_MODEL), w, jnp.tile(gq * (hd ** -0.5 * LOG2E), C_HEADS).reshape(1, width),
      jnp.tile(gk, C_HEADS).reshape(1, width), ones)
    return [outs[3 * g:3 * g + 3] for g in range(len(C_PATTERNS))]


def _band_masked_dist(row0, lo, hi, sub, w):
    nk = sub + 2 * w
    r = lax.broadcasted_iota(jnp.int32, (sub, nk), 0)
    c = lax.broadcasted_iota(jnp.int32, (sub, nk), 1)
    dist = jnp.abs(r + w - c)
    jpos = row0 - w + lax.broadcasted_iota(jnp.int32, (1, nk), 1)
    outside = jnp.where(jpos < lo, w + 1, 0) + jnp.where(jpos >= hi, w + 1, 0)
    return jnp.where(dist + outside <= w, dist.astype(F32), MASK_DIST)


def _two_stage(n_items, first, second, ahead=2):
    pending = {i: first(i) for i in range(min(ahead, n_items))}
    results = []
    for i in range(n_items):
        if i + ahead < n_items:
            pending[i + ahead] = first(i + ahead)
        results.append(second(i, pending.pop(i)))
    return results


def _with_ones(vv):
    return jnp.concatenate([vv, jnp.ones((vv.shape[0], BF16_SUBLANES), BF16)], axis=-1)


def _band_softmax_pv(s, vv_ones, sink):
    m = jnp.max(s, axis=-1, keepdims=True)
    if sink is not None:
        m = jnp.maximum(m, sink)
    hd = vv_ones.shape[-1] - BF16_SUBLANES
    pv = _dot(jnp.exp2(s - m).astype(BF16), vv_ones)
    den = pv[:, hd:hd + 1]
    if sink is not None:
        den = den + jnp.exp2(sink - m)
    return pv[:, :hd] / den, m + jnp.log2(den)


def _attn_a_kernel(slope_ref, sink_ref, q_ref, kp_ref, kc_ref, kn_ref, vp_ref, vc_ref, vn_ref,
                   o_ref, *, tq, p_len, s_len):
    w, sub = A_HALF_WINDOW, BAND_SUB
    group = A_HEADS // A_KV_HEADS
    kvh = pl.program_id(0)
    row0 = pl.program_id(1) * tq
    lo, hi = _seq_bounds(row0, p_len, s_len)
    kk = jnp.concatenate([kp_ref[0], kc_ref[0], kn_ref[0]], axis=0)
    vv = _with_ones(jnp.concatenate([vp_ref[0], vc_ref[0], vn_ref[0]], axis=0))
    sink = jnp.concatenate(
        [jnp.full((sub, 1), sink_ref[kvh * group + g], F32) for g in range(group)], axis=0)

    def keys(r):
        return slice(r * sub, (r + 1) * sub + 2 * w)

    def logits(r):
        md = _band_masked_dist(row0 + r * sub, lo, hi, sub, w)
        bias = jnp.concatenate([md * slope_ref[kvh * group + g] for g in range(group)], axis=0)
        q = q_ref[:, r * sub:(r + 1) * sub, :].reshape(group * sub, HEAD_DIM)
        return _dot_nt(q, kk[keys(r)]) + bias

    tiles = _two_stage(tq // sub, logits, lambda r, s: _band_softmax_pv(s, vv[keys(r)], sink)[0])
    outs = [[o[g * sub:(g + 1) * sub] for o in tiles] for g in range(group)]
    o_ref[...] = jnp.concatenate(
        [jnp.concatenate(rows, axis=0) for rows in outs], axis=-1).astype(BF16)


def _attn_a_call(q, k, v, sink, geo):
    p_len, s_len, t_rows = geo
    tq, w, hd = 2048, A_HALF_WINDOW, HEAD_DIM
    group = A_HEADS // A_KV_HEADS
    per = tq // w
    last = t_rows // w - 1
    prev = pl.BlockSpec((1, w, hd), lambda h, i: (h, jnp.maximum(i * per - 1, 0), 0))
    cur = pl.BlockSpec((1, tq, hd), lambda h, i: (h, i, 0))
    nxt = pl.BlockSpec((1, w, hd), lambda h, i: (h, jnp.minimum((i + 1) * per, last), 0))
    smem = pl.BlockSpec(memory_space=pltpu.SMEM)
    slopes = jnp.asarray(-LOG2E * _alibi_slopes(A_HEADS), F32)
    return pl.pallas_call(
        functools.partial(_attn_a_kernel, tq=tq, p_len=p_len, s_len=s_len),
        grid=(A_KV_HEADS, t_rows // tq),
        in_specs=[smem, smem, pl.BlockSpec((group, tq, hd), lambda h, i: (h, i, 0)),
                  prev, cur, nxt, prev, cur, nxt],
        out_specs=pl.BlockSpec((tq, group * hd), lambda h, i: (i, h)),
        out_shape=jax.ShapeDtypeStruct((t_rows, A_HEADS * hd), BF16),
        compiler_params=_params(("parallel", "parallel")),
        name="attn_a",
    )(slopes, sink.astype(F32) * LOG2E, q, k, k, k, v, v, v)


def _attn_c_kernel(q_ref, kp_ref, kc_ref, kn_ref, vp_ref, vc_ref, vn_ref, o_ref, lse_ref,
                   *, tq, dil, p_len, s_len):
    w, hd, sub = C_STEPS, HEAD_DIM, min(BAND_SUB, tq)
    n_heads = q_ref.shape[-1] // hd
    row0 = pl.program_id(0) * tq
    lo, hi = _seq_bounds(row0, p_len // dil, s_len // dil)
    slopes = -LOG2E * dil * _alibi_slopes(C_HEADS)
    mds = [_band_masked_dist(row0 + r * sub, lo, hi, sub, w) for r in range(tq // sub)]
    n_sub = tq // sub

    heads = {}

    def head_rows(n):
        if n not in heads:
            hs = slice(n * hd, (n + 1) * hd)
            kk = jnp.concatenate([kp_ref[:, hs], kc_ref[:, hs], kn_ref[:, hs]], axis=0)
            vv = jnp.concatenate([vp_ref[:, hs], vc_ref[:, hs], vn_ref[:, hs]], axis=0)
            heads[n] = (q_ref[:, hs], kk, _with_ones(vv))
        return heads[n]

    def logits(i):
        n, r = divmod(i, n_sub)
        q, kk, _ = head_rows(n)
        keys = kk[r * sub:(r + 1) * sub + 2 * w]
        return _dot_nt(q[r * sub:(r + 1) * sub], keys) + mds[r] * float(slopes[n % C_HEADS])

    def attend(i, s):
        n, r = divmod(i, n_sub)
        o, lse2 = _band_softmax_pv(s, head_rows(n)[2][r * sub:(r + 1) * sub + 2 * w], None)
        return o, jnp.broadcast_to(lse2 * (1.0 / LOG2E), (sub, hd))

    tiles = _two_stage(n_heads * n_sub, logits, attend, ahead=3 if n_heads * n_sub > C_HEADS else 2)
    outs = [jnp.concatenate([t[0] for t in tiles[n * n_sub:(n + 1) * n_sub]], axis=0)
            for n in range(n_heads)]
    lses = [jnp.concatenate([t[1] for t in tiles[n * n_sub:(n + 1) * n_sub]], axis=0)
            for n in range(n_heads)]
    o_ref[...] = jnp.concatenate(outs, axis=-1)
    lse_ref[...] = jnp.concatenate(lses, axis=-1)


def _attn_c_call(q, k, v, dil, geo):
    p_len, s_len, t_rows = geo
    w, width = C_STEPS, C_HEADS * HEAD_DIM
    rows = t_rows // dil
    tq = min(1024, s_len // dil)
    per = tq // w
    last = rows // w - 1
    res = min(dil, BAND_SUB * 4 // tq) if tq <= BAND_SUB else 1
    prev = pl.BlockSpec((w, res * width), lambda i, r: (jnp.maximum(i * per - 1, 0), r))
    cur = pl.BlockSpec((tq, res * width), lambda i, r: (i, r))
    nxt = pl.BlockSpec((w, res * width), lambda i, r: (jnp.minimum((i + 1) * per, last), r))
    shape = jax.ShapeDtypeStruct((rows, dil * width), F32)
    return pl.pallas_call(
        functools.partial(_attn_c_kernel, tq=tq, dil=dil, p_len=p_len, s_len=s_len),
        grid=(rows // tq, dil // res),
        in_specs=[cur, prev, cur, nxt, prev, cur, nxt],
        out_specs=[cur, cur],
        out_shape=[shape, shape],
        compiler_params=_params(("parallel", "parallel")),
        name="attn_c_d%d" % dil,
    )(q, k, k, k, v, v, v)


def _attn_dense_kernel(qt_ref, k_ref, vt_ref, o_ref, sa_ref, sb_ref, sc_ref, sd_ref, acc_ref,
                       *, nq, tq, tk, p_len, s_len):
    dv = vt_ref.shape[-2] - BF16_SUBLANES
    width = nq * tq
    shift = int(math.log2(tk))

    def first_chunk(j):
        row0 = (pl.program_id(1) * Q_TILES_PER_STEP + j) * tq
        lo, hi = _seq_bounds(row0, p_len, s_len)
        return lax.shift_right_logical(lo, shift), lax.shift_right_logical(hi - lo, shift)

    def scores(j, c, st_ref):
        start = pl.multiple_of(c * tk, tk)
        qt = jnp.concatenate([qt_ref[g, j] for g in range(nq)], axis=-1)
        st = _dot(k_ref[0, pl.ds(start, tk), :], qt).astype(BF16)
        st_ref[...] = st
        return jnp.max(st, axis=0, keepdims=True).astype(F32)

    def update(c, st_ref, mc, m, acc):
        m_new = jnp.maximum(m, mc)
        alpha = jnp.exp2(m - m_new)
        pt = jnp.exp2(st_ref[...] - m_new.astype(BF16))
        acc_ref[...] = alpha * acc_ref[...] + _dot(vt_ref[0, c], pt)
        return m_new, acc

    def q_tile(j, mcs, start_next):
        first, n_chunks = first_chunk(j)

        def quad(i, carry):
            m, acc, mc_a, mc_b = carry
            c = first + 4 * i
            mc_c = scores(j, c + 2, sc_ref)
            m, acc = update(c, sa_ref, mc_a, m, acc)
            mc_d = scores(j, c + 3, sd_ref)
            m, acc = update(c + 1, sb_ref, mc_b, m, acc)
            mc_a = scores(j, c + 4, sa_ref)
            m, acc = update(c + 2, sc_ref, mc_c, m, acc)
            mc_b = scores(j, c + 5, sb_ref)
            m, acc = update(c + 3, sd_ref, mc_d, m, acc)
            return m, acc, mc_a, mc_b

        acc_ref[...] = jnp.zeros_like(acc_ref)
        carry = (jnp.full((1, width), NEG_INF, F32), 0, *mcs)
        n_body = lax.shift_right_logical(n_chunks, 2) - 1
        carry = lax.fori_loop(0, lax.shift_right_logical(n_body, 1),
                              lambda i, cr: quad(2 * i + 1, quad(2 * i, cr)), carry)
        carry = lax.fori_loop(0, n_body & 1, lambda _, cr: quad(n_body - 1, cr), carry)
        m, acc, mc_a, mc_b = carry
        c = first + n_chunks - 4
        mc_c = scores(j, c + 2, sc_ref)
        m, acc = update(c, sa_ref, mc_a, m, acc)
        mc_d = scores(j, c + 3, sd_ref)
        m, acc = update(c + 1, sb_ref, mc_b, m, acc)
        if start_next:
            first_next, _ = first_chunk(j + 1)
            mc_a = scores(j + 1, first_next, sa_ref)
        m, acc = update(c + 2, sc_ref, mc_c, m, acc)
        if start_next:
            mc_b = scores(j + 1, first_next + 1, sb_ref)
        m, acc = update(c + 3, sd_ref, mc_d, m, acc)
        o = acc_ref[:dv, :] / acc_ref[dv:dv + 1, :]
        rows = pl.ds(pl.multiple_of(j * tq, tq), tq)
        heads = [o[:, g * tq:(g + 1) * tq].T.astype(BF16) for g in range(nq)]
        if nq == 1:
            o_ref[0, rows, :] = heads[0]
        else:
            o_ref[rows, :] = jnp.concatenate(heads, axis=-1)
        return mc_a, mc_b

    first, _ = first_chunk(0)
    mcs = (scores(0, first, sa_ref), scores(0, first + 1, sb_ref))
    mcs = lax.fori_loop(0, Q_TILES_PER_STEP - 1, lambda j, mcs: q_tile(j, mcs, True), mcs)
    q_tile(Q_TILES_PER_STEP - 1, mcs, False)


def _attn_dense_call(qt, k, vt, geo):
    p_len, s_len, t_rows = geo
    hq, n_tiles, dq, tq = qt.shape
    hk, n_chunks, rows_v, tk = vt.shape
    dv = rows_v - BF16_SUBLANES
    nq, per = hq // hk, Q_TILES_PER_STEP
    assert s_len % (4 * tk) == 0 and p_len % (4 * tk) == 0 and n_tiles % per == 0
    return pl.pallas_call(
        functools.partial(_attn_dense_kernel, nq=nq, tq=tq, tk=tk, p_len=p_len, s_len=s_len),
        grid=(hk, n_tiles // per),
        in_specs=[pl.BlockSpec((nq, per, dq, tq), lambda h, i: (h, i, 0, 0)),
                  pl.BlockSpec((1, t_rows, dq), lambda h, i: (h, 0, 0)),
                  pl.BlockSpec((1, n_chunks, dv + BF16_SUBLANES, tk), lambda h, i: (h, 0, 0, 0))],
        out_specs=(pl.BlockSpec((1, per * tq, dv), lambda h, i: (h, i, 0)) if nq == 1
                   else pl.BlockSpec((per * tq, nq * dv), lambda h, i: (i, h))),
        out_shape=jax.ShapeDtypeStruct((hq, t_rows, dv) if nq == 1 else (t_rows, hq * dv), BF16),
        scratch_shapes=[pltpu.VMEM((tk, nq * tq), BF16)] * 4 + [pltpu.VMEM((rows_v, nq * tq), F32)],
        compiler_params=_params(("parallel", "parallel")),
        name="attn_dense_%d" % dq,
    )(qt, k, vt)


def _trunk(x, p, geo, pos):
    for i in range(DEPTH):
        mixer, j = i % 4, i // 4
        x = _ffn_call(x, p["norm_ffn1"][i], p["ffn1_wg"][i], p["ffn1_wu"][i], p["ffn1_wd"][i])
        gn = p["norm_mix"][i]
        if mixer == 0:
            q, k, v = _proj_a_call(x, gn, p["a_wqkv"][j], p["a_gq"][j], p["a_gk"][j])
            pre = (_attn_a_call(q, k, v, p["a_sink"][j], geo), p["a_wo"][j])
        elif mixer == 1:
            qt, k, vt = _proj_b_call(x, gn, p["b_wdq"][j], p["b_gcq"][j], p["b_wuq"][j], p["b_wdkv"][j],
                                     p["b_gckv"][j], p["b_wukv"][j], p["b_gq"][j], p["b_gk"][j], pos, geo)
            pre = (_attn_dense_call(qt, k, vt, geo), p["b_wo"][j])
        elif mixer == 2:
            qkv = _proj_c_call(x, gn, p["c_wqkv"][j], p["c_gq"][j], p["c_gk"][j])
            res = [_attn_c_call(*qkv[g], dil, geo) for g, (_, dil) in enumerate(C_PATTERNS)]
            pre = ([r[0] for r in res], [r[1] for r in res], p["c_wo"][j])
        else:
            qt, k, vt = _proj_d_call(x, gn, p["d_wqkv"][j], p["d_gq"][j], p["d_gk"][j], pos, geo, tq=128)
            pre = (_attn_dense_call(qt, k, vt, geo), p["d_wo"][j])
        ffn2 = (p["norm_ffn2"][i], p["ffn2_wg"][i], p["ffn2_wu"][i], p["ffn2_wd"][i])
        if i < DEPTH - 1:
            x = _ffn_call(x, *ffn2, pre=pre)
        else:
            x = (_ffn_call(x, *ffn2, pre=pre, rows=(0, geo[0])),
                 _ffn_call(x, *ffn2, pre=pre, rows=(geo[0], geo[2] - geo[0])))
    return x


def kernel(x_prompt, x_sample, norm_ffn1, ffn1_wg, ffn1_wu, ffn1_wd, norm_mix, norm_ffn2, ffn2_wg, ffn2_wu, ffn2_wd, a_wqkv, a_gq, a_gk, a_sink, a_wo, b_wdq, b_gcq, b_wuq, b_wdkv, b_gckv, b_wukv, b_gq, b_gk, b_wo, c_wqkv, c_gq, c_gk, c_wo, d_wqkv, d_gq, d_gk, d_wo):
    p = dict(norm_ffn1=norm_ffn1, ffn1_wg=ffn1_wg, ffn1_wu=ffn1_wu, ffn1_wd=ffn1_wd,
             norm_mix=norm_mix, norm_ffn2=norm_ffn2, ffn2_wg=ffn2_wg, ffn2_wu=ffn2_wu,
             ffn2_wd=ffn2_wd, a_wqkv=a_wqkv, a_gq=a_gq, a_gk=a_gk, a_sink=a_sink, a_wo=a_wo,
             b_wdq=b_wdq, b_gcq=b_gcq, b_wuq=b_wuq, b_wdkv=b_wdkv, b_gckv=b_gckv,
             b_wukv=b_wukv, b_gq=b_gq, b_gk=b_gk, b_wo=b_wo, c_wqkv=c_wqkv, c_gq=c_gq,
             c_gk=c_gk, c_wo=c_wo, d_wqkv=d_wqkv, d_gq=d_gq, d_gk=d_gk, d_wo=d_wo)
    n_p, p_seq, _ = x_prompt.shape
    n_s, s_len, _ = x_sample.shape
    assert n_p == 1 and s_len & (s_len - 1) == 0 and p_seq % s_len == 0
    p_len = n_p * p_seq
    t_rows = p_len + n_s * s_len
    geo = (p_len, s_len, t_rows)
    pos = jnp.arange(p_len, dtype=jnp.int32)
    x = (x_prompt.reshape(p_len, D_MODEL), x_sample.reshape(n_s * s_len, D_MODEL))
    y_p, y_s = _trunk(x, p, geo, pos)
    return y_p.reshape(x_prompt.shape), y_s.reshape(x_sample.shape)
```
